```python
import jax, jax.numpy as jnp
from jax import lax
import numpy as np

D_MODEL = 2048
BATCH = 1
SEQ = 16384
DEPTH = 1
DEC_BATCH = 16
DEC_SEQ = 64
PAST_LEN = 4096

CHUNK = 64
D_INNER = 2 * D_MODEL
SSD_HEAD_DIM = 64
SSD_HEADS = D_INNER // SSD_HEAD_DIM
SSD_GROUPS = 8
SSD_HPG = SSD_HEADS // SSD_GROUPS
D_STATE = 128
CONV_WIDTH = 4
CONV_DIM = D_INNER + 2 * SSD_GROUPS * D_STATE
POOL_DIM = D_MODEL
POOL_WINDOWS = (2, 4, 8, 16)
POOL_GROUPS = len(POOL_WINDOWS)
POOL_GROUP_DIM = POOL_DIM // POOL_GROUPS
POOL_BUF = max(POOL_WINDOWS) - 1
N_BRANCHES = 2
Z_END = D_INNER
XBC_END = Z_END + CONV_DIM
DT_END = XBC_END + SSD_HEADS
POOL_END = DT_END + POOL_DIM
IN_DIM = POOL_END + N_BRANCHES * D_MODEL
N_EXPERTS = 32
TOP_K = 4
D_FF = D_MODEL
SWIGLU_ALPHA = 1.702
SWIGLU_LIMIT = 7.0
MOE_BLOCK = 256
EPS = 1e-5

kernel_name = 'hybrid_ssd_pool_moe_stream_step'


def _rmsnorm(x, w):
    xf = x.astype(jnp.float32)
    r = lax.rsqrt(jnp.mean(xf * xf, axis=-1, keepdims=True) + EPS)
    return (xf * r * w.astype(jnp.float32)).astype(x.dtype)


def _ssd_scan(xh, dt, a, bm, cm, state0):
    b, L = xh.shape[0], xh.shape[1]
    nc = -(-L // CHUNK)
    pad = nc * CHUNK - L

    def chunked(t):
        t = jnp.pad(t, [(0, 0), (0, pad)] + [(0, 0)] * (t.ndim - 2))
        return t.reshape((b, nc, CHUNK) + t.shape[2:])

    xh, dt, bm, cm = chunked(xh), chunked(dt), chunked(bm), chunked(cm)
    a_cum = jnp.cumsum(dt * a, axis=2)
    xdt = xh * dt[..., None]
    causal = jnp.tril(jnp.ones((CHUNK, CHUNK), dtype=bool))
    seg = a_cum[:, :, :, None] - a_cum[:, :, None, :]
    decay = jnp.exp(jnp.where(causal[:, :, None, None], seg, -jnp.inf))
    cb = jnp.einsum('bclgn,bcsgn->bclsg', cm, bm)
    y_diag = jnp.einsum('bclsge,bcsgep->bclgep', cb[..., None] * decay, xdt)
    decay_end = jnp.exp(a_cum[:, :, -1:] - a_cum)
    chunk_states = jnp.einsum('bclgn,bclgep->bcgepn', bm, xdt * decay_end[..., None])
    chunk_decay = jnp.exp(a_cum[:, :, -1])

    def step(s, inp):
        dec, st = inp
        return dec[..., None, None] * s + st, s

    final, prev = lax.scan(step, state0.astype(jnp.float32),
                           (jnp.moveaxis(chunk_decay, 1, 0), jnp.moveaxis(chunk_states, 1, 0)))
    prev = jnp.moveaxis(prev, 0, 1)
    y_off = jnp.einsum('bclgn,bcgepn->bclgep', cm, prev) * jnp.exp(a_cum)[..., None]
    y = (y_diag + y_off).reshape((b, nc * CHUNK) + y_diag.shape[3:])[:, :L]
    return y, final


def _mixer(u, pos0, ssm0, conv0, pool0, w_in, conv_w, conv_b, dt_bias, a_log, d_skip,
           ssd_norm_w, w_pool, pool_scale, w_proj_ssd, w_proj_pool, w_out):
    b, L, _ = u.shape
    proj = u @ w_in
    z, xbc, dt_raw, pu, gates = jnp.split(proj, [Z_END, XBC_END, DT_END, POOL_END], axis=-1)
    xbc_full = jnp.concatenate([conv0.astype(xbc.dtype), xbc], axis=1)
    conv = conv_b + sum(conv_w[k] * xbc_full[:, k:k + L] for k in range(CONV_WIDTH))
    new_conv = xbc_full[:, L:]
    xbc_act = jax.nn.silu(conv)
    xs, bm, cm = jnp.split(xbc_act, [D_INNER, D_INNER + SSD_GROUPS * D_STATE], axis=-1)
    dt = jax.nn.softplus(dt_raw.astype(jnp.float32) + dt_bias.astype(jnp.float32))
    a = -jnp.exp(a_log.astype(jnp.float32))
    xh = xs.reshape(b, L, SSD_GROUPS, SSD_HPG, SSD_HEAD_DIM)
    y, new_ssm = _ssd_scan(xh, dt.reshape(b, L, SSD_GROUPS, SSD_HPG),
                           a.reshape(SSD_GROUPS, SSD_HPG),
                           bm.reshape(b, L, SSD_GROUPS, D_STATE), cm.reshape(b, L, SSD_GROUPS, D_STATE),
                           ssm0.reshape(b, SSD_GROUPS, SSD_HPG, SSD_HEAD_DIM, D_STATE))
    y = y + d_skip.reshape(SSD_GROUPS, SSD_HPG)[..., None] * xh
    y = _rmsnorm((y.reshape(b, L, D_INNER) * jax.nn.silu(z)).astype(u.dtype), ssd_norm_w)
    new_ssm = new_ssm.reshape(b, SSD_HEADS, SSD_HEAD_DIM, D_STATE)
    pool_full = jnp.concatenate([pool0.astype(pu.dtype), pu], axis=1)
    new_pool = pool_full[:, L:]
    cs = jnp.concatenate([jnp.zeros((b, 1, POOL_DIM), jnp.float32),
                          jnp.cumsum(pool_full.astype(jnp.float32), axis=1)], axis=1)
    pos = (pos0 + jnp.arange(L, dtype=jnp.int32)).astype(jnp.float32)[None, :, None]
    end = cs[:, POOL_BUF + 1:POOL_BUF + 1 + L]
    pooled = []
    for g, win in enumerate(POOL_WINDOWS):
        sl = slice(g * POOL_GROUP_DIM, (g + 1) * POOL_GROUP_DIM)
        start = cs[:, POOL_BUF + 1 - win:POOL_BUF + 1 - win + L, sl]
        count = jnp.minimum(pos + 1.0, float(win))
        pooled.append((end[..., sl] - start) / count - pu[..., sl].astype(jnp.float32))
    pooled = jnp.stack(pooled, axis=2).astype(u.dtype)
    yp = jnp.einsum('blgc,gcd->blgd', pooled, w_pool).reshape(b, L, POOL_DIM) * pool_scale
    g_ssd, g_pool = jnp.split(gates, 2, axis=-1)
    merged = jax.nn.sigmoid(g_ssd) * (y @ w_proj_ssd) + jax.nn.sigmoid(g_pool) * (yp @ w_proj_pool)
    return merged @ w_out, new_ssm, new_conv, new_pool


def _moe(x, w_router, b_router, w_up, b_up, w_down, b_down):
    T, D = x.shape
    logits = (x @ w_router + b_router).astype(jnp.float32)
    top_val, top_idx = lax.top_k(logits, TOP_K)
    gates = jax.nn.softmax(top_val, axis=-1)
    n_assign = T * TOP_K
    flat_e = top_idx.reshape(-1).astype(jnp.int32)
    flat_tok = jnp.arange(n_assign, dtype=jnp.int32) // TOP_K
    flat_gate = gates.reshape(-1)
    order = jnp.argsort(flat_e)
    sorted_e = flat_e[order]
    counts = jnp.zeros((N_EXPERTS,), jnp.int32).at[flat_e].add(1)
    starts = jnp.cumsum(counts) - counts
    padded = (counts + MOE_BLOCK - 1) // MOE_BLOCK * MOE_BLOCK
    pad_ends = jnp.cumsum(padded)
    pad_starts = pad_ends - padded
    dest = pad_starts[sorted_e] + (jnp.arange(n_assign, dtype=jnp.int32) - starts[sorted_e])
    n_blocks = -(-n_assign // MOE_BLOCK) + N_EXPERTS
    n_rows = n_blocks * MOE_BLOCK
    row_tok = jnp.full((n_rows,), T, jnp.int32).at[dest].set(flat_tok[order])
    row_gate = jnp.zeros((n_rows,), jnp.float32).at[dest].set(flat_gate[order])
    block_start = jnp.arange(n_blocks, dtype=jnp.int32) * MOE_BLOCK
    block_e = jnp.minimum(jnp.searchsorted(pad_ends, block_start, side='right'), N_EXPERTS - 1)
    x_pad = jnp.concatenate([x, jnp.zeros((1, D), x.dtype)], axis=0)

    def expert_block(args):
        e, tok, g = args
        xb = x_pad[tok]
        gu = xb @ w_up[e] + b_up[e]
        glu = jnp.minimum(gu[:, :D_FF], SWIGLU_LIMIT)
        lin = jnp.clip(gu[:, D_FF:], -SWIGLU_LIMIT, SWIGLU_LIMIT)
        act = glu * jax.nn.sigmoid(SWIGLU_ALPHA * glu) * (lin + 1.0)
        out = act @ w_down[e] + b_down[e]
        return (out * g[:, None]).astype(x.dtype)

    outs = lax.map(expert_block, (block_e, row_tok.reshape(n_blocks, MOE_BLOCK),
                                  row_gate.reshape(n_blocks, MOE_BLOCK)))
    y = jax.ops.segment_sum(outs.reshape(n_rows, D), row_tok, num_segments=T + 1)
    return y[:T]


def _layer(x, pos0, ssm0, conv0, pool0, norm_mix_w, w_in, conv_w, conv_b, dt_bias, a_log, d_skip,
           ssd_norm_w, w_pool, pool_scale, w_proj_ssd, w_proj_pool, w_out, norm_ffn_w,
           w_router, b_router, w_up, b_up, w_down, b_down):
    mix, new_ssm, new_conv, new_pool = _mixer(_rmsnorm(x, norm_mix_w), pos0, ssm0, conv0, pool0,
                                              w_in, conv_w, conv_b, dt_bias, a_log, d_skip,
                                              ssd_norm_w, w_pool, pool_scale, w_proj_ssd,
                                              w_proj_pool, w_out)
    h = x + mix
    b, L, D = h.shape
    ffn = _moe(_rmsnorm(h, norm_ffn_w).reshape(b * L, D), w_router, b_router, w_up, b_up, w_down, b_down)
    return h + ffn.reshape(b, L, D), new_ssm, new_conv, new_pool


def setup_inputs(seed: int = 0) -> dict:
    key = jax.random.key(seed)
    ks = jax.random.split(key, 32)
    f32 = jnp.float32
    nrm = lambda k, shape, s: jax.random.normal(k, shape, f32) * s
    dt0 = jnp.exp(jax.random.uniform(ks[8], (DEPTH, SSD_HEADS), f32, np.log(1e-3), np.log(1e-1)))
    return {
        'x_prompt': nrm(ks[0], (BATCH, SEQ, D_MODEL), 1.0),
        'x_sample': nrm(ks[1], (DEC_BATCH, DEC_SEQ, D_MODEL), 1.0),
        'state_ssm': nrm(ks[2], (DEPTH, DEC_BATCH, SSD_HEADS, SSD_HEAD_DIM, D_STATE), 0.5),
        'state_conv': nrm(ks[3], (DEPTH, DEC_BATCH, CONV_WIDTH - 1, CONV_DIM), 1.0),
        'state_pool': nrm(ks[4], (DEPTH, DEC_BATCH, POOL_BUF, POOL_DIM), 1.0),
        'norm_mix_w': 1.0 + nrm(ks[5], (DEPTH, D_MODEL), 0.05),
        'w_in': nrm(ks[6], (DEPTH, D_MODEL, IN_DIM), D_MODEL ** -0.5),
        'conv_w': nrm(ks[7], (DEPTH, CONV_WIDTH, CONV_DIM), CONV_WIDTH ** -0.5),
        'conv_b': nrm(ks[9], (DEPTH, CONV_DIM), 0.02),
        'dt_bias': dt0 + jnp.log(-jnp.expm1(-dt0)),
        'a_log': jnp.log(jax.random.uniform(ks[10], (DEPTH, SSD_HEADS), f32, 1.0, 16.0)),
        'd_skip': 1.0 + nrm(ks[11], (DEPTH, SSD_HEADS), 0.05),
        'ssd_norm_w': 1.0 + nrm(ks[12], (DEPTH, D_INNER), 0.05),
        'w_pool': nrm(ks[13], (DEPTH, POOL_GROUPS, POOL_GROUP_DIM, POOL_GROUP_DIM), POOL_GROUP_DIM ** -0.5),
        'pool_scale': 1.0 + nrm(ks[14], (DEPTH, POOL_DIM), 0.1),
        'w_proj_ssd': nrm(ks[15], (DEPTH, D_INNER, D_MODEL), D_INNER ** -0.5),
        'w_proj_pool': nrm(ks[16], (DEPTH, POOL_DIM, D_MODEL), POOL_DIM ** -0.5),
        'w_out': nrm(ks[17], (DEPTH, D_MODEL, D_MODEL), D_MODEL ** -0.5),
        'norm_ffn_w': 1.0 + nrm(ks[18], (DEPTH, D_MODEL), 0.05),
        'w_router': nrm(ks[19], (DEPTH, D_MODEL, N_EXPERTS), D_MODEL ** -0.5),
        'b_router': nrm(ks[20], (DEPTH, N_EXPERTS), 0.01),
        'w_up': nrm(ks[21], (DEPTH, N_EXPERTS, D_MODEL, 2 * D_FF), D_MODEL ** -0.5),
        'b_up': nrm(ks[22], (DEPTH, N_EXPERTS, 2 * D_FF), 0.02),
        'w_down': nrm(ks[23], (DEPTH, N_EXPERTS, D_FF, D_MODEL), D_FF ** -0.5),
        'b_down': nrm(ks[24], (DEPTH, N_EXPERTS, D_MODEL), 0.02),
        'norm_final_w': 1.0 + nrm(ks[25], (D_MODEL,), 0.05),
    }


def reference(x_prompt, x_sample, state_ssm, state_conv, state_pool, norm_mix_w, w_in, conv_w, conv_b,
              dt_bias, a_log, d_skip, ssd_norm_w, w_pool, pool_scale, w_proj_ssd, w_proj_pool, w_out,
              norm_ffn_w, w_router, b_router, w_up, b_up, w_down, b_down, norm_final_w):
    hp, hs = x_prompt, x_sample
    b_p = x_prompt.shape[0]
    ssm_p, conv_p, pool_p, ssm_s, conv_s, pool_s = [], [], [], [], [], []
    for i in range(DEPTH):
        w = (norm_mix_w[i], w_in[i], conv_w[i], conv_b[i], dt_bias[i], a_log[i], d_skip[i], ssd_norm_w[i],
             w_pool[i], pool_scale[i], w_proj_ssd[i], w_proj_pool[i], w_out[i], norm_ffn_w[i],
             w_router[i], b_router[i], w_up[i], b_up[i], w_down[i], b_down[i])
        hp, s1, c1, p1 = _layer(hp, 0,
                                jnp.zeros((b_p, SSD_HEADS, SSD_HEAD_DIM, D_STATE), jnp.float32),
                                jnp.zeros((b_p, CONV_WIDTH - 1, CONV_DIM), x_prompt.dtype),
                                jnp.zeros((b_p, POOL_BUF, POOL_DIM), x_prompt.dtype), *w)
        hs, s2, c2, p2 = _layer(hs, PAST_LEN, state_ssm[i], state_conv[i], state_pool[i], *w)
        ssm_p.append(s1); conv_p.append(c1); pool_p.append(p1)
        ssm_s.append(s2); conv_s.append(c2); pool_s.append(p2)
    y_prompt = _rmsnorm(hp, norm_final_w)
    y_sample = _rmsnorm(hs, norm_final_w)
    return (y_prompt, y_sample, jnp.stack(ssm_p), jnp.stack(conv_p), jnp.stack(pool_p),
            jnp.stack(ssm_s), jnp.stack(conv_s), jnp.stack(pool_s))
```

```python
import functools

import numpy as np
import jax
import jax.numpy as jnp
from jax import lax
from jax.experimental import pallas as pl
from jax.experimental.pallas import tpu as pltpu

F32 = jnp.float32
BF16 = jnp.bfloat16
HIGHEST = lax.Precision.HIGHEST

CHUNK = 64
HEAD_DIM = 64
N_GROUPS = 8
D_STATE = 128
CONV_WIDTH = 4
POOL_WINDOWS = (2, 4, 8, 16)
POOL_BUF = 15
PAST_LEN = 4096
TOP_K = 4
SWIGLU_ALPHA = 1.702
SWIGLU_LIMIT = 7.0
EPS = 1e-5
LANES = 128
COL_BLOCK = 512
MOE_ROWS = 512
VMEM_LIMIT = 56 * 1024 * 1024


def _pick_tile(n, candidates):
    for c in candidates:
        if n % c == 0:
            return c
    raise ValueError(f"no tile for {n} in {candidates}")


def _params(sem, vmem=VMEM_LIMIT):
    return pltpu.CompilerParams(dimension_semantics=sem, vmem_limit_bytes=vmem)


def _split3(v):
    p1 = v.astype(BF16)
    r1 = v - p1.astype(F32)
    p2 = r1.astype(BF16)
    p3 = (r1 - p2.astype(F32)).astype(BF16)
    return p1, p2, p3


def _silu(v):
    return v * jax.nn.sigmoid(v)


def _norm_kernel(x_ref, w_ref, wdt_ref, u_ref, dt_ref):
    x = x_ref[...]
    r = lax.rsqrt(jnp.mean(x * x, axis=-1, keepdims=True) + EPS)
    u = x * r * w_ref[...]
    u_ref[...] = u.astype(BF16)
    dt_ref[...] = jnp.dot(u, wdt_ref[...], precision=HIGHEST, preferred_element_type=F32)


def _norm_call(x, w, wdt):
    t, d = x.shape
    tm = _pick_tile(t, (512, 256, 128, 64))
    return pl.pallas_call(
        _norm_kernel,
        name="k_norm",
        grid=(t // tm,),
        in_specs=[pl.BlockSpec((tm, d), lambda i: (i, 0)),
                  pl.BlockSpec((1, d), lambda i: (0, 0)),
                  pl.BlockSpec((d, LANES), lambda i: (0, 0))],
        out_specs=[pl.BlockSpec((tm, d), lambda i: (i, 0)),
                   pl.BlockSpec((tm, LANES), lambda i: (i, 0))],
        out_shape=[jax.ShapeDtypeStruct((t, d), BF16), jax.ShapeDtypeStruct((t, LANES), F32)],
        compiler_params=_params(("parallel",)),
    )(x, w, wdt)


def _inproj_kernel(u_ref, w_ref, o_ref):
    o_ref[...] = jnp.dot(u_ref[...], w_ref[...], preferred_element_type=F32)


def _inproj_call(u, w):
    t, d = u.shape
    n = w.shape[1]
    tm = _pick_tile(t, (1024, 512, 256, 128, 64))
    nb = n // COL_BLOCK
    return pl.pallas_call(
        _inproj_kernel,
        name="k_inproj",
        grid=(t // tm, nb),
        in_specs=[pl.BlockSpec((tm, d), lambda i, j: (i, 0)),
                  pl.BlockSpec((d, COL_BLOCK), lambda i, j: (0, j))],
        out_specs=pl.BlockSpec((None, tm, COL_BLOCK), lambda i, j: (j, i, 0)),
        out_shape=jax.ShapeDtypeStruct((nb, t, COL_BLOCK), F32),
        compiler_params=_params(("parallel", "arbitrary")),
    )(u, w)


def _ssd_kernel(z_ref, x_ref, bc_ref, dt_ref, ssm0_ref, cx0_ref, cbc0_ref,
                dtb_ref, alog_ref, dsk_ref, cwx_ref, cbx_ref, cwbc_ref, cbbc_ref,
                tril3_ref, e3_ref, diag_ref, caus_ref, bd_ref,
                yg_ref, ssm_ref, bufx, bufbc, exs, *, n_prompt_chunks):
    c = pl.program_id(0)
    first = jnp.logical_or(c == 0, c >= n_prompt_chunks)

    @pl.when(first)
    def _():
        ssm_ref[...] = ssm0_ref[...]
        bufx[0:8, :] = cx0_ref[...]
        bufbc[0:8, :] = cbc0_ref[...]

    @pl.when(jnp.logical_not(first))
    def _():
        bufx[0:8, :] = bufx[CHUNK:CHUNK + 8, :]
        bufbc[0:8, :] = bufbc[CHUNK:CHUNK + 8, :]

    for g in range(N_GROUPS):
        bufx[8:8 + CHUNK, g * COL_BLOCK:(g + 1) * COL_BLOCK] = x_ref[g]
    for q in range(N_GROUPS // 2):
        bufbc[8:8 + CHUNK, q * COL_BLOCK:(q + 1) * COL_BLOCK] = bc_ref[q]

    dtv = dt_ref[...] + dtb_ref[...]
    dt = jnp.maximum(dtv, 0.0) + jnp.log1p(jnp.exp(-jnp.abs(dtv)))
    d_a = dt * (-jnp.exp(alog_ref[...]))
    p1, p2, p3 = _split3(d_a)
    acum = jnp.dot(tril3_ref[...], jnp.concatenate([p1, p2, p3], axis=0),
                   preferred_element_type=F32)
    q1, q2, q3 = _split3(jnp.concatenate([acum, dt], axis=0))
    exs[...] = jnp.dot(jnp.concatenate([q1, q2, q3], axis=1), e3_ref[...],
                       preferred_element_type=F32)

    caus = caus_ref[...] > 0.0
    bd = bd_ref[...]
    hw = 4 * HEAD_DIM
    for g in range(N_GROUPS):
        sl = slice(g * COL_BLOCK, (g + 1) * COL_BLOCK)
        slb = slice(g * 2 * D_STATE, (g + 1) * 2 * D_STATE)
        xc = cbx_ref[:, sl]
        bcc = cbbc_ref[:, slb]
        for k in range(CONV_WIDTH):
            xc = xc + cwx_ref[k:k + 1, sl] * bufx[5 + k:5 + k + CHUNK, sl]
            bcc = bcc + cwbc_ref[k:k + 1, slb] * bufbc[5 + k:5 + k + CHUNK, slb]
        xs = _silu(xc)
        bca = _silu(bcc)
        b_g = bca[:, :D_STATE]
        c_g = bca[:, D_STATE:]
        acx = exs[0:CHUNK, sl]
        dtx = exs[CHUNK:2 * CHUNK, sl]
        alast = acx[CHUNK - 1:CHUNK, :]
        arow = jnp.sum(acx * diag_ref[:, sl], axis=0, keepdims=True)
        xdt = xs * dtx
        xdtb = xdt.astype(BF16)
        bb = b_g.astype(BF16)
        cb = c_g.astype(BF16)
        cb2 = lax.dot_general(cb, jnp.concatenate([bb, bb], axis=0),
                              (((1,), (1,)), ((), ())), preferred_element_type=F32)
        ydiag = []
        for q in range(2):
            lhs = []
            for d in range(2):
                lo = q * hw + d * LANES
                seg = acx[:, lo:lo + LANES] - arow[:, lo:lo + LANES]
                lhs.append((cb2 * jnp.exp(jnp.where(caus, seg, -jnp.inf))).astype(BF16))
            xq = xdtb[:, q * hw:(q + 1) * hw]
            wq = jnp.concatenate([xq, xq, xq, xq], axis=0) * bd
            ydiag.append(jnp.dot(jnp.concatenate(lhs, axis=1), wq, preferred_element_type=F32))
        s_old = ssm_ref[g]
        yoff = jnp.dot(cb, s_old.astype(BF16), preferred_element_type=F32) * jnp.exp(acx)
        y = jnp.concatenate(ydiag, axis=1) + yoff + dsk_ref[:, sl] * xs
        yg_ref[:, sl] = y * _silu(z_ref[g])
        v = (xdt * jnp.exp(alast - acx)).astype(BF16)
        ssm_ref[g] = jnp.exp(alast) * s_old + jnp.dot(b_g.T.astype(BF16), v,
                                                      preferred_element_type=F32)


def _ssd_call(p, dt_raw, ssm0, cx0, cbc0, dtb, alog, dsk, cwx, cbx, cwbc, cbbc, consts,
              n_prompt_chunks):
    _, t, _ = p.shape
    n_chunks = t // CHUNK
    n_seq = ssm0.shape[0]
    d_inner = N_GROUPS * COL_BLOCK
    d_bc = N_GROUPS * 2 * D_STATE
    tril3, e3, diag, caus, bd = consts

    def seq(c):
        return jnp.maximum(c - (n_prompt_chunks - 1), 0)

    def const2(a):
        return pl.BlockSpec(a.shape, lambda c: (0, 0))

    return pl.pallas_call(
        functools.partial(_ssd_kernel, n_prompt_chunks=n_prompt_chunks),
        name="k_ssd",
        grid=(n_chunks,),
        in_specs=[pl.BlockSpec((N_GROUPS, CHUNK, COL_BLOCK), lambda c: (0, c, 0)),
                  pl.BlockSpec((N_GROUPS, CHUNK, COL_BLOCK), lambda c: (1, c, 0)),
                  pl.BlockSpec((N_GROUPS // 2, CHUNK, COL_BLOCK), lambda c: (4, c, 0)),
                  pl.BlockSpec((CHUNK, LANES), lambda c: (c, 0)),
                  pl.BlockSpec((None, N_GROUPS, D_STATE, COL_BLOCK), lambda c: (seq(c), 0, 0, 0)),
                  pl.BlockSpec((None, 8, d_inner), lambda c: (seq(c), 0, 0)),
                  pl.BlockSpec((None, 8, d_bc), lambda c: (seq(c), 0, 0)),
                  const2(dtb), const2(alog), const2(dsk), const2(cwx), const2(cbx),
                  const2(cwbc), const2(cbbc),
                  const2(tril3), const2(e3), const2(diag), const2(caus), const2(bd)],
        out_specs=[pl.BlockSpec((CHUNK, d_inner), lambda c: (c, 0)),
                   pl.BlockSpec((None, N_GROUPS, D_STATE, COL_BLOCK), lambda c: (seq(c), 0, 0, 0))],
        out_shape=[jax.ShapeDtypeStruct((t, d_inner), F32),
                   jax.ShapeDtypeStruct((n_seq, N_GROUPS, D_STATE, COL_BLOCK), F32)],
        scratch_shapes=[pltpu.VMEM((CHUNK + 8, d_inner), F32),
                        pltpu.VMEM((CHUNK + 8, d_bc), F32),
                        pltpu.VMEM((2 * CHUNK, d_inner), F32)],
        compiler_params=_params(("arbitrary",)),
    )(p, p, p, dt_raw, ssm0, cx0, cbc0, dtb, alog, dsk, cwx, cbx, cwbc, cbbc,
      tril3, e3, diag, caus, bd)


def _ssd_consts():
    l = np.arange(CHUNK)
    tril = (l[:, None] >= l[None, :]).astype(np.float32)
    tril3 = np.concatenate([tril, tril, tril], axis=1)
    n_heads = N_GROUPS * COL_BLOCK // HEAD_DIM
    col_head = np.arange(n_heads * HEAD_DIM) // HEAD_DIM
    col_pos = np.arange(n_heads * HEAD_DIM) % HEAD_DIM
    e = (np.arange(LANES)[:, None] == col_head[None, :]).astype(np.float32)
    e3 = np.concatenate([e, e, e], axis=0)
    diag = (l[:, None] == col_pos[None, :]).astype(np.float32)
    caus = np.concatenate([tril, tril], axis=1)
    r = np.arange(4 * HEAD_DIM)
    bd = (r[:, None] // HEAD_DIM == r[None, :] // HEAD_DIM).astype(np.float32)
    return (jnp.asarray(tril3, BF16), jnp.asarray(e3, BF16), jnp.asarray(diag, F32),
            jnp.asarray(caus, F32), jnp.asarray(bd, BF16))


def _pool_kernel(pu_ref, pool0_ref, wp_ref, scale_ref, yp_ref, pbuf, *, n_prompt_chunks, past_len):
    c = pl.program_id(0)
    first = jnp.logical_or(c == 0, c >= n_prompt_chunks)
    hist = POOL_BUF + 1

    @pl.when(first)
    def _():
        pbuf[0:hist, :] = pool0_ref[...]

    @pl.when(jnp.logical_not(first))
    def _():
        pbuf[0:hist, :] = pbuf[CHUNK:CHUNK + hist, :]

    for g in range(len(POOL_WINDOWS)):
        pbuf[hist:hist + CHUNK, g * COL_BLOCK:(g + 1) * COL_BLOCK] = pu_ref[g]

    pos0 = jnp.where(c < n_prompt_chunks, c * CHUNK, past_len)
    pos = (pos0 + lax.broadcasted_iota(jnp.int32, (CHUNK, 1), 0)).astype(F32)
    for g, win in enumerate(POOL_WINDOWS):
        sl = slice(g * COL_BLOCK, (g + 1) * COL_BLOCK)
        cur = pbuf[hist:hist + CHUNK, sl]
        tot = cur
        for i in range(1, win):
            tot = tot + pbuf[hist - i:hist - i + CHUNK, sl]
        count = jnp.minimum(pos + 1.0, float(win))
        pooled = tot / count - cur
        yp_ref[:, sl] = jnp.dot(pooled.astype(BF16), wp_ref[g],
                                preferred_element_type=F32) * scale_ref[:, sl]


def _pool_call(p, pool0, wp, scale, n_prompt_chunks, past_len):
    _, t, _ = p.shape
    ng = len(POOL_WINDOWS)
    d_pool = ng * COL_BLOCK
    hist = POOL_BUF + 1

    def seq(c):
        return jnp.maximum(c - (n_prompt_chunks - 1), 0)

    return pl.pallas_call(
        functools.partial(_pool_kernel, n_prompt_chunks=n_prompt_chunks, past_len=past_len),
        name="k_pool",
        grid=(t // CHUNK,),
        in_specs=[pl.BlockSpec((ng, CHUNK, COL_BLOCK), lambda c: (5, c, 0)),
                  pl.BlockSpec((None, hist, d_pool), lambda c: (seq(c), 0, 0)),
                  pl.BlockSpec(wp.shape, lambda c: (0, 0, 0)),
                  pl.BlockSpec((1, d_pool), lambda c: (0, 0))],
        out_specs=pl.BlockSpec((CHUNK, d_pool), lambda c: (c, 0)),
        out_shape=jax.ShapeDtypeStruct((t, d_pool), F32),
        scratch_shapes=[pltpu.VMEM((CHUNK + hist, d_pool), F32)],
        compiler_params=_params(("arbitrary",)),
    )(p, pool0, wp, scale)


def _merge_kernel(yg_ref, yp_ref, gs_ref, gp_ref, nw_ref, wps_ref, wpp_ref, o_ref, ynb, ypb):
    @pl.when(pl.program_id(1) == 0)
    def _():
        y = yg_ref[...]
        r = lax.rsqrt(jnp.mean(y * y, axis=-1, keepdims=True) + EPS)
        ynb[...] = (y * r * nw_ref[...]).astype(BF16)
        ypb[...] = yp_ref[...].astype(BF16)

    a = jnp.dot(ynb[...], wps_ref[...], preferred_element_type=F32)
    b = jnp.dot(ypb[...], wpp_ref[...], preferred_element_type=F32)
    o_ref[...] = (jax.nn.sigmoid(gs_ref[...]) * a + jax.nn.sigmoid(gp_ref[...]) * b).astype(BF16)


def _merge_call(yg, yp, p, nw, wps, wpp):
    t, d_inner = yg.shape
    d_pool = yp.shape[1]
    d_model = wps.shape[1]
    tm = _pick_tile(t, (256, 128, 64))
    nj = d_model // COL_BLOCK
    gs0 = 24
    gp0 = gs0 + nj
    return pl.pallas_call(
        _merge_kernel,
        name="k_merge",
        grid=(t // tm, nj),
        in_specs=[pl.BlockSpec((tm, d_inner), lambda i, j: (i, 0)),
                  pl.BlockSpec((tm, d_pool), lambda i, j: (i, 0)),
                  pl.BlockSpec((None, tm, COL_BLOCK), lambda i, j: (gs0 + j, i, 0)),
                  pl.BlockSpec((None, tm, COL_BLOCK), lambda i, j: (gp0 + j, i, 0)),
                  pl.BlockSpec((1, d_inner), lambda i, j: (0, 0)),
                  pl.BlockSpec((d_inner, COL_BLOCK), lambda i, j: (0, j)),
                  pl.BlockSpec((d_pool, COL_BLOCK), lambda i, j: (0, j))],
        out_specs=pl.BlockSpec((tm, COL_BLOCK), lambda i, j: (i, j)),
        out_shape=jax.ShapeDtypeStruct((t, d_model), BF16),
        scratch_shapes=[pltpu.VMEM((tm, d_inner), BF16), pltpu.VMEM((tm, d_pool), BF16)],
        compiler_params=_params(("parallel", "arbitrary")),
    )(yg, yp, p, p, nw, wps, wpp)


def _route_kernel(m_ref, x_ref, wo_ref, nw_ref, wr_ref, br_ref, trs_ref,
                  h_ref, hn_ref, eidx_ref, rank_ref, gate_ref, cnt_ref, carry):
    i = pl.program_id(0)

    @pl.when(i == 0)
    def _():
        carry[...] = jnp.zeros_like(carry)

    h = x_ref[...] + jnp.dot(m_ref[...], wo_ref[...], preferred_element_type=F32)
    h_ref[...] = h
    r = lax.rsqrt(jnp.mean(h * h, axis=-1, keepdims=True) + EPS)
    hn = h * r * nw_ref[...]
    hn_ref[...] = hn
    logits = jnp.dot(hn, wr_ref[...], precision=HIGHEST, preferred_element_type=F32) + br_ref[...]
    lane = lax.broadcasted_iota(jnp.int32, logits.shape, 1)
    work = logits
    member = jnp.zeros(logits.shape, F32)
    vals, idxs = [], []
    for _ in range(TOP_K):
        m = jnp.max(work, axis=-1, keepdims=True)
        idx = jnp.min(jnp.where(work == m, lane, LANES), axis=-1, keepdims=True)
        hit = lane == idx
        member = member + hit.astype(F32)
        work = jnp.where(hit, -jnp.inf, work)
        vals.append(m)
        idxs.append(idx)
    ex = [jnp.exp(v - vals[0]) for v in vals]
    den = ex[0] + ex[1] + ex[2] + ex[3]
    before = jnp.dot(trs_ref[...], member.astype(BF16), preferred_element_type=F32) + carry[0:1, :]
    eidx = jnp.zeros(logits.shape, jnp.int32)
    rank = jnp.zeros(logits.shape, jnp.int32)
    gate = jnp.zeros(logits.shape, F32)
    for k in range(TOP_K):
        rk = jnp.sum(jnp.where(lane == idxs[k], before, 0.0), axis=-1, keepdims=True)
        eidx = jnp.where(lane == k, idxs[k], eidx)
        rank = jnp.where(lane == k, rk.astype(jnp.int32), rank)
        gate = jnp.where(lane == k, ex[k] / den, gate)
    eidx_ref[...] = eidx
    rank_ref[...] = rank
    gate_ref[...] = gate
    carry[0:1, :] = carry[0:1, :] + jnp.sum(member, axis=0, keepdims=True)
    cnt_ref[...] = carry[...].astype(jnp.int32)


def _route_call(merged, x, wo, nw, wr, br):
    t, d = x.shape
    tm = _pick_tile(t, (512, 256, 128, 64))
    ri = np.arange(tm)
    trs = jnp.asarray((ri[:, None] > ri[None, :]).astype(np.float32), BF16)
    row = lambda i: (i, 0)
    fix = lambda i: (0, 0)
    return pl.pallas_call(
        _route_kernel,
        name="k_route",
        grid=(t // tm,),
        in_specs=[pl.BlockSpec((tm, d), row), pl.BlockSpec((tm, d), row),
                  pl.BlockSpec((d, d), fix), pl.BlockSpec((1, d), fix),
                  pl.BlockSpec((d, LANES), fix), pl.BlockSpec((1, LANES), fix),
                  pl.BlockSpec((tm, tm), fix)],
        out_specs=[pl.BlockSpec((tm, d), row), pl.BlockSpec((tm, d), row),
                   pl.BlockSpec((tm, LANES), row), pl.BlockSpec((tm, LANES), row),
                   pl.BlockSpec((tm, LANES), row), pl.BlockSpec((8, LANES), fix)],
        out_shape=[jax.ShapeDtypeStruct((t, d), F32), jax.ShapeDtypeStruct((t, d), F32),
                   jax.ShapeDtypeStruct((t, LANES), jnp.int32),
                   jax.ShapeDtypeStruct((t, LANES), jnp.int32),
                   jax.ShapeDtypeStruct((t, LANES), F32),
                   jax.ShapeDtypeStruct((8, LANES), jnp.int32)],
        scratch_shapes=[pltpu.VMEM((8, LANES), F32)],
        compiler_params=_params(("arbitrary",)),
    )(merged, x, wo, nw, wr, br, trs)


def _gather_kernel(tok_ref, nused_ref, src_hbm, o_ref, sem, *, rows):
    b = pl.program_id(0)

    @pl.when(b < nused_ref[0])
    def _():
        def issue(r, carry):
            tok = tok_ref[b * rows + r]
            pltpu.make_async_copy(src_hbm.at[tok], o_ref.at[r], sem).start()
            return carry

        lax.fori_loop(0, rows, issue, 0, unroll=8)
        pltpu.make_async_copy(src_hbm.at[pl.ds(0, rows)], o_ref, sem).wait()

    @pl.when(b >= nused_ref[0])
    def _():
        o_ref[...] = jnp.zeros_like(o_ref)


def _gather_call(row_tok, nused, src, n_rows, rows):
    d = src.shape[1]
    return pl.pallas_call(
        functools.partial(_gather_kernel, rows=rows),
        name="k_gather",
        grid_spec=pltpu.PrefetchScalarGridSpec(
            num_scalar_prefetch=2,
            grid=(n_rows // rows,),
            in_specs=[pl.BlockSpec(memory_space=pl.ANY)],
            out_specs=pl.BlockSpec((rows, d), lambda b, tok, nu: (b, 0)),
            scratch_shapes=[pltpu.SemaphoreType.DMA]),
        out_shape=jax.ShapeDtypeStruct((n_rows, d), src.dtype),
        compiler_params=_params(("arbitrary",)),
    )(row_tok, nused, src)


def _moe_kernel(be_ref, nused_ref, x_ref, wg_ref, wl_ref, bg_ref, bl_ref, wd_ref, bd_ref,
                o_ref, xb, acc):
    b = pl.program_id(0)
    j = pl.program_id(1)
    nj = pl.num_programs(1)
    live = b < nused_ref[0]

    @pl.when(jnp.logical_and(live, j == 0))
    def _():
        xb[...] = x_ref[...].astype(BF16)
        acc[...] = jnp.zeros_like(acc)

    @pl.when(live)
    def _():
        x = xb[...]
        glu = jnp.dot(x, wg_ref[...], preferred_element_type=F32) + bg_ref[...]
        lin = jnp.dot(x, wl_ref[...], preferred_element_type=F32) + bl_ref[...]
        glu = jnp.minimum(glu, SWIGLU_LIMIT)
        lin = jnp.clip(lin, -SWIGLU_LIMIT, SWIGLU_LIMIT)
        act = glu * jax.nn.sigmoid(SWIGLU_ALPHA * glu) * (lin + 1.0)
        acc[...] += jnp.dot(act.astype(BF16), wd_ref[...], preferred_element_type=F32)

    @pl.when(jnp.logical_and(live, j == nj - 1))
    def _():
        o_ref[...] = acc[...] + bd_ref[...]

    @pl.when(jnp.logical_and(jnp.logical_not(live), j == nj - 1))
    def _():
        o_ref[...] = jnp.zeros_like(o_ref)


def _moe_call(block_e, nused, xs, w_up, b_up, w_down, b_down):
    n_rows, d = xs.shape
    d_ff = w_down.shape[1]
    tf = COL_BLOCK
    nj = d_ff // tf
    nb = n_rows // MOE_ROWS
    return pl.pallas_call(
        _moe_kernel,
        name="k_moe",
        grid_spec=pltpu.PrefetchScalarGridSpec(
            num_scalar_prefetch=2,
            grid=(nb, nj),
            in_specs=[pl.BlockSpec((MOE_ROWS, d), lambda b, j, be, nu: (b, 0)),
                      pl.BlockSpec((None, d, tf), lambda b, j, be, nu: (be[b], 0, j)),
                      pl.BlockSpec((None, d, tf), lambda b, j, be, nu: (be[b], 0, nj + j)),
                      pl.BlockSpec((None, 1, tf), lambda b, j, be, nu: (be[b], 0, j)),
                      pl.BlockSpec((None, 1, tf), lambda b, j, be, nu: (be[b], 0, nj + j)),
                      pl.BlockSpec((None, tf, d), lambda b, j, be, nu: (be[b], j, 0)),
                      pl.BlockSpec((None, 1, d), lambda b, j, be, nu: (be[b], 0, 0))],
            out_specs=pl.BlockSpec((MOE_ROWS, d), lambda b, j, be, nu: (b, 0)),
            scratch_shapes=[pltpu.VMEM((MOE_ROWS, d), BF16), pltpu.VMEM((MOE_ROWS, d), F32)]),
        out_shape=jax.ShapeDtypeStruct((n_rows, d), F32),
        compiler_params=_params(("arbitrary", "arbitrary")),
    )(block_e, nused, xs, w_up, w_up, b_up, b_up, w_down, b_down)


def _combine_kernel(dest_ref, h_ref, gate_ref, nw_ref, eo_hbm, y_ref, gbuf, sem, *, rows):
    i = pl.program_id(0)

    def issue(t, carry):
        for k in range(TOP_K):
            d = dest_ref[(i * rows + t) * TOP_K + k]
            pltpu.make_async_copy(eo_hbm.at[d], gbuf.at[k, t], sem).start()
        return carry

    lax.fori_loop(0, rows, issue, 0, unroll=4)
    for k in range(TOP_K):
        pltpu.make_async_copy(eo_hbm.at[pl.ds(0, rows)], gbuf.at[k], sem).wait()
    gate = gate_ref[...]
    y = h_ref[...]
    for k in range(TOP_K):
        y = y + gate[:, k:k + 1] * gbuf[k]
    r = lax.rsqrt(jnp.mean(y * y, axis=-1, keepdims=True) + EPS)
    y_ref[...] = y * r * nw_ref[...]


def _combine_call(dest, h, gate, nw, eo):
    t, d = h.shape
    rows = _pick_tile(t, (128, 64))
    return pl.pallas_call(
        functools.partial(_combine_kernel, rows=rows),
        name="k_combine",
        grid_spec=pltpu.PrefetchScalarGridSpec(
            num_scalar_prefetch=1,
            grid=(t // rows,),
            in_specs=[pl.BlockSpec((rows, d), lambda i, de: (i, 0)),
                      pl.BlockSpec((rows, LANES), lambda i, de: (i, 0)),
                      pl.BlockSpec((1, d), lambda i, de: (0, 0)),
                      pl.BlockSpec(memory_space=pl.ANY)],
            out_specs=pl.BlockSpec((rows, d), lambda i, de: (i, 0)),
            scratch_shapes=[pltpu.VMEM((TOP_K, rows, d), F32), pltpu.SemaphoreType.DMA]),
        out_shape=jax.ShapeDtypeStruct((t, d), F32),
        compiler_params=_params(("arbitrary",)),
    )(dest, h, gate, nw, eo)


def _group_bc(a):
    lead = a.shape[:-1]
    return a.reshape(lead + (2, N_GROUPS, D_STATE)).swapaxes(-3, -2).reshape(lead + (2 * N_GROUPS * D_STATE,))


def _ungroup_bc(a):
    lead = a.shape[:-1]
    return a.reshape(lead + (N_GROUPS, 2, D_STATE)).swapaxes(-3, -2).reshape(lead + (2 * N_GROUPS * D_STATE,))


def kernel(x_prompt, x_sample, state_ssm, state_conv, state_pool, norm_mix_w, w_in, conv_w, conv_b,
           dt_bias, a_log, d_skip, ssd_norm_w, w_pool, pool_scale, w_proj_ssd, w_proj_pool, w_out,
           norm_ffn_w, w_router, b_router, w_up, b_up, w_down, b_down, norm_final_w):
    batch, seq_len, d_model = x_prompt.shape
    dec_batch, dec_seq, _ = x_sample.shape
    depth, _, n_heads, head_dim, d_state = state_ssm.shape
    assert depth == 1 and batch == 1 and dec_seq == CHUNK and seq_len % CHUNK == 0
    assert head_dim == HEAD_DIM and d_state == D_STATE
    d_inner = n_heads * head_dim
    assert d_inner == N_GROUPS * COL_BLOCK
    d_bc = 2 * N_GROUPS * D_STATE
    d_pool = state_pool.shape[-1]
    assert d_pool == len(POOL_WINDOWS) * COL_BLOCK and state_pool.shape[-2] == POOL_BUF
    n_experts = w_router.shape[-1]
    past_len = PAST_LEN
    n_prompt_chunks = seq_len // CHUNK
    n_seq = batch + dec_batch
    t_prompt = batch * seq_len
    t = t_prompt + dec_batch * dec_seq

    x = jnp.concatenate([x_prompt.reshape(t_prompt, d_model), x_sample.reshape(-1, d_model)], axis=0)

    wi = w_in[0]
    z_end = d_inner
    xbc_end = z_end + d_inner + d_bc
    dt_end = xbc_end + n_heads
    w_all = jnp.concatenate(
        [wi[:, :z_end + d_inner], _group_bc(wi[:, z_end + d_inner:xbc_end]), wi[:, dt_end:]],
        axis=1).astype(BF16)
    w_dt = jnp.pad(wi[:, xbc_end:dt_end], ((0, 0), (0, LANES - n_heads)))
    pad_h = lambda a: jnp.pad(a.reshape(1, n_heads), ((0, 0), (0, LANES - n_heads)))
    cw = conv_w[0]
    cwx, cwbc = cw[:, :d_inner], _group_bc(cw[:, d_inner:])
    cb = conv_b[0].reshape(1, -1)
    cbx, cbbc = cb[:, :d_inner], _group_bc(cb[:, d_inner:])
    dsk = jnp.repeat(d_skip[0], head_dim).reshape(1, d_inner)

    ssm_t = state_ssm[0].reshape(dec_batch, N_GROUPS, n_heads // N_GROUPS, head_dim, d_state)
    ssm_t = ssm_t.transpose(0, 1, 4, 2, 3).reshape(dec_batch, N_GROUPS, d_state, COL_BLOCK)
    ssm0 = jnp.concatenate([jnp.zeros((batch,) + ssm_t.shape[1:], F32), ssm_t], axis=0)
    conv0 = jnp.pad(state_conv[0], ((batch, 0), (8 - (CONV_WIDTH - 1), 0), (0, 0)))
    cx0, cbc0 = conv0[..., :d_inner], _group_bc(conv0[..., d_inner:])
    pool0 = jnp.pad(state_pool[0], ((batch, 0), (1, 0), (0, 0)))

    u, dt_raw = _norm_call(x, norm_mix_w[0].reshape(1, -1), w_dt)
    p = _inproj_call(u, w_all)
    yg, ssm_new = _ssd_call(p, dt_raw, ssm0, cx0, cbc0, pad_h(dt_bias[0]), pad_h(a_log[0]), dsk,
                            cwx, cbx, cwbc, cbbc, _ssd_consts(), n_prompt_chunks)
    yp = _pool_call(p, pool0, w_pool[0].astype(BF16), pool_scale[0].reshape(1, -1),
                    n_prompt_chunks, past_len)
    merged = _merge_call(yg, yp, p, ssd_norm_w[0].reshape(1, -1), w_proj_ssd[0].astype(BF16),
                         w_proj_pool[0].astype(BF16))

    wr = jnp.pad(w_router[0], ((0, 0), (0, LANES - n_experts)))
    br = jnp.pad(b_router[0].reshape(1, -1), ((0, 0), (0, LANES - n_experts)), constant_values=-1e30)
    h, hn, eidx, rank, gate, cnt = _route_call(merged, x, w_out[0].astype(BF16),
                                               norm_ffn_w[0].reshape(1, -1), wr, br)

    counts = cnt[0, :n_experts]
    padded = (counts + MOE_ROWS - 1) // MOE_ROWS * MOE_ROWS
    pad_ends = jnp.cumsum(padded)
    pad_starts = pad_ends - padded
    n_blocks = -(-(t * TOP_K) // MOE_ROWS) + n_experts
    n_rows = n_blocks * MOE_ROWS
    e_flat = eidx[:, :TOP_K].reshape(-1)
    dest = pad_starts[e_flat] + rank[:, :TOP_K].reshape(-1)
    tok_flat = jnp.arange(t * TOP_K, dtype=jnp.int32) // TOP_K
    row_tok = jnp.full((n_rows,), t, jnp.int32).at[dest].set(tok_flat)
    block_start = jnp.arange(n_blocks, dtype=jnp.int32) * MOE_ROWS
    block_e = jnp.minimum(jnp.searchsorted(pad_ends, block_start, side='right'),
                          n_experts - 1).astype(jnp.int32)
    nused = (pad_ends[-1:] // MOE_ROWS).astype(jnp.int32)

    hn_pad = jnp.concatenate([hn, jnp.zeros((8, d_model), F32)], axis=0)
    xs = _gather_call(row_tok, nused, hn_pad, n_rows, MOE_ROWS)
    eo = _moe_call(block_e, nused, xs, w_up[0].astype(BF16), b_up[0].reshape(n_experts, 1, -1),
                   w_down[0].astype(BF16), b_down[0].reshape(n_experts, 1, -1))
    y = _combine_call(dest.astype(jnp.int32), h, gate, norm_final_w.reshape(1, -1), eo)

    y_prompt = y[:t_prompt].reshape(batch, seq_len, d_model)
    y_sample = y[t_prompt:].reshape(dec_batch, dec_seq, d_model)
    ssm_out = ssm_new.reshape(n_seq, N_GROUPS, d_state, n_heads // N_GROUPS, head_dim)
    ssm_out = ssm_out.transpose(0, 1, 3, 4, 2).reshape(n_seq, n_heads, head_dim, d_state)
    seq_end = jnp.concatenate([jnp.full((batch,), t_prompt, jnp.int32),
                               t_prompt + (jnp.arange(dec_batch, dtype=jnp.int32) + 1) * dec_seq])
    rows_c = seq_end[:, None] - (CONV_WIDTH - 1) + jnp.arange(CONV_WIDTH - 1)
    rows_p = seq_end[:, None] - POOL_BUF + jnp.arange(POOL_BUF)
    xg, bcg = 8, 16
    conv_x = p[xg:xg + N_GROUPS][:, rows_c].transpose(1, 2, 0, 3).reshape(n_seq, CONV_WIDTH - 1, d_inner)
    conv_bc = p[bcg:bcg + N_GROUPS // 2][:, rows_c].transpose(1, 2, 0, 3).reshape(n_seq, CONV_WIDTH - 1, d_bc)
    conv_new = jnp.concatenate([conv_x, _ungroup_bc(conv_bc)], axis=-1)
    pool_new = p[20:24][:, rows_p].transpose(1, 2, 0, 3).reshape(n_seq, POOL_BUF, d_pool)
    return (y_prompt, y_sample,
            ssm_out[None, :batch], conv_new[None, :batch], pool_new[None, :batch],
            ssm_out[None, batch:], conv_new[None, batch:], pool_new[None, batch:])
```

```python
import functools

import numpy as np
import jax
import jax.numpy as jnp
from jax import lax
from jax.experimental import pallas as pl
from jax.experimental.pallas import tpu as pltpu

F32 = jnp.float32
BF16 = jnp.bfloat16
HIGHEST = lax.Precision.HIGHEST

CHUNK = 64
HEAD_DIM = 64
N_GROUPS = 8
D_STATE = 128
CONV_WIDTH = 4
POOL_WINDOWS = (2, 4, 8, 16)
POOL_BUF = 15
PAST_LEN = 4096
TOP_K = 4
SWIGLU_ALPHA = 1.702
SWIGLU_LIMIT = 7.0
EPS = 1e-5
LANES = 128
COL_BLOCK = 512
MOE_ROWS = 512
P_Z, P_X, P_BC, P_POOL, P_GATE_SSD, P_GATE_POOL = 0, 8, 16, 20, 24, 28
VMEM_LIMIT = 56 * 1024 * 1024


def _pick_tile(n, candidates):
    for c in candidates:
        if n % c == 0:
            return c
    raise ValueError(f"no tile for {n} in {candidates}")


def _params(sem, vmem=VMEM_LIMIT):
    return pltpu.CompilerParams(dimension_semantics=sem, vmem_limit_bytes=vmem)


def _split3(v):
    p1 = v.astype(BF16)
    r1 = v - p1.astype(F32)
    p2 = r1.astype(BF16)
    p3 = (r1 - p2.astype(F32)).astype(BF16)
    return p1, p2, p3


def _silu(v):
    return v * jax.nn.sigmoid(v)


def _norm_kernel(x_ref, w_ref, wdt_ref, u_ref, dt_ref):
    x = x_ref[...]
    r = lax.rsqrt(jnp.mean(x * x, axis=-1, keepdims=True) + EPS)
    u = x * r * w_ref[...]
    u_ref[...] = u.astype(BF16)
    dt_ref[...] = jnp.dot(u, wdt_ref[...], precision=HIGHEST, preferred_element_type=F32)


def _norm_call(x, w, wdt):
    t, d = x.shape
    tm = _pick_tile(t, (512, 256, 128, 64))
    return pl.pallas_call(
        _norm_kernel,
        name="k_norm",
        grid=(t // tm,),
        in_specs=[pl.BlockSpec((tm, d), lambda i: (i, 0)),
                  pl.BlockSpec((1, d), lambda i: (0, 0)),
                  pl.BlockSpec((d, LANES), lambda i: (0, 0))],
        out_specs=[pl.BlockSpec((tm, d), lambda i: (i, 0)),
                   pl.BlockSpec((tm, LANES), lambda i: (i, 0))],
        out_shape=[jax.ShapeDtypeStruct((t, d), BF16), jax.ShapeDtypeStruct((t, LANES), F32)],
        compiler_params=_params(("parallel",)),
    )(x, w, wdt)


def _inproj_kernel(u_ref, w_ref, o_ref):
    o_ref[...] = jnp.dot(u_ref[...], w_ref[...], preferred_element_type=F32)


def _inproj_call(u, w):
    t, d = u.shape
    n = w.shape[1]
    tm = _pick_tile(t, (1024, 512, 256, 128, 64))
    nb = n // COL_BLOCK
    return pl.pallas_call(
        _inproj_kernel,
        name="k_inproj",
        grid=(t // tm, nb),
        in_specs=[pl.BlockSpec((tm, d), lambda i, j: (i, 0)),
                  pl.BlockSpec((d, COL_BLOCK), lambda i, j: (0, j))],
        out_specs=pl.BlockSpec((None, tm, COL_BLOCK), lambda i, j: (j, i, 0)),
        out_shape=jax.ShapeDtypeStruct((nb, t, COL_BLOCK), F32),
        compiler_params=_params(("parallel", "arbitrary")),
    )(u, w)


def _ssd_kernel(z_ref, x_ref, bc_ref, dt_ref, ssm0_ref, cx0_ref, cbc0_ref,
                dtb_ref, alog_ref, dsk_ref, cwx_ref, cbx_ref, cwbc_ref, cbbc_ref,
                tril3_ref, e3_ref, diag_ref, caus_ref, bd_ref,
                yg_ref, ssm_ref, bufx, bufbc, exs, *, n_prompt_chunks):
    c = pl.program_id(0)
    first = jnp.logical_or(c == 0, c >= n_prompt_chunks)

    @pl.when(first)
    def _():
        ssm_ref[...] = ssm0_ref[...]
        bufx[0:8, :] = cx0_ref[...]
        bufbc[0:8, :] = cbc0_ref[...]

    @pl.when(jnp.logical_not(first))
    def _():
        bufx[0:8, :] = bufx[CHUNK:CHUNK + 8, :]
        bufbc[0:8, :] = bufbc[CHUNK:CHUNK + 8, :]

    for g in range(N_GROUPS):
        bufx[8:8 + CHUNK, g * COL_BLOCK:(g + 1) * COL_BLOCK] = x_ref[g]
    for q in range(N_GROUPS // 2):
        bufbc[8:8 + CHUNK, q * COL_BLOCK:(q + 1) * COL_BLOCK] = bc_ref[q]

    dtv = dt_ref[...] + dtb_ref[...]
    dt = jnp.maximum(dtv, 0.0) + jnp.log1p(jnp.exp(-jnp.abs(dtv)))
    d_a = dt * (-jnp.exp(alog_ref[...]))
    p1, p2, p3 = _split3(d_a)
    acum = jnp.dot(tril3_ref[...], jnp.concatenate([p1, p2, p3], axis=0),
                   preferred_element_type=F32)
    q1, q2, q3 = _split3(jnp.concatenate([acum, dt], axis=0))
    exs[...] = jnp.dot(jnp.concatenate([q1, q2, q3], axis=1), e3_ref[...],
                       preferred_element_type=F32)

    caus = caus_ref[...] > 0.0
    bd = bd_ref[...]
    hw = 4 * HEAD_DIM
    for g in range(N_GROUPS):
        sl = slice(g * COL_BLOCK, (g + 1) * COL_BLOCK)
        slb = slice(g * 2 * D_STATE, (g + 1) * 2 * D_STATE)
        xc = cbx_ref[:, sl]
        bcc = cbbc_ref[:, slb]
        for k in range(CONV_WIDTH):
            xc = xc + cwx_ref[k:k + 1, sl] * bufx[5 + k:5 + k + CHUNK, sl]
            bcc = bcc + cwbc_ref[k:k + 1, slb] * bufbc[5 + k:5 + k + CHUNK, slb]
        xs = _silu(xc)
        bca = _silu(bcc)
        b_g = bca[:, :D_STATE]
        c_g = bca[:, D_STATE:]
        acx = exs[0:CHUNK, sl]
        dtx = exs[CHUNK:2 * CHUNK, sl]
        alast = acx[CHUNK - 1:CHUNK, :]
        arow = jnp.sum(acx * diag_ref[:, sl], axis=0, keepdims=True)
        xdt = xs * dtx
        xdtb = xdt.astype(BF16)
        bb = b_g.astype(BF16)
        cb = c_g.astype(BF16)
        cb2 = lax.dot_general(cb, jnp.concatenate([bb, bb], axis=0),
                              (((1,), (1,)), ((), ())), preferred_element_type=F32)
        ydiag = []
        for q in range(2):
            lhs = []
            for d in range(2):
                lo = q * hw + d * LANES
                seg = acx[:, lo:lo + LANES] - arow[:, lo:lo + LANES]
                lhs.append((cb2 * jnp.exp(jnp.where(caus, seg, -jnp.inf))).astype(BF16))
            xq = xdtb[:, q * hw:(q + 1) * hw]
            wq = jnp.concatenate([xq, xq, xq, xq], axis=0) * bd
            ydiag.append(jnp.dot(jnp.concatenate(lhs, axis=1), wq, preferred_element_type=F32))
        s_old = ssm_ref[g]
        yoff = jnp.dot(cb, s_old.astype(BF16), preferred_element_type=F32) * jnp.exp(acx)
        y = jnp.concatenate(ydiag, axis=1) + yoff + dsk_ref[:, sl] * xs
        yg_ref[:, sl] = y * _silu(z_ref[g])
        v = (xdt * jnp.exp(alast - acx)).astype(BF16)
        ssm_ref[g] = jnp.exp(alast) * s_old + jnp.dot(b_g.T.astype(BF16), v,
                                                      preferred_element_type=F32)


def _ssd_call(p, dt_raw, ssm0, cx0, cbc0, dtb, alog, dsk, cwx, cbx, cwbc, cbbc, consts,
              n_prompt_chunks):
    _, t, _ = p.shape
    n_chunks = t // CHUNK
    n_seq = ssm0.shape[0]
    d_inner = N_GROUPS * COL_BLOCK
    d_bc = N_GROUPS * 2 * D_STATE
    tril3, e3, diag, caus, bd = consts

    def seq(c):
        return jnp.maximum(c - (n_prompt_chunks - 1), 0)

    def const2(a):
        return pl.BlockSpec(a.shape, lambda c: (0, 0))

    return pl.pallas_call(
        functools.partial(_ssd_kernel, n_prompt_chunks=n_prompt_chunks),
        name="k_ssd",
        grid=(n_chunks,),
        in_specs=[pl.BlockSpec((N_GROUPS, CHUNK, COL_BLOCK), lambda c: (P_Z // N_GROUPS, c, 0)),
                  pl.BlockSpec((N_GROUPS, CHUNK, COL_BLOCK), lambda c: (P_X // N_GROUPS, c, 0)),
                  pl.BlockSpec((N_GROUPS // 2, CHUNK, COL_BLOCK),
                               lambda c: (P_BC // (N_GROUPS // 2), c, 0)),
                  pl.BlockSpec((CHUNK, LANES), lambda c: (c, 0)),
                  pl.BlockSpec((None, N_GROUPS, D_STATE, COL_BLOCK), lambda c: (seq(c), 0, 0, 0)),
                  pl.BlockSpec((None, 8, d_inner), lambda c: (seq(c), 0, 0)),
                  pl.BlockSpec((None, 8, d_bc), lambda c: (seq(c), 0, 0)),
                  const2(dtb), const2(alog), const2(dsk), const2(cwx), const2(cbx),
                  const2(cwbc), const2(cbbc),
                  const2(tril3), const2(e3), const2(diag), const2(caus), const2(bd)],
        out_specs=[pl.BlockSpec((CHUNK, d_inner), lambda c: (c, 0)),
                   pl.BlockSpec((None, N_GROUPS, D_STATE, COL_BLOCK), lambda c: (seq(c), 0, 0, 0))],
        out_shape=[jax.ShapeDtypeStruct((t, d_inner), F32),
                   jax.ShapeDtypeStruct((n_seq, N_GROUPS, D_STATE, COL_BLOCK), F32)],
        scratch_shapes=[pltpu.VMEM((CHUNK + 8, d_inner), F32),
                        pltpu.VMEM((CHUNK + 8, d_bc), F32),
                        pltpu.VMEM((2 * CHUNK, d_inner), F32)],
        compiler_params=_params(("arbitrary",)),
    )(p, p, p, dt_raw, ssm0, cx0, cbc0, dtb, alog, dsk, cwx, cbx, cwbc, cbbc,
      tril3, e3, diag, caus, bd)


def _ssd_consts():
    l = np.arange(CHUNK)
    tril = (l[:, None] >= l[None, :]).astype(np.float32)
    tril3 = np.concatenate([tril, tril, tril], axis=1)
    n_heads = N_GROUPS * COL_BLOCK // HEAD_DIM
    col_head = np.arange(n_heads * HEAD_DIM) // HEAD_DIM
    col_pos = np.arange(n_heads * HEAD_DIM) % HEAD_DIM
    e = (np.arange(LANES)[:, None] == col_head[None, :]).astype(np.float32)
    e3 = np.concatenate([e, e, e], axis=0)
    diag = (l[:, None] == col_pos[None, :]).astype(np.float32)
    caus = np.concatenate([tril, tril], axis=1)
    r = np.arange(4 * HEAD_DIM)
    bd = (r[:, None] // HEAD_DIM == r[None, :] // HEAD_DIM).astype(np.float32)
    return (jnp.asarray(tril3, BF16), jnp.asarray(e3, BF16), jnp.asarray(diag, F32),
            jnp.asarray(caus, F32), jnp.asarray(bd, BF16))


def _pool_kernel(pu_ref, pool0_ref, wp_ref, scale_ref, yp_ref, pbuf, *, n_prompt_chunks, past_len):
    c = pl.program_id(0)
    first = jnp.logical_or(c == 0, c >= n_prompt_chunks)
    hist = POOL_BUF + 1

    @pl.when(first)
    def _():
        pbuf[0:hist, :] = pool0_ref[...]

    @pl.when(jnp.logical_not(first))
    def _():
        pbuf[0:hist, :] = pbuf[CHUNK:CHUNK + hist, :]

    for g in range(len(POOL_WINDOWS)):
        pbuf[hist:hist + CHUNK, g * COL_BLOCK:(g + 1) * COL_BLOCK] = pu_ref[g]

    pos0 = jnp.where(c < n_prompt_chunks, c * CHUNK, past_len)
    pos = (pos0 + lax.broadcasted_iota(jnp.int32, (CHUNK, 1), 0)).astype(F32)
    for g, win in enumerate(POOL_WINDOWS):
        sl = slice(g * COL_BLOCK, (g + 1) * COL_BLOCK)
        cur = pbuf[hist:hist + CHUNK, sl]
        tot = cur
        for i in range(1, win):
            tot = tot + pbuf[hist - i:hist - i + CHUNK, sl]
        count = jnp.minimum(pos + 1.0, float(win))
        pooled = tot / count - cur
        yp_ref[:, sl] = jnp.dot(pooled.astype(BF16), wp_ref[g],
                                preferred_element_type=F32) * scale_ref[:, sl]


def _pool_call(p, pool0, wp, scale, n_prompt_chunks, past_len):
    _, t, _ = p.shape
    ng = len(POOL_WINDOWS)
    d_pool = ng * COL_BLOCK
    hist = POOL_BUF + 1

    def seq(c):
        return jnp.maximum(c - (n_prompt_chunks - 1), 0)

    return pl.pallas_call(
        functools.partial(_pool_kernel, n_prompt_chunks=n_prompt_chunks, past_len=past_len),
        name="k_pool",
        grid=(t // CHUNK,),
        in_specs=[pl.BlockSpec((ng, CHUNK, COL_BLOCK), lambda c: (P_POOL // ng, c, 0)),
                  pl.BlockSpec((None, hist, d_pool), lambda c: (seq(c), 0, 0)),
                  pl.BlockSpec(wp.shape, lambda c: (0, 0, 0)),
                  pl.BlockSpec((1, d_pool), lambda c: (0, 0))],
        out_specs=pl.BlockSpec((CHUNK, d_pool), lambda c: (c, 0)),
        out_shape=jax.ShapeDtypeStruct((t, d_pool), F32),
        scratch_shapes=[pltpu.VMEM((CHUNK + hist, d_pool), F32)],
        compiler_params=_params(("arbitrary",)),
    )(p, pool0, wp, scale)


def _merge_kernel(yg_ref, yp_ref, gs_ref, gp_ref, nw_ref, wps_ref, wpp_ref, o_ref):
    y = yg_ref[...]
    r = lax.rsqrt(jnp.mean(y * y, axis=-1, keepdims=True) + EPS)
    yn = (y * r * nw_ref[...]).astype(BF16)
    a = jnp.dot(yn, wps_ref[...], preferred_element_type=F32)
    b = jnp.dot(yp_ref[...].astype(BF16), wpp_ref[...], preferred_element_type=F32)
    for j in range(gs_ref.shape[0]):
        sl = slice(j * COL_BLOCK, (j + 1) * COL_BLOCK)
        o_ref[:, sl] = (jax.nn.sigmoid(gs_ref[j]) * a[:, sl]
                        + jax.nn.sigmoid(gp_ref[j]) * b[:, sl]).astype(BF16)


def _merge_call(yg, yp, p, nw, wps, wpp):
    t, d_inner = yg.shape
    d_pool = yp.shape[1]
    d_model = wps.shape[1]
    tm = _pick_tile(t, (256, 128, 64))
    nj = d_model // COL_BLOCK
    once = pl.Buffered(1)
    return pl.pallas_call(
        _merge_kernel,
        name="k_merge",
        grid=(t // tm,),
        in_specs=[pl.BlockSpec((tm, d_inner), lambda i: (i, 0)),
                  pl.BlockSpec((tm, d_pool), lambda i: (i, 0)),
                  pl.BlockSpec((nj, tm, COL_BLOCK), lambda i: (P_GATE_SSD // nj, i, 0)),
                  pl.BlockSpec((nj, tm, COL_BLOCK), lambda i: (P_GATE_POOL // nj, i, 0)),
                  pl.BlockSpec((1, d_inner), lambda i: (0, 0)),
                  pl.BlockSpec((d_inner, d_model), lambda i: (0, 0), pipeline_mode=once),
                  pl.BlockSpec((d_pool, d_model), lambda i: (0, 0), pipeline_mode=once)],
        out_specs=pl.BlockSpec((tm, d_model), lambda i: (i, 0)),
        out_shape=jax.ShapeDtypeStruct((t, d_model), BF16),
        compiler_params=_params(("parallel",)),
    )(yg, yp, p, p, nw, wps, wpp)


def _route_kernel(m_ref, x_ref, wo_ref, nw_ref, wr1_ref, wr2_ref, br_ref, trs_ref,
                  h_ref, hn_ref, eidx_ref, rank_ref, gate_ref, cnt_ref, carry):
    i = pl.program_id(0)

    @pl.when(i == 0)
    def _():
        carry[...] = jnp.zeros_like(carry)

    h = x_ref[...] + jnp.dot(m_ref[...], wo_ref[...], preferred_element_type=F32)
    h_ref[...] = h
    r = lax.rsqrt(jnp.mean(h * h, axis=-1, keepdims=True) + EPS)
    hn = h * r * nw_ref[...]
    hn_ref[...] = hn
    h1 = hn.astype(BF16)
    h2 = (hn - h1.astype(F32)).astype(BF16)
    logits = (jnp.dot(h1, wr1_ref[...], preferred_element_type=F32)
              + jnp.dot(h1, wr2_ref[...], preferred_element_type=F32)
              + jnp.dot(h2, wr1_ref[...], preferred_element_type=F32)) + br_ref[...]
    lane = lax.broadcasted_iota(jnp.int32, logits.shape, 1)
    work = logits
    member = jnp.zeros(logits.shape, F32)
    vals, idxs = [], []
    for _ in range(TOP_K):
        m = jnp.max(work, axis=-1, keepdims=True)
        idx = jnp.min(jnp.where(work == m, lane, LANES), axis=-1, keepdims=True)
        hit = lane == idx
        member = member + hit.astype(F32)
        work = jnp.where(hit, -jnp.inf, work)
        vals.append(m)
        idxs.append(idx)
    ex = [jnp.exp(v - vals[0]) for v in vals]
    den = ex[0] + ex[1] + ex[2] + ex[3]
    before = jnp.dot(trs_ref[...], member.astype(BF16), preferred_element_type=F32) + carry[0:1, :]
    eidx = jnp.zeros(logits.shape, jnp.int32)
    rank = jnp.zeros(logits.shape, jnp.int32)
    gate = jnp.zeros(logits.shape, F32)
    for k in range(TOP_K):
        rk = jnp.sum(jnp.where(lane == idxs[k], before, 0.0), axis=-1, keepdims=True)
        eidx = jnp.where(lane == k, idxs[k], eidx)
        rank = jnp.where(lane == k, rk.astype(jnp.int32), rank)
        gate = jnp.where(lane == k, ex[k] / den, gate)
    eidx_ref[...] = eidx
    rank_ref[...] = rank
    gate_ref[...] = gate
    carry[0:1, :] = carry[0:1, :] + jnp.sum(member, axis=0, keepdims=True)
    cnt_ref[...] = carry[...].astype(jnp.int32)


def _route_call(merged, x, wo, nw, wr1, wr2, br):
    t, d = x.shape
    tm = _pick_tile(t, (512, 256, 128, 64))
    ri = np.arange(tm)
    trs = jnp.asarray((ri[:, None] > ri[None, :]).astype(np.float32), BF16)
    row = lambda i: (i, 0)
    fix = lambda i: (0, 0)
    return pl.pallas_call(
        _route_kernel,
        name="k_route",
        grid=(t // tm,),
        in_specs=[pl.BlockSpec((tm, d), row), pl.BlockSpec((tm, d), row),
                  pl.BlockSpec((d, d), fix), pl.BlockSpec((1, d), fix),
                  pl.BlockSpec((d, LANES), fix), pl.BlockSpec((d, LANES), fix),
                  pl.BlockSpec((1, LANES), fix), pl.BlockSpec((tm, tm), fix)],
        out_specs=[pl.BlockSpec((tm, d), row), pl.BlockSpec((tm, d), row),
                   pl.BlockSpec((tm, LANES), row), pl.BlockSpec((tm, LANES), row),
                   pl.BlockSpec((tm, LANES), row), pl.BlockSpec((8, LANES), fix)],
        out_shape=[jax.ShapeDtypeStruct((t, d), F32), jax.ShapeDtypeStruct((t, d), F32),
                   jax.ShapeDtypeStruct((t, LANES), jnp.int32),
                   jax.ShapeDtypeStruct((t, LANES), jnp.int32),
                   jax.ShapeDtypeStruct((t, LANES), F32),
                   jax.ShapeDtypeStruct((8, LANES), jnp.int32)],
        scratch_shapes=[pltpu.VMEM((8, LANES), F32)],
        compiler_params=_params(("arbitrary",)),
    )(merged, x, wo, nw, wr1, wr2, br, trs)


_PAD_CHUNKS = tuple(1 << s for s in range(MOE_ROWS.bit_length() - 2, 2, -1))
_PAD_SINGLE = 7


def _dispatch_kernel(dest_ref, cnt_ref, pstart_ref, x_ref, o_hbm, zbuf, sem, zsem, *, rows, n_experts):
    i = pl.program_id(0)

    def issue(t, carry):
        for k in range(TOP_K):
            d = dest_ref[(i * rows + t) * TOP_K + k]
            pltpu.make_async_copy(x_ref.at[t], o_hbm.at[d], sem).start()
        return carry

    lax.fori_loop(0, rows, issue, 0, unroll=4)
    for k in range(TOP_K):
        pltpu.make_async_copy(x_ref, o_hbm.at[pl.ds(0, rows)], sem).wait()

    @pl.when(i == pl.num_programs(0) - 1)
    def _():
        zbuf[...] = jnp.zeros_like(zbuf)

        def pad_copies(e, wait):
            cnt = cnt_ref[e]
            start = pstart_ref[e] + cnt
            npad = (-cnt) & (MOE_ROWS - 1)
            head = npad & _PAD_SINGLE
            for r in range(_PAD_SINGLE):
                cp = pltpu.make_async_copy(zbuf.at[0], o_hbm.at[start + r], zsem)
                pl.when(r < head)(cp.wait if wait else cp.start)
            off = start + head
            body = npad - head
            for rows_c in _PAD_CHUNKS:
                cp = pltpu.make_async_copy(zbuf.at[pl.ds(0, rows_c)],
                                           o_hbm.at[pl.ds(pl.multiple_of(off, 8), rows_c)], zsem)
                pl.when((body & rows_c) != 0)(cp.wait if wait else cp.start)
                off = off + (body & rows_c)

        def start_e(e, carry):
            pad_copies(e, False)
            return carry

        def wait_e(e, carry):
            pad_copies(e, True)
            return carry

        lax.fori_loop(0, n_experts, start_e, 0)
        lax.fori_loop(0, n_experts, wait_e, 0)

        zrows = zbuf.shape[0]
        first = (pstart_ref[n_experts - 1] + cnt_ref[n_experts - 1] + MOE_ROWS - 1) // MOE_ROWS
        first = first * (MOE_ROWS // zrows)

        def tail_copy(c):
            return pltpu.make_async_copy(
                zbuf, o_hbm.at[pl.ds(pl.multiple_of(c * zrows, zrows), zrows)], zsem)

        def start_t(c, carry):
            tail_copy(c).start()
            return carry

        def wait_t(c, carry):
            tail_copy(c).wait()
            return carry

        lax.fori_loop(first, o_hbm.shape[0] // zrows, start_t, 0)
        lax.fori_loop(first, o_hbm.shape[0] // zrows, wait_t, 0)


def _dispatch_call(dest, counts, pad_starts, src, n_rows):
    t, d = src.shape
    rows = _pick_tile(t, (256, 128, 64))
    return pl.pallas_call(
        functools.partial(_dispatch_kernel, rows=rows, n_experts=counts.shape[0]),
        name="k_dispatch",
        grid_spec=pltpu.PrefetchScalarGridSpec(
            num_scalar_prefetch=3,
            grid=(t // rows,),
            in_specs=[pl.BlockSpec((rows, d), lambda i, de, cn, ps: (i, 0))],
            out_specs=pl.BlockSpec(memory_space=pl.ANY),
            scratch_shapes=[pltpu.VMEM((_PAD_CHUNKS[0], d), src.dtype),
                            pltpu.SemaphoreType.DMA, pltpu.SemaphoreType.DMA]),
        out_shape=jax.ShapeDtypeStruct((n_rows, d), src.dtype),
        compiler_params=_params(("arbitrary",)),
    )(dest, counts, pad_starts, src)


def _moe_kernel(be_ref, nused_ref, x_ref, wg_ref, wl_ref, bg_ref, bl_ref, wd_ref, bd_ref,
                o_ref, xb, acc):
    b = pl.program_id(0)
    j = pl.program_id(1)
    nj = pl.num_programs(1)
    live = b < nused_ref[0]

    @pl.when(jnp.logical_and(live, j == 0))
    def _():
        xb[...] = x_ref[...].astype(BF16)
        acc[...] = jnp.zeros_like(acc)

    @pl.when(live)
    def _():
        x = xb[...]
        glu = jnp.dot(x, wg_ref[...], preferred_element_type=F32) + bg_ref[...]
        lin = jnp.dot(x, wl_ref[...], preferred_element_type=F32) + bl_ref[...]
        glu = jnp.minimum(glu, SWIGLU_LIMIT)
        lin = jnp.clip(lin, -SWIGLU_LIMIT, SWIGLU_LIMIT)
        act = glu * jax.nn.sigmoid(SWIGLU_ALPHA * glu) * (lin + 1.0)
        acc[...] += jnp.dot(act.astype(BF16), wd_ref[...], preferred_element_type=F32)

    @pl.when(jnp.logical_and(live, j == nj - 1))
    def _():
        o_ref[...] = acc[...] + bd_ref[...]

    @pl.when(jnp.logical_and(jnp.logical_not(live), j == nj - 1))
    def _():
        o_ref[...] = jnp.zeros_like(o_ref)


def _moe_call(block_e, nused, xs, w_up, b_up, w_down, b_down):
    n_rows, d = xs.shape
    d_ff = w_down.shape[1]
    tf = COL_BLOCK
    nj = d_ff // tf
    nb = n_rows // MOE_ROWS

    def bb(b, nu):
        return jnp.minimum(b, nu[0] - 1)

    def jj(b, j, nu):
        return jnp.where(b < nu[0], j, nj - 1)

    return pl.pallas_call(
        _moe_kernel,
        name="k_moe",
        grid_spec=pltpu.PrefetchScalarGridSpec(
            num_scalar_prefetch=2,
            grid=(nb, nj),
            in_specs=[pl.BlockSpec((MOE_ROWS, d), lambda b, j, be, nu: (bb(b, nu), 0)),
                      pl.BlockSpec((None, d, tf), lambda b, j, be, nu: (be[bb(b, nu)], 0, jj(b, j, nu))),
                      pl.BlockSpec((None, d, tf),
                                   lambda b, j, be, nu: (be[bb(b, nu)], 0, nj + jj(b, j, nu))),
                      pl.BlockSpec((None, 1, tf), lambda b, j, be, nu: (be[bb(b, nu)], 0, jj(b, j, nu))),
                      pl.BlockSpec((None, 1, tf),
                                   lambda b, j, be, nu: (be[bb(b, nu)], 0, nj + jj(b, j, nu))),
                      pl.BlockSpec((None, tf, d), lambda b, j, be, nu: (be[bb(b, nu)], jj(b, j, nu), 0)),
                      pl.BlockSpec((None, 1, d), lambda b, j, be, nu: (be[bb(b, nu)], 0, 0))],
            out_specs=pl.BlockSpec((MOE_ROWS, d), lambda b, j, be, nu: (b, 0)),
            scratch_shapes=[pltpu.VMEM((MOE_ROWS, d), BF16), pltpu.VMEM((MOE_ROWS, d), F32)]),
        out_shape=jax.ShapeDtypeStruct((n_rows, d), F32),
        compiler_params=_params(("arbitrary", "arbitrary")),
    )(block_e, nused, xs, w_up, w_up, b_up, b_up, w_down, b_down)


def _combine_kernel(dest_ref, h_ref, gate_ref, nw_ref, eo_hbm, y_ref, gbuf, sem, *, rows):
    i = pl.program_id(0)

    def issue(t, carry):
        for k in range(TOP_K):
            d = dest_ref[(i * rows + t) * TOP_K + k]
            pltpu.make_async_copy(eo_hbm.at[d], gbuf.at[k, t], sem).start()
        return carry

    lax.fori_loop(0, rows, issue, 0, unroll=4)
    for k in range(TOP_K):
        pltpu.make_async_copy(eo_hbm.at[pl.ds(0, rows)], gbuf.at[k], sem).wait()
    gate = gate_ref[...]
    y = h_ref[...]
    for k in range(TOP_K):
        y = y + gate[:, k:k + 1] * gbuf[k]
    r = lax.rsqrt(jnp.mean(y * y, axis=-1, keepdims=True) + EPS)
    y_ref[...] = y * r * nw_ref[...]


def _combine_call(dest, h, gate, nw, eo):
    t, d = h.shape
    rows = _pick_tile(t, (128, 64))
    return pl.pallas_call(
        functools.partial(_combine_kernel, rows=rows),
        name="k_combine",
        grid_spec=pltpu.PrefetchScalarGridSpec(
            num_scalar_prefetch=1,
            grid=(t // rows,),
            in_specs=[pl.BlockSpec((rows, d), lambda i, de: (i, 0)),
                      pl.BlockSpec((rows, LANES), lambda i, de: (i, 0)),
                      pl.BlockSpec((1, d), lambda i, de: (0, 0)),
                      pl.BlockSpec(memory_space=pl.ANY)],
            out_specs=pl.BlockSpec((rows, d), lambda i, de: (i, 0)),
            scratch_shapes=[pltpu.VMEM((TOP_K, rows, d), F32), pltpu.SemaphoreType.DMA]),
        out_shape=jax.ShapeDtypeStruct((t, d), F32),
        compiler_params=_params(("arbitrary",)),
    )(dest, h, gate, nw, eo)


def _group_bc(a):
    lead = a.shape[:-1]
    return a.reshape(lead + (2, N_GROUPS, D_STATE)).swapaxes(-3, -2).reshape(lead + (2 * N_GROUPS * D_STATE,))


def _ungroup_bc(a):
    lead = a.shape[:-1]
    return a.reshape(lead + (N_GROUPS, 2, D_STATE)).swapaxes(-3, -2).reshape(lead + (2 * N_GROUPS * D_STATE,))


def kernel(x_prompt, x_sample, state_ssm, state_conv, state_pool, norm_mix_w, w_in, conv_w, conv_b,
           dt_bias, a_log, d_skip, ssd_norm_w, w_pool, pool_scale, w_proj_ssd, w_proj_pool, w_out,
           norm_ffn_w, w_router, b_router, w_up, b_up, w_down, b_down, norm_final_w):
    batch, seq_len, d_model = x_prompt.shape
    dec_batch, dec_seq, _ = x_sample.shape
    depth, _, n_heads, head_dim, d_state = state_ssm.shape
    assert depth == 1 and batch == 1 and dec_seq == CHUNK and seq_len % CHUNK == 0
    assert head_dim == HEAD_DIM and d_state == D_STATE
    d_inner = n_heads * head_dim
    assert d_inner == N_GROUPS * COL_BLOCK
    d_bc = 2 * N_GROUPS * D_STATE
    d_pool = state_pool.shape[-1]
    assert d_pool == len(POOL_WINDOWS) * COL_BLOCK and state_pool.shape[-2] == POOL_BUF
    n_experts = w_router.shape[-1]
    past_len = PAST_LEN
    n_prompt_chunks = seq_len // CHUNK
    n_seq = batch + dec_batch
    t_prompt = batch * seq_len
    t = t_prompt + dec_batch * dec_seq

    x = jnp.concatenate([x_prompt.reshape(t_prompt, d_model), x_sample.reshape(-1, d_model)], axis=0)

    wi = w_in[0]
    z_end = d_inner
    xbc_end = z_end + d_inner + d_bc
    dt_end = xbc_end + n_heads
    w_all = jnp.concatenate(
        [wi[:, :z_end + d_inner], _group_bc(wi[:, z_end + d_inner:xbc_end]), wi[:, dt_end:]],
        axis=1).astype(BF16)
    w_dt = jnp.pad(wi[:, xbc_end:dt_end], ((0, 0), (0, LANES - n_heads)))
    pad_h = lambda a: jnp.pad(a.reshape(1, n_heads), ((0, 0), (0, LANES - n_heads)))
    cw = conv_w[0]
    cwx, cwbc = cw[:, :d_inner], _group_bc(cw[:, d_inner:])
    cb = conv_b[0].reshape(1, -1)
    cbx, cbbc = cb[:, :d_inner], _group_bc(cb[:, d_inner:])
    dsk = jnp.repeat(d_skip[0], head_dim).reshape(1, d_inner)

    ssm_t = state_ssm[0].reshape(dec_batch, N_GROUPS, n_heads // N_GROUPS, head_dim, d_state)
    ssm_t = ssm_t.transpose(0, 1, 4, 2, 3).reshape(dec_batch, N_GROUPS, d_state, COL_BLOCK)
    ssm0 = jnp.concatenate([jnp.zeros((batch,) + ssm_t.shape[1:], F32), ssm_t], axis=0)
    conv0 = jnp.pad(state_conv[0], ((batch, 0), (8 - (CONV_WIDTH - 1), 0), (0, 0)))
    cx0, cbc0 = conv0[..., :d_inner], _group_bc(conv0[..., d_inner:])
    pool0 = jnp.pad(state_pool[0], ((batch, 0), (1, 0), (0, 0)))

    u, dt_raw = _norm_call(x, norm_mix_w[0].reshape(1, -1), w_dt)
    p = _inproj_call(u, w_all)
    yg, ssm_new = _ssd_call(p, dt_raw, ssm0, cx0, cbc0, pad_h(dt_bias[0]), pad_h(a_log[0]), dsk,
                            cwx, cbx, cwbc, cbbc, _ssd_consts(), n_prompt_chunks)
    yp = _pool_call(p, pool0, w_pool[0].astype(BF16), pool_scale[0].reshape(1, -1),
                    n_prompt_chunks, past_len)
    merged = _merge_call(yg, yp, p, ssd_norm_w[0].reshape(1, -1), w_proj_ssd[0].astype(BF16),
                         w_proj_pool[0].astype(BF16))

    wr = jnp.pad(w_router[0], ((0, 0), (0, LANES - n_experts)))
    wr1 = wr.astype(BF16)
    wr2 = (wr - wr1.astype(F32)).astype(BF16)
    br = jnp.pad(b_router[0].reshape(1, -1), ((0, 0), (0, LANES - n_experts)), constant_values=-1e30)
    h, hn, eidx, rank, gate, cnt = _route_call(merged, x, w_out[0].astype(BF16),
                                               norm_ffn_w[0].reshape(1, -1), wr1, wr2, br)

    counts = cnt[0, :n_experts]
    padded = (counts + MOE_ROWS - 1) // MOE_ROWS * MOE_ROWS
    pad_ends = jnp.cumsum(padded)
    pad_starts = pad_ends - padded
    n_blocks = -(-(t * TOP_K) // MOE_ROWS) + n_experts
    n_rows = n_blocks * MOE_ROWS
    e_flat = eidx[:, :TOP_K].reshape(-1)
    dest = (pad_starts[e_flat] + rank[:, :TOP_K].reshape(-1)).astype(jnp.int32)
    block_start = jnp.arange(n_blocks, dtype=jnp.int32) * MOE_ROWS
    block_e = jnp.minimum(jnp.sum(block_start[:, None] >= pad_ends[None, :], axis=1),
                          n_experts - 1).astype(jnp.int32)
    nused = (pad_ends[-1:] // MOE_ROWS).astype(jnp.int32)

    xs = _dispatch_call(dest, counts, pad_starts.astype(jnp.int32), hn, n_rows)
    eo = _moe_call(block_e, nused, xs, w_up[0].astype(BF16), b_up[0].reshape(n_experts, 1, -1),
                   w_down[0].astype(BF16), b_down[0].reshape(n_experts, 1, -1))
    y = _combine_call(dest, h, gate, norm_final_w.reshape(1, -1), eo)

    y_prompt = y[:t_prompt].reshape(batch, seq_len, d_model)
    y_sample = y[t_prompt:].reshape(dec_batch, dec_seq, d_model)
    ssm_out = ssm_new.reshape(n_seq, N_GROUPS, d_state, n_heads // N_GROUPS, head_dim)
    ssm_out = ssm_out.transpose(0, 1, 3, 4, 2).reshape(n_seq, n_heads, head_dim, d_state)
    seq_end = jnp.concatenate([jnp.full((batch,), t_prompt, jnp.int32),
                               t_prompt + (jnp.arange(dec_batch, dtype=jnp.int32) + 1) * dec_seq])
    tail = p[:, seq_end[:, None] - POOL_BUF + jnp.arange(POOL_BUF)]
    tail = tail.transpose(1, 2, 0, 3)
    ctail = tail[:, POOL_BUF - (CONV_WIDTH - 1):]
    conv_x = ctail[:, :, P_X:P_BC].reshape(n_seq, CONV_WIDTH - 1, d_inner)
    conv_bc = ctail[:, :, P_BC:P_POOL].reshape(n_seq, CONV_WIDTH - 1, d_bc)
    conv_new = jnp.concatenate([conv_x, _ungroup_bc(conv_bc)], axis=-1)
    pool_new = tail[:, :, P_POOL:P_GATE_SSD].reshape(n_seq, POOL_BUF, d_pool)
    return (y_prompt, y_sample,
            ssm_out[None, :batch], conv_new[None, :batch], pool_new[None, :batch],
            ssm_out[None, batch:], conv_new[None, batch:], pool_new[None, batch:])
```

```python
import functools

import numpy as np
import jax
import jax.numpy as jnp
from jax import lax
from jax.experimental import pallas as pl
from jax.experimental.pallas import tpu as pltpu

F32 = jnp.float32
BF16 = jnp.bfloat16
HIGHEST = lax.Precision.HIGHEST

CHUNK = 64
HEAD_DIM = 64
N_GROUPS = 8
D_STATE = 128
CONV_WIDTH = 4
POOL_WINDOWS = (2, 4, 8, 16)
POOL_BUF = 15
PAST_LEN = 4096
TOP_K = 4
SWIGLU_ALPHA = 1.702
SWIGLU_LIMIT = 7.0
EPS = 1e-5
LANES = 128
COL_BLOCK = 512
MOE_ROWS = 512
P_Z, P_X, P_BC, P_POOL, P_GATE_SSD, P_GATE_POOL = 0, 8, 16, 20, 24, 28
VMEM_LIMIT = 56 * 1024 * 1024


def _pick_tile(n, candidates):
    for c in candidates:
        if n % c == 0:
            return c
    raise ValueError(f"no tile for {n} in {candidates}")


def _params(sem, vmem=VMEM_LIMIT):
    return pltpu.CompilerParams(dimension_semantics=sem, vmem_limit_bytes=vmem)


def _split3(v):
    p1 = v.astype(BF16)
    r1 = v - p1.astype(F32)
    p2 = r1.astype(BF16)
    p3 = (r1 - p2.astype(F32)).astype(BF16)
    return p1, p2, p3


def _silu(v):
    return v * jax.nn.sigmoid(v)


def _norm_kernel(x_ref, w_ref, wdt_ref, u_ref, dt_ref):
    x = x_ref[...]
    r = lax.rsqrt(jnp.mean(x * x, axis=-1, keepdims=True) + EPS)
    u = x * r * w_ref[...]
    u_ref[...] = u.astype(BF16)
    dt_ref[...] = jnp.dot(u, wdt_ref[...], precision=HIGHEST, preferred_element_type=F32)


def _norm_call(x, w, wdt):
    t, d = x.shape
    tm = _pick_tile(t, (512, 256, 128, 64))
    return pl.pallas_call(
        _norm_kernel,
        name="k_norm",
        grid=(t // tm,),
        in_specs=[pl.BlockSpec((tm, d), lambda i: (i, 0)),
                  pl.BlockSpec((1, d), lambda i: (0, 0)),
                  pl.BlockSpec((d, LANES), lambda i: (0, 0))],
        out_specs=[pl.BlockSpec((tm, d), lambda i: (i, 0)),
                   pl.BlockSpec((tm, LANES), lambda i: (i, 0))],
        out_shape=[jax.ShapeDtypeStruct((t, d), BF16), jax.ShapeDtypeStruct((t, LANES), F32)],
        compiler_params=_params(("parallel",)),
    )(x, w, wdt)


def _inproj_kernel(u_ref, w_ref, o_ref):
    r = jnp.dot(u_ref[...], w_ref[...], preferred_element_type=F32)
    for k in range(o_ref.shape[0]):
        o_ref[k] = r[:, k * COL_BLOCK:(k + 1) * COL_BLOCK]


def _inproj_call(u, w):
    t, d = u.shape
    n = w.shape[1]
    tm = _pick_tile(t, (1024, 512, 256, 128, 64))
    nb = n // COL_BLOCK
    per = 2
    return pl.pallas_call(
        _inproj_kernel,
        name="k_inproj",
        grid=(t // tm, nb // per),
        in_specs=[pl.BlockSpec((tm, d), lambda i, j: (i, 0)),
                  pl.BlockSpec((d, per * COL_BLOCK), lambda i, j: (0, j))],
        out_specs=pl.BlockSpec((per, tm, COL_BLOCK), lambda i, j: (j, i, 0)),
        out_shape=jax.ShapeDtypeStruct((nb, t, COL_BLOCK), F32),
        compiler_params=_params(("parallel", "arbitrary")),
    )(u, w)


def _ssd_kernel(z_ref, x_ref, bc_ref, dt_ref, ssm0_ref, cx0_ref, cbc0_ref,
                dtb_ref, alog_ref, dsk_ref, cwx_ref, cbx_ref, cwbc_ref, cbbc_ref,
                tril3_ref, e3_ref, diag_ref, caus_ref, bd_ref,
                yg_ref, ssm_ref, bufx, bufbc, exs, *, n_prompt_chunks):
    c = pl.program_id(0)
    first = jnp.logical_or(c == 0, c >= n_prompt_chunks)

    @pl.when(first)
    def _():
        ssm_ref[...] = ssm0_ref[...]
        bufx[0:8, :] = cx0_ref[...]
        bufbc[0:8, :] = cbc0_ref[...]

    @pl.when(jnp.logical_not(first))
    def _():
        bufx[0:8, :] = bufx[CHUNK:CHUNK + 8, :]
        bufbc[0:8, :] = bufbc[CHUNK:CHUNK + 8, :]

    for g in range(N_GROUPS):
        bufx[8:8 + CHUNK, g * COL_BLOCK:(g + 1) * COL_BLOCK] = x_ref[g]
    for q in range(N_GROUPS // 2):
        bufbc[8:8 + CHUNK, q * COL_BLOCK:(q + 1) * COL_BLOCK] = bc_ref[q]

    dtv = dt_ref[...] + dtb_ref[...]
    dt = jnp.maximum(dtv, 0.0) + jnp.log1p(jnp.exp(-jnp.abs(dtv)))
    d_a = dt * (-jnp.exp(alog_ref[...]))
    p1, p2, p3 = _split3(d_a)
    acum = jnp.dot(tril3_ref[...], jnp.concatenate([p1, p2, p3], axis=0),
                   preferred_element_type=F32)
    q1, q2, q3 = _split3(jnp.concatenate([acum, dt], axis=0))
    exs[...] = jnp.dot(jnp.concatenate([q1, q2, q3], axis=1), e3_ref[...],
                       preferred_element_type=F32)

    caus = caus_ref[...] > 0.0
    bd = bd_ref[...]
    hw = 4 * HEAD_DIM
    for g in range(N_GROUPS):
        sl = slice(g * COL_BLOCK, (g + 1) * COL_BLOCK)
        slb = slice(g * 2 * D_STATE, (g + 1) * 2 * D_STATE)
        xc = cbx_ref[:, sl]
        bcc = cbbc_ref[:, slb]
        for k in range(CONV_WIDTH):
            xc = xc + cwx_ref[k:k + 1, sl] * bufx[5 + k:5 + k + CHUNK, sl]
            bcc = bcc + cwbc_ref[k:k + 1, slb] * bufbc[5 + k:5 + k + CHUNK, slb]
        xs = _silu(xc)
        bca = _silu(bcc)
        b_g = bca[:, :D_STATE]
        c_g = bca[:, D_STATE:]
        acx = exs[0:CHUNK, sl]
        dtx = exs[CHUNK:2 * CHUNK, sl]
        alast = acx[CHUNK - 1:CHUNK, :]
        arow = jnp.sum(acx * diag_ref[:, sl], axis=0, keepdims=True)
        xdt = xs * dtx
        xdtb = xdt.astype(BF16)
        bb = b_g.astype(BF16)
        cb = c_g.astype(BF16)
        cb2 = lax.dot_general(cb, jnp.concatenate([bb, bb], axis=0),
                              (((1,), (1,)), ((), ())), preferred_element_type=F32)
        ydiag = []
        for q in range(2):
            lhs = []
            for d in range(2):
                lo = q * hw + d * LANES
                seg = acx[:, lo:lo + LANES] - arow[:, lo:lo + LANES]
                lhs.append((cb2 * jnp.exp(jnp.where(caus, seg, -jnp.inf))).astype(BF16))
            xq = xdtb[:, q * hw:(q + 1) * hw]
            wq = jnp.concatenate([xq, xq, xq, xq], axis=0) * bd
            ydiag.append(jnp.dot(jnp.concatenate(lhs, axis=1), wq, preferred_element_type=F32))
        s_old = ssm_ref[g]
        yoff = jnp.dot(cb, s_old.astype(BF16), preferred_element_type=F32) * jnp.exp(acx)
        y = jnp.concatenate(ydiag, axis=1) + yoff + dsk_ref[:, sl] * xs
        yg_ref[:, sl] = y * _silu(z_ref[g])
        v = (xdt * jnp.exp(alast - acx)).astype(BF16)
        ssm_ref[g] = jnp.exp(alast) * s_old + jnp.dot(b_g.T.astype(BF16), v,
                                                      preferred_element_type=F32)


def _ssd_call(p, dt_raw, ssm0, cx0, cbc0, dtb, alog, dsk, cwx, cbx, cwbc, cbbc, consts,
              n_prompt_chunks):
    _, t, _ = p.shape
    n_chunks = t // CHUNK
    n_seq = ssm0.shape[0]
    d_inner = N_GROUPS * COL_BLOCK
    d_bc = N_GROUPS * 2 * D_STATE
    tril3, e3, diag, caus, bd = consts

    def seq(c):
        return jnp.maximum(c - (n_prompt_chunks - 1), 0)

    def const2(a):
        return pl.BlockSpec(a.shape, lambda c: (0, 0))

    return pl.pallas_call(
        functools.partial(_ssd_kernel, n_prompt_chunks=n_prompt_chunks),
        name="k_ssd",
        grid=(n_chunks,),
        in_specs=[pl.BlockSpec((N_GROUPS, CHUNK, COL_BLOCK), lambda c: (P_Z // N_GROUPS, c, 0)),
                  pl.BlockSpec((N_GROUPS, CHUNK, COL_BLOCK), lambda c: (P_X // N_GROUPS, c, 0)),
                  pl.BlockSpec((N_GROUPS // 2, CHUNK, COL_BLOCK),
                               lambda c: (P_BC // (N_GROUPS // 2), c, 0)),
                  pl.BlockSpec((CHUNK, LANES), lambda c: (c, 0)),
                  pl.BlockSpec((None, N_GROUPS, D_STATE, COL_BLOCK), lambda c: (seq(c), 0, 0, 0)),
                  pl.BlockSpec((None, 8, d_inner), lambda c: (seq(c), 0, 0)),
                  pl.BlockSpec((None, 8, d_bc), lambda c: (seq(c), 0, 0)),
                  const2(dtb), const2(alog), const2(dsk), const2(cwx), const2(cbx),
                  const2(cwbc), const2(cbbc),
                  const2(tril3), const2(e3), const2(diag), const2(caus), const2(bd)],
        out_specs=[pl.BlockSpec((CHUNK, d_inner), lambda c: (c, 0)),
                   pl.BlockSpec((None, N_GROUPS, D_STATE, COL_BLOCK), lambda c: (seq(c), 0, 0, 0))],
        out_shape=[jax.ShapeDtypeStruct((t, d_inner), F32),
                   jax.ShapeDtypeStruct((n_seq, N_GROUPS, D_STATE, COL_BLOCK), F32)],
        scratch_shapes=[pltpu.VMEM((CHUNK + 8, d_inner), F32),
                        pltpu.VMEM((CHUNK + 8, d_bc), F32),
                        pltpu.VMEM((2 * CHUNK, d_inner), F32)],
        compiler_params=_params(("arbitrary",)),
    )(p, p, p, dt_raw, ssm0, cx0, cbc0, dtb, alog, dsk, cwx, cbx, cwbc, cbbc,
      tril3, e3, diag, caus, bd)


def _ssd_consts():
    l = np.arange(CHUNK)
    tril = (l[:, None] >= l[None, :]).astype(np.float32)
    tril3 = np.concatenate([tril, tril, tril], axis=1)
    n_heads = N_GROUPS * COL_BLOCK // HEAD_DIM
    col_head = np.arange(n_heads * HEAD_DIM) // HEAD_DIM
    col_pos = np.arange(n_heads * HEAD_DIM) % HEAD_DIM
    e = (np.arange(LANES)[:, None] == col_head[None, :]).astype(np.float32)
    e3 = np.concatenate([e, e, e], axis=0)
    diag = (l[:, None] == col_pos[None, :]).astype(np.float32)
    caus = np.concatenate([tril, tril], axis=1)
    r = np.arange(4 * HEAD_DIM)
    bd = (r[:, None] // HEAD_DIM == r[None, :] // HEAD_DIM).astype(np.float32)
    return (jnp.asarray(tril3, BF16), jnp.asarray(e3, BF16), jnp.asarray(diag, F32),
            jnp.asarray(caus, F32), jnp.asarray(bd, BF16))


def _pool_kernel(pu_ref, pool0_ref, wp_ref, scale_ref, yp_ref, pbuf, *, n_prompt_chunks, past_len):
    c = pl.program_id(0)
    first = jnp.logical_or(c == 0, c >= n_prompt_chunks)
    hist = POOL_BUF + 1

    @pl.when(first)
    def _():
        pbuf[0:hist, :] = pool0_ref[...]

    @pl.when(jnp.logical_not(first))
    def _():
        pbuf[0:hist, :] = pbuf[CHUNK:CHUNK + hist, :]

    for g in range(len(POOL_WINDOWS)):
        pbuf[hist:hist + CHUNK, g * COL_BLOCK:(g + 1) * COL_BLOCK] = pu_ref[g]

    pos0 = jnp.where(c < n_prompt_chunks, c * CHUNK, past_len)
    pos = (pos0 + lax.broadcasted_iota(jnp.int32, (CHUNK, 1), 0)).astype(F32)
    for g, win in enumerate(POOL_WINDOWS):
        sl = slice(g * COL_BLOCK, (g + 1) * COL_BLOCK)
        cur = pbuf[hist:hist + CHUNK, sl]
        tot = cur
        for i in range(1, win):
            tot = tot + pbuf[hist - i:hist - i + CHUNK, sl]
        count = jnp.minimum(pos + 1.0, float(win))
        pooled = tot / count - cur
        yp_ref[:, sl] = jnp.dot(pooled.astype(BF16), wp_ref[g],
                                preferred_element_type=F32) * scale_ref[:, sl]


def _pool_call(p, pool0, wp, scale, n_prompt_chunks, past_len):
    _, t, _ = p.shape
    ng = len(POOL_WINDOWS)
    d_pool = ng * COL_BLOCK
    hist = POOL_BUF + 1

    def seq(c):
        return jnp.maximum(c - (n_prompt_chunks - 1), 0)

    return pl.pallas_call(
        functools.partial(_pool_kernel, n_prompt_chunks=n_prompt_chunks, past_len=past_len),
        name="k_pool",
        grid=(t // CHUNK,),
        in_specs=[pl.BlockSpec((ng, CHUNK, COL_BLOCK), lambda c: (P_POOL // ng, c, 0)),
                  pl.BlockSpec((None, hist, d_pool), lambda c: (seq(c), 0, 0)),
                  pl.BlockSpec(wp.shape, lambda c: (0, 0, 0)),
                  pl.BlockSpec((1, d_pool), lambda c: (0, 0))],
        out_specs=pl.BlockSpec((CHUNK, d_pool), lambda c: (c, 0)),
        out_shape=jax.ShapeDtypeStruct((t, d_pool), F32),
        scratch_shapes=[pltpu.VMEM((CHUNK + hist, d_pool), F32)],
        compiler_params=_params(("arbitrary",)),
    )(p, pool0, wp, scale)


def _merge_kernel(yg_ref, yp_ref, gs_ref, gp_ref, nw_ref, wps_ref, wpp_ref, o_ref):
    y = yg_ref[...]
    r = lax.rsqrt(jnp.mean(y * y, axis=-1, keepdims=True) + EPS)
    yn = (y * r * nw_ref[...]).astype(BF16)
    a = jnp.dot(yn, wps_ref[...], preferred_element_type=F32)
    b = jnp.dot(yp_ref[...].astype(BF16), wpp_ref[...], preferred_element_type=F32)
    for j in range(gs_ref.shape[0]):
        sl = slice(j * COL_BLOCK, (j + 1) * COL_BLOCK)
        o_ref[:, sl] = (jax.nn.sigmoid(gs_ref[j]) * a[:, sl]
                        + jax.nn.sigmoid(gp_ref[j]) * b[:, sl]).astype(BF16)


def _merge_call(yg, yp, p, nw, wps, wpp):
    t, d_inner = yg.shape
    d_pool = yp.shape[1]
    d_model = wps.shape[1]
    tm = _pick_tile(t, (256, 128, 64))
    nj = d_model // COL_BLOCK
    once = pl.Buffered(1)
    return pl.pallas_call(
        _merge_kernel,
        name="k_merge",
        grid=(t // tm,),
        in_specs=[pl.BlockSpec((tm, d_inner), lambda i: (i, 0)),
                  pl.BlockSpec((tm, d_pool), lambda i: (i, 0)),
                  pl.BlockSpec((nj, tm, COL_BLOCK), lambda i: (P_GATE_SSD // nj, i, 0)),
                  pl.BlockSpec((nj, tm, COL_BLOCK), lambda i: (P_GATE_POOL // nj, i, 0)),
                  pl.BlockSpec((1, d_inner), lambda i: (0, 0)),
                  pl.BlockSpec((d_inner, d_model), lambda i: (0, 0), pipeline_mode=once),
                  pl.BlockSpec((d_pool, d_model), lambda i: (0, 0), pipeline_mode=once)],
        out_specs=pl.BlockSpec((tm, d_model), lambda i: (i, 0)),
        out_shape=jax.ShapeDtypeStruct((t, d_model), BF16),
        compiler_params=_params(("parallel",)),
    )(yg, yp, p, p, nw, wps, wpp)


def _route_kernel(m_ref, x_ref, wo_ref, nw_ref, wr1_ref, wr2_ref, br_ref, trs_ref,
                  h_ref, hn_ref, eidx_ref, rank_ref, gate_ref, cnt_ref, carry):
    i = pl.program_id(0)

    @pl.when(i == 0)
    def _():
        carry[...] = jnp.zeros_like(carry)

    h = x_ref[...] + jnp.dot(m_ref[...], wo_ref[...], preferred_element_type=F32)
    h_ref[...] = h
    r = lax.rsqrt(jnp.mean(h * h, axis=-1, keepdims=True) + EPS)
    hn = h * r * nw_ref[...]
    hn_ref[...] = hn
    h1 = hn.astype(BF16)
    h2 = (hn - h1.astype(F32)).astype(BF16)
    logits = (jnp.dot(h1, wr1_ref[...], preferred_element_type=F32)
              + jnp.dot(h1, wr2_ref[...], preferred_element_type=F32)
              + jnp.dot(h2, wr1_ref[...], preferred_element_type=F32)) + br_ref[...]
    lane = lax.broadcasted_iota(jnp.int32, logits.shape, 1)
    work = logits
    member = jnp.zeros(logits.shape, F32)
    vals, idxs = [], []
    for _ in range(TOP_K):
        m = jnp.max(work, axis=-1, keepdims=True)
        idx = jnp.min(jnp.where(work == m, lane, LANES), axis=-1, keepdims=True)
        hit = lane == idx
        member = member + hit.astype(F32)
        work = jnp.where(hit, -jnp.inf, work)
        vals.append(m)
        idxs.append(idx)
    ex = [jnp.exp(v - vals[0]) for v in vals]
    den = ex[0] + ex[1] + ex[2] + ex[3]
    before = jnp.dot(trs_ref[...], member.astype(BF16), preferred_element_type=F32) + carry[0:1, :]
    eidx = jnp.zeros(logits.shape, jnp.int32)
    rank = jnp.zeros(logits.shape, jnp.int32)
    gate = jnp.zeros(logits.shape, F32)
    for k in range(TOP_K):
        rk = jnp.sum(jnp.where(lane == idxs[k], before, 0.0), axis=-1, keepdims=True)
        eidx = jnp.where(lane == k, idxs[k], eidx)
        rank = jnp.where(lane == k, rk.astype(jnp.int32), rank)
        gate = jnp.where(lane == k, ex[k] / den, gate)
    eidx_ref[...] = eidx
    rank_ref[...] = rank
    gate_ref[...] = gate
    carry[0:1, :] = carry[0:1, :] + jnp.sum(member, axis=0, keepdims=True)
    cnt_ref[...] = carry[...].astype(jnp.int32)


def _route_call(merged, x, wo, nw, wr1, wr2, br):
    t, d = x.shape
    tm = _pick_tile(t, (512, 256, 128, 64))
    ri = np.arange(tm)
    trs = jnp.asarray((ri[:, None] > ri[None, :]).astype(np.float32), BF16)
    row = lambda i: (i, 0)
    fix = lambda i: (0, 0)
    return pl.pallas_call(
        _route_kernel,
        name="k_route",
        grid=(t // tm,),
        in_specs=[pl.BlockSpec((tm, d), row), pl.BlockSpec((tm, d), row),
                  pl.BlockSpec((d, d), fix), pl.BlockSpec((1, d), fix),
                  pl.BlockSpec((d, LANES), fix), pl.BlockSpec((d, LANES), fix),
                  pl.BlockSpec((1, LANES), fix), pl.BlockSpec((tm, tm), fix)],
        out_specs=[pl.BlockSpec((tm, d), row), pl.BlockSpec((tm, d), row),
                   pl.BlockSpec((tm, LANES), row), pl.BlockSpec((tm, LANES), row),
                   pl.BlockSpec((tm, LANES), row), pl.BlockSpec((8, LANES), fix)],
        out_shape=[jax.ShapeDtypeStruct((t, d), F32), jax.ShapeDtypeStruct((t, d), F32),
                   jax.ShapeDtypeStruct((t, LANES), jnp.int32),
                   jax.ShapeDtypeStruct((t, LANES), jnp.int32),
                   jax.ShapeDtypeStruct((t, LANES), F32),
                   jax.ShapeDtypeStruct((8, LANES), jnp.int32)],
        scratch_shapes=[pltpu.VMEM((8, LANES), F32)],
        compiler_params=_params(("arbitrary",)),
    )(merged, x, wo, nw, wr1, wr2, br, trs)


_PAD_CHUNKS = tuple(1 << s for s in range(MOE_ROWS.bit_length() - 2, 2, -1))
_PAD_SINGLE = 7


def _dispatch_kernel(dest_ref, cnt_ref, pstart_ref, x_ref, o_hbm, zbuf, sem, zsem, *, rows, n_experts):
    i = pl.program_id(0)

    def issue(t, carry):
        for k in range(TOP_K):
            d = dest_ref[(i * rows + t) * TOP_K + k]
            pltpu.make_async_copy(x_ref.at[t], o_hbm.at[d], sem).start()
        return carry

    lax.fori_loop(0, rows, issue, 0, unroll=4)
    for k in range(TOP_K):
        pltpu.make_async_copy(x_ref, o_hbm.at[pl.ds(0, rows)], sem).wait()

    @pl.when(i == pl.num_programs(0) - 1)
    def _():
        zbuf[...] = jnp.zeros_like(zbuf)

        def pad_copies(e, wait):
            cnt = cnt_ref[e]
            start = pstart_ref[e] + cnt
            npad = (-cnt) & (MOE_ROWS - 1)
            head = npad & _PAD_SINGLE
            for r in range(_PAD_SINGLE):
                cp = pltpu.make_async_copy(zbuf.at[0], o_hbm.at[start + r], zsem)
                pl.when(r < head)(cp.wait if wait else cp.start)
            off = start + head
            body = npad - head
            for rows_c in _PAD_CHUNKS:
                cp = pltpu.make_async_copy(zbuf.at[pl.ds(0, rows_c)],
                                           o_hbm.at[pl.ds(pl.multiple_of(off, 8), rows_c)], zsem)
                pl.when((body & rows_c) != 0)(cp.wait if wait else cp.start)
                off = off + (body & rows_c)

        def start_e(e, carry):
            pad_copies(e, False)
            return carry

        def wait_e(e, carry):
            pad_copies(e, True)
            return carry

        lax.fori_loop(0, n_experts, start_e, 0)
        lax.fori_loop(0, n_experts, wait_e, 0)

        zrows = zbuf.shape[0]
        first = (pstart_ref[n_experts - 1] + cnt_ref[n_experts - 1] + MOE_ROWS - 1) // MOE_ROWS
        first = first * (MOE_ROWS // zrows)

        def tail_copy(c):
            return pltpu.make_async_copy(
                zbuf, o_hbm.at[pl.ds(pl.multiple_of(c * zrows, zrows), zrows)], zsem)

        def start_t(c, carry):
            tail_copy(c).start()
            return carry

        def wait_t(c, carry):
            tail_copy(c).wait()
            return carry

        lax.fori_loop(first, o_hbm.shape[0] // zrows, start_t, 0)
        lax.fori_loop(first, o_hbm.shape[0] // zrows, wait_t, 0)


def _dispatch_call(dest, counts, pad_starts, src, n_rows):
    t, d = src.shape
    rows = _pick_tile(t, (256, 128, 64))
    return pl.pallas_call(
        functools.partial(_dispatch_kernel, rows=rows, n_experts=counts.shape[0]),
        name="k_dispatch",
        grid_spec=pltpu.PrefetchScalarGridSpec(
            num_scalar_prefetch=3,
            grid=(t // rows,),
            in_specs=[pl.BlockSpec((rows, d), lambda i, de, cn, ps: (i, 0))],
            out_specs=pl.BlockSpec(memory_space=pl.ANY),
            scratch_shapes=[pltpu.VMEM((_PAD_CHUNKS[0], d), src.dtype),
                            pltpu.SemaphoreType.DMA, pltpu.SemaphoreType.DMA]),
        out_shape=jax.ShapeDtypeStruct((n_rows, d), src.dtype),
        compiler_params=_params(("arbitrary",)),
    )(dest, counts, pad_starts, src)


def _moe_kernel(be_ref, nused_ref, x_ref, wg_ref, wl_ref, bg_ref, bl_ref, wd_ref, bd_ref,
                o_ref, xb, acc):
    b = pl.program_id(0)
    j = pl.program_id(1)
    nj = pl.num_programs(1)
    live = b < nused_ref[0]

    @pl.when(jnp.logical_and(live, j == 0))
    def _():
        xb[...] = x_ref[...].astype(BF16)
        acc[...] = jnp.zeros_like(acc)

    @pl.when(live)
    def _():
        x = xb[...]
        glu = jnp.dot(x, wg_ref[...].astype(BF16), preferred_element_type=F32) + bg_ref[...]
        lin = jnp.dot(x, wl_ref[...].astype(BF16), preferred_element_type=F32) + bl_ref[...]
        glu = jnp.minimum(glu, SWIGLU_LIMIT)
        lin = jnp.clip(lin, -SWIGLU_LIMIT, SWIGLU_LIMIT)
        act = glu * jax.nn.sigmoid(SWIGLU_ALPHA * glu) * (lin + 1.0)
        acc[...] += jnp.dot(act.astype(BF16), wd_ref[...].astype(BF16),
                            preferred_element_type=F32)

    @pl.when(jnp.logical_and(live, j == nj - 1))
    def _():
        o_ref[...] = acc[...] + bd_ref[...]

    @pl.when(jnp.logical_and(jnp.logical_not(live), j == nj - 1))
    def _():
        o_ref[...] = jnp.zeros_like(o_ref)


def _moe_call(block_e, nused, xs, w_up, b_up, w_down, b_down):
    n_rows, d = xs.shape
    d_ff = w_down.shape[1]
    tf = COL_BLOCK
    nj = d_ff // tf
    nb = n_rows // MOE_ROWS

    def bb(b, nu):
        return jnp.minimum(b, nu[0] - 1)

    def jj(b, j, nu):
        return jnp.where(b < nu[0], j, nj - 1)

    return pl.pallas_call(
        _moe_kernel,
        name="k_moe",
        grid_spec=pltpu.PrefetchScalarGridSpec(
            num_scalar_prefetch=2,
            grid=(nb, nj),
            in_specs=[pl.BlockSpec((MOE_ROWS, d), lambda b, j, be, nu: (bb(b, nu), 0)),
                      pl.BlockSpec((None, d, tf), lambda b, j, be, nu: (be[bb(b, nu)], 0, jj(b, j, nu))),
                      pl.BlockSpec((None, d, tf),
                                   lambda b, j, be, nu: (be[bb(b, nu)], 0, nj + jj(b, j, nu))),
                      pl.BlockSpec((None, 1, tf), lambda b, j, be, nu: (be[bb(b, nu)], 0, jj(b, j, nu))),
                      pl.BlockSpec((None, 1, tf),
                                   lambda b, j, be, nu: (be[bb(b, nu)], 0, nj + jj(b, j, nu))),
                      pl.BlockSpec((None, tf, d), lambda b, j, be, nu: (be[bb(b, nu)], jj(b, j, nu), 0)),
                      pl.BlockSpec((None, 1, d), lambda b, j, be, nu: (be[bb(b, nu)], 0, 0))],
            out_specs=pl.BlockSpec((MOE_ROWS, d), lambda b, j, be, nu: (b, 0)),
            scratch_shapes=[pltpu.VMEM((MOE_ROWS, d), BF16), pltpu.VMEM((MOE_ROWS, d), F32)]),
        out_shape=jax.ShapeDtypeStruct((n_rows, d), F32),
        compiler_params=_params(("arbitrary", "arbitrary")),
    )(block_e, nused, xs, w_up, w_up, b_up, b_up, w_down, b_down)


def _combine_kernel(dest_ref, h_ref, gate_ref, nw_ref, eo_hbm, y_ref, gbuf, sem, *, rows):
    i = pl.program_id(0)

    def issue(t, carry):
        for k in range(TOP_K):
            d = dest_ref[(i * rows + t) * TOP_K + k]
            pltpu.make_async_copy(eo_hbm.at[d], gbuf.at[k, t], sem).start()
        return carry

    lax.fori_loop(0, rows, issue, 0, unroll=4)
    for k in range(TOP_K):
        pltpu.make_async_copy(eo_hbm.at[pl.ds(0, rows)], gbuf.at[k], sem).wait()
    gate = gate_ref[...]
    y = h_ref[...]
    for k in range(TOP_K):
        y = y + gate[:, k:k + 1] * gbuf[k]
    r = lax.rsqrt(jnp.mean(y * y, axis=-1, keepdims=True) + EPS)
    y_ref[...] = y * r * nw_ref[...]


def _combine_call(dest, h, gate, nw, eo):
    t, d = h.shape
    rows = _pick_tile(t, (128, 64))
    return pl.pallas_call(
        functools.partial(_combine_kernel, rows=rows),
        name="k_combine",
        grid_spec=pltpu.PrefetchScalarGridSpec(
            num_scalar_prefetch=1,
            grid=(t // rows,),
            in_specs=[pl.BlockSpec((rows, d), lambda i, de: (i, 0)),
                      pl.BlockSpec((rows, LANES), lambda i, de: (i, 0)),
                      pl.BlockSpec((1, d), lambda i, de: (0, 0)),
                      pl.BlockSpec(memory_space=pl.ANY)],
            out_specs=pl.BlockSpec((rows, d), lambda i, de: (i, 0)),
            scratch_shapes=[pltpu.VMEM((TOP_K, rows, d), F32), pltpu.SemaphoreType.DMA]),
        out_shape=jax.ShapeDtypeStruct((t, d), F32),
        compiler_params=_params(("arbitrary",)),
    )(dest, h, gate, nw, eo)


def _group_bc(a):
    lead = a.shape[:-1]
    return a.reshape(lead + (2, N_GROUPS, D_STATE)).swapaxes(-3, -2).reshape(lead + (2 * N_GROUPS * D_STATE,))


def _ungroup_bc(a):
    lead = a.shape[:-1]
    return a.reshape(lead + (N_GROUPS, 2, D_STATE)).swapaxes(-3, -2).reshape(lead + (2 * N_GROUPS * D_STATE,))


def kernel(x_prompt, x_sample, state_ssm, state_conv, state_pool, norm_mix_w, w_in, conv_w, conv_b,
           dt_bias, a_log, d_skip, ssd_norm_w, w_pool, pool_scale, w_proj_ssd, w_proj_pool, w_out,
           norm_ffn_w, w_router, b_router, w_up, b_up, w_down, b_down, norm_final_w):
    batch, seq_len, d_model = x_prompt.shape
    dec_batch, dec_seq, _ = x_sample.shape
    depth, _, n_heads, head_dim, d_state = state_ssm.shape
    assert depth == 1 and batch == 1 and dec_seq == CHUNK and seq_len % CHUNK == 0
    assert head_dim == HEAD_DIM and d_state == D_STATE
    d_inner = n_heads * head_dim
    assert d_inner == N_GROUPS * COL_BLOCK
    d_bc = 2 * N_GROUPS * D_STATE
    d_pool = state_pool.shape[-1]
    assert d_pool == len(POOL_WINDOWS) * COL_BLOCK and state_pool.shape[-2] == POOL_BUF
    n_experts = w_router.shape[-1]
    past_len = PAST_LEN
    n_prompt_chunks = seq_len // CHUNK
    n_seq = batch + dec_batch
    t_prompt = batch * seq_len
    t = t_prompt + dec_batch * dec_seq

    x = jnp.concatenate([x_prompt.reshape(t_prompt, d_model), x_sample.reshape(-1, d_model)], axis=0)

    wi = w_in[0]
    z_end = d_inner
    xbc_end = z_end + d_inner + d_bc
    dt_end = xbc_end + n_heads
    w_all = jnp.concatenate(
        [wi[:, :z_end + d_inner], _group_bc(wi[:, z_end + d_inner:xbc_end]), wi[:, dt_end:]],
        axis=1).astype(BF16)
    w_dt = jnp.pad(wi[:, xbc_end:dt_end], ((0, 0), (0, LANES - n_heads)))
    pad_h = lambda a: jnp.pad(a.reshape(1, n_heads), ((0, 0), (0, LANES - n_heads)))
    cw = conv_w[0]
    cwx, cwbc = cw[:, :d_inner], _group_bc(cw[:, d_inner:])
    cb = conv_b[0].reshape(1, -1)
    cbx, cbbc = cb[:, :d_inner], _group_bc(cb[:, d_inner:])
    dsk = jnp.repeat(d_skip[0], head_dim).reshape(1, d_inner)

    ssm_t = state_ssm[0].reshape(dec_batch, N_GROUPS, n_heads // N_GROUPS, head_dim, d_state)
    ssm_t = ssm_t.transpose(0, 1, 4, 2, 3).reshape(dec_batch, N_GROUPS, d_state, COL_BLOCK)
    ssm0 = jnp.concatenate([jnp.zeros((batch,) + ssm_t.shape[1:], F32), ssm_t], axis=0)
    conv0 = jnp.pad(state_conv[0], ((batch, 0), (8 - (CONV_WIDTH - 1), 0), (0, 0)))
    cx0, cbc0 = conv0[..., :d_inner], _group_bc(conv0[..., d_inner:])
    pool0 = jnp.pad(state_pool[0], ((batch, 0), (1, 0), (0, 0)))

    u, dt_raw = _norm_call(x, norm_mix_w[0].reshape(1, -1), w_dt)
    p = _inproj_call(u, w_all)
    yg, ssm_new = _ssd_call(p, dt_raw, ssm0, cx0, cbc0, pad_h(dt_bias[0]), pad_h(a_log[0]), dsk,
                            cwx, cbx, cwbc, cbbc, _ssd_consts(), n_prompt_chunks)
    yp = _pool_call(p, pool0, w_pool[0].astype(BF16), pool_scale[0].reshape(1, -1),
                    n_prompt_chunks, past_len)
    merged = _merge_call(yg, yp, p, ssd_norm_w[0].reshape(1, -1), w_proj_ssd[0].astype(BF16),
                         w_proj_pool[0].astype(BF16))

    wr = jnp.pad(w_router[0], ((0, 0), (0, LANES - n_experts)))
    wr1 = wr.astype(BF16)
    wr2 = (wr - wr1.astype(F32)).astype(BF16)
    br = jnp.pad(b_router[0].reshape(1, -1), ((0, 0), (0, LANES - n_experts)), constant_values=-1e30)
    h, hn, eidx, rank, gate, cnt = _route_call(merged, x, w_out[0].astype(BF16),
                                               norm_ffn_w[0].reshape(1, -1), wr1, wr2, br)

    counts = cnt[0, :n_experts]
    padded = (counts + MOE_ROWS - 1) // MOE_ROWS * MOE_ROWS
    pad_ends = jnp.cumsum(padded)
    pad_starts = pad_ends - padded
    n_blocks = -(-(t * TOP_K) // MOE_ROWS) + n_experts
    n_rows = n_blocks * MOE_ROWS
    e_flat = eidx[:, :TOP_K].reshape(-1)
    dest = (pad_starts[e_flat] + rank[:, :TOP_K].reshape(-1)).astype(jnp.int32)
    block_start = jnp.arange(n_blocks, dtype=jnp.int32) * MOE_ROWS
    block_e = jnp.minimum(jnp.sum(block_start[:, None] >= pad_ends[None, :], axis=1),
                          n_experts - 1).astype(jnp.int32)
    nused = (pad_ends[-1:] // MOE_ROWS).astype(jnp.int32)

    xs = _dispatch_call(dest, counts, pad_starts.astype(jnp.int32), hn, n_rows)
    eo = _moe_call(block_e, nused, xs, w_up[0], b_up[0].reshape(n_experts, 1, -1),
                   w_down[0], b_down[0].reshape(n_experts, 1, -1))
    y = _combine_call(dest, h, gate, norm_final_w.reshape(1, -1), eo)

    y_prompt = y[:t_prompt].reshape(batch, seq_len, d_model)
    y_sample = y[t_prompt:].reshape(dec_batch, dec_seq, d_model)
    ssm_out = ssm_new.reshape(n_seq, N_GROUPS, d_state, n_heads // N_GROUPS, head_dim)
    ssm_out = ssm_out.transpose(0, 1, 3, 4, 2).reshape(n_seq, n_heads, head_dim, d_state)
    seq_ends = [t_prompt] * batch + [t_prompt + (s + 1) * dec_seq for s in range(dec_batch)]
    tail = jnp.stack([p[:, e - POOL_BUF:e] for e in seq_ends], axis=0)
    tail = tail.transpose(0, 2, 1, 3)
    ctail = tail[:, POOL_BUF - (CONV_WIDTH - 1):]
    conv_x = ctail[:, :, P_X:P_BC].reshape(n_seq, CONV_WIDTH - 1, d_inner)
    conv_bc = ctail[:, :, P_BC:P_POOL].reshape(n_seq, CONV_WIDTH - 1, d_bc)
    conv_new = jnp.concatenate([conv_x, _ungroup_bc(conv_bc)], axis=-1)
    pool_new = tail[:, :, P_POOL:P_GATE_SSD].reshape(n_seq, POOL_BUF, d_pool)
    return (y_prompt, y_sample,
            ssm_out[None, :batch], conv_new[None, :batch], pool_new[None, :batch],
            ssm_out[None, batch:], conv_new[None, batch:], pool_new[None, batch:])
```

```python
import functools

import numpy as np
import jax
import jax.numpy as jnp
from jax import lax
from jax.experimental import pallas as pl
from jax.experimental.pallas import tpu as pltpu

F32 = jnp.float32
BF16 = jnp.bfloat16
HIGHEST = lax.Precision.HIGHEST

CHUNK = 64
HEAD_DIM = 64
N_GROUPS = 8
D_STATE = 128
CONV_WIDTH = 4
POOL_WINDOWS = (2, 4, 8, 16)
POOL_BUF = 15
PAST_LEN = 4096
TOP_K = 4
SWIGLU_ALPHA = 1.702
SWIGLU_LIMIT = 7.0
EPS = 1e-5
LANES = 128
COL_BLOCK = 512
MOE_ROWS = 1024
MOE_HALF = MOE_ROWS // 2
MOE_FF_TILE = 256
P_Z, P_X, P_BC, P_POOL, P_GATE_SSD, P_GATE_POOL = 0, 8, 16, 20, 24, 28
VMEM_LIMIT = 56 * 1024 * 1024


def _pick_tile(n, candidates):
    for c in candidates:
        if n % c == 0:
            return c
    raise ValueError(f"no tile for {n} in {candidates}")


def _params(sem, vmem=VMEM_LIMIT):
    return pltpu.CompilerParams(dimension_semantics=sem, vmem_limit_bytes=vmem)


def _split3(v):
    p1 = v.astype(BF16)
    r1 = v - p1.astype(F32)
    p2 = r1.astype(BF16)
    p3 = (r1 - p2.astype(F32)).astype(BF16)
    return p1, p2, p3


def _silu(v):
    return v * jax.nn.sigmoid(v)


def _norm_kernel(x_ref, w_ref, wdt_ref, u_ref, dt_ref):
    x = x_ref[...]
    r = lax.rsqrt(jnp.mean(x * x, axis=-1, keepdims=True) + EPS)
    u = x * r * w_ref[...]
    u_ref[...] = u.astype(BF16)
    dt_ref[...] = jnp.dot(u, wdt_ref[...], precision=HIGHEST, preferred_element_type=F32)


def _norm_call(x, w, wdt):
    t, d = x.shape
    tm = _pick_tile(t, (512, 256, 128, 64))
    return pl.pallas_call(
        _norm_kernel,
        name="k_norm",
        grid=(t // tm,),
        in_specs=[pl.BlockSpec((tm, d), lambda i: (i, 0)),
                  pl.BlockSpec((1, d), lambda i: (0, 0)),
                  pl.BlockSpec((d, LANES), lambda i: (0, 0))],
        out_specs=[pl.BlockSpec((tm, d), lambda i: (i, 0)),
                   pl.BlockSpec((tm, LANES), lambda i: (i, 0))],
        out_shape=[jax.ShapeDtypeStruct((t, d), BF16), jax.ShapeDtypeStruct((t, LANES), F32)],
        compiler_params=_params(("parallel",)),
    )(x, w, wdt)


def _inproj_kernel(u_ref, w_ref, o_ref):
    r = jnp.dot(u_ref[...], w_ref[...], preferred_element_type=F32)
    for k in range(o_ref.shape[0]):
        o_ref[k] = r[:, k * COL_BLOCK:(k + 1) * COL_BLOCK]


def _inproj_call(u, w):
    t, d = u.shape
    n = w.shape[1]
    tm = _pick_tile(t, (1024, 512, 256, 128, 64))
    nb = n // COL_BLOCK
    per = 2
    return pl.pallas_call(
        _inproj_kernel,
        name="k_inproj",
        grid=(t // tm, nb // per),
        in_specs=[pl.BlockSpec((tm, d), lambda i, j: (i, 0)),
                  pl.BlockSpec((d, per * COL_BLOCK), lambda i, j: (0, j))],
        out_specs=pl.BlockSpec((per, tm, COL_BLOCK), lambda i, j: (j, i, 0)),
        out_shape=jax.ShapeDtypeStruct((nb, t, COL_BLOCK), F32),
        compiler_params=_params(("parallel", "arbitrary")),
    )(u, w)


def _ssd_kernel(z_ref, x_ref, bc_ref, dt_ref, ssm0_ref, cx0_ref, cbc0_ref,
                dtb_ref, alog_ref, dsk_ref, cwx_ref, cbx_ref, cwbc_ref, cbbc_ref,
                tril3_ref, e3_ref, diag_ref, caus_ref, bd_ref,
                yg_ref, ssm_ref, bufx, bufbc, exs, *, n_prompt_chunks):
    c = pl.program_id(0)
    first = jnp.logical_or(c == 0, c >= n_prompt_chunks)

    @pl.when(first)
    def _():
        ssm_ref[...] = ssm0_ref[...]
        bufx[0:8, :] = cx0_ref[...]
        bufbc[0:8, :] = cbc0_ref[...]

    @pl.when(jnp.logical_not(first))
    def _():
        bufx[0:8, :] = bufx[CHUNK:CHUNK + 8, :]
        bufbc[0:8, :] = bufbc[CHUNK:CHUNK + 8, :]

    for g in range(N_GROUPS):
        bufx[8:8 + CHUNK, g * COL_BLOCK:(g + 1) * COL_BLOCK] = x_ref[g]
    for q in range(N_GROUPS // 2):
        bufbc[8:8 + CHUNK, q * COL_BLOCK:(q + 1) * COL_BLOCK] = bc_ref[q]

    dtv = dt_ref[...] + dtb_ref[...]
    dt = jnp.maximum(dtv, 0.0) + jnp.log1p(jnp.exp(-jnp.abs(dtv)))
    d_a = dt * (-jnp.exp(alog_ref[...]))
    p1, p2, p3 = _split3(d_a)
    acum = jnp.dot(tril3_ref[...], jnp.concatenate([p1, p2, p3], axis=0),
                   preferred_element_type=F32)
    q1, q2, q3 = _split3(jnp.concatenate([acum, dt], axis=0))
    exs[...] = jnp.dot(jnp.concatenate([q1, q2, q3], axis=1), e3_ref[...],
                       preferred_element_type=F32)

    caus = caus_ref[...] > 0.0
    bd = bd_ref[...]
    hw = 4 * HEAD_DIM
    for g in range(N_GROUPS):
        sl = slice(g * COL_BLOCK, (g + 1) * COL_BLOCK)
        slb = slice(g * 2 * D_STATE, (g + 1) * 2 * D_STATE)
        xc = cbx_ref[:, sl]
        bcc = cbbc_ref[:, slb]
        for k in range(CONV_WIDTH):
            xc = xc + cwx_ref[k:k + 1, sl] * bufx[5 + k:5 + k + CHUNK, sl]
            bcc = bcc + cwbc_ref[k:k + 1, slb] * bufbc[5 + k:5 + k + CHUNK, slb]
        xs = _silu(xc)
        bca = _silu(bcc)
        b_g = bca[:, :D_STATE]
        c_g = bca[:, D_STATE:]
        acx = exs[0:CHUNK, sl]
        dtx = exs[CHUNK:2 * CHUNK, sl]
        alast = acx[CHUNK - 1:CHUNK, :]
        arow = jnp.sum(acx * diag_ref[:, sl], axis=0, keepdims=True)
        xdt = xs * dtx
        xdtb = xdt.astype(BF16)
        bb = b_g.astype(BF16)
        cb = c_g.astype(BF16)
        cb2 = lax.dot_general(cb, jnp.concatenate([bb, bb], axis=0),
                              (((1,), (1,)), ((), ())), preferred_element_type=F32)
        ydiag = []
        for q in range(2):
            lhs = []
            for d in range(2):
                lo = q * hw + d * LANES
                seg = acx[:, lo:lo + LANES] - arow[:, lo:lo + LANES]
                lhs.append((cb2 * jnp.exp(jnp.where(caus, seg, -jnp.inf))).astype(BF16))
            xq = xdtb[:, q * hw:(q + 1) * hw]
            wq = jnp.concatenate([xq, xq, xq, xq], axis=0) * bd
            ydiag.append(jnp.dot(jnp.concatenate(lhs, axis=1), wq, preferred_element_type=F32))
        s_old = ssm_ref[g]
        yoff = jnp.dot(cb, s_old.astype(BF16), preferred_element_type=F32) * jnp.exp(acx)
        y = jnp.concatenate(ydiag, axis=1) + yoff + dsk_ref[:, sl] * xs
        yg_ref[:, sl] = y * _silu(z_ref[g])
        v = (xdt * jnp.exp(alast - acx)).astype(BF16)
        ssm_ref[g] = jnp.exp(alast) * s_old + jnp.dot(b_g.T.astype(BF16), v,
                                                      preferred_element_type=F32)


def _ssd_call(p, dt_raw, ssm0, cx0, cbc0, dtb, alog, dsk, cwx, cbx, cwbc, cbbc, consts,
              n_prompt_chunks):
    _, t, _ = p.shape
    n_chunks = t // CHUNK
    n_seq = ssm0.shape[0]
    d_inner = N_GROUPS * COL_BLOCK
    d_bc = N_GROUPS * 2 * D_STATE
    tril3, e3, diag, caus, bd = consts

    def seq(c):
        return jnp.maximum(c - (n_prompt_chunks - 1), 0)

    def const2(a):
        return pl.BlockSpec(a.shape, lambda c: (0, 0))

    return pl.pallas_call(
        functools.partial(_ssd_kernel, n_prompt_chunks=n_prompt_chunks),
        name="k_ssd",
        grid=(n_chunks,),
        in_specs=[pl.BlockSpec((N_GROUPS, CHUNK, COL_BLOCK), lambda c: (P_Z // N_GROUPS, c, 0)),
                  pl.BlockSpec((N_GROUPS, CHUNK, COL_BLOCK), lambda c: (P_X // N_GROUPS, c, 0)),
                  pl.BlockSpec((N_GROUPS // 2, CHUNK, COL_BLOCK),
                               lambda c: (P_BC // (N_GROUPS // 2), c, 0)),
                  pl.BlockSpec((CHUNK, LANES), lambda c: (c, 0)),
                  pl.BlockSpec((None, N_GROUPS, D_STATE, COL_BLOCK), lambda c: (seq(c), 0, 0, 0)),
                  pl.BlockSpec((None, 8, d_inner), lambda c: (seq(c), 0, 0)),
                  pl.BlockSpec((None, 8, d_bc), lambda c: (seq(c), 0, 0)),
                  const2(dtb), const2(alog), const2(dsk), const2(cwx), const2(cbx),
                  const2(cwbc), const2(cbbc),
                  const2(tril3), const2(e3), const2(diag), const2(caus), const2(bd)],
        out_specs=[pl.BlockSpec((CHUNK, d_inner), lambda c: (c, 0)),
                   pl.BlockSpec((None, N_GROUPS, D_STATE, COL_BLOCK), lambda c: (seq(c), 0, 0, 0))],
        out_shape=[jax.ShapeDtypeStruct((t, d_inner), F32),
                   jax.ShapeDtypeStruct((n_seq, N_GROUPS, D_STATE, COL_BLOCK), F32)],
        scratch_shapes=[pltpu.VMEM((CHUNK + 8, d_inner), F32),
                        pltpu.VMEM((CHUNK + 8, d_bc), F32),
                        pltpu.VMEM((2 * CHUNK, d_inner), F32)],
        compiler_params=_params(("arbitrary",)),
    )(p, p, p, dt_raw, ssm0, cx0, cbc0, dtb, alog, dsk, cwx, cbx, cwbc, cbbc,
      tril3, e3, diag, caus, bd)


def _ssd_consts():
    l = np.arange(CHUNK)
    tril = (l[:, None] >= l[None, :]).astype(np.float32)
    tril3 = np.concatenate([tril, tril, tril], axis=1)
    n_heads = N_GROUPS * COL_BLOCK // HEAD_DIM
    col_head = np.arange(n_heads * HEAD_DIM) // HEAD_DIM
    col_pos = np.arange(n_heads * HEAD_DIM) % HEAD_DIM
    e = (np.arange(LANES)[:, None] == col_head[None, :]).astype(np.float32)
    e3 = np.concatenate([e, e, e], axis=0)
    diag = (l[:, None] == col_pos[None, :]).astype(np.float32)
    caus = np.concatenate([tril, tril], axis=1)
    r = np.arange(4 * HEAD_DIM)
    bd = (r[:, None] // HEAD_DIM == r[None, :] // HEAD_DIM).astype(np.float32)
    return (jnp.asarray(tril3, BF16), jnp.asarray(e3, BF16), jnp.asarray(diag, F32),
            jnp.asarray(caus, F32), jnp.asarray(bd, BF16))


def _pool_kernel(pu_ref, pool0_ref, wp_ref, scale_ref, yp_ref, pbuf, *, n_prompt_chunks, past_len):
    c = pl.program_id(0)
    first = jnp.logical_or(c == 0, c >= n_prompt_chunks)
    hist = POOL_BUF + 1

    @pl.when(first)
    def _():
        pbuf[0:hist, :] = pool0_ref[...]

    @pl.when(jnp.logical_not(first))
    def _():
        pbuf[0:hist, :] = pbuf[CHUNK:CHUNK + hist, :]

    for g in range(len(POOL_WINDOWS)):
        pbuf[hist:hist + CHUNK, g * COL_BLOCK:(g + 1) * COL_BLOCK] = pu_ref[g]

    pos0 = jnp.where(c < n_prompt_chunks, c * CHUNK, past_len)
    pos = (pos0 + lax.broadcasted_iota(jnp.int32, (CHUNK, 1), 0)).astype(F32)
    for g, win in enumerate(POOL_WINDOWS):
        sl = slice(g * COL_BLOCK, (g + 1) * COL_BLOCK)
        cur = pbuf[hist:hist + CHUNK, sl]
        tot = cur
        for i in range(1, win):
            tot = tot + pbuf[hist - i:hist - i + CHUNK, sl]
        count = jnp.minimum(pos + 1.0, float(win))
        pooled = tot / count - cur
        yp_ref[:, sl] = jnp.dot(pooled.astype(BF16), wp_ref[g],
                                preferred_element_type=F32) * scale_ref[:, sl]


def _pool_call(p, pool0, wp, scale, n_prompt_chunks, past_len):
    _, t, _ = p.shape
    ng = len(POOL_WINDOWS)
    d_pool = ng * COL_BLOCK
    hist = POOL_BUF + 1

    def seq(c):
        return jnp.maximum(c - (n_prompt_chunks - 1), 0)

    return pl.pallas_call(
        functools.partial(_pool_kernel, n_prompt_chunks=n_prompt_chunks, past_len=past_len),
        name="k_pool",
        grid=(t // CHUNK,),
        in_specs=[pl.BlockSpec((ng, CHUNK, COL_BLOCK), lambda c: (P_POOL // ng, c, 0)),
                  pl.BlockSpec((None, hist, d_pool), lambda c: (seq(c), 0, 0)),
                  pl.BlockSpec(wp.shape, lambda c: (0, 0, 0)),
                  pl.BlockSpec((1, d_pool), lambda c: (0, 0))],
        out_specs=pl.BlockSpec((CHUNK, d_pool), lambda c: (c, 0)),
        out_shape=jax.ShapeDtypeStruct((t, d_pool), F32),
        scratch_shapes=[pltpu.VMEM((CHUNK + hist, d_pool), F32)],
        compiler_params=_params(("arbitrary",)),
    )(p, pool0, wp, scale)


def _merge_kernel(yg_ref, yp_ref, gs_ref, gp_ref, nw_ref, wps_ref, wpp_ref, o_ref):
    y = yg_ref[...]
    r = lax.rsqrt(jnp.mean(y * y, axis=-1, keepdims=True) + EPS)
    yn = (y * r * nw_ref[...]).astype(BF16)
    a = jnp.dot(yn, wps_ref[...], preferred_element_type=F32)
    b = jnp.dot(yp_ref[...].astype(BF16), wpp_ref[...], preferred_element_type=F32)
    for j in range(gs_ref.shape[0]):
        sl = slice(j * COL_BLOCK, (j + 1) * COL_BLOCK)
        o_ref[:, sl] = (jax.nn.sigmoid(gs_ref[j]) * a[:, sl]
                        + jax.nn.sigmoid(gp_ref[j]) * b[:, sl]).astype(BF16)


def _merge_call(yg, yp, p, nw, wps, wpp):
    t, d_inner = yg.shape
    d_pool = yp.shape[1]
    d_model = wps.shape[1]
    tm = _pick_tile(t, (256, 128, 64))
    nj = d_model // COL_BLOCK
    once = pl.Buffered(1)
    return pl.pallas_call(
        _merge_kernel,
        name="k_merge",
        grid=(t // tm,),
        in_specs=[pl.BlockSpec((tm, d_inner), lambda i: (i, 0)),
                  pl.BlockSpec((tm, d_pool), lambda i: (i, 0)),
                  pl.BlockSpec((nj, tm, COL_BLOCK), lambda i: (P_GATE_SSD // nj, i, 0)),
                  pl.BlockSpec((nj, tm, COL_BLOCK), lambda i: (P_GATE_POOL // nj, i, 0)),
                  pl.BlockSpec((1, d_inner), lambda i: (0, 0)),
                  pl.BlockSpec((d_inner, d_model), lambda i: (0, 0), pipeline_mode=once),
                  pl.BlockSpec((d_pool, d_model), lambda i: (0, 0), pipeline_mode=once)],
        out_specs=pl.BlockSpec((tm, d_model), lambda i: (i, 0)),
        out_shape=jax.ShapeDtypeStruct((t, d_model), BF16),
        compiler_params=_params(("parallel",)),
    )(yg, yp, p, p, nw, wps, wpp)


def _route_kernel(m_ref, x_ref, wo_ref, nw_ref, wr1_ref, wr2_ref, br_ref, trs_ref,
                  h_ref, hn_ref, eidx_ref, rank_ref, gate_ref, cnt_ref, carry):
    i = pl.program_id(0)

    @pl.when(i == 0)
    def _():
        carry[...] = jnp.zeros_like(carry)

    h = x_ref[...] + jnp.dot(m_ref[...], wo_ref[...], preferred_element_type=F32)
    h_ref[...] = h
    r = lax.rsqrt(jnp.mean(h * h, axis=-1, keepdims=True) + EPS)
    hn = h * r * nw_ref[...]
    hn_ref[...] = hn
    h1 = hn.astype(BF16)
    h2 = (hn - h1.astype(F32)).astype(BF16)
    logits = (jnp.dot(h1, wr1_ref[...], preferred_element_type=F32)
              + jnp.dot(h1, wr2_ref[...], preferred_element_type=F32)
              + jnp.dot(h2, wr1_ref[...], preferred_element_type=F32)) + br_ref[...]
    lane = lax.broadcasted_iota(jnp.int32, logits.shape, 1)
    work = logits
    member = jnp.zeros(logits.shape, F32)
    vals, idxs = [], []
    for _ in range(TOP_K):
        m = jnp.max(work, axis=-1, keepdims=True)
        idx = jnp.min(jnp.where(work == m, lane, LANES), axis=-1, keepdims=True)
        hit = lane == idx
        member = member + hit.astype(F32)
        work = jnp.where(hit, -jnp.inf, work)
        vals.append(m)
        idxs.append(idx)
    ex = [jnp.exp(v - vals[0]) for v in vals]
    den = ex[0] + ex[1] + ex[2] + ex[3]
    before = jnp.dot(trs_ref[...], member.astype(BF16), preferred_element_type=F32) + carry[0:1, :]
    eidx = jnp.zeros(logits.shape, jnp.int32)
    rank = jnp.zeros(logits.shape, jnp.int32)
    gate = jnp.zeros(logits.shape, F32)
    for k in range(TOP_K):
        rk = jnp.sum(jnp.where(lane == idxs[k], before, 0.0), axis=-1, keepdims=True)
        eidx = jnp.where(lane == k, idxs[k], eidx)
        rank = jnp.where(lane == k, rk.astype(jnp.int32), rank)
        gate = jnp.where(lane == k, ex[k] / den, gate)
    eidx_ref[...] = eidx
    rank_ref[...] = rank
    gate_ref[...] = gate
    carry[0:1, :] = carry[0:1, :] + jnp.sum(member, axis=0, keepdims=True)
    cnt_ref[...] = carry[...].astype(jnp.int32)


def _route_call(merged, x, wo, nw, wr1, wr2, br):
    t, d = x.shape
    tm = _pick_tile(t, (512, 256, 128, 64))
    ri = np.arange(tm)
    trs = jnp.asarray((ri[:, None] > ri[None, :]).astype(np.float32), BF16)
    row = lambda i: (i, 0)
    fix = lambda i: (0, 0)
    return pl.pallas_call(
        _route_kernel,
        name="k_route",
        grid=(t // tm,),
        in_specs=[pl.BlockSpec((tm, d), row), pl.BlockSpec((tm, d), row),
                  pl.BlockSpec((d, d), fix), pl.BlockSpec((1, d), fix),
                  pl.BlockSpec((d, LANES), fix), pl.BlockSpec((d, LANES), fix),
                  pl.BlockSpec((1, LANES), fix), pl.BlockSpec((tm, tm), fix)],
        out_specs=[pl.BlockSpec((tm, d), row), pl.BlockSpec((tm, d), row),
                   pl.BlockSpec((tm, LANES), row), pl.BlockSpec((tm, LANES), row),
                   pl.BlockSpec((tm, LANES), row), pl.BlockSpec((8, LANES), fix)],
        out_shape=[jax.ShapeDtypeStruct((t, d), F32), jax.ShapeDtypeStruct((t, d), F32),
                   jax.ShapeDtypeStruct((t, LANES), jnp.int32),
                   jax.ShapeDtypeStruct((t, LANES), jnp.int32),
                   jax.ShapeDtypeStruct((t, LANES), F32),
                   jax.ShapeDtypeStruct((8, LANES), jnp.int32)],
        scratch_shapes=[pltpu.VMEM((8, LANES), F32)],
        compiler_params=_params(("arbitrary",)),
    )(merged, x, wo, nw, wr1, wr2, br, trs)


_PAD_CHUNKS = tuple(1 << s for s in range(MOE_ROWS.bit_length() - 2, 2, -1))
_PAD_SINGLE = 7


def _dispatch_kernel(dest_ref, cnt_ref, pstart_ref, x_ref, o_hbm, zbuf, sem, zsem, *, rows, n_experts):
    i = pl.program_id(0)

    def issue(t, carry):
        for k in range(TOP_K):
            d = dest_ref[(i * rows + t) * TOP_K + k]
            pltpu.make_async_copy(x_ref.at[t], o_hbm.at[d], sem).start()
        return carry

    lax.fori_loop(0, rows, issue, 0, unroll=4)
    for k in range(TOP_K):
        pltpu.make_async_copy(x_ref, o_hbm.at[pl.ds(0, rows)], sem).wait()

    @pl.when(i == pl.num_programs(0) - 1)
    def _():
        zbuf[...] = jnp.zeros_like(zbuf)

        def pad_copies(e, wait):
            cnt = cnt_ref[e]
            start = pstart_ref[e] + cnt
            npad = (-cnt) & (MOE_ROWS - 1)
            head = npad & _PAD_SINGLE
            for r in range(_PAD_SINGLE):
                cp = pltpu.make_async_copy(zbuf.at[0], o_hbm.at[start + r], zsem)
                pl.when(r < head)(cp.wait if wait else cp.start)
            off = start + head
            body = npad - head
            for rows_c in _PAD_CHUNKS:
                cp = pltpu.make_async_copy(zbuf.at[pl.ds(0, rows_c)],
                                           o_hbm.at[pl.ds(pl.multiple_of(off, 8), rows_c)], zsem)
                pl.when((body & rows_c) != 0)(cp.wait if wait else cp.start)
                off = off + (body & rows_c)

        def start_e(e, carry):
            pad_copies(e, False)
            return carry

        def wait_e(e, carry):
            pad_copies(e, True)
            return carry

        lax.fori_loop(0, n_experts, start_e, 0)
        lax.fori_loop(0, n_experts, wait_e, 0)

        zrows = zbuf.shape[0]
        first = (pstart_ref[n_experts - 1] + cnt_ref[n_experts - 1] + MOE_ROWS - 1) // MOE_ROWS
        first = first * (MOE_ROWS // zrows)

        def tail_copy(c):
            return pltpu.make_async_copy(
                zbuf, o_hbm.at[pl.ds(pl.multiple_of(c * zrows, zrows), zrows)], zsem)

        def start_t(c, carry):
            tail_copy(c).start()
            return carry

        def wait_t(c, carry):
            tail_copy(c).wait()
            return carry

        lax.fori_loop(first, o_hbm.shape[0] // zrows, start_t, 0)
        lax.fori_loop(first, o_hbm.shape[0] // zrows, wait_t, 0)


def _dispatch_call(dest, counts, pad_starts, src, n_rows):
    t, d = src.shape
    rows = _pick_tile(t, (256, 128, 64))
    return pl.pallas_call(
        functools.partial(_dispatch_kernel, rows=rows, n_experts=counts.shape[0]),
        name="k_dispatch",
        grid_spec=pltpu.PrefetchScalarGridSpec(
            num_scalar_prefetch=3,
            grid=(t // rows,),
            in_specs=[pl.BlockSpec((rows, d), lambda i, de, cn, ps: (i, 0))],
            out_specs=pl.BlockSpec(memory_space=pl.ANY),
            scratch_shapes=[pltpu.VMEM((_PAD_CHUNKS[0], d), src.dtype),
                            pltpu.SemaphoreType.DMA, pltpu.SemaphoreType.DMA]),
        out_shape=jax.ShapeDtypeStruct((n_rows, d), src.dtype),
        compiler_params=_params(("arbitrary",)),
    )(dest, counts, pad_starts, src)


def _moe_kernel(be_ref, nv_ref, nused_ref, x_ref, wg_ref, wl_ref, bg_ref, bl_ref, wd_ref, bd_ref,
                o_ref, xbf, wgb, wlb, wdb):
    b = pl.program_id(0)
    j = pl.program_id(1)
    live = b < nused_ref[0]
    both = jnp.logical_and(live, nv_ref[b] > MOE_HALF)

    @pl.when(jnp.logical_and(j == 0, live))
    def _():
        xbf[...] = x_ref[...].astype(BF16)
        o_ref[...] = jnp.broadcast_to(bd_ref[...], o_ref.shape)

    @pl.when(jnp.logical_and(j == 0, jnp.logical_not(live)))
    def _():
        o_ref[...] = jnp.zeros_like(o_ref)

    def half(lo):
        x = xbf[lo:lo + MOE_HALF, :]
        glu = jnp.dot(x, wgb[...], preferred_element_type=F32) + bg_ref[...]
        lin = jnp.dot(x, wlb[...], preferred_element_type=F32) + bl_ref[...]
        glu = jnp.minimum(glu, SWIGLU_LIMIT)
        lin = jnp.clip(lin, -SWIGLU_LIMIT, SWIGLU_LIMIT)
        act = glu * jax.nn.sigmoid(SWIGLU_ALPHA * glu) * (lin + 1.0)
        o_ref[lo:lo + MOE_HALF, :] += jnp.dot(act.astype(BF16), wdb[...],
                                              preferred_element_type=F32)

    @pl.when(live)
    def _():
        wgb[...] = wg_ref[...].astype(BF16)
        wlb[...] = wl_ref[...].astype(BF16)
        wdb[...] = wd_ref[...].astype(BF16)
        half(0)

    @pl.when(both)
    def _():
        half(MOE_HALF)


def _moe_call(block_e, block_nv, nused, xs, w_up, b_up, w_down, b_down):
    n_rows, d = xs.shape
    d_ff = w_down.shape[1]
    tf = MOE_FF_TILE
    nj = d_ff // tf
    nb = n_rows // MOE_ROWS

    def bb(b, nu):
        return jnp.minimum(b, nu[0] - 1)

    def jj(b, j, nu):
        return jnp.where(b < nu[0], j, nj - 1)

    return pl.pallas_call(
        _moe_kernel,
        name="k_moe",
        grid_spec=pltpu.PrefetchScalarGridSpec(
            num_scalar_prefetch=3,
            grid=(nb, nj),
            in_specs=[pl.BlockSpec((MOE_ROWS, d), lambda b, j, be, nv, nu: (bb(b, nu), 0)),
                      pl.BlockSpec((None, d, tf),
                                   lambda b, j, be, nv, nu: (be[bb(b, nu)], 0, jj(b, j, nu))),
                      pl.BlockSpec((None, d, tf),
                                   lambda b, j, be, nv, nu: (be[bb(b, nu)], 0, nj + jj(b, j, nu))),
                      pl.BlockSpec((None, 1, tf),
                                   lambda b, j, be, nv, nu: (be[bb(b, nu)], 0, jj(b, j, nu))),
                      pl.BlockSpec((None, 1, tf),
                                   lambda b, j, be, nv, nu: (be[bb(b, nu)], 0, nj + jj(b, j, nu))),
                      pl.BlockSpec((None, tf, d),
                                   lambda b, j, be, nv, nu: (be[bb(b, nu)], jj(b, j, nu), 0)),
                      pl.BlockSpec((None, 1, d), lambda b, j, be, nv, nu: (be[bb(b, nu)], 0, 0))],
            out_specs=pl.BlockSpec((MOE_ROWS, d), lambda b, j, be, nv, nu: (b, 0)),
            scratch_shapes=[pltpu.VMEM((MOE_ROWS, d), BF16), pltpu.VMEM((d, tf), BF16),
                            pltpu.VMEM((d, tf), BF16), pltpu.VMEM((tf, d), BF16)]),
        out_shape=jax.ShapeDtypeStruct((n_rows, d), F32),
        compiler_params=_params(("arbitrary", "arbitrary")),
    )(block_e, block_nv, nused, xs, w_up, w_up, b_up, b_up, w_down, b_down)


def _combine_kernel(dest_ref, h_ref, gate_ref, nw_ref, eo_hbm, y_ref, gbuf, sem, *, rows):
    i = pl.program_id(0)

    def issue(t, carry):
        for k in range(TOP_K):
            d = dest_ref[(i * rows + t) * TOP_K + k]
            pltpu.make_async_copy(eo_hbm.at[d], gbuf.at[k, t], sem).start()
        return carry

    lax.fori_loop(0, rows, issue, 0, unroll=4)
    for k in range(TOP_K):
        pltpu.make_async_copy(eo_hbm.at[pl.ds(0, rows)], gbuf.at[k], sem).wait()
    gate = gate_ref[...]
    y = h_ref[...]
    for k in range(TOP_K):
        y = y + gate[:, k:k + 1] * gbuf[k]
    r = lax.rsqrt(jnp.mean(y * y, axis=-1, keepdims=True) + EPS)
    y_ref[...] = y * r * nw_ref[...]


def _combine_call(dest, h, gate, nw, eo):
    t, d = h.shape
    rows = _pick_tile(t, (128, 64))
    return pl.pallas_call(
        functools.partial(_combine_kernel, rows=rows),
        name="k_combine",
        grid_spec=pltpu.PrefetchScalarGridSpec(
            num_scalar_prefetch=1,
            grid=(t // rows,),
            in_specs=[pl.BlockSpec((rows, d), lambda i, de: (i, 0)),
                      pl.BlockSpec((rows, LANES), lambda i, de: (i, 0)),
                      pl.BlockSpec((1, d), lambda i, de: (0, 0)),
                      pl.BlockSpec(memory_space=pl.ANY)],
            out_specs=pl.BlockSpec((rows, d), lambda i, de: (i, 0)),
            scratch_shapes=[pltpu.VMEM((TOP_K, rows, d), F32), pltpu.SemaphoreType.DMA]),
        out_shape=jax.ShapeDtypeStruct((t, d), F32),
        compiler_params=_params(("arbitrary",)),
    )(dest, h, gate, nw, eo)


def _group_bc(a):
    lead = a.shape[:-1]
    return a.reshape(lead + (2, N_GROUPS, D_STATE)).swapaxes(-3, -2).reshape(lead + (2 * N_GROUPS * D_STATE,))


def _ungroup_bc(a):
    lead = a.shape[:-1]
    return a.reshape(lead + (N_GROUPS, 2, D_STATE)).swapaxes(-3, -2).reshape(lead + (2 * N_GROUPS * D_STATE,))


def kernel(x_prompt, x_sample, state_ssm, state_conv, state_pool, norm_mix_w, w_in, conv_w, conv_b,
           dt_bias, a_log, d_skip, ssd_norm_w, w_pool, pool_scale, w_proj_ssd, w_proj_pool, w_out,
           norm_ffn_w, w_router, b_router, w_up, b_up, w_down, b_down, norm_final_w):
    batch, seq_len, d_model = x_prompt.shape
    dec_batch, dec_seq, _ = x_sample.shape
    depth, _, n_heads, head_dim, d_state = state_ssm.shape
    assert depth == 1 and batch == 1 and dec_seq == CHUNK and seq_len % CHUNK == 0
    assert head_dim == HEAD_DIM and d_state == D_STATE
    d_inner = n_heads * head_dim
    assert d_inner == N_GROUPS * COL_BLOCK
    d_bc = 2 * N_GROUPS * D_STATE
    d_pool = state_pool.shape[-1]
    assert d_pool == len(POOL_WINDOWS) * COL_BLOCK and state_pool.shape[-2] == POOL_BUF
    n_experts = w_router.shape[-1]
    past_len = PAST_LEN
    n_prompt_chunks = seq_len // CHUNK
    n_seq = batch + dec_batch
    t_prompt = batch * seq_len
    t = t_prompt + dec_batch * dec_seq

    x = jnp.concatenate([x_prompt.reshape(t_prompt, d_model), x_sample.reshape(-1, d_model)], axis=0)

    wi = w_in[0]
    z_end = d_inner
    xbc_end = z_end + d_inner + d_bc
    dt_end = xbc_end + n_heads
    w_all = jnp.concatenate(
        [wi[:, :z_end + d_inner], _group_bc(wi[:, z_end + d_inner:xbc_end]), wi[:, dt_end:]],
        axis=1).astype(BF16)
    w_dt = jnp.pad(wi[:, xbc_end:dt_end], ((0, 0), (0, LANES - n_heads)))
    pad_h = lambda a: jnp.pad(a.reshape(1, n_heads), ((0, 0), (0, LANES - n_heads)))
    cw = conv_w[0]
    cwx, cwbc = cw[:, :d_inner], _group_bc(cw[:, d_inner:])
    cb = conv_b[0].reshape(1, -1)
    cbx, cbbc = cb[:, :d_inner], _group_bc(cb[:, d_inner:])
    dsk = jnp.repeat(d_skip[0], head_dim).reshape(1, d_inner)

    ssm_t = state_ssm[0].reshape(dec_batch, N_GROUPS, n_heads // N_GROUPS, head_dim, d_state)
    ssm_t = ssm_t.transpose(0, 1, 4, 2, 3).reshape(dec_batch, N_GROUPS, d_state, COL_BLOCK)
    ssm0 = jnp.concatenate([jnp.zeros((batch,) + ssm_t.shape[1:], F32), ssm_t], axis=0)
    conv0 = jnp.pad(state_conv[0], ((batch, 0), (8 - (CONV_WIDTH - 1), 0), (0, 0)))
    cx0, cbc0 = conv0[..., :d_inner], _group_bc(conv0[..., d_inner:])
    pool0 = jnp.pad(state_pool[0], ((batch, 0), (1, 0), (0, 0)))

    u, dt_raw = _norm_call(x, norm_mix_w[0].reshape(1, -1), w_dt)
    p = _inproj_call(u, w_all)
    yg, ssm_new = _ssd_call(p, dt_raw, ssm0, cx0, cbc0, pad_h(dt_bias[0]), pad_h(a_log[0]), dsk,
                            cwx, cbx, cwbc, cbbc, _ssd_consts(), n_prompt_chunks)
    yp = _pool_call(p, pool0, w_pool[0].astype(BF16), pool_scale[0].reshape(1, -1),
                    n_prompt_chunks, past_len)
    merged = _merge_call(yg, yp, p, ssd_norm_w[0].reshape(1, -1), w_proj_ssd[0].astype(BF16),
                         w_proj_pool[0].astype(BF16))

    wr = jnp.pad(w_router[0], ((0, 0), (0, LANES - n_experts)))
    wr1 = wr.astype(BF16)
    wr2 = (wr - wr1.astype(F32)).astype(BF16)
    br = jnp.pad(b_router[0].reshape(1, -1), ((0, 0), (0, LANES - n_experts)), constant_values=-1e30)
    h, hn, eidx, rank, gate, cnt = _route_call(merged, x, w_out[0].astype(BF16),
                                               norm_ffn_w[0].reshape(1, -1), wr1, wr2, br)

    counts = cnt[0, :n_experts]
    padded = (counts + MOE_ROWS - 1) // MOE_ROWS * MOE_ROWS
    pad_ends = jnp.cumsum(padded)
    pad_starts = pad_ends - padded
    n_blocks = -(-(t * TOP_K) // MOE_ROWS) + n_experts
    n_rows = n_blocks * MOE_ROWS
    e_flat = eidx[:, :TOP_K].reshape(-1)
    dest = (pad_starts[e_flat] + rank[:, :TOP_K].reshape(-1)).astype(jnp.int32)
    block_start = jnp.arange(n_blocks, dtype=jnp.int32) * MOE_ROWS
    block_e = jnp.minimum(jnp.sum(block_start[:, None] >= pad_ends[None, :], axis=1),
                          n_experts - 1).astype(jnp.int32)
    block_nv = jnp.clip(pad_starts[block_e] + counts[block_e] - block_start, 0,
                        MOE_ROWS).astype(jnp.int32)
    nused = (pad_ends[-1:] // MOE_ROWS).astype(jnp.int32)

    xs = _dispatch_call(dest, counts, pad_starts.astype(jnp.int32), hn, n_rows)
    eo = _moe_call(block_e, block_nv, nused, xs, w_up[0], b_up[0].reshape(n_experts, 1, -1),
                   w_down[0], b_down[0].reshape(n_experts, 1, -1))
    y = _combine_call(dest, h, gate, norm_final_w.reshape(1, -1), eo)

    y_prompt = y[:t_prompt].reshape(batch, seq_len, d_model)
    y_sample = y[t_prompt:].reshape(dec_batch, dec_seq, d_model)
    ssm_out = ssm_new.reshape(n_seq, N_GROUPS, d_state, n_heads // N_GROUPS, head_dim)
    ssm_out = ssm_out.transpose(0, 1, 3, 4, 2).reshape(n_seq, n_heads, head_dim, d_state)
    seq_ends = [t_prompt] * batch + [t_prompt + (s + 1) * dec_seq for s in range(dec_batch)]
    tail = jnp.stack([p[:, e - POOL_BUF:e] for e in seq_ends], axis=0)
    tail = tail.transpose(0, 2, 1, 3)
    ctail = tail[:, POOL_BUF - (CONV_WIDTH - 1):]
    conv_x = ctail[:, :, P_X:P_BC].reshape(n_seq, CONV_WIDTH - 1, d_inner)
    conv_bc = ctail[:, :, P_BC:P_POOL].reshape(n_seq, CONV_WIDTH - 1, d_bc)
    conv_new = jnp.concatenate([conv_x, _ungroup_bc(conv_bc)], axis=-1)
    pool_new = tail[:, :, P_POOL:P_GATE_SSD].reshape(n_seq, POOL_BUF, d_pool)
    return (y_prompt, y_sample,
            ssm_out[None, :batch], conv_new[None, :batch], pool_new[None, :batch],
            ssm_out[None, batch:], conv_new[None, batch:], pool_new[None, batch:])
```

```python
import functools

import numpy as np
import jax
import jax.numpy as jnp
from jax import lax
from jax.experimental import pallas as pl
from jax.experimental.pallas import tpu as pltpu

F32 = jnp.float32
BF16 = jnp.bfloat16

CHUNK = 64
HEAD_DIM = 64
N_GROUPS = 8
D_STATE = 128
CONV_WIDTH = 4
POOL_WINDOWS = (2, 4, 8, 16)
POOL_BUF = 15
PAST_LEN = 4096
TOP_K = 4
SWIGLU_ALPHA = 1.702
SWIGLU_LIMIT = 7.0
EPS = 1e-5
LANES = 128
COL_BLOCK = 512
MOE_ROWS = 1024
MOE_HALF = MOE_ROWS // 2
MOE_FF_TILE = 256
P_Z, P_X, P_BC, P_POOL, P_GATE_SSD, P_GATE_POOL = 0, 8, 16, 20, 24, 28
VMEM_LIMIT = 56 * 1024 * 1024


def _pick_tile(n, candidates):
    for c in candidates:
        if n % c == 0:
            return c
    raise ValueError(f"no tile for {n} in {candidates}")


def _params(sem, vmem=VMEM_LIMIT):
    return pltpu.CompilerParams(dimension_semantics=sem, vmem_limit_bytes=vmem)


def _split3(v):
    p1 = v.astype(BF16)
    r1 = v - p1.astype(F32)
    p2 = r1.astype(BF16)
    p3 = (r1 - p2.astype(F32)).astype(BF16)
    return p1, p2, p3


def _hi_lo(v):
    hi = v.astype(BF16)
    return hi, (v - hi.astype(F32)).astype(BF16)


def _dot_hi_lo(a1, a2, b1_ref, b2_ref):
    return (jnp.dot(a1, b1_ref[...], preferred_element_type=F32)
            + jnp.dot(a1, b2_ref[...], preferred_element_type=F32)
            + jnp.dot(a2, b1_ref[...], preferred_element_type=F32))


def _silu(v):
    return v * jax.nn.sigmoid(v)


def _two_part_specs(tm, d, n_first):
    return [pl.BlockSpec((tm, d), lambda i, *_: (jnp.minimum(i, n_first - 1), 0)),
            pl.BlockSpec((tm, d), lambda i, *_: (jnp.maximum(i - n_first, 0), 0))]


def _norm_kernel(xp_ref, xs_ref, w_ref, wdt1_ref, wdt2_ref, u_ref, dt_ref, *, n_first):
    x = jnp.where(pl.program_id(0) < n_first, xp_ref[...], xs_ref[...])
    r = lax.rsqrt(jnp.mean(x * x, axis=-1, keepdims=True) + EPS)
    u = x * r * w_ref[...]
    u1, u2 = _hi_lo(u)
    u_ref[...] = u1
    dt_ref[...] = _dot_hi_lo(u1, u2, wdt1_ref, wdt2_ref)


def _norm_call(xp, xs, w, wdt1, wdt2):
    d = xp.shape[1]
    t = xp.shape[0] + xs.shape[0]
    tm = _pick_tile(np.gcd(xp.shape[0], xs.shape[0]), (512, 256, 128, 64))
    n_first = xp.shape[0] // tm
    return pl.pallas_call(
        functools.partial(_norm_kernel, n_first=n_first),
        name="k_norm",
        grid=(t // tm,),
        in_specs=_two_part_specs(tm, d, n_first) + [
            pl.BlockSpec((1, d), lambda i: (0, 0)),
            pl.BlockSpec((d, LANES), lambda i: (0, 0)),
            pl.BlockSpec((d, LANES), lambda i: (0, 0))],
        out_specs=[pl.BlockSpec((tm, d), lambda i: (i, 0)),
                   pl.BlockSpec((tm, LANES), lambda i: (i, 0))],
        out_shape=[jax.ShapeDtypeStruct((t, d), BF16), jax.ShapeDtypeStruct((t, LANES), F32)],
        compiler_params=_params(("parallel",)),
    )(xp, xs, w, wdt1, wdt2)


def _inproj_kernel(u_ref, w_ref, o_ref):
    r = jnp.dot(u_ref[...], w_ref[...], preferred_element_type=F32)
    for k in range(o_ref.shape[0]):
        o_ref[k] = r[:, k * COL_BLOCK:(k + 1) * COL_BLOCK]


def _inproj_call(u, w):
    t, d = u.shape
    n = w.shape[1]
    tm = _pick_tile(t, (1024, 512, 256, 128, 64))
    nb = n // COL_BLOCK
    per = 2
    return pl.pallas_call(
        _inproj_kernel,
        name="k_inproj",
        grid=(t // tm, nb // per),
        in_specs=[pl.BlockSpec((tm, d), lambda i, j: (i, 0)),
                  pl.BlockSpec((d, per * COL_BLOCK), lambda i, j: (0, j))],
        out_specs=pl.BlockSpec((per, tm, COL_BLOCK), lambda i, j: (j, i, 0)),
        out_shape=jax.ShapeDtypeStruct((nb, t, COL_BLOCK), F32),
        compiler_params=_params(("parallel", "arbitrary")),
    )(u, w)


def _pool_chunk(c, first, pu_ref, pool0_ref, wp_ref, scale_ref, yp_ref, pbuf, n_prompt_chunks):
    hist = POOL_BUF + 1

    @pl.when(first)
    def _():
        pbuf[0:hist, :] = pool0_ref[...]

    @pl.when(jnp.logical_not(first))
    def _():
        pbuf[0:hist, :] = pbuf[CHUNK:CHUNK + hist, :]

    for g in range(len(POOL_WINDOWS)):
        pbuf[hist:hist + CHUNK, g * COL_BLOCK:(g + 1) * COL_BLOCK] = pu_ref[g]

    pos0 = jnp.where(c < n_prompt_chunks, c * CHUNK, PAST_LEN)
    pos = (pos0 + lax.broadcasted_iota(jnp.int32, (CHUNK, 1), 0)).astype(F32)
    for g, win in enumerate(POOL_WINDOWS):
        sl = slice(g * COL_BLOCK, (g + 1) * COL_BLOCK)
        cur = pbuf[hist:hist + CHUNK, sl]
        tot = cur
        for i in range(1, win):
            tot = tot + pbuf[hist - i:hist - i + CHUNK, sl]
        count = jnp.minimum(pos + 1.0, float(win))
        pooled = tot / count - cur
        yp_ref[:, sl] = jnp.dot(pooled.astype(BF16), wp_ref[g],
                                preferred_element_type=F32) * scale_ref[:, sl]


def _mixer_kernel(z_ref, x_ref, bc_ref, pu_ref, dt_ref, ssm0_ref, cx0_ref, cbc0_ref, pool0_ref,
                  dtb_ref, alog_ref, dsk_ref, cwx_ref, cbx_ref, cwbc_ref, cbbc_ref, wp_ref, scale_ref,
                  tril3_ref, e3_ref, diag_ref, caus_ref, bd_ref,
                  yg_ref, yp_ref, ssm_ref, st, bufx, bufbc, exs, pbuf, *, n_prompt_chunks):
    c = pl.program_id(0)
    first = jnp.logical_or(c == 0, c >= n_prompt_chunks)
    last = c >= n_prompt_chunks - 1
    pairs = N_GROUPS * COL_BLOCK // LANES

    @pl.when(c == 0)
    def _():
        st[...] = jnp.zeros_like(st)

    @pl.when(c >= n_prompt_chunks)
    def _():
        for q in range(pairs):
            blk = jnp.concatenate([ssm0_ref[2 * q], ssm0_ref[2 * q + 1]], axis=0)
            g, o = divmod(q * LANES, COL_BLOCK)
            st[g, :, o:o + LANES] = blk.T

    @pl.when(first)
    def _():
        bufx[0:8, :] = cx0_ref[...]
        bufbc[0:8, :] = cbc0_ref[...]

    @pl.when(jnp.logical_not(first))
    def _():
        bufx[0:8, :] = bufx[CHUNK:CHUNK + 8, :]
        bufbc[0:8, :] = bufbc[CHUNK:CHUNK + 8, :]

    for g in range(N_GROUPS):
        bufx[8:8 + CHUNK, g * COL_BLOCK:(g + 1) * COL_BLOCK] = x_ref[g]
    for q in range(N_GROUPS // 2):
        bufbc[8:8 + CHUNK, q * COL_BLOCK:(q + 1) * COL_BLOCK] = bc_ref[q]

    dtv = dt_ref[...] + dtb_ref[...]
    dt = jnp.maximum(dtv, 0.0) + jnp.log1p(jnp.exp(-jnp.abs(dtv)))
    d_a = dt * (-jnp.exp(alog_ref[...]))
    p1, p2, p3 = _split3(d_a)
    acum = jnp.dot(tril3_ref[...], jnp.concatenate([p1, p2, p3], axis=0),
                   preferred_element_type=F32)
    q1, q2, q3 = _split3(jnp.concatenate([acum, dt], axis=0))
    exs[...] = jnp.dot(jnp.concatenate([q1, q2, q3], axis=1), e3_ref[...],
                       preferred_element_type=F32)

    caus = caus_ref[...] > 0.0
    bd = bd_ref[...]
    hw = 4 * HEAD_DIM
    for g in range(N_GROUPS):
        sl = slice(g * COL_BLOCK, (g + 1) * COL_BLOCK)
        slb = slice(g * 2 * D_STATE, (g + 1) * 2 * D_STATE)
        xc = cbx_ref[:, sl]
        bcc = cbbc_ref[:, slb]
        for k in range(CONV_WIDTH):
            xc = xc + cwx_ref[k:k + 1, sl] * bufx[5 + k:5 + k + CHUNK, sl]
            bcc = bcc + cwbc_ref[k:k + 1, slb] * bufbc[5 + k:5 + k + CHUNK, slb]
        xs = _silu(xc)
        bca = _silu(bcc)
        b_g = bca[:, :D_STATE]
        c_g = bca[:, D_STATE:]
        acx = exs[0:CHUNK, sl]
        dtx = exs[CHUNK:2 * CHUNK, sl]
        alast = acx[CHUNK - 1:CHUNK, :]
        arow = jnp.sum(acx * diag_ref[:, sl], axis=0, keepdims=True)
        xdt = xs * dtx
        xdtb = xdt.astype(BF16)
        bb = b_g.astype(BF16)
        cb = c_g.astype(BF16)
        cb2 = lax.dot_general(cb, jnp.concatenate([bb, bb], axis=0),
                              (((1,), (1,)), ((), ())), preferred_element_type=F32)
        ydiag = []
        for q in range(2):
            lhs = []
            for d in range(2):
                lo = q * hw + d * LANES
                seg = acx[:, lo:lo + LANES] - arow[:, lo:lo + LANES]
                lhs.append((cb2 * jnp.exp(jnp.where(caus, seg, -jnp.inf))).astype(BF16))
            xq = xdtb[:, q * hw:(q + 1) * hw]
            wq = jnp.concatenate([xq, xq, xq, xq], axis=0) * bd
            ydiag.append(jnp.dot(jnp.concatenate(lhs, axis=1), wq, preferred_element_type=F32))
        s_old = st[g]
        yoff = jnp.dot(cb, s_old.astype(BF16), preferred_element_type=F32) * jnp.exp(acx)
        y = jnp.concatenate(ydiag, axis=1) + yoff + dsk_ref[:, sl] * xs
        yg_ref[:, sl] = y * _silu(z_ref[g])
        v = (xdt * jnp.exp(alast - acx)).astype(BF16)
        st[g] = jnp.exp(alast) * s_old + jnp.dot(b_g.T.astype(BF16), v,
                                                 preferred_element_type=F32)

    @pl.when(last)
    def _():
        for q in range(pairs):
            g, o = divmod(q * LANES, COL_BLOCK)
            blk = st[g, :, o:o + LANES].T
            ssm_ref[2 * q] = blk[:HEAD_DIM]
            ssm_ref[2 * q + 1] = blk[HEAD_DIM:]

    _pool_chunk(c, first, pu_ref, pool0_ref, wp_ref, scale_ref, yp_ref, pbuf, n_prompt_chunks)


def _mixer_call(p, dt_raw, ssm_in, cx0, cbc0, pool0, dtb, alog, dsk, cwx, cbx, cwbc, cbbc, wp, scale,
                consts, n_prompt_chunks):
    _, t, _ = p.shape
    n_chunks = t // CHUNK
    n_seq = cx0.shape[0]
    n_heads, head_dim, d_state = ssm_in.shape[1:]
    d_inner = N_GROUPS * COL_BLOCK
    d_bc = N_GROUPS * 2 * D_STATE
    ng = len(POOL_WINDOWS)
    d_pool = ng * COL_BLOCK
    hist = POOL_BUF + 1
    tril3, e3, diag, caus, bd = consts

    def seq(c):
        return jnp.maximum(c - (n_prompt_chunks - 1), 0)

    def const(a):
        return pl.BlockSpec(a.shape, lambda c: (0,) * a.ndim)

    state_block = (None, n_heads, head_dim, d_state)
    return pl.pallas_call(
        functools.partial(_mixer_kernel, n_prompt_chunks=n_prompt_chunks),
        name="k_mixer",
        grid=(n_chunks,),
        in_specs=[pl.BlockSpec((N_GROUPS, CHUNK, COL_BLOCK), lambda c: (P_Z // N_GROUPS, c, 0)),
                  pl.BlockSpec((N_GROUPS, CHUNK, COL_BLOCK), lambda c: (P_X // N_GROUPS, c, 0)),
                  pl.BlockSpec((N_GROUPS // 2, CHUNK, COL_BLOCK),
                               lambda c: (P_BC // (N_GROUPS // 2), c, 0)),
                  pl.BlockSpec((ng, CHUNK, COL_BLOCK), lambda c: (P_POOL // ng, c, 0)),
                  pl.BlockSpec((CHUNK, LANES), lambda c: (c, 0)),
                  pl.BlockSpec(state_block, lambda c: (jnp.maximum(c - n_prompt_chunks, 0), 0, 0, 0)),
                  pl.BlockSpec((None, 8, d_inner), lambda c: (seq(c), 0, 0)),
                  pl.BlockSpec((None, 8, d_bc), lambda c: (seq(c), 0, 0)),
                  pl.BlockSpec((None, hist, d_pool), lambda c: (seq(c), 0, 0)),
                  const(dtb), const(alog), const(dsk), const(cwx), const(cbx),
                  const(cwbc), const(cbbc), const(wp), const(scale),
                  const(tril3), const(e3), const(diag), const(caus), const(bd)],
        out_specs=[pl.BlockSpec((CHUNK, d_inner), lambda c: (c, 0)),
                   pl.BlockSpec((CHUNK, d_pool), lambda c: (c, 0)),
                   pl.BlockSpec(state_block, lambda c: (seq(c), 0, 0, 0))],
        out_shape=[jax.ShapeDtypeStruct((t, d_inner), F32),
                   jax.ShapeDtypeStruct((t, d_pool), F32),
                   jax.ShapeDtypeStruct((n_seq, n_heads, head_dim, d_state), F32)],
        scratch_shapes=[pltpu.VMEM((N_GROUPS, D_STATE, COL_BLOCK), F32),
                        pltpu.VMEM((CHUNK + 8, d_inner), F32),
                        pltpu.VMEM((CHUNK + 8, d_bc), F32),
                        pltpu.VMEM((2 * CHUNK, d_inner), F32),
                        pltpu.VMEM((CHUNK + hist, d_pool), F32)],
        compiler_params=_params(("arbitrary",)),
    )(p, p, p, p, dt_raw, ssm_in, cx0, cbc0, pool0, dtb, alog, dsk, cwx, cbx, cwbc, cbbc, wp, scale,
      tril3, e3, diag, caus, bd)


def _ssd_consts():
    l = np.arange(CHUNK)
    tril = (l[:, None] >= l[None, :]).astype(np.float32)
    tril3 = np.concatenate([tril, tril, tril], axis=1)
    n_heads = N_GROUPS * COL_BLOCK // HEAD_DIM
    col_head = np.arange(n_heads * HEAD_DIM) // HEAD_DIM
    col_pos = np.arange(n_heads * HEAD_DIM) % HEAD_DIM
    e = (np.arange(LANES)[:, None] == col_head[None, :]).astype(np.float32)
    e3 = np.concatenate([e, e, e], axis=0)
    diag = (l[:, None] == col_pos[None, :]).astype(np.float32)
    caus = np.concatenate([tril, tril], axis=1)
    r = np.arange(4 * HEAD_DIM)
    bd = (r[:, None] // HEAD_DIM == r[None, :] // HEAD_DIM).astype(np.float32)
    return (jnp.asarray(tril3, BF16), jnp.asarray(e3, BF16), jnp.asarray(diag, F32),
            jnp.asarray(caus, F32), jnp.asarray(bd, BF16))


def _merge_kernel(yg_ref, yp_ref, gs_ref, gp_ref, nw_ref, wps_ref, wpp_ref, o_ref):
    y = yg_ref[...]
    r = lax.rsqrt(jnp.mean(y * y, axis=-1, keepdims=True) + EPS)
    yn = (y * r * nw_ref[...]).astype(BF16)
    a = jnp.dot(yn, wps_ref[...], preferred_element_type=F32)
    b = jnp.dot(yp_ref[...].astype(BF16), wpp_ref[...], preferred_element_type=F32)
    for j in range(gs_ref.shape[0]):
        sl = slice(j * COL_BLOCK, (j + 1) * COL_BLOCK)
        o_ref[:, sl] = (jax.nn.sigmoid(gs_ref[j]) * a[:, sl]
                        + jax.nn.sigmoid(gp_ref[j]) * b[:, sl]).astype(BF16)


def _merge_call(yg, yp, p, nw, wps, wpp):
    t, d_inner = yg.shape
    d_pool = yp.shape[1]
    d_model = wps.shape[1]
    tm = _pick_tile(t, (256, 128, 64))
    nj = d_model // COL_BLOCK
    once = pl.Buffered(1)
    return pl.pallas_call(
        _merge_kernel,
        name="k_merge",
        grid=(t // tm,),
        in_specs=[pl.BlockSpec((tm, d_inner), lambda i: (i, 0)),
                  pl.BlockSpec((tm, d_pool), lambda i: (i, 0)),
                  pl.BlockSpec((nj, tm, COL_BLOCK), lambda i: (P_GATE_SSD // nj, i, 0)),
                  pl.BlockSpec((nj, tm, COL_BLOCK), lambda i: (P_GATE_POOL // nj, i, 0)),
                  pl.BlockSpec((1, d_inner), lambda i: (0, 0)),
                  pl.BlockSpec((d_inner, d_model), lambda i: (0, 0), pipeline_mode=once),
                  pl.BlockSpec((d_pool, d_model), lambda i: (0, 0), pipeline_mode=once)],
        out_specs=pl.BlockSpec((tm, d_model), lambda i: (i, 0)),
        out_shape=jax.ShapeDtypeStruct((t, d_model), BF16),
        compiler_params=_params(("parallel",)),
    )(yg, yp, p, p, nw, wps, wpp)


def _route_kernel(m_ref, xp_ref, xs_ref, wo_ref, nw_ref, wr1_ref, wr2_ref, br_ref, trs_ref,
                  h_ref, hn_ref, eidx_ref, rank_ref, gate_ref, cnt_ref, carry, *, n_first):
    i = pl.program_id(0)

    @pl.when(i == 0)
    def _():
        carry[...] = jnp.zeros_like(carry)

    x = jnp.where(i < n_first, xp_ref[...], xs_ref[...])
    h = x + jnp.dot(m_ref[...], wo_ref[...], preferred_element_type=F32)
    h_ref[...] = h
    r = lax.rsqrt(jnp.mean(h * h, axis=-1, keepdims=True) + EPS)
    hn = h * r * nw_ref[...]
    hn_ref[...] = hn
    h1, h2 = _hi_lo(hn)
    logits = _dot_hi_lo(h1, h2, wr1_ref, wr2_ref) + br_ref[...]
    lane = lax.broadcasted_iota(jnp.int32, logits.shape, 1)
    work = logits
    member = jnp.zeros(logits.shape, F32)
    vals, idxs = [], []
    for _ in range(TOP_K):
        m = jnp.max(work, axis=-1, keepdims=True)
        idx = jnp.min(jnp.where(work == m, lane, LANES), axis=-1, keepdims=True)
        hit = lane == idx
        member = member + hit.astype(F32)
        work = jnp.where(hit, -jnp.inf, work)
        vals.append(m)
        idxs.append(idx)
    ex = [jnp.exp(v - vals[0]) for v in vals]
    den = ex[0] + ex[1] + ex[2] + ex[3]
    before = jnp.dot(trs_ref[...], member.astype(BF16), preferred_element_type=F32) + carry[0:1, :]
    eidx = jnp.zeros(logits.shape, jnp.int32)
    rank = jnp.zeros(logits.shape, jnp.int32)
    gate = jnp.zeros(logits.shape, F32)
    for k in range(TOP_K):
        rk = jnp.sum(jnp.where(lane == idxs[k], before, 0.0), axis=-1, keepdims=True)
        eidx = jnp.where(lane == k, idxs[k], eidx)
        rank = jnp.where(lane == k, rk.astype(jnp.int32), rank)
        gate = jnp.where(lane == k, ex[k] / den, gate)
    eidx_ref[...] = eidx
    rank_ref[...] = rank
    gate_ref[...] = gate
    carry[0:1, :] = carry[0:1, :] + jnp.sum(member, axis=0, keepdims=True)
    cnt_ref[...] = carry[...].astype(jnp.int32)


def _route_call(merged, xp, xs, wo, nw, wr1, wr2, br):
    t, d = merged.shape
    tm = _pick_tile(np.gcd(xp.shape[0], xs.shape[0]), (512, 256, 128, 64))
    n_first = xp.shape[0] // tm
    ri = np.arange(tm)
    trs = jnp.asarray((ri[:, None] > ri[None, :]).astype(np.float32), BF16)
    row = lambda i: (i, 0)
    fix = lambda i: (0, 0)
    return pl.pallas_call(
        functools.partial(_route_kernel, n_first=n_first),
        name="k_route",
        grid=(t // tm,),
        in_specs=[pl.BlockSpec((tm, d), row)] + _two_part_specs(tm, d, n_first) + [
                  pl.BlockSpec((d, d), fix), pl.BlockSpec((1, d), fix),
                  pl.BlockSpec((d, LANES), fix), pl.BlockSpec((d, LANES), fix),
                  pl.BlockSpec((1, LANES), fix), pl.BlockSpec((tm, tm), fix)],
        out_specs=[pl.BlockSpec((tm, d), row), pl.BlockSpec((tm, d), row),
                   pl.BlockSpec((tm, LANES), row), pl.BlockSpec((tm, LANES), row),
                   pl.BlockSpec((tm, LANES), row), pl.BlockSpec((8, LANES), fix)],
        out_shape=[jax.ShapeDtypeStruct((t, d), F32), jax.ShapeDtypeStruct((t, d), F32),
                   jax.ShapeDtypeStruct((t, LANES), jnp.int32),
                   jax.ShapeDtypeStruct((t, LANES), jnp.int32),
                   jax.ShapeDtypeStruct((t, LANES), F32),
                   jax.ShapeDtypeStruct((8, LANES), jnp.int32)],
        scratch_shapes=[pltpu.VMEM((8, LANES), F32)],
        compiler_params=_params(("arbitrary",)),
    )(merged, xp, xs, wo, nw, wr1, wr2, br, trs)


_PAD_CHUNKS = tuple(1 << s for s in range(MOE_ROWS.bit_length() - 2, 2, -1))
_PAD_SINGLE = 7


def _dispatch_kernel(dest_ref, cnt_ref, pstart_ref, x_ref, o_hbm, zbuf, sem, zsem, *, rows, n_experts):
    i = pl.program_id(0)

    def issue(t, carry):
        for k in range(TOP_K):
            d = dest_ref[(i * rows + t) * TOP_K + k]
            pltpu.make_async_copy(x_ref.at[t], o_hbm.at[d], sem).start()
        return carry

    lax.fori_loop(0, rows, issue, 0, unroll=4)
    for k in range(TOP_K):
        pltpu.make_async_copy(x_ref, o_hbm.at[pl.ds(0, rows)], sem).wait()

    @pl.when(i == pl.num_programs(0) - 1)
    def _():
        zbuf[...] = jnp.zeros_like(zbuf)

        def pad_copies(e, wait):
            cnt = cnt_ref[e]
            start = pstart_ref[e] + cnt
            npad = (-cnt) & (MOE_ROWS - 1)
            head = npad & _PAD_SINGLE
            for r in range(_PAD_SINGLE):
                cp = pltpu.make_async_copy(zbuf.at[0], o_hbm.at[start + r], zsem)
                pl.when(r < head)(cp.wait if wait else cp.start)
            off = start + head
            body = npad - head
            for rows_c in _PAD_CHUNKS:
                cp = pltpu.make_async_copy(zbuf.at[pl.ds(0, rows_c)],
                                           o_hbm.at[pl.ds(pl.multiple_of(off, 8), rows_c)], zsem)
                pl.when((body & rows_c) != 0)(cp.wait if wait else cp.start)
                off = off + (body & rows_c)

        def start_e(e, carry):
            pad_copies(e, False)
            return carry

        def wait_e(e, carry):
            pad_copies(e, True)
            return carry

        lax.fori_loop(0, n_experts, start_e, 0)
        lax.fori_loop(0, n_experts, wait_e, 0)

        zrows = zbuf.shape[0]
        first = (pstart_ref[n_experts - 1] + cnt_ref[n_experts - 1] + MOE_ROWS - 1) // MOE_ROWS
        first = first * (MOE_ROWS // zrows)

        def tail_copy(c):
            return pltpu.make_async_copy(
                zbuf, o_hbm.at[pl.ds(pl.multiple_of(c * zrows, zrows), zrows)], zsem)

        def start_t(c, carry):
            tail_copy(c).start()
            return carry

        def wait_t(c, carry):
            tail_copy(c).wait()
            return carry

        lax.fori_loop(first, o_hbm.shape[0] // zrows, start_t, 0)
        lax.fori_loop(first, o_hbm.shape[0] // zrows, wait_t, 0)


def _dispatch_call(dest, counts, pad_starts, src, n_rows):
    t, d = src.shape
    rows = _pick_tile(t, (256, 128, 64))
    return pl.pallas_call(
        functools.partial(_dispatch_kernel, rows=rows, n_experts=counts.shape[0]),
        name="k_dispatch",
        grid_spec=pltpu.PrefetchScalarGridSpec(
            num_scalar_prefetch=3,
            grid=(t // rows,),
            in_specs=[pl.BlockSpec((rows, d), lambda i, de, cn, ps: (i, 0))],
            out_specs=pl.BlockSpec(memory_space=pl.ANY),
            scratch_shapes=[pltpu.VMEM((_PAD_CHUNKS[0], d), src.dtype),
                            pltpu.SemaphoreType.DMA, pltpu.SemaphoreType.DMA]),
        out_shape=jax.ShapeDtypeStruct((n_rows, d), src.dtype),
        compiler_params=_params(("arbitrary",)),
    )(dest, counts, pad_starts, src)


def _moe_kernel(be_ref, nv_ref, nused_ref, x_ref, wg_ref, wl_ref, bg_ref, bl_ref, wd_ref, bd_ref,
                o_ref, xbf, wgb, wlb, wdb):
    b = pl.program_id(0)
    j = pl.program_id(1)
    live = b < nused_ref[0]
    both = jnp.logical_and(live, nv_ref[b] > MOE_HALF)

    @pl.when(jnp.logical_and(j == 0, live))
    def _():
        xbf[...] = x_ref[...].astype(BF16)
        o_ref[...] = jnp.broadcast_to(bd_ref[...], o_ref.shape)

    @pl.when(jnp.logical_and(j == 0, jnp.logical_not(live)))
    def _():
        o_ref[...] = jnp.zeros_like(o_ref)

    def half(lo):
        x = xbf[lo:lo + MOE_HALF, :]
        glu = jnp.dot(x, wgb[...], preferred_element_type=F32) + bg_ref[...]
        lin = jnp.dot(x, wlb[...], preferred_element_type=F32) + bl_ref[...]
        glu = jnp.minimum(glu, SWIGLU_LIMIT)
        lin = jnp.clip(lin, -SWIGLU_LIMIT, SWIGLU_LIMIT)
        act = glu * jax.nn.sigmoid(SWIGLU_ALPHA * glu) * (lin + 1.0)
        o_ref[lo:lo + MOE_HALF, :] += jnp.dot(act.astype(BF16), wdb[...],
                                              preferred_element_type=F32)

    @pl.when(live)
    def _():
        wgb[...] = wg_ref[...].astype(BF16)
        wlb[...] = wl_ref[...].astype(BF16)
        wdb[...] = wd_ref[...].astype(BF16)
        half(0)

    @pl.when(both)
    def _():
        half(MOE_HALF)


def _moe_call(block_e, block_nv, nused, xs, w_up, b_up, w_down, b_down):
    n_rows, d = xs.shape
    d_ff = w_down.shape[1]
    tf = MOE_FF_TILE
    nj = d_ff // tf
    nb = n_rows // MOE_ROWS

    def bb(b, nu):
        return jnp.minimum(b, nu[0] - 1)

    def jj(b, j, nu):
        return jnp.where(b < nu[0], j, nj - 1)

    return pl.pallas_call(
        _moe_kernel,
        name="k_moe",
        grid_spec=pltpu.PrefetchScalarGridSpec(
            num_scalar_prefetch=3,
            grid=(nb, nj),
            in_specs=[pl.BlockSpec((MOE_ROWS, d), lambda b, j, be, nv, nu: (bb(b, nu), 0)),
                      pl.BlockSpec((None, d, tf),
                                   lambda b, j, be, nv, nu: (be[bb(b, nu)], 0, jj(b, j, nu))),
                      pl.BlockSpec((None, d, tf),
                                   lambda b, j, be, nv, nu: (be[bb(b, nu)], 0, nj + jj(b, j, nu))),
                      pl.BlockSpec((None, 1, tf),
                                   lambda b, j, be, nv, nu: (be[bb(b, nu)], 0, jj(b, j, nu))),
                      pl.BlockSpec((None, 1, tf),
                                   lambda b, j, be, nv, nu: (be[bb(b, nu)], 0, nj + jj(b, j, nu))),
                      pl.BlockSpec((None, tf, d),
                                   lambda b, j, be, nv, nu: (be[bb(b, nu)], jj(b, j, nu), 0)),
                      pl.BlockSpec((None, 1, d), lambda b, j, be, nv, nu: (be[bb(b, nu)], 0, 0))],
            out_specs=pl.BlockSpec((MOE_ROWS, d), lambda b, j, be, nv, nu: (b, 0)),
            scratch_shapes=[pltpu.VMEM((MOE_ROWS, d), BF16), pltpu.VMEM((d, tf), BF16),
                            pltpu.VMEM((d, tf), BF16), pltpu.VMEM((tf, d), BF16)]),
        out_shape=jax.ShapeDtypeStruct((n_rows, d), F32),
        compiler_params=_params(("arbitrary", "arbitrary")),
    )(block_e, block_nv, nused, xs, w_up, w_up, b_up, b_up, w_down, b_down)


def _combine_kernel(dest_ref, h_ref, gate_ref, nw_ref, eo_hbm, yp_ref, ys_ref, gbuf, sem,
                    *, rows, n_first):
    i = pl.program_id(0)

    def issue(t, carry):
        for k in range(TOP_K):
            d = dest_ref[(i * rows + t) * TOP_K + k]
            pltpu.make_async_copy(eo_hbm.at[d], gbuf.at[k, t], sem).start()
        return carry

    lax.fori_loop(0, rows, issue, 0, unroll=4)
    for k in range(TOP_K):
        pltpu.make_async_copy(eo_hbm.at[pl.ds(0, rows)], gbuf.at[k], sem).wait()
    gate = gate_ref[...]
    y = h_ref[...]
    for k in range(TOP_K):
        y = y + gate[:, k:k + 1] * gbuf[k]
    r = lax.rsqrt(jnp.mean(y * y, axis=-1, keepdims=True) + EPS)
    y = y * r * nw_ref[...]

    @pl.when(i < n_first)
    def _():
        yp_ref[...] = y

    @pl.when(i >= n_first)
    def _():
        ys_ref[...] = y


def _combine_call(dest, h, gate, nw, eo, t_first):
    t, d = h.shape
    rows = _pick_tile(np.gcd(t_first, t - t_first), (128, 64))
    n_first = t_first // rows
    return pl.pallas_call(
        functools.partial(_combine_kernel, rows=rows, n_first=n_first),
        name="k_combine",
        grid_spec=pltpu.PrefetchScalarGridSpec(
            num_scalar_prefetch=1,
            grid=(t // rows,),
            in_specs=[pl.BlockSpec((rows, d), lambda i, de: (i, 0)),
                      pl.BlockSpec((rows, LANES), lambda i, de: (i, 0)),
                      pl.BlockSpec((1, d), lambda i, de: (0, 0)),
                      pl.BlockSpec(memory_space=pl.ANY)],
            out_specs=_two_part_specs(rows, d, n_first),
            scratch_shapes=[pltpu.VMEM((TOP_K, rows, d), F32), pltpu.SemaphoreType.DMA]),
        out_shape=[jax.ShapeDtypeStruct((t_first, d), F32),
                   jax.ShapeDtypeStruct((t - t_first, d), F32)],
        compiler_params=_params(("arbitrary",)),
    )(dest, h, gate, nw, eo)


def _group_bc(a):
    lead = a.shape[:-1]
    return a.reshape(lead + (2, N_GROUPS, D_STATE)).swapaxes(-3, -2).reshape(lead + (2 * N_GROUPS * D_STATE,))


def _ungroup_bc(a):
    lead = a.shape[:-1]
    return a.reshape(lead + (N_GROUPS, 2, D_STATE)).swapaxes(-3, -2).reshape(lead + (2 * N_GROUPS * D_STATE,))


def kernel(x_prompt, x_sample, state_ssm, state_conv, state_pool, norm_mix_w, w_in, conv_w, conv_b,
           dt_bias, a_log, d_skip, ssd_norm_w, w_pool, pool_scale, w_proj_ssd, w_proj_pool, w_out,
           norm_ffn_w, w_router, b_router, w_up, b_up, w_down, b_down, norm_final_w):
    batch, seq_len, d_model = x_prompt.shape
    dec_batch, dec_seq, _ = x_sample.shape
    depth, _, n_heads, head_dim, d_state = state_ssm.shape
    assert depth == 1 and batch == 1 and dec_seq == CHUNK and seq_len % CHUNK == 0
    assert head_dim == HEAD_DIM and d_state == D_STATE
    d_inner = n_heads * head_dim
    assert d_inner == N_GROUPS * COL_BLOCK
    d_bc = 2 * N_GROUPS * D_STATE
    d_pool = state_pool.shape[-1]
    assert d_pool == len(POOL_WINDOWS) * COL_BLOCK and state_pool.shape[-2] == POOL_BUF
    n_experts = w_router.shape[-1]
    n_prompt_chunks = seq_len // CHUNK
    n_seq = batch + dec_batch
    t_prompt = batch * seq_len
    t = t_prompt + dec_batch * dec_seq

    xp = x_prompt.reshape(t_prompt, d_model)
    xs_tok = x_sample.reshape(-1, d_model)

    wi = w_in[0]
    z_end = d_inner
    xbc_end = z_end + d_inner + d_bc
    dt_end = xbc_end + n_heads
    w_all = jnp.concatenate(
        [wi[:, :z_end + d_inner], _group_bc(wi[:, z_end + d_inner:xbc_end]), wi[:, dt_end:]],
        axis=1).astype(BF16)
    w_dt1, w_dt2 = _hi_lo(jnp.pad(wi[:, xbc_end:dt_end], ((0, 0), (0, LANES - n_heads))))
    pad_h = lambda a: jnp.pad(a.reshape(1, n_heads), ((0, 0), (0, LANES - n_heads)))
    cw = conv_w[0]
    cwx, cwbc = cw[:, :d_inner], _group_bc(cw[:, d_inner:])
    cb = conv_b[0].reshape(1, -1)
    cbx, cbbc = cb[:, :d_inner], _group_bc(cb[:, d_inner:])
    dsk = jnp.repeat(d_skip[0], head_dim).reshape(1, d_inner)

    conv0 = jnp.pad(state_conv[0], ((batch, 0), (8 - (CONV_WIDTH - 1), 0), (0, 0)))
    cx0, cbc0 = conv0[..., :d_inner], _group_bc(conv0[..., d_inner:])
    pool0 = jnp.pad(state_pool[0], ((batch, 0), (1, 0), (0, 0)))

    u, dt_raw = _norm_call(xp, xs_tok, norm_mix_w[0].reshape(1, -1), w_dt1, w_dt2)
    p = _inproj_call(u, w_all)
    yg, yp, ssm_new = _mixer_call(p, dt_raw, state_ssm[0], cx0, cbc0, pool0, pad_h(dt_bias[0]),
                                  pad_h(a_log[0]), dsk, cwx, cbx, cwbc, cbbc,
                                  w_pool[0].astype(BF16), pool_scale[0].reshape(1, -1),
                                  _ssd_consts(), n_prompt_chunks)
    merged = _merge_call(yg, yp, p, ssd_norm_w[0].reshape(1, -1), w_proj_ssd[0].astype(BF16),
                         w_proj_pool[0].astype(BF16))

    wr1, wr2 = _hi_lo(jnp.pad(w_router[0], ((0, 0), (0, LANES - n_experts))))
    br = jnp.pad(b_router[0].reshape(1, -1), ((0, 0), (0, LANES - n_experts)), constant_values=-1e30)
    h, hn, eidx, rank, gate, cnt = _route_call(merged, xp, xs_tok, w_out[0].astype(BF16),
                                               norm_ffn_w[0].reshape(1, -1), wr1, wr2, br)

    counts = cnt[0, :n_experts]
    padded = (counts + MOE_ROWS - 1) // MOE_ROWS * MOE_ROWS
    pad_ends = jnp.cumsum(padded)
    pad_starts = pad_ends - padded
    n_blocks = -(-(t * TOP_K) // MOE_ROWS) + n_experts
    n_rows = n_blocks * MOE_ROWS
    e_flat = eidx[:, :TOP_K].reshape(-1)
    dest = (pad_starts[e_flat] + rank[:, :TOP_K].reshape(-1)).astype(jnp.int32)
    block_start = jnp.arange(n_blocks, dtype=jnp.int32) * MOE_ROWS
    block_e = jnp.minimum(jnp.sum(block_start[:, None] >= pad_ends[None, :], axis=1),
                          n_experts - 1).astype(jnp.int32)
    block_nv = jnp.clip(pad_starts[block_e] + counts[block_e] - block_start, 0,
                        MOE_ROWS).astype(jnp.int32)
    nused = (pad_ends[-1:] // MOE_ROWS).astype(jnp.int32)

    xs = _dispatch_call(dest, counts, pad_starts.astype(jnp.int32), hn, n_rows)
    eo = _moe_call(block_e, block_nv, nused, xs, w_up[0], b_up[0].reshape(n_experts, 1, -1),
                   w_down[0], b_down[0].reshape(n_experts, 1, -1))
    y_p, y_s = _combine_call(dest, h, gate, norm_final_w.reshape(1, -1), eo, t_prompt)

    y_prompt = y_p.reshape(batch, seq_len, d_model)
    y_sample = y_s.reshape(dec_batch, dec_seq, d_model)
    seq_ends = [t_prompt] * batch + [t_prompt + (s + 1) * dec_seq for s in range(dec_batch)]
    tail = jnp.stack([p[:, e - POOL_BUF:e] for e in seq_ends], axis=0)
    tail = tail.transpose(0, 2, 1, 3)
    ctail = tail[:, POOL_BUF - (CONV_WIDTH - 1):]
    conv_x = ctail[:, :, P_X:P_BC].reshape(n_seq, CONV_WIDTH - 1, d_inner)
    conv_bc = ctail[:, :, P_BC:P_POOL].reshape(n_seq, CONV_WIDTH - 1, d_bc)
    conv_new = jnp.concatenate([conv_x, _ungroup_bc(conv_bc)], axis=-1)
    pool_new = tail[:, :, P_POOL:P_GATE_SSD].reshape(n_seq, POOL_BUF, d_pool)
    return (y_prompt, y_sample,
            ssm_new[None, :batch], conv_new[None, :batch], pool_new[None, :batch],
            ssm_new[None, batch:], conv_new[None, batch:], pool_new[None, batch:])
```

```python
import functools

import numpy as np
import jax
import jax.numpy as jnp
from jax import lax
from jax.experimental import pallas as pl
from jax.experimental.pallas import tpu as pltpu

F32 = jnp.float32
BF16 = jnp.bfloat16

CHUNK = 64
HEAD_DIM = 64
N_GROUPS = 8
D_STATE = 128
CONV_WIDTH = 4
POOL_WINDOWS = (2, 4, 8, 16)
POOL_BUF = 15
PAST_LEN = 4096
TOP_K = 4
SWIGLU_ALPHA = 1.702
SWIGLU_LIMIT = 7.0
EPS = 1e-5
LANES = 128
COL_BLOCK = 512
MOE_ROWS = 1024
MOE_HALF = MOE_ROWS // 2
MOE_FF_TILE = 256
P_Z, P_X, P_BC, P_POOL, P_GATE_SSD, P_GATE_POOL = 0, 8, 16, 20, 24, 28
VMEM_LIMIT = 56 * 1024 * 1024


def _pick_tile(n, candidates):
    for c in candidates:
        if n % c == 0:
            return c
    raise ValueError(f"no tile for {n} in {candidates}")


def _params(sem, vmem=VMEM_LIMIT):
    return pltpu.CompilerParams(dimension_semantics=sem, vmem_limit_bytes=vmem)


def _split3(v):
    p1 = v.astype(BF16)
    r1 = v - p1.astype(F32)
    p2 = r1.astype(BF16)
    p3 = (r1 - p2.astype(F32)).astype(BF16)
    return p1, p2, p3


def _hi_lo(v):
    hi = v.astype(BF16)
    return hi, (v - hi.astype(F32)).astype(BF16)


def _dot_hi_lo(a1, a2, b1_ref, b2_ref):
    return (jnp.dot(a1, b1_ref[...], preferred_element_type=F32)
            + jnp.dot(a1, b2_ref[...], preferred_element_type=F32)
            + jnp.dot(a2, b1_ref[...], preferred_element_type=F32))


def _silu(v):
    return v * jax.nn.sigmoid(v)


def _packed_rows(d):
    return d // 2 // LANES


def _store_packed_rows(ref, v_bf16):
    tm, d = v_bf16.shape
    bits = lax.bitcast_convert_type(v_bf16.astype(F32), jnp.uint32)
    words = (bits[:, :d // 2] >> 16) | (bits[:, d // 2:] & jnp.uint32(0xFFFF0000))
    nr = _packed_rows(d)
    for s in range(nr):
        ref[pl.ds(s, tm, stride=nr), :] = words[:, s * LANES:(s + 1) * LANES]


def _load_packed_rows(ref, out_ref, n_tokens):
    d = out_ref.shape[1]
    nr = _packed_rows(d)
    for s in range(nr):
        words = ref[pl.ds(s, n_tokens, stride=nr), :]
        lo = lax.bitcast_convert_type(words << 16, F32)
        hi = lax.bitcast_convert_type(words & jnp.uint32(0xFFFF0000), F32)
        out_ref[:, s * LANES:(s + 1) * LANES] = lo.astype(BF16)
        out_ref[:, d // 2 + s * LANES:d // 2 + (s + 1) * LANES] = hi.astype(BF16)


def _two_part_specs(tm, d, n_first):
    return [pl.BlockSpec((tm, d), lambda i, *_: (jnp.minimum(i, n_first - 1), 0)),
            pl.BlockSpec((tm, d), lambda i, *_: (jnp.maximum(i - n_first, 0), 0))]


def _norm_kernel(xp_ref, xs_ref, w_ref, wdt1_ref, wdt2_ref, u_ref, dt_ref, *, n_first):
    x = jnp.where(pl.program_id(0) < n_first, xp_ref[...], xs_ref[...])
    r = lax.rsqrt(jnp.mean(x * x, axis=-1, keepdims=True) + EPS)
    u = x * r * w_ref[...]
    u1, u2 = _hi_lo(u)
    u_ref[...] = u1
    dt_ref[...] = _dot_hi_lo(u1, u2, wdt1_ref, wdt2_ref)


def _norm_call(xp, xs, w, wdt1, wdt2):
    d = xp.shape[1]
    t = xp.shape[0] + xs.shape[0]
    tm = _pick_tile(np.gcd(xp.shape[0], xs.shape[0]), (512, 256, 128, 64))
    n_first = xp.shape[0] // tm
    return pl.pallas_call(
        functools.partial(_norm_kernel, n_first=n_first),
        name="k_norm",
        grid=(t // tm,),
        in_specs=_two_part_specs(tm, d, n_first) + [
            pl.BlockSpec((1, d), lambda i: (0, 0)),
            pl.BlockSpec((d, LANES), lambda i: (0, 0)),
            pl.BlockSpec((d, LANES), lambda i: (0, 0))],
        out_specs=[pl.BlockSpec((tm, d), lambda i: (i, 0)),
                   pl.BlockSpec((tm, LANES), lambda i: (i, 0))],
        out_shape=[jax.ShapeDtypeStruct((t, d), BF16), jax.ShapeDtypeStruct((t, LANES), F32)],
        compiler_params=_params(("parallel",)),
    )(xp, xs, w, wdt1, wdt2)


def _inproj_kernel(u_ref, w_ref, o_ref):
    r = jnp.dot(u_ref[...], w_ref[...], preferred_element_type=F32)
    for k in range(o_ref.shape[0]):
        o_ref[k] = r[:, k * COL_BLOCK:(k + 1) * COL_BLOCK]


def _inproj_call(u, w):
    t, d = u.shape
    n = w.shape[1]
    tm = _pick_tile(t, (1024, 512, 256, 128, 64))
    nb = n // COL_BLOCK
    per = 2
    return pl.pallas_call(
        _inproj_kernel,
        name="k_inproj",
        grid=(t // tm, nb // per),
        in_specs=[pl.BlockSpec((tm, d), lambda i, j: (i, 0)),
                  pl.BlockSpec((d, per * COL_BLOCK), lambda i, j: (0, j))],
        out_specs=pl.BlockSpec((per, tm, COL_BLOCK), lambda i, j: (j, i, 0)),
        out_shape=jax.ShapeDtypeStruct((nb, t, COL_BLOCK), F32),
        compiler_params=_params(("parallel", "arbitrary")),
    )(u, w)


def _pool_chunk(c, first, pu_ref, pool0_ref, wp_ref, scale_ref, yp_ref, pbuf, n_prompt_chunks):
    hist = POOL_BUF + 1

    @pl.when(first)
    def _():
        pbuf[0:hist, :] = pool0_ref[...]

    @pl.when(jnp.logical_not(first))
    def _():
        pbuf[0:hist, :] = pbuf[CHUNK:CHUNK + hist, :]

    for g in range(len(POOL_WINDOWS)):
        pbuf[hist:hist + CHUNK, g * COL_BLOCK:(g + 1) * COL_BLOCK] = pu_ref[g]

    pos0 = jnp.where(c < n_prompt_chunks, c * CHUNK, PAST_LEN)
    pos = (pos0 + lax.broadcasted_iota(jnp.int32, (CHUNK, 1), 0)).astype(F32)
    for g, win in enumerate(POOL_WINDOWS):
        sl = slice(g * COL_BLOCK, (g + 1) * COL_BLOCK)
        cur = pbuf[hist:hist + CHUNK, sl]
        tot = cur
        for i in range(1, win):
            tot = tot + pbuf[hist - i:hist - i + CHUNK, sl]
        count = jnp.minimum(pos + 1.0, float(win))
        pooled = tot / count - cur
        yp_ref[:, sl] = jnp.dot(pooled.astype(BF16), wp_ref[g],
                                preferred_element_type=F32) * scale_ref[:, sl]


def _mixer_kernel(z_ref, x_ref, bc_ref, pu_ref, dt_ref, ssm0_ref, cx0_ref, cbc0_ref, pool0_ref,
                  dtb_ref, alog_ref, dsk_ref, cwx_ref, cbx_ref, cwbc_ref, cbbc_ref, wp_ref, scale_ref,
                  tril3_ref, e3_ref, diag_ref, caus_ref, bd_ref,
                  yg_ref, yp_ref, ssm_ref, st, bufx, bufbc, exs, pbuf, *, n_prompt_chunks):
    c = pl.program_id(0)
    first = jnp.logical_or(c == 0, c >= n_prompt_chunks)
    last = c >= n_prompt_chunks - 1
    pairs = N_GROUPS * COL_BLOCK // LANES

    @pl.when(c == 0)
    def _():
        st[...] = jnp.zeros_like(st)

    @pl.when(c >= n_prompt_chunks)
    def _():
        for q in range(pairs):
            blk = jnp.concatenate([ssm0_ref[2 * q], ssm0_ref[2 * q + 1]], axis=0)
            g, o = divmod(q * LANES, COL_BLOCK)
            st[g, :, o:o + LANES] = blk.T

    @pl.when(first)
    def _():
        bufx[0:8, :] = cx0_ref[...]
        bufbc[0:8, :] = cbc0_ref[...]

    @pl.when(jnp.logical_not(first))
    def _():
        bufx[0:8, :] = bufx[CHUNK:CHUNK + 8, :]
        bufbc[0:8, :] = bufbc[CHUNK:CHUNK + 8, :]

    for g in range(N_GROUPS):
        bufx[8:8 + CHUNK, g * COL_BLOCK:(g + 1) * COL_BLOCK] = x_ref[g]
    for q in range(N_GROUPS // 2):
        bufbc[8:8 + CHUNK, q * COL_BLOCK:(q + 1) * COL_BLOCK] = bc_ref[q]

    dtv = dt_ref[...] + dtb_ref[...]
    dt = jnp.maximum(dtv, 0.0) + jnp.log1p(jnp.exp(-jnp.abs(dtv)))
    d_a = dt * (-jnp.exp(alog_ref[...]))
    p1, p2, p3 = _split3(d_a)
    acum = jnp.dot(tril3_ref[...], jnp.concatenate([p1, p2, p3], axis=0),
                   preferred_element_type=F32)
    q1, q2, q3 = _split3(jnp.concatenate([acum, dt], axis=0))
    exs[...] = jnp.dot(jnp.concatenate([q1, q2, q3], axis=1), e3_ref[...],
                       preferred_element_type=F32)

    caus = caus_ref[...] > 0.0
    bd = bd_ref[...]
    hw = 4 * HEAD_DIM
    for g in range(N_GROUPS):
        sl = slice(g * COL_BLOCK, (g + 1) * COL_BLOCK)
        slb = slice(g * 2 * D_STATE, (g + 1) * 2 * D_STATE)
        xc = cbx_ref[:, sl]
        bcc = cbbc_ref[:, slb]
        for k in range(CONV_WIDTH):
            xc = xc + cwx_ref[k:k + 1, sl] * bufx[5 + k:5 + k + CHUNK, sl]
            bcc = bcc + cwbc_ref[k:k + 1, slb] * bufbc[5 + k:5 + k + CHUNK, slb]
        xs = _silu(xc)
        bca = _silu(bcc)
        b_g = bca[:, :D_STATE]
        c_g = bca[:, D_STATE:]
        acx = exs[0:CHUNK, sl]
        dtx = exs[CHUNK:2 * CHUNK, sl]
        alast = acx[CHUNK - 1:CHUNK, :]
        arow = jnp.sum(acx * diag_ref[:, sl], axis=0, keepdims=True)
        xdt = xs * dtx
        xdtb = xdt.astype(BF16)
        bb = b_g.astype(BF16)
        cb = c_g.astype(BF16)
        cb2 = lax.dot_general(cb, jnp.concatenate([bb, bb], axis=0),
                              (((1,), (1,)), ((), ())), preferred_element_type=F32)
        ydiag = []
        for q in range(2):
            lhs = []
            for d in range(2):
                lo = q * hw + d * LANES
                seg = acx[:, lo:lo + LANES] - arow[:, lo:lo + LANES]
                lhs.append((cb2 * jnp.exp(jnp.where(caus, seg, -jnp.inf))).astype(BF16))
            xq = xdtb[:, q * hw:(q + 1) * hw]
            wq = jnp.concatenate([xq, xq, xq, xq], axis=0) * bd
            ydiag.append(jnp.dot(jnp.concatenate(lhs, axis=1), wq, preferred_element_type=F32))
        s_old = st[g]
        yoff = jnp.dot(cb, s_old.astype(BF16), preferred_element_type=F32) * jnp.exp(acx)
        y = jnp.concatenate(ydiag, axis=1) + yoff + dsk_ref[:, sl] * xs
        yg_ref[:, sl] = y * _silu(z_ref[g])
        v = (xdt * jnp.exp(alast - acx)).astype(BF16)
        st[g] = jnp.exp(alast) * s_old + jnp.dot(b_g.T.astype(BF16), v,
                                                 preferred_element_type=F32)

    @pl.when(last)
    def _():
        for q in range(pairs):
            g, o = divmod(q * LANES, COL_BLOCK)
            blk = st[g, :, o:o + LANES].T
            ssm_ref[2 * q] = blk[:HEAD_DIM]
            ssm_ref[2 * q + 1] = blk[HEAD_DIM:]

    _pool_chunk(c, first, pu_ref, pool0_ref, wp_ref, scale_ref, yp_ref, pbuf, n_prompt_chunks)


def _mixer_call(p, dt_raw, ssm_in, cx0, cbc0, pool0, dtb, alog, dsk, cwx, cbx, cwbc, cbbc, wp, scale,
                consts, n_prompt_chunks):
    _, t, _ = p.shape
    n_chunks = t // CHUNK
    n_seq = cx0.shape[0]
    n_heads, head_dim, d_state = ssm_in.shape[1:]
    d_inner = N_GROUPS * COL_BLOCK
    d_bc = N_GROUPS * 2 * D_STATE
    ng = len(POOL_WINDOWS)
    d_pool = ng * COL_BLOCK
    hist = POOL_BUF + 1
    tril3, e3, diag, caus, bd = consts

    def seq(c):
        return jnp.maximum(c - (n_prompt_chunks - 1), 0)

    def const(a):
        return pl.BlockSpec(a.shape, lambda c: (0,) * a.ndim)

    state_block = (None, n_heads, head_dim, d_state)
    return pl.pallas_call(
        functools.partial(_mixer_kernel, n_prompt_chunks=n_prompt_chunks),
        name="k_mixer",
        grid=(n_chunks,),
        in_specs=[pl.BlockSpec((N_GROUPS, CHUNK, COL_BLOCK), lambda c: (P_Z // N_GROUPS, c, 0)),
                  pl.BlockSpec((N_GROUPS, CHUNK, COL_BLOCK), lambda c: (P_X // N_GROUPS, c, 0)),
                  pl.BlockSpec((N_GROUPS // 2, CHUNK, COL_BLOCK),
                               lambda c: (P_BC // (N_GROUPS // 2), c, 0)),
                  pl.BlockSpec((ng, CHUNK, COL_BLOCK), lambda c: (P_POOL // ng, c, 0)),
                  pl.BlockSpec((CHUNK, LANES), lambda c: (c, 0)),
                  pl.BlockSpec(state_block, lambda c: (jnp.maximum(c - n_prompt_chunks, 0), 0, 0, 0)),
                  pl.BlockSpec((None, 8, d_inner), lambda c: (seq(c), 0, 0)),
                  pl.BlockSpec((None, 8, d_bc), lambda c: (seq(c), 0, 0)),
                  pl.BlockSpec((None, hist, d_pool), lambda c: (seq(c), 0, 0)),
                  const(dtb), const(alog), const(dsk), const(cwx), const(cbx),
                  const(cwbc), const(cbbc), const(wp), const(scale),
                  const(tril3), const(e3), const(diag), const(caus), const(bd)],
        out_specs=[pl.BlockSpec((CHUNK, d_inner), lambda c: (c, 0)),
                   pl.BlockSpec((CHUNK, d_pool), lambda c: (c, 0)),
                   pl.BlockSpec(state_block, lambda c: (seq(c), 0, 0, 0))],
        out_shape=[jax.ShapeDtypeStruct((t, d_inner), F32),
                   jax.ShapeDtypeStruct((t, d_pool), F32),
                   jax.ShapeDtypeStruct((n_seq, n_heads, head_dim, d_state), F32)],
        scratch_shapes=[pltpu.VMEM((N_GROUPS, D_STATE, COL_BLOCK), F32),
                        pltpu.VMEM((CHUNK + 8, d_inner), F32),
                        pltpu.VMEM((CHUNK + 8, d_bc), F32),
                        pltpu.VMEM((2 * CHUNK, d_inner), F32),
                        pltpu.VMEM((CHUNK + hist, d_pool), F32)],
        compiler_params=_params(("arbitrary",)),
    )(p, p, p, p, dt_raw, ssm_in, cx0, cbc0, pool0, dtb, alog, dsk, cwx, cbx, cwbc, cbbc, wp, scale,
      tril3, e3, diag, caus, bd)


def _ssd_consts():
    l = np.arange(CHUNK)
    tril = (l[:, None] >= l[None, :]).astype(np.float32)
    tril3 = np.concatenate([tril, tril, tril], axis=1)
    n_heads = N_GROUPS * COL_BLOCK // HEAD_DIM
    col_head = np.arange(n_heads * HEAD_DIM) // HEAD_DIM
    col_pos = np.arange(n_heads * HEAD_DIM) % HEAD_DIM
    e = (np.arange(LANES)[:, None] == col_head[None, :]).astype(np.float32)
    e3 = np.concatenate([e, e, e], axis=0)
    diag = (l[:, None] == col_pos[None, :]).astype(np.float32)
    caus = np.concatenate([tril, tril], axis=1)
    r = np.arange(4 * HEAD_DIM)
    bd = (r[:, None] // HEAD_DIM == r[None, :] // HEAD_DIM).astype(np.float32)
    return (jnp.asarray(tril3, BF16), jnp.asarray(e3, BF16), jnp.asarray(diag, F32),
            jnp.asarray(caus, F32), jnp.asarray(bd, BF16))


def _merge_kernel(yg_ref, yp_ref, gs_ref, gp_ref, nw_ref, wps_ref, wpp_ref, o_ref):
    y = yg_ref[...]
    r = lax.rsqrt(jnp.mean(y * y, axis=-1, keepdims=True) + EPS)
    yn = (y * r * nw_ref[...]).astype(BF16)
    a = jnp.dot(yn, wps_ref[...], preferred_element_type=F32)
    b = jnp.dot(yp_ref[...].astype(BF16), wpp_ref[...], preferred_element_type=F32)
    for j in range(gs_ref.shape[0]):
        sl = slice(j * COL_BLOCK, (j + 1) * COL_BLOCK)
        o_ref[:, sl] = (jax.nn.sigmoid(gs_ref[j]) * a[:, sl]
                        + jax.nn.sigmoid(gp_ref[j]) * b[:, sl]).astype(BF16)


def _merge_call(yg, yp, p, nw, wps, wpp):
    t, d_inner = yg.shape
    d_pool = yp.shape[1]
    d_model = wps.shape[1]
    tm = _pick_tile(t, (256, 128, 64))
    nj = d_model // COL_BLOCK
    once = pl.Buffered(1)
    return pl.pallas_call(
        _merge_kernel,
        name="k_merge",
        grid=(t // tm,),
        in_specs=[pl.BlockSpec((tm, d_inner), lambda i: (i, 0)),
                  pl.BlockSpec((tm, d_pool), lambda i: (i, 0)),
                  pl.BlockSpec((nj, tm, COL_BLOCK), lambda i: (P_GATE_SSD // nj, i, 0)),
                  pl.BlockSpec((nj, tm, COL_BLOCK), lambda i: (P_GATE_POOL // nj, i, 0)),
                  pl.BlockSpec((1, d_inner), lambda i: (0, 0)),
                  pl.BlockSpec((d_inner, d_model), lambda i: (0, 0), pipeline_mode=once),
                  pl.BlockSpec((d_pool, d_model), lambda i: (0, 0), pipeline_mode=once)],
        out_specs=pl.BlockSpec((tm, d_model), lambda i: (i, 0)),
        out_shape=jax.ShapeDtypeStruct((t, d_model), BF16),
        compiler_params=_params(("parallel",)),
    )(yg, yp, p, p, nw, wps, wpp)


def _route_kernel(m_ref, xp_ref, xs_ref, wo_ref, nw_ref, wr1_ref, wr2_ref, br_ref, trs_ref,
                  h_ref, hn_ref, eidx_ref, rank_ref, gate_ref, cnt_ref, carry, *, n_first):
    i = pl.program_id(0)

    @pl.when(i == 0)
    def _():
        carry[...] = jnp.zeros_like(carry)

    x = jnp.where(i < n_first, xp_ref[...], xs_ref[...])
    h = x + jnp.dot(m_ref[...], wo_ref[...], preferred_element_type=F32)
    h_ref[...] = h
    r = lax.rsqrt(jnp.mean(h * h, axis=-1, keepdims=True) + EPS)
    hn = h * r * nw_ref[...]
    h1, h2 = _hi_lo(hn)
    _store_packed_rows(hn_ref, h1)
    logits = _dot_hi_lo(h1, h2, wr1_ref, wr2_ref) + br_ref[...]
    lane = lax.broadcasted_iota(jnp.int32, logits.shape, 1)
    work = logits
    member = jnp.zeros(logits.shape, F32)
    vals, idxs = [], []
    for _ in range(TOP_K):
        m = jnp.max(work, axis=-1, keepdims=True)
        idx = jnp.min(jnp.where(work == m, lane, LANES), axis=-1, keepdims=True)
        hit = lane == idx
        member = member + hit.astype(F32)
        work = jnp.where(hit, -jnp.inf, work)
        vals.append(m)
        idxs.append(idx)
    ex = [jnp.exp(v - vals[0]) for v in vals]
    den = ex[0] + ex[1] + ex[2] + ex[3]
    before = jnp.dot(trs_ref[...], member.astype(BF16), preferred_element_type=F32) + carry[0:1, :]
    eidx = jnp.zeros(logits.shape, jnp.int32)
    rank = jnp.zeros(logits.shape, jnp.int32)
    gate = jnp.zeros(logits.shape, F32)
    for k in range(TOP_K):
        rk = jnp.sum(jnp.where(lane == idxs[k], before, 0.0), axis=-1, keepdims=True)
        eidx = jnp.where(lane == k, idxs[k], eidx)
        rank = jnp.where(lane == k, rk.astype(jnp.int32), rank)
        gate = jnp.where(lane == k, ex[k] / den, gate)
    eidx_ref[...] = eidx
    rank_ref[...] = rank
    gate_ref[...] = gate
    carry[0:1, :] = carry[0:1, :] + jnp.sum(member, axis=0, keepdims=True)
    cnt_ref[...] = carry[...].astype(jnp.int32)


def _route_call(merged, xp, xs, wo, nw, wr1, wr2, br):
    t, d = merged.shape
    tm = _pick_tile(np.gcd(xp.shape[0], xs.shape[0]), (512, 256, 128, 64))
    n_first = xp.shape[0] // tm
    ri = np.arange(tm)
    trs = jnp.asarray((ri[:, None] > ri[None, :]).astype(np.float32), BF16)
    row = lambda i: (i, 0)
    fix = lambda i: (0, 0)
    return pl.pallas_call(
        functools.partial(_route_kernel, n_first=n_first),
        name="k_route",
        grid=(t // tm,),
        in_specs=[pl.BlockSpec((tm, d), row)] + _two_part_specs(tm, d, n_first) + [
                  pl.BlockSpec((d, d), fix), pl.BlockSpec((1, d), fix),
                  pl.BlockSpec((d, LANES), fix), pl.BlockSpec((d, LANES), fix),
                  pl.BlockSpec((1, LANES), fix), pl.BlockSpec((tm, tm), fix)],
        out_specs=[pl.BlockSpec((tm, d), row), pl.BlockSpec((tm * _packed_rows(d), LANES), row),
                   pl.BlockSpec((tm, LANES), row), pl.BlockSpec((tm, LANES), row),
                   pl.BlockSpec((tm, LANES), row), pl.BlockSpec((8, LANES), fix)],
        out_shape=[jax.ShapeDtypeStruct((t, d), F32),
                   jax.ShapeDtypeStruct((t * _packed_rows(d), LANES), jnp.uint32),
                   jax.ShapeDtypeStruct((t, LANES), jnp.int32),
                   jax.ShapeDtypeStruct((t, LANES), jnp.int32),
                   jax.ShapeDtypeStruct((t, LANES), F32),
                   jax.ShapeDtypeStruct((8, LANES), jnp.int32)],
        scratch_shapes=[pltpu.VMEM((8, LANES), F32)],
        compiler_params=_params(("arbitrary",)),
    )(merged, xp, xs, wo, nw, wr1, wr2, br, trs)


_PAD_CHUNKS = tuple(1 << s for s in range(MOE_ROWS.bit_length() - 2, -1, -1))


def _dispatch_kernel(dest_ref, cnt_ref, pstart_ref, x_ref, o_hbm, zbuf, sem, zsem,
                     *, rows, nr, n_experts):
    i = pl.program_id(0)

    def tokens(ref, first, n):
        return ref.at[pl.ds(pl.multiple_of(first * nr, nr), n * nr)]

    def issue(t, carry):
        for k in range(TOP_K):
            d = dest_ref[(i * rows + t) * TOP_K + k]
            pltpu.make_async_copy(tokens(x_ref, t, 1), tokens(o_hbm, d, 1), sem).start()
        return carry

    lax.fori_loop(0, rows, issue, 0, unroll=4)
    for k in range(TOP_K):
        pltpu.make_async_copy(x_ref, tokens(o_hbm, 0, rows), sem).wait()

    @pl.when(i == pl.num_programs(0) - 1)
    def _():
        zbuf[...] = jnp.zeros_like(zbuf)

        def pad_copies(e, wait):
            cnt = cnt_ref[e]
            off = pstart_ref[e] + cnt
            npad = (-cnt) & (MOE_ROWS - 1)
            for n_c in _PAD_CHUNKS:
                cp = pltpu.make_async_copy(tokens(zbuf, 0, n_c), tokens(o_hbm, off, n_c), zsem)
                pl.when((npad & n_c) != 0)(cp.wait if wait else cp.start)
                off = off + (npad & n_c)

        def start_e(e, carry):
            pad_copies(e, False)
            return carry

        def wait_e(e, carry):
            pad_copies(e, True)
            return carry

        lax.fori_loop(0, n_experts, start_e, 0)
        lax.fori_loop(0, n_experts, wait_e, 0)

        ztok = zbuf.shape[0] // nr
        first = (pstart_ref[n_experts - 1] + cnt_ref[n_experts - 1] + MOE_ROWS - 1) // MOE_ROWS
        first = first * (MOE_ROWS // ztok)

        def tail_copy(c):
            return pltpu.make_async_copy(zbuf, tokens(o_hbm, c * ztok, ztok), zsem)

        def start_t(c, carry):
            tail_copy(c).start()
            return carry

        def wait_t(c, carry):
            tail_copy(c).wait()
            return carry

        lax.fori_loop(first, o_hbm.shape[0] // zbuf.shape[0], start_t, 0)
        lax.fori_loop(first, o_hbm.shape[0] // zbuf.shape[0], wait_t, 0)


def _dispatch_call(dest, counts, pad_starts, src, n_rows, nr):
    t = src.shape[0] // nr
    rows = _pick_tile(t, (256, 128, 64))
    return pl.pallas_call(
        functools.partial(_dispatch_kernel, rows=rows, nr=nr, n_experts=counts.shape[0]),
        name="k_dispatch",
        grid_spec=pltpu.PrefetchScalarGridSpec(
            num_scalar_prefetch=3,
            grid=(t // rows,),
            in_specs=[pl.BlockSpec((rows * nr, LANES), lambda i, de, cn, ps: (i, 0))],
            out_specs=pl.BlockSpec(memory_space=pl.ANY),
            scratch_shapes=[pltpu.VMEM((_PAD_CHUNKS[0] * nr, LANES), src.dtype),
                            pltpu.SemaphoreType.DMA, pltpu.SemaphoreType.DMA]),
        out_shape=jax.ShapeDtypeStruct((n_rows * nr, LANES), src.dtype),
        compiler_params=_params(("arbitrary",)),
    )(dest, counts, pad_starts, src)


def _moe_kernel(be_ref, nv_ref, nused_ref, x_ref, wg_ref, wl_ref, bg_ref, bl_ref, wd_ref, bd_ref,
                o_ref, xbf, wgb, wlb, wdb):
    b = pl.program_id(0)
    j = pl.program_id(1)
    live = b < nused_ref[0]
    both = jnp.logical_and(live, nv_ref[b] > MOE_HALF)

    @pl.when(jnp.logical_and(j == 0, live))
    def _():
        _load_packed_rows(x_ref, xbf, MOE_ROWS)
        o_ref[...] = jnp.broadcast_to(bd_ref[...], o_ref.shape)

    @pl.when(jnp.logical_and(j == 0, jnp.logical_not(live)))
    def _():
        o_ref[...] = jnp.zeros_like(o_ref)

    def half(lo):
        x = xbf[lo:lo + MOE_HALF, :]
        glu = jnp.dot(x, wgb[...], preferred_element_type=F32) + bg_ref[...]
        lin = jnp.dot(x, wlb[...], preferred_element_type=F32) + bl_ref[...]
        glu = jnp.minimum(glu, SWIGLU_LIMIT)
        lin = jnp.clip(lin, -SWIGLU_LIMIT, SWIGLU_LIMIT)
        act = glu * jax.nn.sigmoid(SWIGLU_ALPHA * glu) * (lin + 1.0)
        o_ref[lo:lo + MOE_HALF, :] += jnp.dot(act.astype(BF16), wdb[...],
                                              preferred_element_type=F32)

    @pl.when(live)
    def _():
        wgb[...] = wg_ref[...].astype(BF16)
        wlb[...] = wl_ref[...].astype(BF16)
        wdb[...] = wd_ref[...].astype(BF16)
        half(0)

    @pl.when(both)
    def _():
        half(MOE_HALF)


def _moe_call(block_e, block_nv, nused, xs, w_up, b_up, w_down, b_down):
    d_ff, d = w_down.shape[1:]
    nr = _packed_rows(d)
    n_rows = xs.shape[0] // nr
    tf = MOE_FF_TILE
    nj = d_ff // tf
    nb = n_rows // MOE_ROWS

    def bb(b, nu):
        return jnp.minimum(b, nu[0] - 1)

    def jj(b, j, nu):
        return jnp.where(b < nu[0], j, nj - 1)

    return pl.pallas_call(
        _moe_kernel,
        name="k_moe",
        grid_spec=pltpu.PrefetchScalarGridSpec(
            num_scalar_prefetch=3,
            grid=(nb, nj),
            in_specs=[pl.BlockSpec((MOE_ROWS * nr, LANES), lambda b, j, be, nv, nu: (bb(b, nu), 0)),
                      pl.BlockSpec((None, d, tf),
                                   lambda b, j, be, nv, nu: (be[bb(b, nu)], 0, jj(b, j, nu))),
                      pl.BlockSpec((None, d, tf),
                                   lambda b, j, be, nv, nu: (be[bb(b, nu)], 0, nj + jj(b, j, nu))),
                      pl.BlockSpec((None, 1, tf),
                                   lambda b, j, be, nv, nu: (be[bb(b, nu)], 0, jj(b, j, nu))),
                      pl.BlockSpec((None, 1, tf),
                                   lambda b, j, be, nv, nu: (be[bb(b, nu)], 0, nj + jj(b, j, nu))),
                      pl.BlockSpec((None, tf, d),
                                   lambda b, j, be, nv, nu: (be[bb(b, nu)], jj(b, j, nu), 0)),
                      pl.BlockSpec((None, 1, d), lambda b, j, be, nv, nu: (be[bb(b, nu)], 0, 0))],
            out_specs=pl.BlockSpec((MOE_ROWS, d), lambda b, j, be, nv, nu: (b, 0)),
            scratch_shapes=[pltpu.VMEM((MOE_ROWS, d), BF16), pltpu.VMEM((d, tf), BF16),
                            pltpu.VMEM((d, tf), BF16), pltpu.VMEM((tf, d), BF16)]),
        out_shape=jax.ShapeDtypeStruct((n_rows, d), F32),
        compiler_params=_params(("arbitrary", "arbitrary")),
    )(block_e, block_nv, nused, xs, w_up, w_up, b_up, b_up, w_down, b_down)


def _combine_kernel(dest_ref, h_ref, gate_ref, nw_ref, eo_hbm, yp_ref, ys_ref, gbuf, sem,
                    *, rows, n_first):
    i = pl.program_id(0)
    n = pl.num_programs(0)
    slot = i % 2

    def gather(tile, into):
        def issue(t, carry):
            for k in range(TOP_K):
                d = dest_ref[(tile * rows + t) * TOP_K + k]
                pltpu.make_async_copy(eo_hbm.at[d], gbuf.at[into, k, t], sem.at[into]).start()
            return carry

        lax.fori_loop(0, rows, issue, 0, unroll=4)

    @pl.when(i == 0)
    def _():
        gather(i, slot)

    @pl.when(i + 1 < n)
    def _():
        gather(i + 1, 1 - slot)

    for k in range(TOP_K):
        pltpu.make_async_copy(eo_hbm.at[pl.ds(0, rows)], gbuf.at[slot, k], sem.at[slot]).wait()
    gate = gate_ref[...]
    y = h_ref[...]
    for k in range(TOP_K):
        y = y + gate[:, k:k + 1] * gbuf[slot, k]
    r = lax.rsqrt(jnp.mean(y * y, axis=-1, keepdims=True) + EPS)
    y = y * r * nw_ref[...]

    @pl.when(i < n_first)
    def _():
        yp_ref[...] = y

    @pl.when(i >= n_first)
    def _():
        ys_ref[...] = y


def _combine_call(dest, h, gate, nw, eo, t_first):
    t, d = h.shape
    rows = _pick_tile(np.gcd(t_first, t - t_first), (128, 64))
    n_first = t_first // rows
    return pl.pallas_call(
        functools.partial(_combine_kernel, rows=rows, n_first=n_first),
        name="k_combine",
        grid_spec=pltpu.PrefetchScalarGridSpec(
            num_scalar_prefetch=1,
            grid=(t // rows,),
            in_specs=[pl.BlockSpec((rows, d), lambda i, de: (i, 0)),
                      pl.BlockSpec((rows, LANES), lambda i, de: (i, 0)),
                      pl.BlockSpec((1, d), lambda i, de: (0, 0)),
                      pl.BlockSpec(memory_space=pl.ANY)],
            out_specs=_two_part_specs(rows, d, n_first),
            scratch_shapes=[pltpu.VMEM((2, TOP_K, rows, d), F32), pltpu.SemaphoreType.DMA((2,))]),
        out_shape=[jax.ShapeDtypeStruct((t_first, d), F32),
                   jax.ShapeDtypeStruct((t - t_first, d), F32)],
        compiler_params=_params(("arbitrary",)),
    )(dest, h, gate, nw, eo)


def _group_bc(a):
    lead = a.shape[:-1]
    return a.reshape(lead + (2, N_GROUPS, D_STATE)).swapaxes(-3, -2).reshape(lead + (2 * N_GROUPS * D_STATE,))


def _ungroup_bc(a):
    lead = a.shape[:-1]
    return a.reshape(lead + (N_GROUPS, 2, D_STATE)).swapaxes(-3, -2).reshape(lead + (2 * N_GROUPS * D_STATE,))


def kernel(x_prompt, x_sample, state_ssm, state_conv, state_pool, norm_mix_w, w_in, conv_w, conv_b,
           dt_bias, a_log, d_skip, ssd_norm_w, w_pool, pool_scale, w_proj_ssd, w_proj_pool, w_out,
           norm_ffn_w, w_router, b_router, w_up, b_up, w_down, b_down, norm_final_w):
    batch, seq_len, d_model = x_prompt.shape
    dec_batch, dec_seq, _ = x_sample.shape
    depth, _, n_heads, head_dim, d_state = state_ssm.shape
    assert depth == 1 and batch == 1 and dec_seq == CHUNK and seq_len % CHUNK == 0
    assert head_dim == HEAD_DIM and d_state == D_STATE
    d_inner = n_heads * head_dim
    assert d_inner == N_GROUPS * COL_BLOCK
    d_bc = 2 * N_GROUPS * D_STATE
    d_pool = state_pool.shape[-1]
    assert d_pool == len(POOL_WINDOWS) * COL_BLOCK and state_pool.shape[-2] == POOL_BUF
    n_experts = w_router.shape[-1]
    n_prompt_chunks = seq_len // CHUNK
    n_seq = batch + dec_batch
    t_prompt = batch * seq_len
    t = t_prompt + dec_batch * dec_seq

    xp = x_prompt.reshape(t_prompt, d_model)
    xs_tok = x_sample.reshape(-1, d_model)

    wi = w_in[0]
    z_end = d_inner
    xbc_end = z_end + d_inner + d_bc
    dt_end = xbc_end + n_heads
    w_all = jnp.concatenate(
        [wi[:, :z_end + d_inner], _group_bc(wi[:, z_end + d_inner:xbc_end]), wi[:, dt_end:]],
        axis=1).astype(BF16)
    w_dt1, w_dt2 = _hi_lo(jnp.pad(wi[:, xbc_end:dt_end], ((0, 0), (0, LANES - n_heads))))
    pad_h = lambda a: jnp.pad(a.reshape(1, n_heads), ((0, 0), (0, LANES - n_heads)))
    cw = conv_w[0]
    cwx, cwbc = cw[:, :d_inner], _group_bc(cw[:, d_inner:])
    cb = conv_b[0].reshape(1, -1)
    cbx, cbbc = cb[:, :d_inner], _group_bc(cb[:, d_inner:])
    dsk = jnp.repeat(d_skip[0], head_dim).reshape(1, d_inner)

    conv0 = jnp.pad(state_conv[0], ((batch, 0), (8 - (CONV_WIDTH - 1), 0), (0, 0)))
    cx0, cbc0 = conv0[..., :d_inner], _group_bc(conv0[..., d_inner:])
    pool0 = jnp.pad(state_pool[0], ((batch, 0), (1, 0), (0, 0)))

    u, dt_raw = _norm_call(xp, xs_tok, norm_mix_w[0].reshape(1, -1), w_dt1, w_dt2)
    p = _inproj_call(u, w_all)
    yg, yp, ssm_new = _mixer_call(p, dt_raw, state_ssm[0], cx0, cbc0, pool0, pad_h(dt_bias[0]),
                                  pad_h(a_log[0]), dsk, cwx, cbx, cwbc, cbbc,
                                  w_pool[0].astype(BF16), pool_scale[0].reshape(1, -1),
                                  _ssd_consts(), n_prompt_chunks)
    merged = _merge_call(yg, yp, p, ssd_norm_w[0].reshape(1, -1), w_proj_ssd[0].astype(BF16),
                         w_proj_pool[0].astype(BF16))

    wr1, wr2 = _hi_lo(jnp.pad(w_router[0], ((0, 0), (0, LANES - n_experts))))
    br = jnp.pad(b_router[0].reshape(1, -1), ((0, 0), (0, LANES - n_experts)), constant_values=-1e30)
    h, hn, eidx, rank, gate, cnt = _route_call(merged, xp, xs_tok, w_out[0].astype(BF16),
                                               norm_ffn_w[0].reshape(1, -1), wr1, wr2, br)

    counts = cnt[0, :n_experts]
    padded = (counts + MOE_ROWS - 1) // MOE_ROWS * MOE_ROWS
    pad_ends = jnp.cumsum(padded)
    pad_starts = pad_ends - padded
    n_blocks = -(-(t * TOP_K) // MOE_ROWS) + n_experts
    n_rows = n_blocks * MOE_ROWS
    e_flat = eidx[:, :TOP_K].reshape(-1)
    dest = (pad_starts[e_flat] + rank[:, :TOP_K].reshape(-1)).astype(jnp.int32)
    block_start = jnp.arange(n_blocks, dtype=jnp.int32) * MOE_ROWS
    block_e = jnp.minimum(jnp.sum(block_start[:, None] >= pad_ends[None, :], axis=1),
                          n_experts - 1).astype(jnp.int32)
    block_nv = jnp.clip(pad_starts[block_e] + counts[block_e] - block_start, 0,
                        MOE_ROWS).astype(jnp.int32)
    nused = (pad_ends[-1:] // MOE_ROWS).astype(jnp.int32)

    xs = _dispatch_call(dest, counts, pad_starts.astype(jnp.int32), hn, n_rows, _packed_rows(d_model))
    eo = _moe_call(block_e, block_nv, nused, xs, w_up[0], b_up[0].reshape(n_experts, 1, -1),
                   w_down[0], b_down[0].reshape(n_experts, 1, -1))
    y_p, y_s = _combine_call(dest, h, gate, norm_final_w.reshape(1, -1), eo, t_prompt)

    y_prompt = y_p.reshape(batch, seq_len, d_model)
    y_sample = y_s.reshape(dec_batch, dec_seq, d_model)
    seq_ends = [t_prompt] * batch + [t_prompt + (s + 1) * dec_seq for s in range(dec_batch)]
    tail = jnp.stack([p[:, e - POOL_BUF:e] for e in seq_ends], axis=0)
    tail = tail.transpose(0, 2, 1, 3)
    ctail = tail[:, POOL_BUF - (CONV_WIDTH - 1):]
    conv_x = ctail[:, :, P_X:P_BC].reshape(n_seq, CONV_WIDTH - 1, d_inner)
    conv_bc = ctail[:, :, P_BC:P_POOL].reshape(n_seq, CONV_WIDTH - 1, d_bc)
    conv_new = jnp.concatenate([conv_x, _ungroup_bc(conv_bc)], axis=-1)
    pool_new = tail[:, :, P_POOL:P_GATE_SSD].reshape(n_seq, POOL_BUF, d_pool)
    return (y_prompt, y_sample,
            ssm_new[None, :batch], conv_new[None, :batch], pool_new[None, :batch],
            ssm_new[None, batch:], conv_new[None, batch:], pool_new[None, batch:])
```

```python
import functools

import numpy as np
import jax
import jax.numpy as jnp
from jax import lax
from jax.experimental import pallas as pl
from jax.experimental.pallas import tpu as pltpu

F32 = jnp.float32
BF16 = jnp.bfloat16

CHUNK = 64
HEAD_DIM = 64
N_GROUPS = 8
D_STATE = 128
CONV_WIDTH = 4
POOL_WINDOWS = (2, 4, 8, 16)
POOL_BUF = 15
PAST_LEN = 4096
TOP_K = 4
SWIGLU_ALPHA = 1.702
SWIGLU_LIMIT = 7.0
EPS = 1e-5
LANES = 128
COL_BLOCK = 512
MOE_ROWS = 1024
MOE_HALF = MOE_ROWS // 2
MOE_FF_TILE = 256
P_Z, P_X, P_BC, P_POOL, P_GATE_SSD, P_GATE_POOL = 0, 8, 16, 20, 24, 28
VMEM_LIMIT = 56 * 1024 * 1024


def _pick_tile(n, candidates):
    for c in candidates:
        if n % c == 0:
            return c
    raise ValueError(f"no tile for {n} in {candidates}")


def _params(sem, vmem=VMEM_LIMIT):
    return pltpu.CompilerParams(dimension_semantics=sem, vmem_limit_bytes=vmem)


def _split3(v):
    p1 = v.astype(BF16)
    r1 = v - p1.astype(F32)
    p2 = r1.astype(BF16)
    p3 = (r1 - p2.astype(F32)).astype(BF16)
    return p1, p2, p3


def _hi_lo(v):
    hi = v.astype(BF16)
    return hi, (v - hi.astype(F32)).astype(BF16)


def _dot_hi_lo(a1, a2, b1_ref, b2_ref):
    return (jnp.dot(a1, b1_ref[...], preferred_element_type=F32)
            + jnp.dot(a1, b2_ref[...], preferred_element_type=F32)
            + jnp.dot(a2, b1_ref[...], preferred_element_type=F32))


def _silu(v):
    return v * jax.nn.sigmoid(v)


def _packed_rows(d):
    return d // 2 // LANES


def _store_packed_rows(ref, v_bf16):
    tm, d = v_bf16.shape
    bits = lax.bitcast_convert_type(v_bf16.astype(F32), jnp.uint32)
    words = (bits[:, :d // 2] >> 16) | (bits[:, d // 2:] & jnp.uint32(0xFFFF0000))
    nr = _packed_rows(d)
    for s in range(nr):
        ref[pl.ds(s, tm, stride=nr), :] = words[:, s * LANES:(s + 1) * LANES]


def _load_packed_rows(ref, out_ref, n_tokens):
    d = out_ref.shape[1]
    nr = _packed_rows(d)
    for s in range(nr):
        words = ref[pl.ds(s, n_tokens, stride=nr), :]
        lo = lax.bitcast_convert_type(words << 16, F32)
        hi = lax.bitcast_convert_type(words & jnp.uint32(0xFFFF0000), F32)
        out_ref[:, s * LANES:(s + 1) * LANES] = lo.astype(BF16)
        out_ref[:, d // 2 + s * LANES:d // 2 + (s + 1) * LANES] = hi.astype(BF16)


def _two_part_specs(tm, d, n_first):
    return [pl.BlockSpec((tm, d), lambda i, *_: (jnp.minimum(i, n_first - 1), 0)),
            pl.BlockSpec((tm, d), lambda i, *_: (jnp.maximum(i - n_first, 0), 0))]


def _norm_kernel(xp_ref, xs_ref, w_ref, wdt1_ref, wdt2_ref, u_ref, dt_ref, *, n_first):
    x = jnp.where(pl.program_id(0) < n_first, xp_ref[...], xs_ref[...])
    r = lax.rsqrt(jnp.mean(x * x, axis=-1, keepdims=True) + EPS)
    u = x * r * w_ref[...]
    u1, u2 = _hi_lo(u)
    u_ref[...] = u1
    dt_ref[...] = _dot_hi_lo(u1, u2, wdt1_ref, wdt2_ref)


def _norm_call(xp, xs, w, wdt1, wdt2):
    d = xp.shape[1]
    t = xp.shape[0] + xs.shape[0]
    tm = _pick_tile(np.gcd(xp.shape[0], xs.shape[0]), (512, 256, 128, 64))
    n_first = xp.shape[0] // tm
    return pl.pallas_call(
        functools.partial(_norm_kernel, n_first=n_first),
        name="k_norm",
        grid=(t // tm,),
        in_specs=_two_part_specs(tm, d, n_first) + [
            pl.BlockSpec((1, d), lambda i: (0, 0)),
            pl.BlockSpec((d, LANES), lambda i: (0, 0)),
            pl.BlockSpec((d, LANES), lambda i: (0, 0))],
        out_specs=[pl.BlockSpec((tm, d), lambda i: (i, 0)),
                   pl.BlockSpec((tm, LANES), lambda i: (i, 0))],
        out_shape=[jax.ShapeDtypeStruct((t, d), BF16), jax.ShapeDtypeStruct((t, LANES), F32)],
        compiler_params=_params(("parallel",)),
    )(xp, xs, w, wdt1, wdt2)


def _inproj_kernel(u_ref, wa_ref, wb_ref, o_ref, wbf, *, n_head_tiles):
    j = pl.program_id(0)

    @pl.when(pl.program_id(1) == 0)
    def _():
        wbf[...] = jnp.where(j < n_head_tiles, wa_ref[...], wb_ref[...]).astype(BF16)

    r = jnp.dot(u_ref[...], wbf[...], preferred_element_type=F32)
    for k in range(o_ref.shape[0]):
        o_ref[k] = r[:, k * COL_BLOCK:(k + 1) * COL_BLOCK]


def _inproj_call(u, w_head, w_tail, n_head_cols):
    t, d = u.shape
    per = 2
    tn = per * COL_BLOCK
    tm = _pick_tile(t, (1024, 512, 256, 128, 64))
    n_head_tiles = n_head_cols // tn
    n_tiles = n_head_tiles + w_tail.shape[1] // tn
    return pl.pallas_call(
        functools.partial(_inproj_kernel, n_head_tiles=n_head_tiles),
        name="k_inproj",
        grid=(n_tiles, t // tm),
        in_specs=[pl.BlockSpec((tm, d), lambda j, i: (i, 0)),
                  pl.BlockSpec((d, tn), lambda j, i: (0, jnp.minimum(j, n_head_tiles - 1))),
                  pl.BlockSpec((d, tn), lambda j, i: (0, jnp.maximum(j - n_head_tiles, 0)))],
        out_specs=pl.BlockSpec((per, tm, COL_BLOCK), lambda j, i: (j, i, 0)),
        out_shape=jax.ShapeDtypeStruct((n_tiles * per, t, COL_BLOCK), F32),
        scratch_shapes=[pltpu.VMEM((d, tn), BF16)],
        compiler_params=_params(("arbitrary", "arbitrary")),
    )(u, w_head, w_tail)


def _pool_chunk(c, first, pu_ref, pool0_ref, wp_ref, scale_ref, yp_ref, pbuf, n_prompt_chunks):
    hist = POOL_BUF + 1

    @pl.when(first)
    def _():
        pbuf[0:hist, :] = pool0_ref[...]

    @pl.when(jnp.logical_not(first))
    def _():
        pbuf[0:hist, :] = pbuf[CHUNK:CHUNK + hist, :]

    for g in range(len(POOL_WINDOWS)):
        pbuf[hist:hist + CHUNK, g * COL_BLOCK:(g + 1) * COL_BLOCK] = pu_ref[g]

    pos0 = jnp.where(c < n_prompt_chunks, c * CHUNK, PAST_LEN)
    pos = (pos0 + lax.broadcasted_iota(jnp.int32, (CHUNK, 1), 0)).astype(F32)
    for g, win in enumerate(POOL_WINDOWS):
        sl = slice(g * COL_BLOCK, (g + 1) * COL_BLOCK)
        cur = pbuf[hist:hist + CHUNK, sl]
        tot = cur
        for i in range(1, win):
            tot = tot + pbuf[hist - i:hist - i + CHUNK, sl]
        count = jnp.minimum(pos + 1.0, float(win))
        pooled = tot / count - cur
        yp_ref[:, sl] = jnp.dot(pooled.astype(BF16), wp_ref[g],
                                preferred_element_type=F32) * scale_ref[:, sl]


def _mixer_kernel(z_ref, x_ref, bc_ref, pu_ref, dt_ref, ssm0_ref, cx0_ref, cbc0_ref, pool0_ref,
                  dtb_ref, alog_ref, dsk_ref, cwx_ref, cbx_ref, cwbc_ref, cbbc_ref, wp_ref, scale_ref,
                  tril3_ref, e3_ref, diag_ref, caus_ref, bd_ref,
                  yg_ref, yp_ref, ssm_ref, st, bufx, bufbc, exs, pbuf, *, n_prompt_chunks):
    c = pl.program_id(0)
    first = jnp.logical_or(c == 0, c >= n_prompt_chunks)
    last = c >= n_prompt_chunks - 1
    pairs = N_GROUPS * COL_BLOCK // LANES

    @pl.when(c == 0)
    def _():
        st[...] = jnp.zeros_like(st)

    @pl.when(c >= n_prompt_chunks)
    def _():
        for q in range(pairs):
            blk = jnp.concatenate([ssm0_ref[2 * q], ssm0_ref[2 * q + 1]], axis=0)
            g, o = divmod(q * LANES, COL_BLOCK)
            st[g, :, o:o + LANES] = blk.T

    @pl.when(first)
    def _():
        bufx[0:8, :] = cx0_ref[...]
        bufbc[0:8, :] = cbc0_ref[...]

    @pl.when(jnp.logical_not(first))
    def _():
        bufx[0:8, :] = bufx[CHUNK:CHUNK + 8, :]
        bufbc[0:8, :] = bufbc[CHUNK:CHUNK + 8, :]

    for g in range(N_GROUPS):
        bufx[8:8 + CHUNK, g * COL_BLOCK:(g + 1) * COL_BLOCK] = x_ref[g]
    for q in range(N_GROUPS // 2):
        bufbc[8:8 + CHUNK, q * COL_BLOCK:(q + 1) * COL_BLOCK] = bc_ref[q]

    dtv = dt_ref[...] + dtb_ref[...]
    dt = jnp.maximum(dtv, 0.0) + jnp.log1p(jnp.exp(-jnp.abs(dtv)))
    d_a = dt * (-jnp.exp(alog_ref[...]))
    p1, p2, p3 = _split3(d_a)
    acum = jnp.dot(tril3_ref[...], jnp.concatenate([p1, p2, p3], axis=0),
                   preferred_element_type=F32)
    q1, q2, q3 = _split3(jnp.concatenate([acum, dt], axis=0))
    exs[...] = jnp.dot(jnp.concatenate([q1, q2, q3], axis=1), e3_ref[...],
                       preferred_element_type=F32)

    caus = caus_ref[...] > 0.0
    bd = bd_ref[...]
    hw = 4 * HEAD_DIM
    for g in range(N_GROUPS):
        sl = slice(g * COL_BLOCK, (g + 1) * COL_BLOCK)
        xc = cbx_ref[:, sl]
        for k in range(CONV_WIDTH):
            xc = xc + cwx_ref[k:k + 1, sl] * bufx[5 + k:5 + k + CHUNK, sl]
        xs = _silu(xc)
        bc_g = []
        for lo in (g * D_STATE, (N_GROUPS + g) * D_STATE):
            slb = slice(lo, lo + D_STATE)
            acc = cbbc_ref[:, slb]
            for k in range(CONV_WIDTH):
                acc = acc + cwbc_ref[k:k + 1, slb] * bufbc[5 + k:5 + k + CHUNK, slb]
            bc_g.append(_silu(acc))
        b_g, c_g = bc_g
        acx = exs[0:CHUNK, sl]
        dtx = exs[CHUNK:2 * CHUNK, sl]
        alast = acx[CHUNK - 1:CHUNK, :]
        arow = jnp.sum(acx * diag_ref[:, sl], axis=0, keepdims=True)
        xdt = xs * dtx
        xdtb = xdt.astype(BF16)
        bb = b_g.astype(BF16)
        cb = c_g.astype(BF16)
        cb2 = lax.dot_general(cb, jnp.concatenate([bb, bb], axis=0),
                              (((1,), (1,)), ((), ())), preferred_element_type=F32)
        ydiag = []
        for q in range(2):
            lhs = []
            for d in range(2):
                lo = q * hw + d * LANES
                seg = acx[:, lo:lo + LANES] - arow[:, lo:lo + LANES]
                lhs.append((cb2 * jnp.exp(jnp.where(caus, seg, -jnp.inf))).astype(BF16))
            xq = xdtb[:, q * hw:(q + 1) * hw]
            wq = jnp.concatenate([xq, xq, xq, xq], axis=0) * bd
            ydiag.append(jnp.dot(jnp.concatenate(lhs, axis=1), wq, preferred_element_type=F32))
        s_old = st[g]
        yoff = jnp.dot(cb, s_old.astype(BF16), preferred_element_type=F32) * jnp.exp(acx)
        y = jnp.concatenate(ydiag, axis=1) + yoff + dsk_ref[:, sl] * xs
        yg_ref[:, sl] = y * _silu(z_ref[g])
        v = (xdt * jnp.exp(alast - acx)).astype(BF16)
        st[g] = jnp.exp(alast) * s_old + jnp.dot(b_g.T.astype(BF16), v,
                                                 preferred_element_type=F32)

    @pl.when(last)
    def _():
        for q in range(pairs):
            g, o = divmod(q * LANES, COL_BLOCK)
            blk = st[g, :, o:o + LANES].T
            ssm_ref[2 * q] = blk[:HEAD_DIM]
            ssm_ref[2 * q + 1] = blk[HEAD_DIM:]

    _pool_chunk(c, first, pu_ref, pool0_ref, wp_ref, scale_ref, yp_ref, pbuf, n_prompt_chunks)


def _mixer_call(p, dt_raw, ssm_in, cx0, cbc0, pool0, dtb, alog, dsk, cwx, cbx, cwbc, cbbc, wp, scale,
                consts, n_prompt_chunks):
    _, t, _ = p.shape
    n_chunks = t // CHUNK
    n_seq = cx0.shape[0]
    n_heads, head_dim, d_state = ssm_in.shape[1:]
    d_inner = N_GROUPS * COL_BLOCK
    d_bc = N_GROUPS * 2 * D_STATE
    ng = len(POOL_WINDOWS)
    d_pool = ng * COL_BLOCK
    hist = POOL_BUF + 1
    tril3, e3, diag, caus, bd = consts

    def seq(c):
        return jnp.maximum(c - (n_prompt_chunks - 1), 0)

    def const(a):
        return pl.BlockSpec(a.shape, lambda c: (0,) * a.ndim)

    state_block = (None, n_heads, head_dim, d_state)
    return pl.pallas_call(
        functools.partial(_mixer_kernel, n_prompt_chunks=n_prompt_chunks),
        name="k_mixer",
        grid=(n_chunks,),
        in_specs=[pl.BlockSpec((N_GROUPS, CHUNK, COL_BLOCK), lambda c: (P_Z // N_GROUPS, c, 0)),
                  pl.BlockSpec((N_GROUPS, CHUNK, COL_BLOCK), lambda c: (P_X // N_GROUPS, c, 0)),
                  pl.BlockSpec((N_GROUPS // 2, CHUNK, COL_BLOCK),
                               lambda c: (P_BC // (N_GROUPS // 2), c, 0)),
                  pl.BlockSpec((ng, CHUNK, COL_BLOCK), lambda c: (P_POOL // ng, c, 0)),
                  pl.BlockSpec((CHUNK, LANES), lambda c: (c, 0)),
                  pl.BlockSpec(state_block, lambda c: (jnp.maximum(c - n_prompt_chunks, 0), 0, 0, 0)),
                  pl.BlockSpec((None, 8, d_inner), lambda c: (seq(c), 0, 0)),
                  pl.BlockSpec((None, 8, d_bc), lambda c: (seq(c), 0, 0)),
                  pl.BlockSpec((None, hist, d_pool), lambda c: (seq(c), 0, 0)),
                  const(dtb), const(alog), const(dsk), const(cwx), const(cbx),
                  const(cwbc), const(cbbc), const(wp), const(scale),
                  const(tril3), const(e3), const(diag), const(caus), const(bd)],
        out_specs=[pl.BlockSpec((CHUNK, d_inner), lambda c: (c, 0)),
                   pl.BlockSpec((CHUNK, d_pool), lambda c: (c, 0)),
                   pl.BlockSpec(state_block, lambda c: (seq(c), 0, 0, 0))],
        out_shape=[jax.ShapeDtypeStruct((t, d_inner), F32),
                   jax.ShapeDtypeStruct((t, d_pool), F32),
                   jax.ShapeDtypeStruct((n_seq, n_heads, head_dim, d_state), F32)],
        scratch_shapes=[pltpu.VMEM((N_GROUPS, D_STATE, COL_BLOCK), F32),
                        pltpu.VMEM((CHUNK + 8, d_inner), F32),
                        pltpu.VMEM((CHUNK + 8, d_bc), F32),
                        pltpu.VMEM((2 * CHUNK, d_inner), F32),
                        pltpu.VMEM((CHUNK + hist, d_pool), F32)],
        compiler_params=_params(("arbitrary",)),
    )(p, p, p, p, dt_raw, ssm_in, cx0, cbc0, pool0, dtb, alog, dsk, cwx, cbx, cwbc, cbbc, wp, scale,
      tril3, e3, diag, caus, bd)


def _ssd_consts():
    l = np.arange(CHUNK)
    tril = (l[:, None] >= l[None, :]).astype(np.float32)
    tril3 = np.concatenate([tril, tril, tril], axis=1)
    n_heads = N_GROUPS * COL_BLOCK // HEAD_DIM
    col_head = np.arange(n_heads * HEAD_DIM) // HEAD_DIM
    col_pos = np.arange(n_heads * HEAD_DIM) % HEAD_DIM
    e = (np.arange(LANES)[:, None] == col_head[None, :]).astype(np.float32)
    e3 = np.concatenate([e, e, e], axis=0)
    diag = (l[:, None] == col_pos[None, :]).astype(np.float32)
    caus = np.concatenate([tril, tril], axis=1)
    r = np.arange(4 * HEAD_DIM)
    bd = (r[:, None] // HEAD_DIM == r[None, :] // HEAD_DIM).astype(np.float32)
    return (jnp.asarray(tril3, BF16), jnp.asarray(e3, BF16), jnp.asarray(diag, F32),
            jnp.asarray(caus, F32), jnp.asarray(bd, BF16))


def _merge_kernel(yg_ref, yp_ref, gs_ref, gp_ref, nw_ref, wps_ref, wpp_ref, o_ref):
    y = yg_ref[...]
    r = lax.rsqrt(jnp.mean(y * y, axis=-1, keepdims=True) + EPS)
    yn = (y * r * nw_ref[...]).astype(BF16)
    a = jnp.dot(yn, wps_ref[...], preferred_element_type=F32)
    b = jnp.dot(yp_ref[...].astype(BF16), wpp_ref[...], preferred_element_type=F32)
    for j in range(gs_ref.shape[0]):
        sl = slice(j * COL_BLOCK, (j + 1) * COL_BLOCK)
        o_ref[:, sl] = (jax.nn.sigmoid(gs_ref[j]) * a[:, sl]
                        + jax.nn.sigmoid(gp_ref[j]) * b[:, sl]).astype(BF16)


def _merge_call(yg, yp, p, nw, wps, wpp):
    t, d_inner = yg.shape
    d_pool = yp.shape[1]
    d_model = wps.shape[1]
    tm = _pick_tile(t, (256, 128, 64))
    nj = d_model // COL_BLOCK
    once = pl.Buffered(1)
    return pl.pallas_call(
        _merge_kernel,
        name="k_merge",
        grid=(t // tm,),
        in_specs=[pl.BlockSpec((tm, d_inner), lambda i: (i, 0)),
                  pl.BlockSpec((tm, d_pool), lambda i: (i, 0)),
                  pl.BlockSpec((nj, tm, COL_BLOCK), lambda i: (P_GATE_SSD // nj, i, 0)),
                  pl.BlockSpec((nj, tm, COL_BLOCK), lambda i: (P_GATE_POOL // nj, i, 0)),
                  pl.BlockSpec((1, d_inner), lambda i: (0, 0)),
                  pl.BlockSpec((d_inner, d_model), lambda i: (0, 0), pipeline_mode=once),
                  pl.BlockSpec((d_pool, d_model), lambda i: (0, 0), pipeline_mode=once)],
        out_specs=pl.BlockSpec((tm, d_model), lambda i: (i, 0)),
        out_shape=jax.ShapeDtypeStruct((t, d_model), BF16),
        compiler_params=_params(("parallel",)),
    )(yg, yp, p, p, nw, wps, wpp)


def _route_kernel(m_ref, xp_ref, xs_ref, wo_ref, nw_ref, wr1_ref, wr2_ref, br_ref, trs_ref,
                  h_ref, hn_ref, eidx_ref, rank_ref, gate_ref, cnt_ref, carry, *, n_first):
    i = pl.program_id(0)

    @pl.when(i == 0)
    def _():
        carry[...] = jnp.zeros_like(carry)

    x = jnp.where(i < n_first, xp_ref[...], xs_ref[...])
    h = x + jnp.dot(m_ref[...], wo_ref[...], preferred_element_type=F32)
    h_ref[...] = h
    r = lax.rsqrt(jnp.mean(h * h, axis=-1, keepdims=True) + EPS)
    hn = h * r * nw_ref[...]
    h1, h2 = _hi_lo(hn)
    _store_packed_rows(hn_ref, h1)
    logits = _dot_hi_lo(h1, h2, wr1_ref, wr2_ref) + br_ref[...]
    lane = lax.broadcasted_iota(jnp.int32, logits.shape, 1)
    work = logits
    member = jnp.zeros(logits.shape, F32)
    vals, idxs = [], []
    for _ in range(TOP_K):
        m = jnp.max(work, axis=-1, keepdims=True)
        idx = jnp.min(jnp.where(work == m, lane, LANES), axis=-1, keepdims=True)
        hit = lane == idx
        member = member + hit.astype(F32)
        work = jnp.where(hit, -jnp.inf, work)
        vals.append(m)
        idxs.append(idx)
    ex = [jnp.exp(v - vals[0]) for v in vals]
    den = ex[0] + ex[1] + ex[2] + ex[3]
    before = jnp.dot(trs_ref[...], member.astype(BF16), preferred_element_type=F32) + carry[0:1, :]
    eidx = jnp.zeros(logits.shape, jnp.int32)
    rank = jnp.zeros(logits.shape, jnp.int32)
    gate = jnp.zeros(logits.shape, F32)
    for k in range(TOP_K):
        rk = jnp.sum(jnp.where(lane == idxs[k], before, 0.0), axis=-1, keepdims=True)
        eidx = jnp.where(lane == k, idxs[k], eidx)
        rank = jnp.where(lane == k, rk.astype(jnp.int32), rank)
        gate = jnp.where(lane == k, ex[k] / den, gate)
    eidx_ref[...] = eidx
    rank_ref[...] = rank
    gate_ref[...] = gate
    carry[0:1, :] = carry[0:1, :] + jnp.sum(member, axis=0, keepdims=True)
    cnt_ref[...] = carry[...].astype(jnp.int32)


def _route_call(merged, xp, xs, wo, nw, wr1, wr2, br):
    t, d = merged.shape
    tm = _pick_tile(np.gcd(xp.shape[0], xs.shape[0]), (512, 256, 128, 64))
    n_first = xp.shape[0] // tm
    ri = np.arange(tm)
    trs = jnp.asarray((ri[:, None] > ri[None, :]).astype(np.float32), BF16)
    row = lambda i: (i, 0)
    fix = lambda i: (0, 0)
    return pl.pallas_call(
        functools.partial(_route_kernel, n_first=n_first),
        name="k_route",
        grid=(t // tm,),
        in_specs=[pl.BlockSpec((tm, d), row)] + _two_part_specs(tm, d, n_first) + [
                  pl.BlockSpec((d, d), fix), pl.BlockSpec((1, d), fix),
                  pl.BlockSpec((d, LANES), fix), pl.BlockSpec((d, LANES), fix),
                  pl.BlockSpec((1, LANES), fix), pl.BlockSpec((tm, tm), fix)],
        out_specs=[pl.BlockSpec((tm, d), row), pl.BlockSpec((tm * _packed_rows(d), LANES), row),
                   pl.BlockSpec((tm, LANES), row), pl.BlockSpec((tm, LANES), row),
                   pl.BlockSpec((tm, LANES), row), pl.BlockSpec((8, LANES), fix)],
        out_shape=[jax.ShapeDtypeStruct((t, d), F32),
                   jax.ShapeDtypeStruct((t * _packed_rows(d), LANES), jnp.uint32),
                   jax.ShapeDtypeStruct((t, LANES), jnp.int32),
                   jax.ShapeDtypeStruct((t, LANES), jnp.int32),
                   jax.ShapeDtypeStruct((t, LANES), F32),
                   jax.ShapeDtypeStruct((8, LANES), jnp.int32)],
        scratch_shapes=[pltpu.VMEM((8, LANES), F32)],
        compiler_params=_params(("arbitrary",)),
    )(merged, xp, xs, wo, nw, wr1, wr2, br, trs)


_PAD_CHUNKS = tuple(1 << s for s in range(MOE_ROWS.bit_length() - 2, -1, -1))


def _dispatch_kernel(dest_ref, cnt_ref, pstart_ref, x_ref, o_hbm, zbuf, sem, zsem,
                     *, rows, nr, n_experts):
    i = pl.program_id(0)

    def tokens(ref, first, n):
        return ref.at[pl.ds(pl.multiple_of(first * nr, nr), n * nr)]

    def issue(t, carry):
        for k in range(TOP_K):
            d = dest_ref[(i * rows + t) * TOP_K + k]
            pltpu.make_async_copy(tokens(x_ref, t, 1), tokens(o_hbm, d, 1), sem).start(priority=k % 2)
        return carry

    lax.fori_loop(0, rows, issue, 0, unroll=4)
    for k in range(TOP_K):
        pltpu.make_async_copy(x_ref, tokens(o_hbm, 0, rows), sem).wait()

    @pl.when(i == pl.num_programs(0) - 1)
    def _():
        zbuf[...] = jnp.zeros_like(zbuf)

        def pad_copies(e, wait):
            cnt = cnt_ref[e]
            off = pstart_ref[e] + cnt
            npad = (-cnt) & (MOE_ROWS - 1)
            for n_c in _PAD_CHUNKS:
                cp = pltpu.make_async_copy(tokens(zbuf, 0, n_c), tokens(o_hbm, off, n_c), zsem)
                pl.when((npad & n_c) != 0)(cp.wait if wait else cp.start)
                off = off + (npad & n_c)

        def start_e(e, carry):
            pad_copies(e, False)
            return carry

        def wait_e(e, carry):
            pad_copies(e, True)
            return carry

        lax.fori_loop(0, n_experts, start_e, 0)
        lax.fori_loop(0, n_experts, wait_e, 0)

        ztok = zbuf.shape[0] // nr
        first = (pstart_ref[n_experts - 1] + cnt_ref[n_experts - 1] + MOE_ROWS - 1) // MOE_ROWS
        first = first * (MOE_ROWS // ztok)

        def tail_copy(c):
            return pltpu.make_async_copy(zbuf, tokens(o_hbm, c * ztok, ztok), zsem)

        def start_t(c, carry):
            tail_copy(c).start()
            return carry

        def wait_t(c, carry):
            tail_copy(c).wait()
            return carry

        lax.fori_loop(first, o_hbm.shape[0] // zbuf.shape[0], start_t, 0)
        lax.fori_loop(first, o_hbm.shape[0] // zbuf.shape[0], wait_t, 0)


def _dispatch_call(dest, counts, pad_starts, src, n_rows, nr):
    t = src.shape[0] // nr
    rows = _pick_tile(t, (256, 128, 64))
    return pl.pallas_call(
        functools.partial(_dispatch_kernel, rows=rows, nr=nr, n_experts=counts.shape[0]),
        name="k_dispatch",
        grid_spec=pltpu.PrefetchScalarGridSpec(
            num_scalar_prefetch=3,
            grid=(t // rows,),
            in_specs=[pl.BlockSpec((rows * nr, LANES), lambda i, de, cn, ps: (i, 0))],
            out_specs=pl.BlockSpec(memory_space=pl.ANY),
            scratch_shapes=[pltpu.VMEM((_PAD_CHUNKS[0] * nr, LANES), src.dtype),
                            pltpu.SemaphoreType.DMA, pltpu.SemaphoreType.DMA]),
        out_shape=jax.ShapeDtypeStruct((n_rows * nr, LANES), src.dtype),
        compiler_params=_params(("arbitrary",)),
    )(dest, counts, pad_starts, src)


def _moe_kernel(be_ref, nv_ref, nused_ref, x_ref, wg_ref, wl_ref, bg_ref, bl_ref, wd_ref, bd_ref,
                o_ref, xbf, wgb, wlb, wdb):
    b = pl.program_id(0)
    j = pl.program_id(1)
    live = b < nused_ref[0]
    both = jnp.logical_and(live, nv_ref[b] > MOE_HALF)

    @pl.when(jnp.logical_and(j == 0, live))
    def _():
        _load_packed_rows(x_ref, xbf, MOE_ROWS)
        o_ref[...] = jnp.broadcast_to(bd_ref[...], o_ref.shape)

    @pl.when(jnp.logical_and(j == 0, jnp.logical_not(live)))
    def _():
        o_ref[...] = jnp.zeros_like(o_ref)

    def half(lo):
        x = xbf[lo:lo + MOE_HALF, :]
        glu = jnp.dot(x, wgb[...], preferred_element_type=F32) + bg_ref[...]
        lin = jnp.dot(x, wlb[...], preferred_element_type=F32) + bl_ref[...]
        glu = jnp.minimum(glu, SWIGLU_LIMIT)
        lin = jnp.clip(lin, -SWIGLU_LIMIT, SWIGLU_LIMIT)
        act = glu * jax.nn.sigmoid(SWIGLU_ALPHA * glu) * (lin + 1.0)
        o_ref[lo:lo + MOE_HALF, :] += jnp.dot(act.astype(BF16), wdb[...],
                                              preferred_element_type=F32)

    @pl.when(live)
    def _():
        wgb[...] = wg_ref[...].astype(BF16)
        wlb[...] = wl_ref[...].astype(BF16)
        wdb[...] = wd_ref[...].astype(BF16)
        half(0)

    @pl.when(both)
    def _():
        half(MOE_HALF)


def _moe_call(block_e, block_nv, nused, xs, w_up, b_up, w_down, b_down):
    d_ff, d = w_down.shape[1:]
    nr = _packed_rows(d)
    n_rows = xs.shape[0] // nr
    tf = MOE_FF_TILE
    nj = d_ff // tf
    nb = n_rows // MOE_ROWS

    def bb(b, nu):
        return jnp.minimum(b, nu[0] - 1)

    def jj(b, j, nu):
        return jnp.where(b < nu[0], j, nj - 1)

    return pl.pallas_call(
        _moe_kernel,
        name="k_moe",
        grid_spec=pltpu.PrefetchScalarGridSpec(
            num_scalar_prefetch=3,
            grid=(nb, nj),
            in_specs=[pl.BlockSpec((MOE_ROWS * nr, LANES), lambda b, j, be, nv, nu: (bb(b, nu), 0)),
                      pl.BlockSpec((None, d, tf),
                                   lambda b, j, be, nv, nu: (be[bb(b, nu)], 0, jj(b, j, nu))),
                      pl.BlockSpec((None, d, tf),
                                   lambda b, j, be, nv, nu: (be[bb(b, nu)], 0, nj + jj(b, j, nu))),
                      pl.BlockSpec((None, 1, tf),
                                   lambda b, j, be, nv, nu: (be[bb(b, nu)], 0, jj(b, j, nu))),
                      pl.BlockSpec((None, 1, tf),
                                   lambda b, j, be, nv, nu: (be[bb(b, nu)], 0, nj + jj(b, j, nu))),
                      pl.BlockSpec((None, tf, d),
                                   lambda b, j, be, nv, nu: (be[bb(b, nu)], jj(b, j, nu), 0)),
                      pl.BlockSpec((None, 1, d), lambda b, j, be, nv, nu: (be[bb(b, nu)], 0, 0))],
            out_specs=pl.BlockSpec((MOE_ROWS, d), lambda b, j, be, nv, nu: (b, 0)),
            scratch_shapes=[pltpu.VMEM((MOE_ROWS, d), BF16), pltpu.VMEM((d, tf), BF16),
                            pltpu.VMEM((d, tf), BF16), pltpu.VMEM((tf, d), BF16)]),
        out_shape=jax.ShapeDtypeStruct((n_rows, d), F32),
        compiler_params=_params(("arbitrary", "arbitrary")),
    )(block_e, block_nv, nused, xs, w_up, w_up, b_up, b_up, w_down, b_down)


def _combine_kernel(dest_ref, h_ref, gate_ref, nw_ref, eo_hbm, yp_ref, ys_ref, gbuf, sem,
                    *, rows, n_first):
    i = pl.program_id(0)
    n = pl.num_programs(0)
    slot = i % 2

    def gather(tile, into):
        def issue(t, carry):
            for k in range(TOP_K):
                d = dest_ref[(tile * rows + t) * TOP_K + k]
                pltpu.make_async_copy(eo_hbm.at[d], gbuf.at[into, k, t],
                                      sem.at[into]).start(priority=k % 2)
            return carry

        lax.fori_loop(0, rows, issue, 0, unroll=4)

    @pl.when(i == 0)
    def _():
        gather(i, slot)

    @pl.when(i + 1 < n)
    def _():
        gather(i + 1, 1 - slot)

    for k in range(TOP_K):
        pltpu.make_async_copy(eo_hbm.at[pl.ds(0, rows)], gbuf.at[slot, k], sem.at[slot]).wait()
    gate = gate_ref[...]
    y = h_ref[...]
    for k in range(TOP_K):
        y = y + gate[:, k:k + 1] * gbuf[slot, k]
    r = lax.rsqrt(jnp.mean(y * y, axis=-1, keepdims=True) + EPS)
    y = y * r * nw_ref[...]

    @pl.when(i < n_first)
    def _():
        yp_ref[...] = y

    @pl.when(i >= n_first)
    def _():
        ys_ref[...] = y


def _combine_call(dest, h, gate, nw, eo, t_first):
    t, d = h.shape
    rows = _pick_tile(np.gcd(t_first, t - t_first), (128, 64))
    n_first = t_first // rows
    return pl.pallas_call(
        functools.partial(_combine_kernel, rows=rows, n_first=n_first),
        name="k_combine",
        grid_spec=pltpu.PrefetchScalarGridSpec(
            num_scalar_prefetch=1,
            grid=(t // rows,),
            in_specs=[pl.BlockSpec((rows, d), lambda i, de: (i, 0)),
                      pl.BlockSpec((rows, LANES), lambda i, de: (i, 0)),
                      pl.BlockSpec((1, d), lambda i, de: (0, 0)),
                      pl.BlockSpec(memory_space=pl.ANY)],
            out_specs=_two_part_specs(rows, d, n_first),
            scratch_shapes=[pltpu.VMEM((2, TOP_K, rows, d), F32), pltpu.SemaphoreType.DMA((2,))]),
        out_shape=[jax.ShapeDtypeStruct((t_first, d), F32),
                   jax.ShapeDtypeStruct((t - t_first, d), F32)],
        compiler_params=_params(("arbitrary",)),
    )(dest, h, gate, nw, eo)


def kernel(x_prompt, x_sample, state_ssm, state_conv, state_pool, norm_mix_w, w_in, conv_w, conv_b,
           dt_bias, a_log, d_skip, ssd_norm_w, w_pool, pool_scale, w_proj_ssd, w_proj_pool, w_out,
           norm_ffn_w, w_router, b_router, w_up, b_up, w_down, b_down, norm_final_w):
    batch, seq_len, d_model = x_prompt.shape
    dec_batch, dec_seq, _ = x_sample.shape
    depth, _, n_heads, head_dim, d_state = state_ssm.shape
    assert depth == 1 and batch == 1 and dec_seq == CHUNK and seq_len % CHUNK == 0
    assert head_dim == HEAD_DIM and d_state == D_STATE
    d_inner = n_heads * head_dim
    assert d_inner == N_GROUPS * COL_BLOCK
    d_bc = 2 * N_GROUPS * D_STATE
    d_pool = state_pool.shape[-1]
    assert d_pool == len(POOL_WINDOWS) * COL_BLOCK and state_pool.shape[-2] == POOL_BUF
    n_experts = w_router.shape[-1]
    n_prompt_chunks = seq_len // CHUNK
    n_seq = batch + dec_batch
    t_prompt = batch * seq_len
    t = t_prompt + dec_batch * dec_seq

    xp = x_prompt.reshape(t_prompt, d_model)
    xs_tok = x_sample.reshape(-1, d_model)

    wi = w_in[0]
    xbc_end = 2 * d_inner + d_bc
    dt_end = xbc_end + n_heads
    w_dt1, w_dt2 = _hi_lo(jnp.pad(wi[:, xbc_end:dt_end], ((0, 0), (0, LANES - n_heads))))
    pad_h = lambda a: jnp.pad(a.reshape(1, n_heads), ((0, 0), (0, LANES - n_heads)))
    cw = conv_w[0]
    cwx, cwbc = cw[:, :d_inner], cw[:, d_inner:]
    cb = conv_b[0].reshape(1, -1)
    cbx, cbbc = cb[:, :d_inner], cb[:, d_inner:]
    dsk = jnp.repeat(d_skip[0], head_dim).reshape(1, d_inner)

    conv0 = jnp.pad(state_conv[0], ((batch, 0), (8 - (CONV_WIDTH - 1), 0), (0, 0)))
    cx0, cbc0 = conv0[..., :d_inner], conv0[..., d_inner:]
    pool0 = jnp.pad(state_pool[0], ((batch, 0), (1, 0), (0, 0)))

    u, dt_raw = _norm_call(xp, xs_tok, norm_mix_w[0].reshape(1, -1), w_dt1, w_dt2)
    p = _inproj_call(u, wi, wi[:, dt_end:], xbc_end)
    yg, yp, ssm_new = _mixer_call(p, dt_raw, state_ssm[0], cx0, cbc0, pool0, pad_h(dt_bias[0]),
                                  pad_h(a_log[0]), dsk, cwx, cbx, cwbc, cbbc,
                                  w_pool[0].astype(BF16), pool_scale[0].reshape(1, -1),
                                  _ssd_consts(), n_prompt_chunks)
    merged = _merge_call(yg, yp, p, ssd_norm_w[0].reshape(1, -1), w_proj_ssd[0].astype(BF16),
                         w_proj_pool[0].astype(BF16))

    wr1, wr2 = _hi_lo(jnp.pad(w_router[0], ((0, 0), (0, LANES - n_experts))))
    br = jnp.pad(b_router[0].reshape(1, -1), ((0, 0), (0, LANES - n_experts)), constant_values=-1e30)
    h, hn, eidx, rank, gate, cnt = _route_call(merged, xp, xs_tok, w_out[0].astype(BF16),
                                               norm_ffn_w[0].reshape(1, -1), wr1, wr2, br)

    counts = cnt[0, :n_experts]
    padded = (counts + MOE_ROWS - 1) // MOE_ROWS * MOE_ROWS
    pad_ends = jnp.cumsum(padded)
    pad_starts = pad_ends - padded
    n_blocks = -(-(t * TOP_K) // MOE_ROWS) + n_experts
    n_rows = n_blocks * MOE_ROWS
    e_flat = eidx[:, :TOP_K].reshape(-1)
    dest = (pad_starts[e_flat] + rank[:, :TOP_K].reshape(-1)).astype(jnp.int32)
    block_start = jnp.arange(n_blocks, dtype=jnp.int32) * MOE_ROWS
    block_e = jnp.minimum(jnp.sum(block_start[:, None] >= pad_ends[None, :], axis=1),
                          n_experts - 1).astype(jnp.int32)
    block_nv = jnp.clip(pad_starts[block_e] + counts[block_e] - block_start, 0,
                        MOE_ROWS).astype(jnp.int32)
    nused = (pad_ends[-1:] // MOE_ROWS).astype(jnp.int32)

    xs = _dispatch_call(dest, counts, pad_starts.astype(jnp.int32), hn, n_rows, _packed_rows(d_model))
    eo = _moe_call(block_e, block_nv, nused, xs, w_up[0], b_up[0].reshape(n_experts, 1, -1),
                   w_down[0], b_down[0].reshape(n_experts, 1, -1))
    y_p, y_s = _combine_call(dest, h, gate, norm_final_w.reshape(1, -1), eo, t_prompt)

    y_prompt = y_p.reshape(batch, seq_len, d_model)
    y_sample = y_s.reshape(dec_batch, dec_seq, d_model)
    seq_ends = [t_prompt] * batch + [t_prompt + (s + 1) * dec_seq for s in range(dec_batch)]
    tail = jnp.stack([p[:, e - POOL_BUF:e] for e in seq_ends], axis=0)
    tail = tail.transpose(0, 2, 1, 3)
    ctail = tail[:, POOL_BUF - (CONV_WIDTH - 1):]
    conv_x = ctail[:, :, P_X:P_BC].reshape(n_seq, CONV_WIDTH - 1, d_inner)
    conv_bc = ctail[:, :, P_BC:P_POOL].reshape(n_seq, CONV_WIDTH - 1, d_bc)
    conv_new = jnp.concatenate([conv_x, conv_bc], axis=-1)
    pool_new = tail[:, :, P_POOL:P_GATE_SSD].reshape(n_seq, POOL_BUF, d_pool)
    return (y_prompt, y_sample,
            ssm_new[None, :batch], conv_new[None, :batch], pool_new[None, :batch],
            ssm_new[None, batch:], conv_new[None, batch:], pool_new[None, batch:])
```

```python
import functools

import numpy as np
import jax
import jax.numpy as jnp
from jax import lax
from jax.experimental import pallas as pl
from jax.experimental.pallas import tpu as pltpu

F32 = jnp.float32
BF16 = jnp.bfloat16

CHUNK = 64
HEAD_DIM = 64
N_GROUPS = 8
D_STATE = 128
CONV_WIDTH = 4
POOL_WINDOWS = (2, 4, 8, 16)
POOL_BUF = 15
PAST_LEN = 4096
TOP_K = 4
SWIGLU_ALPHA = 1.702
SWIGLU_LIMIT = 7.0
EPS = 1e-5
LANES = 128
COL_BLOCK = 512
MOE_ROWS = 1024
MOE_HALF = MOE_ROWS // 2
MOE_FF_TILE = 256
P_Z, P_X, P_BC, P_POOL, P_GATE_SSD, P_GATE_POOL = 0, 8, 16, 20, 24, 28
VMEM_LIMIT = 56 * 1024 * 1024


def _pick_tile(n, candidates):
    for c in candidates:
        if n % c == 0:
            return c
    raise ValueError(f"no tile for {n} in {candidates}")


def _params(sem, vmem=VMEM_LIMIT):
    return pltpu.CompilerParams(dimension_semantics=sem, vmem_limit_bytes=vmem)


def _split3(v):
    p1 = v.astype(BF16)
    r1 = v - p1.astype(F32)
    p2 = r1.astype(BF16)
    p3 = (r1 - p2.astype(F32)).astype(BF16)
    return p1, p2, p3


def _hi_lo(v):
    hi = v.astype(BF16)
    return hi, (v - hi.astype(F32)).astype(BF16)


def _dot_hi_lo(a1, a2, b1_ref, b2_ref):
    return (jnp.dot(a1, b1_ref[...], preferred_element_type=F32)
            + jnp.dot(a1, b2_ref[...], preferred_element_type=F32)
            + jnp.dot(a2, b1_ref[...], preferred_element_type=F32))


def _silu(v):
    return v * jax.nn.sigmoid(v)


def _packed_rows(d):
    return d // 2 // LANES


def _store_packed_rows(ref, v_bf16):
    tm, d = v_bf16.shape
    bits = lax.bitcast_convert_type(v_bf16.astype(F32), jnp.uint32)
    words = (bits[:, :d // 2] >> 16) | (bits[:, d // 2:] & jnp.uint32(0xFFFF0000))
    nr = _packed_rows(d)
    for s in range(nr):
        ref[pl.ds(s, tm, stride=nr), :] = words[:, s * LANES:(s + 1) * LANES]


def _load_packed_rows(ref, out_ref, n_tokens):
    d = out_ref.shape[1]
    nr = _packed_rows(d)
    for s in range(nr):
        words = ref[pl.ds(s, n_tokens, stride=nr), :]
        lo = lax.bitcast_convert_type(words << 16, F32)
        hi = lax.bitcast_convert_type(words & jnp.uint32(0xFFFF0000), F32)
        out_ref[:, s * LANES:(s + 1) * LANES] = lo.astype(BF16)
        out_ref[:, d // 2 + s * LANES:d // 2 + (s + 1) * LANES] = hi.astype(BF16)


def _two_part_specs(tm, d, n_first):
    return [pl.BlockSpec((tm, d), lambda i, *_: (jnp.minimum(i, n_first - 1), 0)),
            pl.BlockSpec((tm, d), lambda i, *_: (jnp.maximum(i - n_first, 0), 0))]


def _norm_kernel(xp_ref, xs_ref, w_ref, wdt1_ref, wdt2_ref, u_ref, dt_ref, *, n_first):
    x = jnp.where(pl.program_id(0) < n_first, xp_ref[...], xs_ref[...])
    r = lax.rsqrt(jnp.mean(x * x, axis=-1, keepdims=True) + EPS)
    u = x * r * w_ref[...]
    u1, u2 = _hi_lo(u)
    u_ref[...] = u1
    dt_ref[...] = _dot_hi_lo(u1, u2, wdt1_ref, wdt2_ref)


def _norm_call(xp, xs, w, wdt1, wdt2):
    d = xp.shape[1]
    t = xp.shape[0] + xs.shape[0]
    tm = _pick_tile(np.gcd(xp.shape[0], xs.shape[0]), (512, 256, 128, 64))
    n_first = xp.shape[0] // tm
    return pl.pallas_call(
        functools.partial(_norm_kernel, n_first=n_first),
        name="k_norm",
        grid=(t // tm,),
        in_specs=_two_part_specs(tm, d, n_first) + [
            pl.BlockSpec((1, d), lambda i: (0, 0)),
            pl.BlockSpec((d, LANES), lambda i: (0, 0)),
            pl.BlockSpec((d, LANES), lambda i: (0, 0))],
        out_specs=[pl.BlockSpec((tm, d), lambda i: (i, 0)),
                   pl.BlockSpec((tm, LANES), lambda i: (i, 0))],
        out_shape=[jax.ShapeDtypeStruct((t, d), BF16), jax.ShapeDtypeStruct((t, LANES), F32)],
        compiler_params=_params(("parallel",)),
    )(xp, xs, w, wdt1, wdt2)


def _inproj_kernel(u_ref, wt_ref, o_ref, wbf):
    @pl.when(pl.program_id(1) == 0)
    def _():
        wbf[...] = wt_ref[...].T.astype(BF16)

    r = jnp.dot(u_ref[...], wbf[...], preferred_element_type=F32)
    for k in range(o_ref.shape[0]):
        o_ref[k] = r[:, k * COL_BLOCK:(k + 1) * COL_BLOCK]


def _inproj_call(u, w_t, n_head_cols, tail_start):
    t, d = u.shape
    per = 2
    tn = per * COL_BLOCK
    tm = _pick_tile(t, (1024, 512, 256, 128, 64))
    n_head_tiles = n_head_cols // tn
    n_tiles = n_head_tiles + (w_t.shape[0] - tail_start) // tn

    def first_row(j):
        assert tn % 8 == 0 and tail_start % 8 == 0
        return pl.multiple_of(
            jnp.where(j < n_head_tiles, j * tn, tail_start + (j - n_head_tiles) * tn), 8)

    return pl.pallas_call(
        _inproj_kernel,
        name="k_inproj",
        grid=(n_tiles, t // tm),
        in_specs=[pl.BlockSpec((tm, d), lambda j, i: (i, 0)),
                  pl.BlockSpec((pl.Element(tn), pl.Element(d)), lambda j, i: (first_row(j), 0))],
        out_specs=pl.BlockSpec((per, tm, COL_BLOCK), lambda j, i: (j, i, 0)),
        out_shape=jax.ShapeDtypeStruct((n_tiles * per, t, COL_BLOCK), F32),
        scratch_shapes=[pltpu.VMEM((d, tn), BF16)],
        compiler_params=_params(("arbitrary", "arbitrary")),
    )(u, w_t)


def _pool_chunk(c, first, pu_ref, pool0_ref, wp_ref, scale_ref, yp_ref, pbuf, n_prompt_chunks):
    hist = POOL_BUF + 1

    @pl.when(first)
    def _():
        pbuf[0:hist, :] = pool0_ref[...]

    @pl.when(jnp.logical_not(first))
    def _():
        pbuf[0:hist, :] = pbuf[CHUNK:CHUNK + hist, :]

    for g in range(len(POOL_WINDOWS)):
        pbuf[hist:hist + CHUNK, g * COL_BLOCK:(g + 1) * COL_BLOCK] = pu_ref[g]

    pos0 = jnp.where(c < n_prompt_chunks, c * CHUNK, PAST_LEN)
    pos = (pos0 + lax.broadcasted_iota(jnp.int32, (CHUNK, 1), 0)).astype(F32)
    for g, win in enumerate(POOL_WINDOWS):
        sl = slice(g * COL_BLOCK, (g + 1) * COL_BLOCK)
        cur = pbuf[hist:hist + CHUNK, sl]
        tot = cur
        for i in range(1, win):
            tot = tot + pbuf[hist - i:hist - i + CHUNK, sl]
        count = jnp.minimum(pos + 1.0, float(win))
        pooled = tot / count - cur
        yp_ref[:, sl] = jnp.dot(pooled.astype(BF16), wp_ref[g],
                                preferred_element_type=F32) * scale_ref[:, sl]


def _mixer_kernel(z_ref, x_ref, bc_ref, pu_ref, dt_ref, ssm0_ref, cx0_ref, cbc0_ref, pool0_ref,
                  dtb_ref, alog_ref, dsk_ref, cwx_ref, cbx_ref, cwbc_ref, cbbc_ref, wp_ref, scale_ref,
                  tril3_ref, e3_ref, diag_ref, caus_ref, bd_ref,
                  yg_ref, yp_ref, ssm_ref, st, bufx, bufbc, exs, pbuf, *, n_prompt_chunks):
    c = pl.program_id(0)
    first = jnp.logical_or(c == 0, c >= n_prompt_chunks)
    last = c >= n_prompt_chunks - 1
    pairs = N_GROUPS * COL_BLOCK // LANES

    @pl.when(c == 0)
    def _():
        st[...] = jnp.zeros_like(st)

    @pl.when(c >= n_prompt_chunks)
    def _():
        for q in range(pairs):
            blk = jnp.concatenate([ssm0_ref[2 * q], ssm0_ref[2 * q + 1]], axis=0)
            g, o = divmod(q * LANES, COL_BLOCK)
            st[g, :, o:o + LANES] = blk.T

    @pl.when(first)
    def _():
        bufx[0:8, :] = cx0_ref[...]
        bufbc[0:8, :] = cbc0_ref[...]

    @pl.when(jnp.logical_not(first))
    def _():
        bufx[0:8, :] = bufx[CHUNK:CHUNK + 8, :]
        bufbc[0:8, :] = bufbc[CHUNK:CHUNK + 8, :]

    for g in range(N_GROUPS):
        bufx[8:8 + CHUNK, g * COL_BLOCK:(g + 1) * COL_BLOCK] = x_ref[g]
    for q in range(N_GROUPS // 2):
        bufbc[8:8 + CHUNK, q * COL_BLOCK:(q + 1) * COL_BLOCK] = bc_ref[q]

    dtv = dt_ref[...] + dtb_ref[...]
    dt = jnp.maximum(dtv, 0.0) + jnp.log1p(jnp.exp(-jnp.abs(dtv)))
    d_a = dt * (-jnp.exp(alog_ref[...]))
    p1, p2, p3 = _split3(d_a)
    acum = jnp.dot(tril3_ref[...], jnp.concatenate([p1, p2, p3], axis=0),
                   preferred_element_type=F32)
    q1, q2, q3 = _split3(jnp.concatenate([acum, dt], axis=0))
    exs[...] = jnp.dot(jnp.concatenate([q1, q2, q3], axis=1), e3_ref[...],
                       preferred_element_type=F32)

    caus = caus_ref[...] > 0.0
    bd = bd_ref[...]
    hw = 4 * HEAD_DIM
    for g in range(N_GROUPS):
        sl = slice(g * COL_BLOCK, (g + 1) * COL_BLOCK)
        xc = cbx_ref[:, sl]
        for k in range(CONV_WIDTH):
            xc = xc + cwx_ref[k:k + 1, sl] * bufx[5 + k:5 + k + CHUNK, sl]
        xs = _silu(xc)
        bc_g = []
        for lo in (g * D_STATE, (N_GROUPS + g) * D_STATE):
            slb = slice(lo, lo + D_STATE)
            acc = cbbc_ref[:, slb]
            for k in range(CONV_WIDTH):
                acc = acc + cwbc_ref[k:k + 1, slb] * bufbc[5 + k:5 + k + CHUNK, slb]
            bc_g.append(_silu(acc))
        b_g, c_g = bc_g
        acx = exs[0:CHUNK, sl]
        dtx = exs[CHUNK:2 * CHUNK, sl]
        alast = acx[CHUNK - 1:CHUNK, :]
        arow = jnp.sum(acx * diag_ref[:, sl], axis=0, keepdims=True)
        xdt = xs * dtx
        xdtb = xdt.astype(BF16)
        bb = b_g.astype(BF16)
        cb = c_g.astype(BF16)
        cb2 = lax.dot_general(cb, jnp.concatenate([bb, bb], axis=0),
                              (((1,), (1,)), ((), ())), preferred_element_type=F32)
        ydiag = []
        for q in range(2):
            lhs = []
            for d in range(2):
                lo = q * hw + d * LANES
                seg = acx[:, lo:lo + LANES] - arow[:, lo:lo + LANES]
                lhs.append((cb2 * jnp.exp(jnp.where(caus, seg, -jnp.inf))).astype(BF16))
            xq = xdtb[:, q * hw:(q + 1) * hw]
            wq = jnp.concatenate([xq, xq, xq, xq], axis=0) * bd
            ydiag.append(jnp.dot(jnp.concatenate(lhs, axis=1), wq, preferred_element_type=F32))
        s_old = st[g]
        yoff = jnp.dot(cb, s_old.astype(BF16), preferred_element_type=F32) * jnp.exp(acx)
        y = jnp.concatenate(ydiag, axis=1) + yoff + dsk_ref[:, sl] * xs
        yg_ref[:, sl] = y * _silu(z_ref[g])
        v = (xdt * jnp.exp(alast - acx)).astype(BF16)
        st[g] = jnp.exp(alast) * s_old + jnp.dot(b_g.T.astype(BF16), v,
                                                 preferred_element_type=F32)

    @pl.when(last)
    def _():
        for q in range(pairs):
            g, o = divmod(q * LANES, COL_BLOCK)
            blk = st[g, :, o:o + LANES].T
            ssm_ref[2 * q] = blk[:HEAD_DIM]
            ssm_ref[2 * q + 1] = blk[HEAD_DIM:]

    _pool_chunk(c, first, pu_ref, pool0_ref, wp_ref, scale_ref, yp_ref, pbuf, n_prompt_chunks)


def _mixer_call(p, dt_raw, ssm_in, cx0, cbc0, pool0, dtb, alog, dsk, cwx, cbx, cwbc, cbbc, wp, scale,
                consts, n_prompt_chunks):
    _, t, _ = p.shape
    n_chunks = t // CHUNK
    n_seq = cx0.shape[0]
    n_heads, head_dim, d_state = ssm_in.shape[1:]
    d_inner = N_GROUPS * COL_BLOCK
    d_bc = N_GROUPS * 2 * D_STATE
    ng = len(POOL_WINDOWS)
    d_pool = ng * COL_BLOCK
    hist = POOL_BUF + 1
    tril3, e3, diag, caus, bd = consts

    def seq(c):
        return jnp.maximum(c - (n_prompt_chunks - 1), 0)

    def const(a):
        return pl.BlockSpec(a.shape, lambda c: (0,) * a.ndim)

    state_block = (None, n_heads, head_dim, d_state)
    return pl.pallas_call(
        functools.partial(_mixer_kernel, n_prompt_chunks=n_prompt_chunks),
        name="k_mixer",
        grid=(n_chunks,),
        in_specs=[pl.BlockSpec((N_GROUPS, CHUNK, COL_BLOCK), lambda c: (P_Z // N_GROUPS, c, 0)),
                  pl.BlockSpec((N_GROUPS, CHUNK, COL_BLOCK), lambda c: (P_X // N_GROUPS, c, 0)),
                  pl.BlockSpec((N_GROUPS // 2, CHUNK, COL_BLOCK),
                               lambda c: (P_BC // (N_GROUPS // 2), c, 0)),
                  pl.BlockSpec((ng, CHUNK, COL_BLOCK), lambda c: (P_POOL // ng, c, 0)),
                  pl.BlockSpec((CHUNK, LANES), lambda c: (c, 0)),
                  pl.BlockSpec(state_block, lambda c: (jnp.maximum(c - n_prompt_chunks, 0), 0, 0, 0)),
                  pl.BlockSpec((None, 8, d_inner), lambda c: (seq(c), 0, 0)),
                  pl.BlockSpec((None, 8, d_bc), lambda c: (seq(c), 0, 0)),
                  pl.BlockSpec((None, hist, d_pool), lambda c: (seq(c), 0, 0)),
                  const(dtb), const(alog), const(dsk), const(cwx), const(cbx),
                  const(cwbc), const(cbbc), const(wp), const(scale),
                  const(tril3), const(e3), const(diag), const(caus), const(bd)],
        out_specs=[pl.BlockSpec((CHUNK, d_inner), lambda c: (c, 0)),
                   pl.BlockSpec((CHUNK, d_pool), lambda c: (c, 0)),
                   pl.BlockSpec(state_block, lambda c: (seq(c), 0, 0, 0))],
        out_shape=[jax.ShapeDtypeStruct((t, d_inner), F32),
                   jax.ShapeDtypeStruct((t, d_pool), F32),
                   jax.ShapeDtypeStruct((n_seq, n_heads, head_dim, d_state), F32)],
        scratch_shapes=[pltpu.VMEM((N_GROUPS, D_STATE, COL_BLOCK), F32),
                        pltpu.VMEM((CHUNK + 8, d_inner), F32),
                        pltpu.VMEM((CHUNK + 8, d_bc), F32),
                        pltpu.VMEM((2 * CHUNK, d_inner), F32),
                        pltpu.VMEM((CHUNK + hist, d_pool), F32)],
        compiler_params=_params(("arbitrary",)),
    )(p, p, p, p, dt_raw, ssm_in, cx0, cbc0, pool0, dtb, alog, dsk, cwx, cbx, cwbc, cbbc, wp, scale,
      tril3, e3, diag, caus, bd)


def _ssd_consts():
    l = np.arange(CHUNK)
    tril = (l[:, None] >= l[None, :]).astype(np.float32)
    tril3 = np.concatenate([tril, tril, tril], axis=1)
    n_heads = N_GROUPS * COL_BLOCK // HEAD_DIM
    col_head = np.arange(n_heads * HEAD_DIM) // HEAD_DIM
    col_pos = np.arange(n_heads * HEAD_DIM) % HEAD_DIM
    e = (np.arange(LANES)[:, None] == col_head[None, :]).astype(np.float32)
    e3 = np.concatenate([e, e, e], axis=0)
    diag = (l[:, None] == col_pos[None, :]).astype(np.float32)
    caus = np.concatenate([tril, tril], axis=1)
    r = np.arange(4 * HEAD_DIM)
    bd = (r[:, None] // HEAD_DIM == r[None, :] // HEAD_DIM).astype(np.float32)
    return (jnp.asarray(tril3, BF16), jnp.asarray(e3, BF16), jnp.asarray(diag, F32),
            jnp.asarray(caus, F32), jnp.asarray(bd, BF16))


def _merge_kernel(yg_ref, yp_ref, gs_ref, gp_ref, nw_ref, wps_ref, wpp_ref, o_ref):
    y = yg_ref[...]
    r = lax.rsqrt(jnp.mean(y * y, axis=-1, keepdims=True) + EPS)
    yn = (y * r * nw_ref[...]).astype(BF16)
    a = jnp.dot(yn, wps_ref[...], preferred_element_type=F32)
    b = jnp.dot(yp_ref[...].astype(BF16), wpp_ref[...], preferred_element_type=F32)
    for j in range(gs_ref.shape[0]):
        sl = slice(j * COL_BLOCK, (j + 1) * COL_BLOCK)
        o_ref[:, sl] = (jax.nn.sigmoid(gs_ref[j]) * a[:, sl]
                        + jax.nn.sigmoid(gp_ref[j]) * b[:, sl]).astype(BF16)


def _merge_call(yg, yp, p, nw, wps, wpp):
    t, d_inner = yg.shape
    d_pool = yp.shape[1]
    d_model = wps.shape[1]
    tm = _pick_tile(t, (256, 128, 64))
    nj = d_model // COL_BLOCK
    once = pl.Buffered(1)
    return pl.pallas_call(
        _merge_kernel,
        name="k_merge",
        grid=(t // tm,),
        in_specs=[pl.BlockSpec((tm, d_inner), lambda i: (i, 0)),
                  pl.BlockSpec((tm, d_pool), lambda i: (i, 0)),
                  pl.BlockSpec((nj, tm, COL_BLOCK), lambda i: (P_GATE_SSD // nj, i, 0)),
                  pl.BlockSpec((nj, tm, COL_BLOCK), lambda i: (P_GATE_POOL // nj, i, 0)),
                  pl.BlockSpec((1, d_inner), lambda i: (0, 0)),
                  pl.BlockSpec((d_inner, d_model), lambda i: (0, 0), pipeline_mode=once),
                  pl.BlockSpec((d_pool, d_model), lambda i: (0, 0), pipeline_mode=once)],
        out_specs=pl.BlockSpec((tm, d_model), lambda i: (i, 0)),
        out_shape=jax.ShapeDtypeStruct((t, d_model), BF16),
        compiler_params=_params(("parallel",)),
    )(yg, yp, p, p, nw, wps, wpp)


def _route_kernel(m_ref, xp_ref, xs_ref, wo_ref, nw_ref, wr1_ref, wr2_ref, br_ref, trs_ref,
                  h_ref, hn_ref, eidx_ref, rank_ref, gate_ref, cnt_ref, carry, *, n_first):
    i = pl.program_id(0)

    @pl.when(i == 0)
    def _():
        carry[...] = jnp.zeros_like(carry)

    x = jnp.where(i < n_first, xp_ref[...], xs_ref[...])
    h = x + jnp.dot(m_ref[...], wo_ref[...], preferred_element_type=F32)
    h_ref[...] = h
    r = lax.rsqrt(jnp.mean(h * h, axis=-1, keepdims=True) + EPS)
    hn = h * r * nw_ref[...]
    h1, h2 = _hi_lo(hn)
    _store_packed_rows(hn_ref, h1)
    logits = _dot_hi_lo(h1, h2, wr1_ref, wr2_ref) + br_ref[...]
    lane = lax.broadcasted_iota(jnp.int32, logits.shape, 1)
    work = logits
    member = jnp.zeros(logits.shape, F32)
    vals, idxs = [], []
    for _ in range(TOP_K):
        m = jnp.max(work, axis=-1, keepdims=True)
        idx = jnp.min(jnp.where(work == m, lane, LANES), axis=-1, keepdims=True)
        hit = lane == idx
        member = member + hit.astype(F32)
        work = jnp.where(hit, -jnp.inf, work)
        vals.append(m)
        idxs.append(idx)
    ex = [jnp.exp(v - vals[0]) for v in vals]
    den = ex[0] + ex[1] + ex[2] + ex[3]
    before = jnp.dot(trs_ref[...], member.astype(BF16), preferred_element_type=F32) + carry[0:1, :]
    eidx = jnp.zeros(logits.shape, jnp.int32)
    rank = jnp.zeros(logits.shape, jnp.int32)
    gate = jnp.zeros(logits.shape, F32)
    for k in range(TOP_K):
        rk = jnp.sum(jnp.where(lane == idxs[k], before, 0.0), axis=-1, keepdims=True)
        eidx = jnp.where(lane == k, idxs[k], eidx)
        rank = jnp.where(lane == k, rk.astype(jnp.int32), rank)
        gate = jnp.where(lane == k, ex[k] / den, gate)
    eidx_ref[...] = eidx
    rank_ref[...] = rank
    gate_ref[...] = gate
    carry[0:1, :] = carry[0:1, :] + jnp.sum(member, axis=0, keepdims=True)
    cnt_ref[...] = carry[...].astype(jnp.int32)


def _route_call(merged, xp, xs, wo, nw, wr1, wr2, br):
    t, d = merged.shape
    tm = _pick_tile(np.gcd(xp.shape[0], xs.shape[0]), (512, 256, 128, 64))
    n_first = xp.shape[0] // tm
    ri = np.arange(tm)
    trs = jnp.asarray((ri[:, None] > ri[None, :]).astype(np.float32), BF16)
    row = lambda i: (i, 0)
    fix = lambda i: (0, 0)
    return pl.pallas_call(
        functools.partial(_route_kernel, n_first=n_first),
        name="k_route",
        grid=(t // tm,),
        in_specs=[pl.BlockSpec((tm, d), row)] + _two_part_specs(tm, d, n_first) + [
                  pl.BlockSpec((d, d), fix), pl.BlockSpec((1, d), fix),
                  pl.BlockSpec((d, LANES), fix), pl.BlockSpec((d, LANES), fix),
                  pl.BlockSpec((1, LANES), fix), pl.BlockSpec((tm, tm), fix)],
        out_specs=[pl.BlockSpec((tm, d), row), pl.BlockSpec((tm * _packed_rows(d), LANES), row),
                   pl.BlockSpec((tm, LANES), row), pl.BlockSpec((tm, LANES), row),
                   pl.BlockSpec((tm, LANES), row), pl.BlockSpec((8, LANES), fix)],
        out_shape=[jax.ShapeDtypeStruct((t, d), F32),
                   jax.ShapeDtypeStruct((t * _packed_rows(d), LANES), jnp.uint32),
                   jax.ShapeDtypeStruct((t, LANES), jnp.int32),
                   jax.ShapeDtypeStruct((t, LANES), jnp.int32),
                   jax.ShapeDtypeStruct((t, LANES), F32),
                   jax.ShapeDtypeStruct((8, LANES), jnp.int32)],
        scratch_shapes=[pltpu.VMEM((8, LANES), F32)],
        compiler_params=_params(("arbitrary",)),
    )(merged, xp, xs, wo, nw, wr1, wr2, br, trs)


_PAD_CHUNKS = tuple(1 << s for s in range(MOE_ROWS.bit_length() - 2, -1, -1))


def _dispatch_kernel(dest_ref, cnt_ref, pstart_ref, x_ref, o_hbm, zbuf, sem, zsem,
                     *, rows, nr, n_experts):
    i = pl.program_id(0)

    def tokens(ref, first, n):
        return ref.at[pl.ds(pl.multiple_of(first * nr, nr), n * nr)]

    def issue(t, carry):
        for k in range(TOP_K):
            d = dest_ref[(i * rows + t) * TOP_K + k]
            pltpu.make_async_copy(tokens(x_ref, t, 1), tokens(o_hbm, d, 1), sem).start(priority=k % 2)
        return carry

    lax.fori_loop(0, rows, issue, 0, unroll=4)
    for k in range(TOP_K):
        pltpu.make_async_copy(x_ref, tokens(o_hbm, 0, rows), sem).wait()

    @pl.when(i == pl.num_programs(0) - 1)
    def _():
        zbuf[...] = jnp.zeros_like(zbuf)

        def pad_copies(e, wait):
            cnt = cnt_ref[e]
            off = pstart_ref[e] + cnt
            npad = (-cnt) & (MOE_ROWS - 1)
            for n_c in _PAD_CHUNKS:
                cp = pltpu.make_async_copy(tokens(zbuf, 0, n_c), tokens(o_hbm, off, n_c), zsem)
                pl.when((npad & n_c) != 0)(cp.wait if wait else cp.start)
                off = off + (npad & n_c)

        def start_e(e, carry):
            pad_copies(e, False)
            return carry

        def wait_e(e, carry):
            pad_copies(e, True)
            return carry

        lax.fori_loop(0, n_experts, start_e, 0)
        lax.fori_loop(0, n_experts, wait_e, 0)

        ztok = zbuf.shape[0] // nr
        first = (pstart_ref[n_experts - 1] + cnt_ref[n_experts - 1] + MOE_ROWS - 1) // MOE_ROWS
        first = first * (MOE_ROWS // ztok)

        def tail_copy(c):
            return pltpu.make_async_copy(zbuf, tokens(o_hbm, c * ztok, ztok), zsem)

        def start_t(c, carry):
            tail_copy(c).start()
            return carry

        def wait_t(c, carry):
            tail_copy(c).wait()
            return carry

        lax.fori_loop(first, o_hbm.shape[0] // zbuf.shape[0], start_t, 0)
        lax.fori_loop(first, o_hbm.shape[0] // zbuf.shape[0], wait_t, 0)


def _dispatch_call(dest, counts, pad_starts, src, n_rows, nr):
    t = src.shape[0] // nr
    rows = _pick_tile(t, (256, 128, 64))
    return pl.pallas_call(
        functools.partial(_dispatch_kernel, rows=rows, nr=nr, n_experts=counts.shape[0]),
        name="k_dispatch",
        grid_spec=pltpu.PrefetchScalarGridSpec(
            num_scalar_prefetch=3,
            grid=(t // rows,),
            in_specs=[pl.BlockSpec((rows * nr, LANES), lambda i, de, cn, ps: (i, 0))],
            out_specs=pl.BlockSpec(memory_space=pl.ANY),
            scratch_shapes=[pltpu.VMEM((_PAD_CHUNKS[0] * nr, LANES), src.dtype),
                            pltpu.SemaphoreType.DMA, pltpu.SemaphoreType.DMA]),
        out_shape=jax.ShapeDtypeStruct((n_rows * nr, LANES), src.dtype),
        compiler_params=_params(("arbitrary",)),
    )(dest, counts, pad_starts, src)


def _moe_kernel(be_ref, nv_ref, nused_ref, x_ref, wg_ref, wl_ref, bg_ref, bl_ref, wd_ref, bd_ref,
                o_ref, xbf, wub, wdb):
    b = pl.program_id(0)
    j = pl.program_id(1)
    live = b < nused_ref[0]
    both = jnp.logical_and(live, nv_ref[b] > MOE_HALF)

    @pl.when(jnp.logical_and(j == 0, live))
    def _():
        _load_packed_rows(x_ref, xbf, MOE_ROWS)
        o_ref[...] = jnp.broadcast_to(bd_ref[...], o_ref.shape)

    @pl.when(jnp.logical_and(j == 0, jnp.logical_not(live)))
    def _():
        o_ref[...] = jnp.zeros_like(o_ref)

    tf = wg_ref.shape[1]

    def cast_weights():
        wub[:, :tf] = wg_ref[...].astype(BF16)
        wub[:, tf:] = wl_ref[...].astype(BF16)
        wdb[...] = wd_ref[...].astype(BF16)

    def half(lo):
        gu = jnp.dot(xbf[lo:lo + MOE_HALF, :], wub[...], preferred_element_type=F32)
        glu = jnp.minimum(gu[:, :tf] + bg_ref[...], SWIGLU_LIMIT)
        lin = jnp.clip(gu[:, tf:] + bl_ref[...], -SWIGLU_LIMIT, SWIGLU_LIMIT)
        act = glu * jax.nn.sigmoid(SWIGLU_ALPHA * glu) * (lin + 1.0)
        o_ref[lo:lo + MOE_HALF, :] += jnp.dot(act.astype(BF16), wdb[...],
                                              preferred_element_type=F32)

    @pl.when(both)
    def _():
        cast_weights()
        half(0)
        half(MOE_HALF)

    @pl.when(jnp.logical_and(live, jnp.logical_not(both)))
    def _():
        cast_weights()
        half(0)


def _moe_call(block_e, block_nv, nused, xs, w_up, b_up, w_down, b_down):
    d_ff, d = w_down.shape[1:]
    nr = _packed_rows(d)
    n_rows = xs.shape[0] // nr
    tf = MOE_FF_TILE
    nj = d_ff // tf
    nb = n_rows // MOE_ROWS

    def bb(b, nu):
        return jnp.minimum(b, nu[0] - 1)

    def jj(b, j, nu):
        return jnp.where(b < nu[0], j, nj - 1)

    return pl.pallas_call(
        _moe_kernel,
        name="k_moe",
        grid_spec=pltpu.PrefetchScalarGridSpec(
            num_scalar_prefetch=3,
            grid=(nb, nj),
            in_specs=[pl.BlockSpec((MOE_ROWS * nr, LANES), lambda b, j, be, nv, nu: (bb(b, nu), 0)),
                      pl.BlockSpec((None, d, tf),
                                   lambda b, j, be, nv, nu: (be[bb(b, nu)], 0, jj(b, j, nu))),
                      pl.BlockSpec((None, d, tf),
                                   lambda b, j, be, nv, nu: (be[bb(b, nu)], 0, nj + jj(b, j, nu))),
                      pl.BlockSpec((None, 1, tf),
                                   lambda b, j, be, nv, nu: (be[bb(b, nu)], 0, jj(b, j, nu))),
                      pl.BlockSpec((None, 1, tf),
                                   lambda b, j, be, nv, nu: (be[bb(b, nu)], 0, nj + jj(b, j, nu))),
                      pl.BlockSpec((None, tf, d),
                                   lambda b, j, be, nv, nu: (be[bb(b, nu)], jj(b, j, nu), 0)),
                      pl.BlockSpec((None, 1, d), lambda b, j, be, nv, nu: (be[bb(b, nu)], 0, 0))],
            out_specs=pl.BlockSpec((MOE_ROWS, d), lambda b, j, be, nv, nu: (b, 0)),
            scratch_shapes=[pltpu.VMEM((MOE_ROWS, d), BF16), pltpu.VMEM((d, 2 * tf), BF16),
                            pltpu.VMEM((tf, d), BF16)]),
        out_shape=jax.ShapeDtypeStruct((n_rows, d), F32),
        compiler_params=_params(("arbitrary", "arbitrary")),
    )(block_e, block_nv, nused, xs, w_up, w_up, b_up, b_up, w_down, b_down)


def _combine_kernel(dest_ref, h_ref, gate_ref, nw_ref, eo_hbm, yp_ref, ys_ref, gbuf, sem,
                    *, rows, n_first):
    i = pl.program_id(0)
    n = pl.num_programs(0)
    slot = i % 2

    def gather(tile, into):
        def issue(t, carry):
            for k in range(TOP_K):
                d = dest_ref[(tile * rows + t) * TOP_K + k]
                pltpu.make_async_copy(eo_hbm.at[d], gbuf.at[into, k, t],
                                      sem.at[into]).start(priority=k % 2)
            return carry

        lax.fori_loop(0, rows, issue, 0, unroll=4)

    @pl.when(i == 0)
    def _():
        gather(i, slot)

    @pl.when(i + 1 < n)
    def _():
        gather(i + 1, 1 - slot)

    for k in range(TOP_K):
        pltpu.make_async_copy(eo_hbm.at[pl.ds(0, rows)], gbuf.at[slot, k], sem.at[slot]).wait()
    gate = gate_ref[...]
    y = h_ref[...]
    for k in range(TOP_K):
        y = y + gate[:, k:k + 1] * gbuf[slot, k]
    r = lax.rsqrt(jnp.mean(y * y, axis=-1, keepdims=True) + EPS)
    y = y * r * nw_ref[...]

    @pl.when(i < n_first)
    def _():
        yp_ref[...] = y

    @pl.when(i >= n_first)
    def _():
        ys_ref[...] = y


def _combine_call(dest, h, gate, nw, eo, t_first):
    t, d = h.shape
    rows = _pick_tile(np.gcd(t_first, t - t_first), (128, 64))
    n_first = t_first // rows
    return pl.pallas_call(
        functools.partial(_combine_kernel, rows=rows, n_first=n_first),
        name="k_combine",
        grid_spec=pltpu.PrefetchScalarGridSpec(
            num_scalar_prefetch=1,
            grid=(t // rows,),
            in_specs=[pl.BlockSpec((rows, d), lambda i, de: (i, 0)),
                      pl.BlockSpec((rows, LANES), lambda i, de: (i, 0)),
                      pl.BlockSpec((1, d), lambda i, de: (0, 0)),
                      pl.BlockSpec(memory_space=pl.ANY)],
            out_specs=_two_part_specs(rows, d, n_first),
            scratch_shapes=[pltpu.VMEM((2, TOP_K, rows, d), F32), pltpu.SemaphoreType.DMA((2,))]),
        out_shape=[jax.ShapeDtypeStruct((t_first, d), F32),
                   jax.ShapeDtypeStruct((t - t_first, d), F32)],
        compiler_params=_params(("arbitrary",)),
    )(dest, h, gate, nw, eo)


def kernel(x_prompt, x_sample, state_ssm, state_conv, state_pool, norm_mix_w, w_in, conv_w, conv_b,
           dt_bias, a_log, d_skip, ssd_norm_w, w_pool, pool_scale, w_proj_ssd, w_proj_pool, w_out,
           norm_ffn_w, w_router, b_router, w_up, b_up, w_down, b_down, norm_final_w):
    batch, seq_len, d_model = x_prompt.shape
    dec_batch, dec_seq, _ = x_sample.shape
    depth, _, n_heads, head_dim, d_state = state_ssm.shape
    assert depth == 1 and batch == 1 and dec_seq == CHUNK and seq_len % CHUNK == 0
    assert head_dim == HEAD_DIM and d_state == D_STATE
    d_inner = n_heads * head_dim
    assert d_inner == N_GROUPS * COL_BLOCK
    d_bc = 2 * N_GROUPS * D_STATE
    d_pool = state_pool.shape[-1]
    assert d_pool == len(POOL_WINDOWS) * COL_BLOCK and state_pool.shape[-2] == POOL_BUF
    n_experts = w_router.shape[-1]
    n_prompt_chunks = seq_len // CHUNK
    n_seq = batch + dec_batch
    t_prompt = batch * seq_len
    t = t_prompt + dec_batch * dec_seq

    xp = x_prompt.reshape(t_prompt, d_model)
    xs_tok = x_sample.reshape(-1, d_model)

    w_t = jnp.swapaxes(w_in[0], 0, 1)
    xbc_end = 2 * d_inner + d_bc
    dt_end = xbc_end + n_heads
    w_dt1, w_dt2 = _hi_lo(jnp.pad(w_t[xbc_end:dt_end].T, ((0, 0), (0, LANES - n_heads))))
    pad_h = lambda a: jnp.pad(a.reshape(1, n_heads), ((0, 0), (0, LANES - n_heads)))
    cw = conv_w[0]
    cwx, cwbc = cw[:, :d_inner], cw[:, d_inner:]
    cb = conv_b[0].reshape(1, -1)
    cbx, cbbc = cb[:, :d_inner], cb[:, d_inner:]
    dsk = jnp.repeat(d_skip[0], head_dim).reshape(1, d_inner)

    conv0 = jnp.pad(state_conv[0], ((batch, 0), (8 - (CONV_WIDTH - 1), 0), (0, 0)))
    cx0, cbc0 = conv0[..., :d_inner], conv0[..., d_inner:]
    pool0 = jnp.pad(state_pool[0], ((batch, 0), (1, 0), (0, 0)))

    u, dt_raw = _norm_call(xp, xs_tok, norm_mix_w[0].reshape(1, -1), w_dt1, w_dt2)
    p = _inproj_call(u, w_t, xbc_end, dt_end)
    yg, yp, ssm_new = _mixer_call(p, dt_raw, state_ssm[0], cx0, cbc0, pool0, pad_h(dt_bias[0]),
                                  pad_h(a_log[0]), dsk, cwx, cbx, cwbc, cbbc,
                                  w_pool[0].astype(BF16), pool_scale[0].reshape(1, -1),
                                  _ssd_consts(), n_prompt_chunks)
    merged = _merge_call(yg, yp, p, ssd_norm_w[0].reshape(1, -1), w_proj_ssd[0].astype(BF16),
                         w_proj_pool[0].astype(BF16))

    wr1, wr2 = _hi_lo(jnp.pad(w_router[0], ((0, 0), (0, LANES - n_experts))))
    br = jnp.pad(b_router[0].reshape(1, -1), ((0, 0), (0, LANES - n_experts)), constant_values=-1e30)
    h, hn, eidx, rank, gate, cnt = _route_call(merged, xp, xs_tok, w_out[0].astype(BF16),
                                               norm_ffn_w[0].reshape(1, -1), wr1, wr2, br)

    counts = cnt[0, :n_experts]
    padded = (counts + MOE_ROWS - 1) // MOE_ROWS * MOE_ROWS
    pad_ends = jnp.cumsum(padded)
    pad_starts = pad_ends - padded
    n_blocks = -(-(t * TOP_K) // MOE_ROWS) + n_experts
    n_rows = n_blocks * MOE_ROWS
    e_flat = eidx[:, :TOP_K].reshape(-1)
    dest = (pad_starts[e_flat] + rank[:, :TOP_K].reshape(-1)).astype(jnp.int32)
    block_start = jnp.arange(n_blocks, dtype=jnp.int32) * MOE_ROWS
    block_e = jnp.minimum(jnp.sum(block_start[:, None] >= pad_ends[None, :], axis=1),
                          n_experts - 1).astype(jnp.int32)
    block_nv = jnp.clip(pad_starts[block_e] + counts[block_e] - block_start, 0,
                        MOE_ROWS).astype(jnp.int32)
    nused = (pad_ends[-1:] // MOE_ROWS).astype(jnp.int32)

    xs = _dispatch_call(dest, counts, pad_starts.astype(jnp.int32), hn, n_rows, _packed_rows(d_model))
    eo = _moe_call(block_e, block_nv, nused, xs, w_up[0], b_up[0].reshape(n_experts, 1, -1),
                   w_down[0], b_down[0].reshape(n_experts, 1, -1))
    y_p, y_s = _combine_call(dest, h, gate, norm_final_w.reshape(1, -1), eo, t_prompt)

    y_prompt = y_p.reshape(batch, seq_len, d_model)
    y_sample = y_s.reshape(dec_batch, dec_seq, d_model)
    seq_ends = [t_prompt] * batch + [t_prompt + (s + 1) * dec_seq for s in range(dec_batch)]
    tail = jnp.stack([p[:, e - POOL_BUF:e] for e in seq_ends], axis=0)
    tail = tail.transpose(0, 2, 1, 3)
    ctail = tail[:, POOL_BUF - (CONV_WIDTH - 1):]
    conv_x = ctail[:, :, P_X:P_BC].reshape(n_seq, CONV_WIDTH - 1, d_inner)
    conv_bc = ctail[:, :, P_BC:P_POOL].reshape(n_seq, CONV_WIDTH - 1, d_bc)
    conv_new = jnp.concatenate([conv_x, conv_bc], axis=-1)
    pool_new = tail[:, :, P_POOL:P_GATE_SSD].reshape(n_seq, POOL_BUF, d_pool)
    return (y_prompt, y_sample,
            ssm_new[None, :batch], conv_new[None, :batch], pool_new[None, :batch],
            ssm_new[None, batch:], conv_new[None, batch:], pool_new[None, batch:])
```

```python
import functools

import numpy as np
import jax
import jax.numpy as jnp
from jax import lax
from jax.experimental import pallas as pl
from jax.experimental.pallas import tpu as pltpu

F32 = jnp.float32
BF16 = jnp.bfloat16

CHUNK = 64
HEAD_DIM = 64
N_GROUPS = 8
D_STATE = 128
CONV_WIDTH = 4
POOL_WINDOWS = (2, 4, 8, 16)
POOL_BUF = 15
PAST_LEN = 4096
TOP_K = 4
SWIGLU_ALPHA = 1.702
SWIGLU_LIMIT = 7.0
EPS = 1e-5
LANES = 128
COL_BLOCK = 512
MOE_ROWS = 1024
MOE_HALF = MOE_ROWS // 2
MOE_FF_TILE = 256
P_Z, P_X, P_BC, P_POOL, P_GATE_SSD, P_GATE_POOL = 0, 8, 16, 20, 24, 28
VMEM_LIMIT = 56 * 1024 * 1024


def _pick_tile(n, candidates):
    for c in candidates:
        if n % c == 0:
            return c
    raise ValueError(f"no tile for {n} in {candidates}")


def _params(sem, vmem=VMEM_LIMIT):
    return pltpu.CompilerParams(dimension_semantics=sem, vmem_limit_bytes=vmem)


def _split3(v):
    p1 = v.astype(BF16)
    r1 = v - p1.astype(F32)
    p2 = r1.astype(BF16)
    p3 = (r1 - p2.astype(F32)).astype(BF16)
    return p1, p2, p3


def _hi_lo(v):
    hi = v.astype(BF16)
    return hi, (v - hi.astype(F32)).astype(BF16)


def _dot_hi_lo(a1, a2, b1_ref, b2_ref):
    return (jnp.dot(a1, b1_ref[...], preferred_element_type=F32)
            + jnp.dot(a1, b2_ref[...], preferred_element_type=F32)
            + jnp.dot(a2, b1_ref[...], preferred_element_type=F32))


def _silu(v):
    return v * jax.nn.sigmoid(v)


def _packed_rows(d):
    return d // 2 // LANES


def _store_packed_rows(ref, v_bf16):
    tm, d = v_bf16.shape
    bits = lax.bitcast_convert_type(v_bf16.astype(F32), jnp.uint32)
    words = (bits[:, :d // 2] >> 16) | (bits[:, d // 2:] & jnp.uint32(0xFFFF0000))
    nr = _packed_rows(d)
    for s in range(nr):
        ref[pl.ds(s, tm, stride=nr), :] = words[:, s * LANES:(s + 1) * LANES]


def _load_packed_rows(ref, out_ref, n_tokens):
    d = out_ref.shape[1]
    nr = _packed_rows(d)
    for s in range(nr):
        words = ref[pl.ds(s, n_tokens, stride=nr), :]
        lo = lax.bitcast_convert_type(words << 16, F32)
        hi = lax.bitcast_convert_type(words & jnp.uint32(0xFFFF0000), F32)
        out_ref[:, s * LANES:(s + 1) * LANES] = lo.astype(BF16)
        out_ref[:, d // 2 + s * LANES:d // 2 + (s + 1) * LANES] = hi.astype(BF16)


def _two_part_specs(tm, d, n_first):
    return [pl.BlockSpec((tm, d), lambda i, *_: (jnp.minimum(i, n_first - 1), 0)),
            pl.BlockSpec((tm, d), lambda i, *_: (jnp.maximum(i - n_first, 0), 0))]


def _norm_kernel(xp_ref, xs_ref, w_ref, wdt1_ref, wdt2_ref, u_ref, dt_ref, *, n_first):
    x = jnp.where(pl.program_id(0) < n_first, xp_ref[...], xs_ref[...])
    r = lax.rsqrt(jnp.mean(x * x, axis=-1, keepdims=True) + EPS)
    u = x * r * w_ref[...]
    u1, u2 = _hi_lo(u)
    u_ref[...] = u1
    dt_ref[...] = _dot_hi_lo(u1, u2, wdt1_ref, wdt2_ref)


def _norm_call(xp, xs, w, wdt1, wdt2):
    d = xp.shape[1]
    t = xp.shape[0] + xs.shape[0]
    tm = _pick_tile(np.gcd(xp.shape[0], xs.shape[0]), (512, 256, 128, 64))
    n_first = xp.shape[0] // tm
    return pl.pallas_call(
        functools.partial(_norm_kernel, n_first=n_first),
        name="k_norm",
        grid=(t // tm,),
        in_specs=_two_part_specs(tm, d, n_first) + [
            pl.BlockSpec((1, d), lambda i: (0, 0)),
            pl.BlockSpec((d, LANES), lambda i: (0, 0)),
            pl.BlockSpec((d, LANES), lambda i: (0, 0))],
        out_specs=[pl.BlockSpec((tm, d), lambda i: (i, 0)),
                   pl.BlockSpec((tm, LANES), lambda i: (i, 0))],
        out_shape=[jax.ShapeDtypeStruct((t, d), BF16), jax.ShapeDtypeStruct((t, LANES), F32)],
        compiler_params=_params(("parallel",)),
    )(xp, xs, w, wdt1, wdt2)


def _inproj_kernel(u_ref, wt_ref, o_ref, wbf):
    @pl.when(pl.program_id(1) == 0)
    def _():
        wbf[...] = wt_ref[...].T.astype(BF16)

    r = jnp.dot(u_ref[...], wbf[...], preferred_element_type=F32)
    for k in range(o_ref.shape[0]):
        o_ref[k] = r[:, k * COL_BLOCK:(k + 1) * COL_BLOCK]


def _inproj_call(u, w_t, n_head_cols, tail_start):
    t, d = u.shape
    per = 2
    tn = per * COL_BLOCK
    tm = _pick_tile(t, (1024, 512, 256, 128, 64))
    n_head_tiles = n_head_cols // tn
    n_tiles = n_head_tiles + (w_t.shape[0] - tail_start) // tn

    def first_row(j):
        assert tn % 8 == 0 and tail_start % 8 == 0
        return pl.multiple_of(
            jnp.where(j < n_head_tiles, j * tn, tail_start + (j - n_head_tiles) * tn), 8)

    return pl.pallas_call(
        _inproj_kernel,
        name="k_inproj",
        grid=(n_tiles, t // tm),
        in_specs=[pl.BlockSpec((tm, d), lambda j, i: (i, 0)),
                  pl.BlockSpec((pl.Element(tn), pl.Element(d)), lambda j, i: (first_row(j), 0))],
        out_specs=pl.BlockSpec((per, tm, COL_BLOCK), lambda j, i: (j, i, 0)),
        out_shape=jax.ShapeDtypeStruct((n_tiles * per, t, COL_BLOCK), F32),
        scratch_shapes=[pltpu.VMEM((d, tn), BF16)],
        compiler_params=_params(("arbitrary", "arbitrary")),
    )(u, w_t)


def _pool_chunk(c, first, pu_ref, pool0_ref, wp_ref, scale_ref, yp_ref, pbuf, n_prompt_chunks):
    hist = POOL_BUF + 1

    @pl.when(first)
    def _():
        pbuf[0:hist, :] = pool0_ref[...]

    @pl.when(jnp.logical_not(first))
    def _():
        pbuf[0:hist, :] = pbuf[CHUNK:CHUNK + hist, :]

    for g in range(len(POOL_WINDOWS)):
        pbuf[hist:hist + CHUNK, g * COL_BLOCK:(g + 1) * COL_BLOCK] = pu_ref[g]

    pos0 = jnp.where(c < n_prompt_chunks, c * CHUNK, PAST_LEN)
    pos = (pos0 + lax.broadcasted_iota(jnp.int32, (CHUNK, 1), 0)).astype(F32)
    for g, win in enumerate(POOL_WINDOWS):
        sl = slice(g * COL_BLOCK, (g + 1) * COL_BLOCK)
        tot = pbuf[:, sl]
        w = 1
        while w < win:
            tot = tot + pltpu.roll(tot, w, axis=0)
            w *= 2
        cur = pbuf[hist:hist + CHUNK, sl]
        count = jnp.minimum(pos + 1.0, float(win))
        pooled = tot[hist:] / count - cur
        yp_ref[:, sl] = jnp.dot(pooled.astype(BF16), wp_ref[g],
                                preferred_element_type=F32) * scale_ref[:, sl]


def _mixer_kernel(z_ref, x_ref, bc_ref, pu_ref, dt_ref, ssm0_ref, cx0_ref, cbc0_ref, pool0_ref,
                  dtb_ref, alog_ref, dsk_ref, cwx_ref, cbx_ref, cwbc_ref, cbbc_ref, wp_ref, scale_ref,
                  tril3_ref, e3_ref, diag_ref, caus_ref, bd_ref,
                  yg_ref, yp_ref, ssm_ref, st, bufx, bufbc, exs, pbuf, *, n_prompt_chunks):
    c = pl.program_id(0)
    first = jnp.logical_or(c == 0, c >= n_prompt_chunks)
    last = c >= n_prompt_chunks - 1
    pairs = N_GROUPS * COL_BLOCK // LANES

    @pl.when(c == 0)
    def _():
        st[...] = jnp.zeros_like(st)

    @pl.when(c >= n_prompt_chunks)
    def _():
        for q in range(pairs):
            blk = jnp.concatenate([ssm0_ref[2 * q], ssm0_ref[2 * q + 1]], axis=0)
            g, o = divmod(q * LANES, COL_BLOCK)
            st[g, :, o:o + LANES] = blk.T

    @pl.when(first)
    def _():
        bufx[0:8, :] = cx0_ref[...]
        bufbc[0:8, :] = cbc0_ref[...]

    @pl.when(jnp.logical_not(first))
    def _():
        bufx[0:8, :] = bufx[CHUNK:CHUNK + 8, :]
        bufbc[0:8, :] = bufbc[CHUNK:CHUNK + 8, :]

    for g in range(N_GROUPS):
        bufx[8:8 + CHUNK, g * COL_BLOCK:(g + 1) * COL_BLOCK] = x_ref[g]
    for q in range(N_GROUPS // 2):
        bufbc[8:8 + CHUNK, q * COL_BLOCK:(q + 1) * COL_BLOCK] = bc_ref[q]

    dtv = dt_ref[...] + dtb_ref[...]
    dt = jnp.maximum(dtv, 0.0) + jnp.log1p(jnp.exp(-jnp.abs(dtv)))
    d_a = dt * (-jnp.exp(alog_ref[...]))
    p1, p2, p3 = _split3(d_a)
    acum = jnp.dot(tril3_ref[...], jnp.concatenate([p1, p2, p3], axis=0),
                   preferred_element_type=F32)
    q1, q2, q3 = _split3(jnp.concatenate([acum, dt], axis=0))
    exs[...] = jnp.dot(jnp.concatenate([q1, q2, q3], axis=1), e3_ref[...],
                       preferred_element_type=F32)

    caus = caus_ref[...] > 0.0
    bd = bd_ref[...]
    hw = 4 * HEAD_DIM
    for g in range(N_GROUPS):
        sl = slice(g * COL_BLOCK, (g + 1) * COL_BLOCK)
        xc = cbx_ref[:, sl]
        for k in range(CONV_WIDTH):
            xc = xc + cwx_ref[k:k + 1, sl] * bufx[5 + k:5 + k + CHUNK, sl]
        xs = _silu(xc)
        bc_g = []
        for lo in (g * D_STATE, (N_GROUPS + g) * D_STATE):
            slb = slice(lo, lo + D_STATE)
            acc = cbbc_ref[:, slb]
            for k in range(CONV_WIDTH):
                acc = acc + cwbc_ref[k:k + 1, slb] * bufbc[5 + k:5 + k + CHUNK, slb]
            bc_g.append(_silu(acc))
        b_g, c_g = bc_g
        acx = exs[0:CHUNK, sl]
        dtx = exs[CHUNK:2 * CHUNK, sl]
        alast = acx[CHUNK - 1:CHUNK, :]
        arow = jnp.sum(acx * diag_ref[:, sl], axis=0, keepdims=True)
        xdt = xs * dtx
        xdtb = xdt.astype(BF16)
        bb = b_g.astype(BF16)
        cb = c_g.astype(BF16)
        cb2 = lax.dot_general(cb, jnp.concatenate([bb, bb], axis=0),
                              (((1,), (1,)), ((), ())), preferred_element_type=F32)
        ydiag = []
        for q in range(2):
            lhs = []
            for d in range(2):
                lo = q * hw + d * LANES
                seg = acx[:, lo:lo + LANES] - arow[:, lo:lo + LANES]
                lhs.append((cb2 * jnp.exp(jnp.where(caus, seg, -jnp.inf))).astype(BF16))
            xq = xdtb[:, q * hw:(q + 1) * hw]
            wq = jnp.concatenate([xq, xq, xq, xq], axis=0) * bd
            ydiag.append(jnp.dot(jnp.concatenate(lhs, axis=1), wq, preferred_element_type=F32))
        s_old = st[g]
        yoff = jnp.dot(cb, s_old.astype(BF16), preferred_element_type=F32) * jnp.exp(acx)
        y = jnp.concatenate(ydiag, axis=1) + yoff + dsk_ref[:, sl] * xs
        yg_ref[:, sl] = y * _silu(z_ref[g])
        v = (xdt * jnp.exp(alast - acx)).astype(BF16)
        st[g] = jnp.exp(alast) * s_old + jnp.dot(b_g.T.astype(BF16), v,
                                                 preferred_element_type=F32)

    @pl.when(last)
    def _():
        for q in range(pairs):
            g, o = divmod(q * LANES, COL_BLOCK)
            blk = st[g, :, o:o + LANES].T
            ssm_ref[2 * q] = blk[:HEAD_DIM]
            ssm_ref[2 * q + 1] = blk[HEAD_DIM:]

    _pool_chunk(c, first, pu_ref, pool0_ref, wp_ref, scale_ref, yp_ref, pbuf, n_prompt_chunks)


def _mixer_call(p, dt_raw, ssm_in, cx0, cbc0, pool0, dtb, alog, dsk, cwx, cbx, cwbc, cbbc, wp, scale,
                consts, n_prompt_chunks):
    _, t, _ = p.shape
    n_chunks = t // CHUNK
    n_seq = cx0.shape[0]
    n_heads, head_dim, d_state = ssm_in.shape[1:]
    d_inner = N_GROUPS * COL_BLOCK
    d_bc = N_GROUPS * 2 * D_STATE
    ng = len(POOL_WINDOWS)
    d_pool = ng * COL_BLOCK
    hist = POOL_BUF + 1
    tril3, e3, diag, caus, bd = consts

    def seq(c):
        return jnp.maximum(c - (n_prompt_chunks - 1), 0)

    def const(a):
        return pl.BlockSpec(a.shape, lambda c: (0,) * a.ndim)

    state_block = (None, n_heads, head_dim, d_state)
    return pl.pallas_call(
        functools.partial(_mixer_kernel, n_prompt_chunks=n_prompt_chunks),
        name="k_mixer",
        grid=(n_chunks,),
        in_specs=[pl.BlockSpec((N_GROUPS, CHUNK, COL_BLOCK), lambda c: (P_Z // N_GROUPS, c, 0)),
                  pl.BlockSpec((N_GROUPS, CHUNK, COL_BLOCK), lambda c: (P_X // N_GROUPS, c, 0)),
                  pl.BlockSpec((N_GROUPS // 2, CHUNK, COL_BLOCK),
                               lambda c: (P_BC // (N_GROUPS // 2), c, 0)),
                  pl.BlockSpec((ng, CHUNK, COL_BLOCK), lambda c: (P_POOL // ng, c, 0)),
                  pl.BlockSpec((CHUNK, LANES), lambda c: (c, 0)),
                  pl.BlockSpec(state_block, lambda c: (jnp.maximum(c - n_prompt_chunks, 0), 0, 0, 0)),
                  pl.BlockSpec((None, 8, d_inner), lambda c: (seq(c), 0, 0)),
                  pl.BlockSpec((None, 8, d_bc), lambda c: (seq(c), 0, 0)),
                  pl.BlockSpec((None, hist, d_pool), lambda c: (seq(c), 0, 0)),
                  const(dtb), const(alog), const(dsk), const(cwx), const(cbx),
                  const(cwbc), const(cbbc), const(wp), const(scale),
                  const(tril3), const(e3), const(diag), const(caus), const(bd)],
        out_specs=[pl.BlockSpec((CHUNK, d_inner), lambda c: (c, 0)),
                   pl.BlockSpec((CHUNK, d_pool), lambda c: (c, 0)),
                   pl.BlockSpec(state_block, lambda c: (seq(c), 0, 0, 0))],
        out_shape=[jax.ShapeDtypeStruct((t, d_inner), F32),
                   jax.ShapeDtypeStruct((t, d_pool), F32),
                   jax.ShapeDtypeStruct((n_seq, n_heads, head_dim, d_state), F32)],
        scratch_shapes=[pltpu.VMEM((N_GROUPS, D_STATE, COL_BLOCK), F32),
                        pltpu.VMEM((CHUNK + 8, d_inner), F32),
                        pltpu.VMEM((CHUNK + 8, d_bc), F32),
                        pltpu.VMEM((2 * CHUNK, d_inner), F32),
                        pltpu.VMEM((CHUNK + hist, d_pool), F32)],
        compiler_params=_params(("arbitrary",)),
    )(p, p, p, p, dt_raw, ssm_in, cx0, cbc0, pool0, dtb, alog, dsk, cwx, cbx, cwbc, cbbc, wp, scale,
      tril3, e3, diag, caus, bd)


def _ssd_consts():
    l = np.arange(CHUNK)
    tril = (l[:, None] >= l[None, :]).astype(np.float32)
    tril3 = np.concatenate([tril, tril, tril], axis=1)
    n_heads = N_GROUPS * COL_BLOCK // HEAD_DIM
    col_head = np.arange(n_heads * HEAD_DIM) // HEAD_DIM
    col_pos = np.arange(n_heads * HEAD_DIM) % HEAD_DIM
    e = (np.arange(LANES)[:, None] == col_head[None, :]).astype(np.float32)
    e3 = np.concatenate([e, e, e], axis=0)
    diag = (l[:, None] == col_pos[None, :]).astype(np.float32)
    caus = np.concatenate([tril, tril], axis=1)
    r = np.arange(4 * HEAD_DIM)
    bd = (r[:, None] // HEAD_DIM == r[None, :] // HEAD_DIM).astype(np.float32)
    return (jnp.asarray(tril3, BF16), jnp.asarray(e3, BF16), jnp.asarray(diag, F32),
            jnp.asarray(caus, F32), jnp.asarray(bd, BF16))


def _merge_kernel(yg_ref, yp_ref, gs_ref, gp_ref, nw_ref, wps_ref, wpp_ref, o_ref):
    y = yg_ref[...]
    r = lax.rsqrt(jnp.mean(y * y, axis=-1, keepdims=True) + EPS)
    yn = (y * r * nw_ref[...]).astype(BF16)
    a = jnp.dot(yn, wps_ref[...], preferred_element_type=F32)
    b = jnp.dot(yp_ref[...].astype(BF16), wpp_ref[...], preferred_element_type=F32)
    for j in range(gs_ref.shape[0]):
        sl = slice(j * COL_BLOCK, (j + 1) * COL_BLOCK)
        o_ref[:, sl] = (jax.nn.sigmoid(gs_ref[j]) * a[:, sl]
                        + jax.nn.sigmoid(gp_ref[j]) * b[:, sl]).astype(BF16)


def _merge_call(yg, yp, p, nw, wps, wpp):
    t, d_inner = yg.shape
    d_pool = yp.shape[1]
    d_model = wps.shape[1]
    tm = _pick_tile(t, (256, 128, 64))
    nj = d_model // COL_BLOCK
    once = pl.Buffered(1)
    return pl.pallas_call(
        _merge_kernel,
        name="k_merge",
        grid=(t // tm,),
        in_specs=[pl.BlockSpec((tm, d_inner), lambda i: (i, 0)),
                  pl.BlockSpec((tm, d_pool), lambda i: (i, 0)),
                  pl.BlockSpec((nj, tm, COL_BLOCK), lambda i: (P_GATE_SSD // nj, i, 0)),
                  pl.BlockSpec((nj, tm, COL_BLOCK), lambda i: (P_GATE_POOL // nj, i, 0)),
                  pl.BlockSpec((1, d_inner), lambda i: (0, 0)),
                  pl.BlockSpec((d_inner, d_model), lambda i: (0, 0), pipeline_mode=once),
                  pl.BlockSpec((d_pool, d_model), lambda i: (0, 0), pipeline_mode=once)],
        out_specs=pl.BlockSpec((tm, d_model), lambda i: (i, 0)),
        out_shape=jax.ShapeDtypeStruct((t, d_model), BF16),
        compiler_params=_params(("parallel",)),
    )(yg, yp, p, p, nw, wps, wpp)


def _route_kernel(m_ref, xp_ref, xs_ref, wo_ref, nw_ref, wr1_ref, wr2_ref, br_ref, trs_ref,
                  h_ref, hn_ref, eidx_ref, rank_ref, gate_ref, cnt_ref, carry, *, n_first):
    i = pl.program_id(0)

    @pl.when(i == 0)
    def _():
        carry[...] = jnp.zeros_like(carry)

    x = jnp.where(i < n_first, xp_ref[...], xs_ref[...])
    h = x + jnp.dot(m_ref[...], wo_ref[...], preferred_element_type=F32)
    h_ref[...] = h
    r = lax.rsqrt(jnp.mean(h * h, axis=-1, keepdims=True) + EPS)
    hn = h * r * nw_ref[...]
    h1, h2 = _hi_lo(hn)
    _store_packed_rows(hn_ref, h1)
    logits = _dot_hi_lo(h1, h2, wr1_ref, wr2_ref) + br_ref[...]
    lane = lax.broadcasted_iota(jnp.int32, logits.shape, 1)
    work = logits
    member = jnp.zeros(logits.shape, F32)
    vals, idxs = [], []
    for _ in range(TOP_K):
        m = jnp.max(work, axis=-1, keepdims=True)
        idx = jnp.min(jnp.where(work == m, lane, LANES), axis=-1, keepdims=True)
        hit = lane == idx
        member = member + hit.astype(F32)
        work = jnp.where(hit, -jnp.inf, work)
        vals.append(m)
        idxs.append(idx)
    ex = [jnp.exp(v - vals[0]) for v in vals]
    den = ex[0] + ex[1] + ex[2] + ex[3]
    before = jnp.dot(trs_ref[...], member.astype(BF16), preferred_element_type=F32) + carry[0:1, :]
    eidx = jnp.zeros(logits.shape, jnp.int32)
    rank = jnp.zeros(logits.shape, jnp.int32)
    gate = jnp.zeros(logits.shape, F32)
    for k in range(TOP_K):
        rk = jnp.sum(jnp.where(lane == idxs[k], before, 0.0), axis=-1, keepdims=True)
        eidx = jnp.where(lane == k, idxs[k], eidx)
        rank = jnp.where(lane == k, rk.astype(jnp.int32), rank)
        gate = jnp.where(lane == k, ex[k] / den, gate)
    eidx_ref[...] = eidx
    rank_ref[...] = rank
    gate_ref[...] = gate
    carry[0:1, :] = carry[0:1, :] + jnp.sum(member, axis=0, keepdims=True)
    cnt_ref[...] = carry[...].astype(jnp.int32)


def _route_call(merged, xp, xs, wo, nw, wr1, wr2, br):
    t, d = merged.shape
    tm = _pick_tile(np.gcd(xp.shape[0], xs.shape[0]), (512, 256, 128, 64))
    n_first = xp.shape[0] // tm
    ri = np.arange(tm)
    trs = jnp.asarray((ri[:, None] > ri[None, :]).astype(np.float32), BF16)
    row = lambda i: (i, 0)
    fix = lambda i: (0, 0)
    return pl.pallas_call(
        functools.partial(_route_kernel, n_first=n_first),
        name="k_route",
        grid=(t // tm,),
        in_specs=[pl.BlockSpec((tm, d), row)] + _two_part_specs(tm, d, n_first) + [
                  pl.BlockSpec((d, d), fix), pl.BlockSpec((1, d), fix),
                  pl.BlockSpec((d, LANES), fix), pl.BlockSpec((d, LANES), fix),
                  pl.BlockSpec((1, LANES), fix), pl.BlockSpec((tm, tm), fix)],
        out_specs=[pl.BlockSpec((tm, d), row), pl.BlockSpec((tm * _packed_rows(d), LANES), row),
                   pl.BlockSpec((tm, LANES), row), pl.BlockSpec((tm, LANES), row),
                   pl.BlockSpec((tm, LANES), row), pl.BlockSpec((8, LANES), fix)],
        out_shape=[jax.ShapeDtypeStruct((t, d), F32),
                   jax.ShapeDtypeStruct((t * _packed_rows(d), LANES), jnp.uint32),
                   jax.ShapeDtypeStruct((t, LANES), jnp.int32),
                   jax.ShapeDtypeStruct((t, LANES), jnp.int32),
                   jax.ShapeDtypeStruct((t, LANES), F32),
                   jax.ShapeDtypeStruct((8, LANES), jnp.int32)],
        scratch_shapes=[pltpu.VMEM((8, LANES), F32)],
        compiler_params=_params(("arbitrary",)),
    )(merged, xp, xs, wo, nw, wr1, wr2, br, trs)


_PAD_CHUNKS = tuple(1 << s for s in range(MOE_ROWS.bit_length() - 2, -1, -1))


def _dispatch_kernel(dest_ref, cnt_ref, pstart_ref, x_ref, o_hbm, zbuf, sem, zsem,
                     *, rows, nr, n_experts):
    i = pl.program_id(0)

    def tokens(ref, first, n):
        return ref.at[pl.ds(pl.multiple_of(first * nr, nr), n * nr)]

    def issue(t, carry):
        for k in range(TOP_K):
            d = dest_ref[(i * rows + t) * TOP_K + k]
            pltpu.make_async_copy(tokens(x_ref, t, 1), tokens(o_hbm, d, 1), sem).start(priority=k % 2)
        return carry

    lax.fori_loop(0, rows, issue, 0, unroll=4)
    for k in range(TOP_K):
        pltpu.make_async_copy(x_ref, tokens(o_hbm, 0, rows), sem).wait()

    @pl.when(i == pl.num_programs(0) - 1)
    def _():
        zbuf[...] = jnp.zeros_like(zbuf)

        def pad_copies(e, wait):
            cnt = cnt_ref[e]
            off = pstart_ref[e] + cnt
            npad = (-cnt) & (MOE_ROWS - 1)
            for n_c in _PAD_CHUNKS:
                cp = pltpu.make_async_copy(tokens(zbuf, 0, n_c), tokens(o_hbm, off, n_c), zsem)
                pl.when((npad & n_c) != 0)(cp.wait if wait else cp.start)
                off = off + (npad & n_c)

        def start_e(e, carry):
            pad_copies(e, False)
            return carry

        def wait_e(e, carry):
            pad_copies(e, True)
            return carry

        lax.fori_loop(0, n_experts, start_e, 0)
        lax.fori_loop(0, n_experts, wait_e, 0)

        ztok = zbuf.shape[0] // nr
        first = (pstart_ref[n_experts - 1] + cnt_ref[n_experts - 1] + MOE_ROWS - 1) // MOE_ROWS
        first = first * (MOE_ROWS // ztok)

        def tail_copy(c):
            return pltpu.make_async_copy(zbuf, tokens(o_hbm, c * ztok, ztok), zsem)

        def start_t(c, carry):
            tail_copy(c).start()
            return carry

        def wait_t(c, carry):
            tail_copy(c).wait()
            return carry

        lax.fori_loop(first, o_hbm.shape[0] // zbuf.shape[0], start_t, 0)
        lax.fori_loop(first, o_hbm.shape[0] // zbuf.shape[0], wait_t, 0)


def _dispatch_call(dest, counts, pad_starts, src, n_rows, nr):
    t = src.shape[0] // nr
    rows = _pick_tile(t, (256, 128, 64))
    return pl.pallas_call(
        functools.partial(_dispatch_kernel, rows=rows, nr=nr, n_experts=counts.shape[0]),
        name="k_dispatch",
        grid_spec=pltpu.PrefetchScalarGridSpec(
            num_scalar_prefetch=3,
            grid=(t // rows,),
            in_specs=[pl.BlockSpec((rows * nr, LANES), lambda i, de, cn, ps: (i, 0))],
            out_specs=pl.BlockSpec(memory_space=pl.ANY),
            scratch_shapes=[pltpu.VMEM((_PAD_CHUNKS[0] * nr, LANES), src.dtype),
                            pltpu.SemaphoreType.DMA, pltpu.SemaphoreType.DMA]),
        out_shape=jax.ShapeDtypeStruct((n_rows * nr, LANES), src.dtype),
        compiler_params=_params(("arbitrary",)),
    )(dest, counts, pad_starts, src)


def _moe_kernel(be_ref, nv_ref, nused_ref, x_ref, wg_ref, wl_ref, bg_ref, bl_ref, wd_ref, bd_ref,
                o_ref, xbf, acc, wub, wdb):
    b = pl.program_id(0)
    j = pl.program_id(1)
    live = b < nused_ref[0]
    both = jnp.logical_and(live, nv_ref[b] > MOE_HALF)

    @pl.when(jnp.logical_and(j == 0, live))
    def _():
        _load_packed_rows(x_ref, xbf, MOE_ROWS)
        acc[...] = jnp.broadcast_to(bd_ref[...], acc.shape)

    @pl.when(jnp.logical_and(j == 0, jnp.logical_not(live)))
    def _():
        o_ref[...] = jnp.zeros_like(o_ref)

    tf = wg_ref.shape[1]

    def cast_weights():
        wub[:, :tf] = wg_ref[...].astype(BF16)
        wub[:, tf:] = wl_ref[...].astype(BF16)
        wdb[...] = wd_ref[...].astype(BF16)

    def half(lo):
        gu = jnp.dot(xbf[lo:lo + MOE_HALF, :], wub[...], preferred_element_type=F32)
        glu = jnp.minimum(gu[:, :tf] + bg_ref[...], SWIGLU_LIMIT)
        lin = jnp.clip(gu[:, tf:] + bl_ref[...], -SWIGLU_LIMIT, SWIGLU_LIMIT)
        act = glu * jax.nn.sigmoid(SWIGLU_ALPHA * glu) * (lin + 1.0)
        acc[lo:lo + MOE_HALF, :] += jnp.dot(act.astype(BF16), wdb[...],
                                            preferred_element_type=F32)

    @pl.when(both)
    def _():
        cast_weights()
        half(0)
        half(MOE_HALF)

    @pl.when(jnp.logical_and(live, jnp.logical_not(both)))
    def _():
        cast_weights()
        half(0)

    @pl.when(jnp.logical_and(live, j == pl.num_programs(1) - 1))
    def _():
        _store_packed_rows(o_ref, acc[...].astype(BF16))


def _moe_call(block_e, block_nv, nused, xs, w_up, b_up, w_down, b_down):
    d_ff, d = w_down.shape[1:]
    nr = _packed_rows(d)
    n_rows = xs.shape[0] // nr
    tf = MOE_FF_TILE
    nj = d_ff // tf
    nb = n_rows // MOE_ROWS

    def bb(b, nu):
        return jnp.minimum(b, nu[0] - 1)

    def jj(b, j, nu):
        return jnp.where(b < nu[0], j, nj - 1)

    return pl.pallas_call(
        _moe_kernel,
        name="k_moe",
        grid_spec=pltpu.PrefetchScalarGridSpec(
            num_scalar_prefetch=3,
            grid=(nb, nj),
            in_specs=[pl.BlockSpec((MOE_ROWS * nr, LANES), lambda b, j, be, nv, nu: (bb(b, nu), 0)),
                      pl.BlockSpec((None, d, tf),
                                   lambda b, j, be, nv, nu: (be[bb(b, nu)], 0, jj(b, j, nu))),
                      pl.BlockSpec((None, d, tf),
                                   lambda b, j, be, nv, nu: (be[bb(b, nu)], 0, nj + jj(b, j, nu))),
                      pl.BlockSpec((None, 1, tf),
                                   lambda b, j, be, nv, nu: (be[bb(b, nu)], 0, jj(b, j, nu))),
                      pl.BlockSpec((None, 1, tf),
                                   lambda b, j, be, nv, nu: (be[bb(b, nu)], 0, nj + jj(b, j, nu))),
                      pl.BlockSpec((None, tf, d),
                                   lambda b, j, be, nv, nu: (be[bb(b, nu)], jj(b, j, nu), 0)),
                      pl.BlockSpec((None, 1, d), lambda b, j, be, nv, nu: (be[bb(b, nu)], 0, 0))],
            out_specs=pl.BlockSpec((MOE_ROWS * nr, LANES), lambda b, j, be, nv, nu: (b, 0)),
            scratch_shapes=[pltpu.VMEM((MOE_ROWS, d), BF16), pltpu.VMEM((MOE_ROWS, d), F32),
                            pltpu.VMEM((d, 2 * tf), BF16), pltpu.VMEM((tf, d), BF16)]),
        out_shape=jax.ShapeDtypeStruct((n_rows * nr, LANES), jnp.uint32),
        compiler_params=_params(("arbitrary", "arbitrary")),
    )(block_e, block_nv, nused, xs, w_up, w_up, b_up, b_up, w_down, b_down)


def _combine_kernel(dest_ref, h_ref, gate_ref, nw_ref, eo_hbm, yp_ref, ys_ref, gbuf, sem,
                    *, rows, n_first):
    i = pl.program_id(0)
    n = pl.num_programs(0)
    slot = i % 2
    d = h_ref.shape[1]
    nr = _packed_rows(d)

    def gather(tile, into):
        def issue(t, carry):
            dst_rows = pl.ds(pl.multiple_of(t * nr, nr), nr)
            for k in range(TOP_K):
                e_row = dest_ref[(tile * rows + t) * TOP_K + k]
                src = eo_hbm.at[pl.ds(pl.multiple_of(e_row * nr, nr), nr)]
                pltpu.make_async_copy(src, gbuf.at[into, k, dst_rows],
                                      sem.at[into]).start(priority=k % 2)
            return carry

        lax.fori_loop(0, rows, issue, 0, unroll=4)

    @pl.when(i == 0)
    def _():
        gather(i, slot)

    @pl.when(i + 1 < n)
    def _():
        gather(i + 1, 1 - slot)

    for k in range(TOP_K):
        pltpu.make_async_copy(eo_hbm.at[pl.ds(0, rows * nr)], gbuf.at[slot, k], sem.at[slot]).wait()
    gate = gate_ref[...]
    gates = [gate[:, k:k + 1] for k in range(TOP_K)]
    lo_cols, hi_cols = [], []
    for s in range(nr):
        lo = hi = None
        for k in range(TOP_K):
            words = gbuf[slot, k, pl.ds(s, rows, stride=nr), :]
            a = gates[k] * lax.bitcast_convert_type(words << 16, F32)
            b = gates[k] * lax.bitcast_convert_type(words & jnp.uint32(0xFFFF0000), F32)
            lo = a if lo is None else lo + a
            hi = b if hi is None else hi + b
        lo_cols.append(lo)
        hi_cols.append(hi)
    y = h_ref[...] + jnp.concatenate(lo_cols + hi_cols, axis=1)
    r = lax.rsqrt(jnp.mean(y * y, axis=-1, keepdims=True) + EPS)
    y = y * r * nw_ref[...]

    @pl.when(i < n_first)
    def _():
        yp_ref[...] = y

    @pl.when(i >= n_first)
    def _():
        ys_ref[...] = y


def _combine_call(dest, h, gate, nw, eo, t_first):
    t, d = h.shape
    rows = _pick_tile(np.gcd(t_first, t - t_first), (256, 128, 64))
    n_first = t_first // rows
    return pl.pallas_call(
        functools.partial(_combine_kernel, rows=rows, n_first=n_first),
        name="k_combine",
        grid_spec=pltpu.PrefetchScalarGridSpec(
            num_scalar_prefetch=1,
            grid=(t // rows,),
            in_specs=[pl.BlockSpec((rows, d), lambda i, de: (i, 0)),
                      pl.BlockSpec((rows, LANES), lambda i, de: (i, 0)),
                      pl.BlockSpec((1, d), lambda i, de: (0, 0)),
                      pl.BlockSpec(memory_space=pl.ANY)],
            out_specs=_two_part_specs(rows, d, n_first),
            scratch_shapes=[pltpu.VMEM((2, TOP_K, rows * _packed_rows(d), LANES), jnp.uint32),
                            pltpu.SemaphoreType.DMA((2,))]),
        out_shape=[jax.ShapeDtypeStruct((t_first, d), F32),
                   jax.ShapeDtypeStruct((t - t_first, d), F32)],
        compiler_params=_params(("arbitrary",)),
    )(dest, h, gate, nw, eo)


def kernel(x_prompt, x_sample, state_ssm, state_conv, state_pool, norm_mix_w, w_in, conv_w, conv_b,
           dt_bias, a_log, d_skip, ssd_norm_w, w_pool, pool_scale, w_proj_ssd, w_proj_pool, w_out,
           norm_ffn_w, w_router, b_router, w_up, b_up, w_down, b_down, norm_final_w):
    batch, seq_len, d_model = x_prompt.shape
    dec_batch, dec_seq, _ = x_sample.shape
    depth, _, n_heads, head_dim, d_state = state_ssm.shape
    assert depth == 1 and batch == 1 and dec_seq == CHUNK and seq_len % CHUNK == 0
    assert head_dim == HEAD_DIM and d_state == D_STATE
    d_inner = n_heads * head_dim
    assert d_inner == N_GROUPS * COL_BLOCK
    d_bc = 2 * N_GROUPS * D_STATE
    d_pool = state_pool.shape[-1]
    assert d_pool == len(POOL_WINDOWS) * COL_BLOCK and state_pool.shape[-2] == POOL_BUF
    n_experts = w_router.shape[-1]
    n_prompt_chunks = seq_len // CHUNK
    n_seq = batch + dec_batch
    t_prompt = batch * seq_len
    t = t_prompt + dec_batch * dec_seq

    xp = x_prompt.reshape(t_prompt, d_model)
    xs_tok = x_sample.reshape(-1, d_model)

    w_t = jnp.swapaxes(w_in[0], 0, 1)
    xbc_end = 2 * d_inner + d_bc
    dt_end = xbc_end + n_heads
    w_dt1, w_dt2 = _hi_lo(jnp.pad(w_t[xbc_end:dt_end].T, ((0, 0), (0, LANES - n_heads))))
    pad_h = lambda a: jnp.pad(a.reshape(1, n_heads), ((0, 0), (0, LANES - n_heads)))
    cw = conv_w[0]
    cwx, cwbc = cw[:, :d_inner], cw[:, d_inner:]
    cb = conv_b[0].reshape(1, -1)
    cbx, cbbc = cb[:, :d_inner], cb[:, d_inner:]
    dsk = jnp.repeat(d_skip[0], head_dim).reshape(1, d_inner)

    conv0 = jnp.pad(state_conv[0], ((batch, 0), (8 - (CONV_WIDTH - 1), 0), (0, 0)))
    cx0, cbc0 = conv0[..., :d_inner], conv0[..., d_inner:]
    pool0 = jnp.pad(state_pool[0], ((batch, 0), (1, 0), (0, 0)))

    u, dt_raw = _norm_call(xp, xs_tok, norm_mix_w[0].reshape(1, -1), w_dt1, w_dt2)
    p = _inproj_call(u, w_t, xbc_end, dt_end)
    yg, yp, ssm_new = _mixer_call(p, dt_raw, state_ssm[0], cx0, cbc0, pool0, pad_h(dt_bias[0]),
                                  pad_h(a_log[0]), dsk, cwx, cbx, cwbc, cbbc,
                                  w_pool[0].astype(BF16), pool_scale[0].reshape(1, -1),
                                  _ssd_consts(), n_prompt_chunks)
    merged = _merge_call(yg, yp, p, ssd_norm_w[0].reshape(1, -1), w_proj_ssd[0].astype(BF16),
                         w_proj_pool[0].astype(BF16))

    wr1, wr2 = _hi_lo(jnp.pad(w_router[0], ((0, 0), (0, LANES - n_experts))))
    br = jnp.pad(b_router[0].reshape(1, -1), ((0, 0), (0, LANES - n_experts)), constant_values=-1e30)
    h, hn, eidx, rank, gate, cnt = _route_call(merged, xp, xs_tok, w_out[0].astype(BF16),
                                               norm_ffn_w[0].reshape(1, -1), wr1, wr2, br)

    counts = cnt[0, :n_experts]
    padded = (counts + MOE_ROWS - 1) // MOE_ROWS * MOE_ROWS
    pad_ends = jnp.cumsum(padded)
    pad_starts = pad_ends - padded
    n_blocks = -(-(t * TOP_K) // MOE_ROWS) + n_experts
    n_rows = n_blocks * MOE_ROWS
    e_flat = eidx[:, :TOP_K].reshape(-1)
    dest = (pad_starts[e_flat] + rank[:, :TOP_K].reshape(-1)).astype(jnp.int32)
    block_start = jnp.arange(n_blocks, dtype=jnp.int32) * MOE_ROWS
    block_e = jnp.minimum(jnp.sum(block_start[:, None] >= pad_ends[None, :], axis=1),
                          n_experts - 1).astype(jnp.int32)
    block_nv = jnp.clip(pad_starts[block_e] + counts[block_e] - block_start, 0,
                        MOE_ROWS).astype(jnp.int32)
    nused = (pad_ends[-1:] // MOE_ROWS).astype(jnp.int32)

    xs = _dispatch_call(dest, counts, pad_starts.astype(jnp.int32), hn, n_rows, _packed_rows(d_model))
    eo = _moe_call(block_e, block_nv, nused, xs, w_up[0], b_up[0].reshape(n_experts, 1, -1),
                   w_down[0], b_down[0].reshape(n_experts, 1, -1))
    y_p, y_s = _combine_call(dest, h, gate, norm_final_w.reshape(1, -1), eo, t_prompt)

    y_prompt = y_p.reshape(batch, seq_len, d_model)
    y_sample = y_s.reshape(dec_batch, dec_seq, d_model)
    seq_ends = [t_prompt] * batch + [t_prompt + (s + 1) * dec_seq for s in range(dec_batch)]
    tail = jnp.stack([p[:, e - POOL_BUF:e] for e in seq_ends], axis=0)
    tail = tail.transpose(0, 2, 1, 3)
    ctail = tail[:, POOL_BUF - (CONV_WIDTH - 1):]
    conv_x = ctail[:, :, P_X:P_BC].reshape(n_seq, CONV_WIDTH - 1, d_inner)
    conv_bc = ctail[:, :, P_BC:P_POOL].reshape(n_seq, CONV_WIDTH - 1, d_bc)
    conv_new = jnp.concatenate([conv_x, conv_bc], axis=-1)
    pool_new = tail[:, :, P_POOL:P_GATE_SSD].reshape(n_seq, POOL_BUF, d_pool)
    return (y_prompt, y_sample,
            ssm_new[None, :batch], conv_new[None, :batch], pool_new[None, :batch],
            ssm_new[None, batch:], conv_new[None, batch:], pool_new[None, batch:])
```

```python
import functools

import numpy as np
import jax
import jax.numpy as jnp
from jax import lax
from jax.experimental import pallas as pl
from jax.experimental.pallas import tpu as pltpu

F32 = jnp.float32
BF16 = jnp.bfloat16

CHUNK = 64
HEAD_DIM = 64
N_GROUPS = 8
D_STATE = 128
CONV_WIDTH = 4
POOL_WINDOWS = (2, 4, 8, 16)
POOL_BUF = 15
PAST_LEN = 4096
TOP_K = 4
SWIGLU_ALPHA = 1.702
SWIGLU_LIMIT = 7.0
EPS = 1e-5
LANES = 128
COL_BLOCK = 512
MOE_ROWS = 1024
MOE_HALF = MOE_ROWS // 2
MOE_FF_TILE = 256
P_Z, P_X, P_BC, P_POOL, P_GATE_SSD, P_GATE_POOL = 0, 8, 16, 20, 24, 28
VMEM_LIMIT = 56 * 1024 * 1024


def _pick_tile(n, candidates):
    for c in candidates:
        if n % c == 0:
            return c
    raise ValueError(f"no tile for {n} in {candidates}")


def _params(sem, vmem=VMEM_LIMIT):
    return pltpu.CompilerParams(dimension_semantics=sem, vmem_limit_bytes=vmem)


def _split3(v):
    p1 = v.astype(BF16)
    r1 = v - p1.astype(F32)
    p2 = r1.astype(BF16)
    p3 = (r1 - p2.astype(F32)).astype(BF16)
    return p1, p2, p3


def _hi_lo(v):
    hi = v.astype(BF16)
    return hi, (v - hi.astype(F32)).astype(BF16)


def _dot_hi_lo(a1, a2, b1_ref, b2_ref):
    return (jnp.dot(a1, b1_ref[...], preferred_element_type=F32)
            + jnp.dot(a1, b2_ref[...], preferred_element_type=F32)
            + jnp.dot(a2, b1_ref[...], preferred_element_type=F32))


def _silu(v):
    return v * jax.nn.sigmoid(v)


def _packed_rows(d):
    return d // 2 // LANES


def _store_packed_rows(ref, v_bf16, first_token=0):
    tm, d = v_bf16.shape
    bits = lax.bitcast_convert_type(v_bf16.astype(F32), jnp.uint32)
    words = (bits[:, :d // 2] >> 16) | (bits[:, d // 2:] & jnp.uint32(0xFFFF0000))
    nr = _packed_rows(d)
    for s in range(nr):
        ref[pl.ds(first_token * nr + s, tm, stride=nr), :] = words[:, s * LANES:(s + 1) * LANES]


def _load_packed_rows(ref, out_ref, first_token, n_tokens):
    d = out_ref.shape[1]
    nr = _packed_rows(d)
    rows = slice(first_token, first_token + n_tokens)
    for s in range(nr):
        words = ref[pl.ds(first_token * nr + s, n_tokens, stride=nr), :]
        lo = lax.bitcast_convert_type(words << 16, F32)
        hi = lax.bitcast_convert_type(words & jnp.uint32(0xFFFF0000), F32)
        out_ref[rows, s * LANES:(s + 1) * LANES] = lo.astype(BF16)
        out_ref[rows, d // 2 + s * LANES:d // 2 + (s + 1) * LANES] = hi.astype(BF16)


def _two_part_specs(tm, d, n_first):
    return [pl.BlockSpec((tm, d), lambda i, *_: (jnp.minimum(i, n_first - 1), 0)),
            pl.BlockSpec((tm, d), lambda i, *_: (jnp.maximum(i - n_first, 0), 0))]


def _norm_kernel(xp_ref, xs_ref, w_ref, wdt1_ref, wdt2_ref, u_ref, dt_ref, *, n_first):
    x = jnp.where(pl.program_id(0) < n_first, xp_ref[...], xs_ref[...])
    r = lax.rsqrt(jnp.mean(x * x, axis=-1, keepdims=True) + EPS)
    u = x * r * w_ref[...]
    u1, u2 = _hi_lo(u)
    u_ref[...] = u1
    dt_ref[...] = _dot_hi_lo(u1, u2, wdt1_ref, wdt2_ref)


def _norm_call(xp, xs, w, wdt1, wdt2):
    d = xp.shape[1]
    t = xp.shape[0] + xs.shape[0]
    tm = _pick_tile(np.gcd(xp.shape[0], xs.shape[0]), (512, 256, 128, 64))
    n_first = xp.shape[0] // tm
    return pl.pallas_call(
        functools.partial(_norm_kernel, n_first=n_first),
        name="k_norm",
        grid=(t // tm,),
        in_specs=_two_part_specs(tm, d, n_first) + [
            pl.BlockSpec((1, d), lambda i: (0, 0)),
            pl.BlockSpec((d, LANES), lambda i: (0, 0)),
            pl.BlockSpec((d, LANES), lambda i: (0, 0))],
        out_specs=[pl.BlockSpec((tm, d), lambda i: (i, 0)),
                   pl.BlockSpec((tm, LANES), lambda i: (i, 0))],
        out_shape=[jax.ShapeDtypeStruct((t, d), BF16), jax.ShapeDtypeStruct((t, LANES), F32)],
        compiler_params=_params(("parallel",)),
    )(xp, xs, w, wdt1, wdt2)


def _inproj_kernel(u_ref, wt_ref, o_ref, wbf):
    @pl.when(pl.program_id(1) == 0)
    def _():
        wbf[...] = wt_ref[...].T.astype(BF16)

    r = jnp.dot(u_ref[...], wbf[...], preferred_element_type=F32)
    for k in range(o_ref.shape[0]):
        o_ref[k] = r[:, k * COL_BLOCK:(k + 1) * COL_BLOCK]


def _inproj_call(u, w_t, n_head_cols, tail_start):
    t, d = u.shape
    per = 2
    tn = per * COL_BLOCK
    tm = _pick_tile(t, (1024, 512, 256, 128, 64))
    n_head_tiles = n_head_cols // tn
    n_tiles = n_head_tiles + (w_t.shape[0] - tail_start) // tn

    def first_row(j):
        assert tn % 8 == 0 and tail_start % 8 == 0
        return pl.multiple_of(
            jnp.where(j < n_head_tiles, j * tn, tail_start + (j - n_head_tiles) * tn), 8)

    return pl.pallas_call(
        _inproj_kernel,
        name="k_inproj",
        grid=(n_tiles, t // tm),
        in_specs=[pl.BlockSpec((tm, d), lambda j, i: (i, 0)),
                  pl.BlockSpec((pl.Element(tn), pl.Element(d)), lambda j, i: (first_row(j), 0))],
        out_specs=pl.BlockSpec((per, tm, COL_BLOCK), lambda j, i: (j, i, 0)),
        out_shape=jax.ShapeDtypeStruct((n_tiles * per, t, COL_BLOCK), F32),
        scratch_shapes=[pltpu.VMEM((d, tn), BF16)],
        compiler_params=_params(("arbitrary", "arbitrary")),
    )(u, w_t)


def _pool_chunk(c, first, pu_ref, pool0_ref, wp_ref, scale_ref, yp_ref, pbuf, n_prompt_chunks):
    hist = POOL_BUF + 1

    @pl.when(first)
    def _():
        pbuf[0:hist, :] = pool0_ref[...]

    @pl.when(jnp.logical_not(first))
    def _():
        pbuf[0:hist, :] = pbuf[CHUNK:CHUNK + hist, :]

    for g in range(len(POOL_WINDOWS)):
        pbuf[hist:hist + CHUNK, g * COL_BLOCK:(g + 1) * COL_BLOCK] = pu_ref[g]

    pos0 = jnp.where(c < n_prompt_chunks, c * CHUNK, PAST_LEN)
    pos = (pos0 + lax.broadcasted_iota(jnp.int32, (CHUNK, 1), 0)).astype(F32)
    for g, win in enumerate(POOL_WINDOWS):
        sl = slice(g * COL_BLOCK, (g + 1) * COL_BLOCK)
        tot = pbuf[:, sl]
        w = 1
        while w < win:
            tot = tot + pltpu.roll(tot, w, axis=0)
            w *= 2
        cur = pbuf[hist:hist + CHUNK, sl]
        count = jnp.minimum(pos + 1.0, float(win))
        pooled = tot[hist:] / count - cur
        yp_ref[:, sl] = jnp.dot(pooled.astype(BF16), wp_ref[g],
                                preferred_element_type=F32) * scale_ref[:, sl]


def _mixer_kernel(z_ref, x_ref, bc_ref, pu_ref, dt_ref, ssm0_ref, cx0_ref, cbc0_ref, pool0_ref,
                  dtb_ref, alog_ref, dsk_ref, cwx_ref, cbx_ref, cwbc_ref, cbbc_ref, wp_ref, scale_ref,
                  tril3_ref, e3_ref, diag_ref, caus_ref, bd_ref,
                  yg_ref, yp_ref, ssm_ref, st, bufx, bufbc, exs, pbuf, *, n_prompt_chunks):
    c = pl.program_id(0)
    first = jnp.logical_or(c == 0, c >= n_prompt_chunks)
    last = c >= n_prompt_chunks - 1
    pairs = N_GROUPS * COL_BLOCK // LANES

    @pl.when(c == 0)
    def _():
        st[...] = jnp.zeros_like(st)

    @pl.when(c >= n_prompt_chunks)
    def _():
        for q in range(pairs):
            blk = jnp.concatenate([ssm0_ref[2 * q], ssm0_ref[2 * q + 1]], axis=0)
            g, o = divmod(q * LANES, COL_BLOCK)
            st[g, :, o:o + LANES] = blk.T

    @pl.when(first)
    def _():
        bufx[0:8, :] = cx0_ref[...]
        bufbc[0:8, :] = cbc0_ref[...]

    @pl.when(jnp.logical_not(first))
    def _():
        bufx[0:8, :] = bufx[CHUNK:CHUNK + 8, :]
        bufbc[0:8, :] = bufbc[CHUNK:CHUNK + 8, :]

    for g in range(N_GROUPS):
        bufx[8:8 + CHUNK, g * COL_BLOCK:(g + 1) * COL_BLOCK] = x_ref[g]
    for q in range(N_GROUPS // 2):
        bufbc[8:8 + CHUNK, q * COL_BLOCK:(q + 1) * COL_BLOCK] = bc_ref[q]

    dtv = dt_ref[...] + dtb_ref[...]
    dt = jnp.maximum(dtv, 0.0) + jnp.log1p(jnp.exp(-jnp.abs(dtv)))
    d_a = dt * (-jnp.exp(alog_ref[...]))
    p1, p2, p3 = _split3(d_a)
    acum = jnp.dot(tril3_ref[...], jnp.concatenate([p1, p2, p3], axis=0),
                   preferred_element_type=F32)
    q1, q2, q3 = _split3(jnp.concatenate([acum, dt], axis=0))
    exs[...] = jnp.dot(jnp.concatenate([q1, q2, q3], axis=1), e3_ref[...],
                       preferred_element_type=F32)

    caus = caus_ref[...] > 0.0
    bd = bd_ref[...]
    hw = 4 * HEAD_DIM
    for g in range(N_GROUPS):
        sl = slice(g * COL_BLOCK, (g + 1) * COL_BLOCK)
        xc = cbx_ref[:, sl]
        for k in range(CONV_WIDTH):
            xc = xc + cwx_ref[k:k + 1, sl] * bufx[5 + k:5 + k + CHUNK, sl]
        xs = _silu(xc)
        bc_g = []
        for lo in (g * D_STATE, (N_GROUPS + g) * D_STATE):
            slb = slice(lo, lo + D_STATE)
            acc = cbbc_ref[:, slb]
            for k in range(CONV_WIDTH):
                acc = acc + cwbc_ref[k:k + 1, slb] * bufbc[5 + k:5 + k + CHUNK, slb]
            bc_g.append(_silu(acc))
        b_g, c_g = bc_g
        acx = exs[0:CHUNK, sl]
        dtx = exs[CHUNK:2 * CHUNK, sl]
        alast = acx[CHUNK - 1:CHUNK, :]
        arow = jnp.sum(acx * diag_ref[:, sl], axis=0, keepdims=True)
        xdt = xs * dtx
        xdtb = xdt.astype(BF16)
        bb = b_g.astype(BF16)
        cb = c_g.astype(BF16)
        cb2 = lax.dot_general(cb, jnp.concatenate([bb, bb], axis=0),
                              (((1,), (1,)), ((), ())), preferred_element_type=F32)
        ydiag = []
        for q in range(2):
            lhs = []
            for d in range(2):
                lo = q * hw + d * LANES
                seg = acx[:, lo:lo + LANES] - arow[:, lo:lo + LANES]
                lhs.append((cb2 * jnp.exp(jnp.where(caus, seg, -jnp.inf))).astype(BF16))
            xq = xdtb[:, q * hw:(q + 1) * hw]
            wq = jnp.concatenate([xq, xq, xq, xq], axis=0) * bd
            ydiag.append(jnp.dot(jnp.concatenate(lhs, axis=1), wq, preferred_element_type=F32))
        s_old = st[g]
        yoff = jnp.dot(cb, s_old.astype(BF16), preferred_element_type=F32) * jnp.exp(acx)
        y = jnp.concatenate(ydiag, axis=1) + yoff + dsk_ref[:, sl] * xs
        yg_ref[:, sl] = y * _silu(z_ref[g])
        v = (xdt * jnp.exp(alast - acx)).astype(BF16)
        st[g] = jnp.exp(alast) * s_old + jnp.dot(b_g.T.astype(BF16), v,
                                                 preferred_element_type=F32)

    @pl.when(last)
    def _():
        for q in range(pairs):
            g, o = divmod(q * LANES, COL_BLOCK)
            blk = st[g, :, o:o + LANES].T
            ssm_ref[2 * q] = blk[:HEAD_DIM]
            ssm_ref[2 * q + 1] = blk[HEAD_DIM:]

    _pool_chunk(c, first, pu_ref, pool0_ref, wp_ref, scale_ref, yp_ref, pbuf, n_prompt_chunks)


def _mixer_call(p, dt_raw, ssm_in, cx0, cbc0, pool0, dtb, alog, dsk, cwx, cbx, cwbc, cbbc, wp, scale,
                consts, n_prompt_chunks):
    _, t, _ = p.shape
    n_chunks = t // CHUNK
    n_seq = cx0.shape[0]
    n_heads, head_dim, d_state = ssm_in.shape[1:]
    d_inner = N_GROUPS * COL_BLOCK
    d_bc = N_GROUPS * 2 * D_STATE
    ng = len(POOL_WINDOWS)
    d_pool = ng * COL_BLOCK
    hist = POOL_BUF + 1
    tril3, e3, diag, caus, bd = consts

    def seq(c):
        return jnp.maximum(c - (n_prompt_chunks - 1), 0)

    def const(a):
        return pl.BlockSpec(a.shape, lambda c: (0,) * a.ndim)

    state_block = (None, n_heads, head_dim, d_state)
    return pl.pallas_call(
        functools.partial(_mixer_kernel, n_prompt_chunks=n_prompt_chunks),
        name="k_mixer",
        grid=(n_chunks,),
        in_specs=[pl.BlockSpec((N_GROUPS, CHUNK, COL_BLOCK), lambda c: (P_Z // N_GROUPS, c, 0)),
                  pl.BlockSpec((N_GROUPS, CHUNK, COL_BLOCK), lambda c: (P_X // N_GROUPS, c, 0)),
                  pl.BlockSpec((N_GROUPS // 2, CHUNK, COL_BLOCK),
                               lambda c: (P_BC // (N_GROUPS // 2), c, 0)),
                  pl.BlockSpec((ng, CHUNK, COL_BLOCK), lambda c: (P_POOL // ng, c, 0)),
                  pl.BlockSpec((CHUNK, LANES), lambda c: (c, 0)),
                  pl.BlockSpec(state_block, lambda c: (jnp.maximum(c - n_prompt_chunks, 0), 0, 0, 0)),
                  pl.BlockSpec((None, 8, d_inner), lambda c: (seq(c), 0, 0)),
                  pl.BlockSpec((None, 8, d_bc), lambda c: (seq(c), 0, 0)),
                  pl.BlockSpec((None, hist, d_pool), lambda c: (seq(c), 0, 0)),
                  const(dtb), const(alog), const(dsk), const(cwx), const(cbx),
                  const(cwbc), const(cbbc), const(wp), const(scale),
                  const(tril3), const(e3), const(diag), const(caus), const(bd)],
        out_specs=[pl.BlockSpec((CHUNK, d_inner), lambda c: (c, 0)),
                   pl.BlockSpec((CHUNK, d_pool), lambda c: (c, 0)),
                   pl.BlockSpec(state_block, lambda c: (seq(c), 0, 0, 0))],
        out_shape=[jax.ShapeDtypeStruct((t, d_inner), F32),
                   jax.ShapeDtypeStruct((t, d_pool), F32),
                   jax.ShapeDtypeStruct((n_seq, n_heads, head_dim, d_state), F32)],
        scratch_shapes=[pltpu.VMEM((N_GROUPS, D_STATE, COL_BLOCK), F32),
                        pltpu.VMEM((CHUNK + 8, d_inner), F32),
                        pltpu.VMEM((CHUNK + 8, d_bc), F32),
                        pltpu.VMEM((2 * CHUNK, d_inner), F32),
                        pltpu.VMEM((CHUNK + hist, d_pool), F32)],
        compiler_params=_params(("arbitrary",)),
    )(p, p, p, p, dt_raw, ssm_in, cx0, cbc0, pool0, dtb, alog, dsk, cwx, cbx, cwbc, cbbc, wp, scale,
      tril3, e3, diag, caus, bd)


def _ssd_consts():
    l = np.arange(CHUNK)
    tril = (l[:, None] >= l[None, :]).astype(np.float32)
    tril3 = np.concatenate([tril, tril, tril], axis=1)
    n_heads = N_GROUPS * COL_BLOCK // HEAD_DIM
    col_head = np.arange(n_heads * HEAD_DIM) // HEAD_DIM
    col_pos = np.arange(n_heads * HEAD_DIM) % HEAD_DIM
    e = (np.arange(LANES)[:, None] == col_head[None, :]).astype(np.float32)
    e3 = np.concatenate([e, e, e], axis=0)
    diag = (l[:, None] == col_pos[None, :]).astype(np.float32)
    caus = np.concatenate([tril, tril], axis=1)
    r = np.arange(4 * HEAD_DIM)
    bd = (r[:, None] // HEAD_DIM == r[None, :] // HEAD_DIM).astype(np.float32)
    return (jnp.asarray(tril3, BF16), jnp.asarray(e3, BF16), jnp.asarray(diag, F32),
            jnp.asarray(caus, F32), jnp.asarray(bd, BF16))


def _merge_kernel(yg_ref, yp_ref, gs_ref, gp_ref, nw_ref, wps_ref, wpp_ref, o_ref):
    y = yg_ref[...]
    r = lax.rsqrt(jnp.mean(y * y, axis=-1, keepdims=True) + EPS)
    yn = (y * r * nw_ref[...]).astype(BF16)
    a = jnp.dot(yn, wps_ref[...], preferred_element_type=F32)
    b = jnp.dot(yp_ref[...].astype(BF16), wpp_ref[...], preferred_element_type=F32)
    for j in range(gs_ref.shape[0]):
        sl = slice(j * COL_BLOCK, (j + 1) * COL_BLOCK)
        o_ref[:, sl] = (jax.nn.sigmoid(gs_ref[j]) * a[:, sl]
                        + jax.nn.sigmoid(gp_ref[j]) * b[:, sl]).astype(BF16)


def _merge_call(yg, yp, p, nw, wps, wpp):
    t, d_inner = yg.shape
    d_pool = yp.shape[1]
    d_model = wps.shape[1]
    tm = _pick_tile(t, (256, 128, 64))
    nj = d_model // COL_BLOCK
    once = pl.Buffered(1)
    return pl.pallas_call(
        _merge_kernel,
        name="k_merge",
        grid=(t // tm,),
        in_specs=[pl.BlockSpec((tm, d_inner), lambda i: (i, 0)),
                  pl.BlockSpec((tm, d_pool), lambda i: (i, 0)),
                  pl.BlockSpec((nj, tm, COL_BLOCK), lambda i: (P_GATE_SSD // nj, i, 0)),
                  pl.BlockSpec((nj, tm, COL_BLOCK), lambda i: (P_GATE_POOL // nj, i, 0)),
                  pl.BlockSpec((1, d_inner), lambda i: (0, 0)),
                  pl.BlockSpec((d_inner, d_model), lambda i: (0, 0), pipeline_mode=once),
                  pl.BlockSpec((d_pool, d_model), lambda i: (0, 0), pipeline_mode=once)],
        out_specs=pl.BlockSpec((tm, d_model), lambda i: (i, 0)),
        out_shape=jax.ShapeDtypeStruct((t, d_model), BF16),
        compiler_params=_params(("parallel",)),
    )(yg, yp, p, p, nw, wps, wpp)


def _route_kernel(m_ref, xp_ref, xs_ref, wo_ref, nw_ref, wr1_ref, wr2_ref, br_ref, trs_ref,
                  h_ref, hn_ref, eidx_ref, rank_ref, gate_ref, cnt_ref, carry, *, n_first):
    i = pl.program_id(0)

    @pl.when(i == 0)
    def _():
        carry[...] = jnp.zeros_like(carry)

    x = jnp.where(i < n_first, xp_ref[...], xs_ref[...])
    h = x + jnp.dot(m_ref[...], wo_ref[...], preferred_element_type=F32)
    h_ref[...] = h
    r = lax.rsqrt(jnp.mean(h * h, axis=-1, keepdims=True) + EPS)
    hn = h * r * nw_ref[...]
    h1, h2 = _hi_lo(hn)
    _store_packed_rows(hn_ref, h1)
    logits = _dot_hi_lo(h1, h2, wr1_ref, wr2_ref) + br_ref[...]
    lane = lax.broadcasted_iota(jnp.int32, logits.shape, 1)
    work = logits
    member = jnp.zeros(logits.shape, F32)
    vals, idxs = [], []
    for _ in range(TOP_K):
        m = jnp.max(work, axis=-1, keepdims=True)
        idx = jnp.min(jnp.where(work == m, lane, LANES), axis=-1, keepdims=True)
        hit = lane == idx
        member = member + hit.astype(F32)
        work = jnp.where(hit, -jnp.inf, work)
        vals.append(m)
        idxs.append(idx)
    ex = [jnp.exp(v - vals[0]) for v in vals]
    den = ex[0] + ex[1] + ex[2] + ex[3]
    before = jnp.dot(trs_ref[...], member.astype(BF16), preferred_element_type=F32) + carry[0:1, :]
    eidx = jnp.zeros(logits.shape, jnp.int32)
    rank = jnp.zeros(logits.shape, jnp.int32)
    gate = jnp.zeros(logits.shape, F32)
    for k in range(TOP_K):
        rk = jnp.sum(jnp.where(lane == idxs[k], before, 0.0), axis=-1, keepdims=True)
        eidx = jnp.where(lane == k, idxs[k], eidx)
        rank = jnp.where(lane == k, rk.astype(jnp.int32), rank)
        gate = jnp.where(lane == k, ex[k] / den, gate)
    eidx_ref[...] = eidx
    rank_ref[...] = rank
    gate_ref[...] = gate
    carry[0:1, :] = carry[0:1, :] + jnp.sum(member, axis=0, keepdims=True)
    cnt_ref[...] = carry[...].astype(jnp.int32)


def _route_call(merged, xp, xs, wo, nw, wr1, wr2, br):
    t, d = merged.shape
    tm = _pick_tile(np.gcd(xp.shape[0], xs.shape[0]), (512, 256, 128, 64))
    n_first = xp.shape[0] // tm
    ri = np.arange(tm)
    trs = jnp.asarray((ri[:, None] > ri[None, :]).astype(np.float32), BF16)
    row = lambda i: (i, 0)
    fix = lambda i: (0, 0)
    return pl.pallas_call(
        functools.partial(_route_kernel, n_first=n_first),
        name="k_route",
        grid=(t // tm,),
        in_specs=[pl.BlockSpec((tm, d), row)] + _two_part_specs(tm, d, n_first) + [
                  pl.BlockSpec((d, d), fix), pl.BlockSpec((1, d), fix),
                  pl.BlockSpec((d, LANES), fix), pl.BlockSpec((d, LANES), fix),
                  pl.BlockSpec((1, LANES), fix), pl.BlockSpec((tm, tm), fix)],
        out_specs=[pl.BlockSpec((tm, d), row), pl.BlockSpec((tm * _packed_rows(d), LANES), row),
                   pl.BlockSpec((tm, LANES), row), pl.BlockSpec((tm, LANES), row),
                   pl.BlockSpec((tm, LANES), row), pl.BlockSpec((8, LANES), fix)],
        out_shape=[jax.ShapeDtypeStruct((t, d), F32),
                   jax.ShapeDtypeStruct((t * _packed_rows(d), LANES), jnp.uint32),
                   jax.ShapeDtypeStruct((t, LANES), jnp.int32),
                   jax.ShapeDtypeStruct((t, LANES), jnp.int32),
                   jax.ShapeDtypeStruct((t, LANES), F32),
                   jax.ShapeDtypeStruct((8, LANES), jnp.int32)],
        scratch_shapes=[pltpu.VMEM((8, LANES), F32)],
        compiler_params=_params(("arbitrary",)),
    )(merged, xp, xs, wo, nw, wr1, wr2, br, trs)


_PAD_CHUNKS = tuple(1 << s for s in range(MOE_ROWS.bit_length() - 2, -1, -1))


def _dispatch_kernel(dest_ref, cnt_ref, pstart_ref, x_ref, o_hbm, zbuf, sem, zsem,
                     *, rows, nr, n_experts):
    i = pl.program_id(0)

    def tokens(ref, first, n):
        return ref.at[pl.ds(pl.multiple_of(first * nr, nr), n * nr)]

    def issue(t, carry):
        for k in range(TOP_K):
            d = dest_ref[(i * rows + t) * TOP_K + k]
            pltpu.make_async_copy(tokens(x_ref, t, 1), tokens(o_hbm, d, 1), sem).start(priority=k % 2)
        return carry

    lax.fori_loop(0, rows, issue, 0, unroll=4)
    for k in range(TOP_K):
        pltpu.make_async_copy(x_ref, tokens(o_hbm, 0, rows), sem).wait()

    @pl.when(i == pl.num_programs(0) - 1)
    def _():
        zbuf[...] = jnp.zeros_like(zbuf)

        def pad_copies(e, wait):
            cnt = cnt_ref[e]
            off = pstart_ref[e] + cnt
            npad = (-cnt) & (MOE_ROWS - 1)
            for n_c in _PAD_CHUNKS:
                cp = pltpu.make_async_copy(tokens(zbuf, 0, n_c), tokens(o_hbm, off, n_c), zsem)
                pl.when((npad & n_c) != 0)(cp.wait if wait else cp.start)
                off = off + (npad & n_c)

        def start_e(e, carry):
            pad_copies(e, False)
            return carry

        def wait_e(e, carry):
            pad_copies(e, True)
            return carry

        lax.fori_loop(0, n_experts, start_e, 0)
        lax.fori_loop(0, n_experts, wait_e, 0)

        ztok = zbuf.shape[0] // nr
        first = (pstart_ref[n_experts - 1] + cnt_ref[n_experts - 1] + MOE_ROWS - 1) // MOE_ROWS
        first = first * (MOE_ROWS // ztok)

        def tail_copy(c):
            return pltpu.make_async_copy(zbuf, tokens(o_hbm, c * ztok, ztok), zsem)

        def start_t(c, carry):
            tail_copy(c).start()
            return carry

        def wait_t(c, carry):
            tail_copy(c).wait()
            return carry

        lax.fori_loop(first, o_hbm.shape[0] // zbuf.shape[0], start_t, 0)
        lax.fori_loop(first, o_hbm.shape[0] // zbuf.shape[0], wait_t, 0)


def _dispatch_call(dest, counts, pad_starts, src, n_rows, nr):
    t = src.shape[0] // nr
    rows = _pick_tile(t, (256, 128, 64))
    return pl.pallas_call(
        functools.partial(_dispatch_kernel, rows=rows, nr=nr, n_experts=counts.shape[0]),
        name="k_dispatch",
        grid_spec=pltpu.PrefetchScalarGridSpec(
            num_scalar_prefetch=3,
            grid=(t // rows,),
            in_specs=[pl.BlockSpec((rows * nr, LANES), lambda i, de, cn, ps: (i, 0))],
            out_specs=pl.BlockSpec(memory_space=pl.ANY),
            scratch_shapes=[pltpu.VMEM((_PAD_CHUNKS[0] * nr, LANES), src.dtype),
                            pltpu.SemaphoreType.DMA, pltpu.SemaphoreType.DMA]),
        out_shape=jax.ShapeDtypeStruct((n_rows * nr, LANES), src.dtype),
        compiler_params=_params(("arbitrary",)),
    )(dest, counts, pad_starts, src)


def _moe_kernel(be_ref, nv_ref, nused_ref, x_ref, wg_ref, wl_ref, bg_ref, bl_ref, wd_ref, bd_ref,
                o_ref, xbf, acc, wub, wdb, *, n_steps):
    b = pl.program_id(0)
    j = pl.program_id(1)
    live = b < nused_ref[0]
    both = jnp.logical_and(live, nv_ref[b] > MOE_HALF)
    single = jnp.logical_and(live, jnp.logical_not(both))
    tf = wg_ref.shape[1]
    nr = _packed_rows(acc.shape[1])

    @pl.when(jnp.logical_and(j == 0, jnp.logical_not(live)))
    def _():
        o_ref[...] = jnp.zeros_like(o_ref)

    def half(lo, unpack, pack):
        rows = slice(lo, lo + MOE_HALF)
        if unpack:
            _load_packed_rows(x_ref, xbf, lo, MOE_HALF)
        gu = jnp.dot(xbf[rows, :], wub[...], preferred_element_type=F32)
        glu = jnp.minimum(gu[:, :tf] + bg_ref[...], SWIGLU_LIMIT)
        lin = jnp.clip(gu[:, tf:] + bl_ref[...], -SWIGLU_LIMIT, SWIGLU_LIMIT)
        act = glu * jax.nn.sigmoid(SWIGLU_ALPHA * glu) * (lin + 1.0)
        down = jnp.dot(act.astype(BF16), wdb[...], preferred_element_type=F32)
        total = down + (bd_ref[...] if unpack else acc[rows, :])
        if pack:
            _store_packed_rows(o_ref, total.astype(BF16), lo)
        else:
            acc[rows, :] = total

    def region(n_halves, unpack, pack):
        wub[:, :tf] = wg_ref[...].astype(BF16)
        wub[:, tf:] = wl_ref[...].astype(BF16)
        wdb[...] = wd_ref[...].astype(BF16)
        for h in range(n_halves):
            half(h * MOE_HALF, unpack, pack)
        if pack and n_halves == 1:
            o_ref[MOE_HALF * nr:, :] = jnp.zeros((MOE_HALF * nr, LANES), o_ref.dtype)

    assert n_steps >= 2
    for n_halves, rows_live in ((2, both), (1, single)):
        for unpack, pack, at in ((True, False, j == 0), (False, True, j == n_steps - 1),
                                 (False, False, jnp.logical_and(j != 0, j != n_steps - 1))):
            pl.when(jnp.logical_and(rows_live, at))(functools.partial(region, n_halves, unpack, pack))


def _moe_call(block_e, block_nv, nused, xs, w_up, b_up, w_down, b_down):
    d_ff, d = w_down.shape[1:]
    nr = _packed_rows(d)
    n_rows = xs.shape[0] // nr
    tf = MOE_FF_TILE
    nj = d_ff // tf
    nb = n_rows // MOE_ROWS

    def bb(b, nu):
        return jnp.minimum(b, nu[0] - 1)

    def jj(b, j, nu):
        return jnp.where(b < nu[0], j, nj - 1)

    return pl.pallas_call(
        functools.partial(_moe_kernel, n_steps=nj),
        name="k_moe",
        grid_spec=pltpu.PrefetchScalarGridSpec(
            num_scalar_prefetch=3,
            grid=(nb, nj),
            in_specs=[pl.BlockSpec((MOE_ROWS * nr, LANES), lambda b, j, be, nv, nu: (bb(b, nu), 0)),
                      pl.BlockSpec((None, d, tf),
                                   lambda b, j, be, nv, nu: (be[bb(b, nu)], 0, jj(b, j, nu))),
                      pl.BlockSpec((None, d, tf),
                                   lambda b, j, be, nv, nu: (be[bb(b, nu)], 0, nj + jj(b, j, nu))),
                      pl.BlockSpec((None, 1, tf),
                                   lambda b, j, be, nv, nu: (be[bb(b, nu)], 0, jj(b, j, nu))),
                      pl.BlockSpec((None, 1, tf),
                                   lambda b, j, be, nv, nu: (be[bb(b, nu)], 0, nj + jj(b, j, nu))),
                      pl.BlockSpec((None, tf, d),
                                   lambda b, j, be, nv, nu: (be[bb(b, nu)], jj(b, j, nu), 0)),
                      pl.BlockSpec((None, 1, d), lambda b, j, be, nv, nu: (be[bb(b, nu)], 0, 0))],
            out_specs=pl.BlockSpec((MOE_ROWS * nr, LANES), lambda b, j, be, nv, nu: (b, 0)),
            scratch_shapes=[pltpu.VMEM((MOE_ROWS, d), BF16), pltpu.VMEM((MOE_ROWS, d), F32),
                            pltpu.VMEM((d, 2 * tf), BF16), pltpu.VMEM((tf, d), BF16)]),
        out_shape=jax.ShapeDtypeStruct((n_rows * nr, LANES), jnp.uint32),
        compiler_params=_params(("arbitrary", "arbitrary")),
    )(block_e, block_nv, nused, xs, w_up, w_up, b_up, b_up, w_down, b_down)


def _combine_kernel(dest_ref, h_ref, gate_ref, nw_ref, eo_hbm, yp_ref, ys_ref, gbuf, sem,
                    *, rows, n_first):
    i = pl.program_id(0)
    n = pl.num_programs(0)
    slot = i % 2
    d = h_ref.shape[1]
    nr = _packed_rows(d)

    def gather(tile, into):
        def issue(t, carry):
            dst_rows = pl.ds(pl.multiple_of(t * nr, nr), nr)
            for k in range(TOP_K):
                e_row = dest_ref[(tile * rows + t) * TOP_K + k]
                src = eo_hbm.at[pl.ds(pl.multiple_of(e_row * nr, nr), nr)]
                pltpu.make_async_copy(src, gbuf.at[into, k, dst_rows],
                                      sem.at[into]).start(priority=k % 2)
            return carry

        lax.fori_loop(0, rows, issue, 0, unroll=4)

    @pl.when(i == 0)
    def _():
        gather(i, slot)

    @pl.when(i + 1 < n)
    def _():
        gather(i + 1, 1 - slot)

    for k in range(TOP_K):
        pltpu.make_async_copy(eo_hbm.at[pl.ds(0, rows * nr)], gbuf.at[slot, k], sem.at[slot]).wait()
    gate = gate_ref[...]
    gates = [gate[:, k:k + 1] for k in range(TOP_K)]
    lo_cols, hi_cols = [], []
    for s in range(nr):
        lo = hi = None
        for k in range(TOP_K):
            words = gbuf[slot, k, pl.ds(s, rows, stride=nr), :]
            a = gates[k] * lax.bitcast_convert_type(words << 16, F32)
            b = gates[k] * lax.bitcast_convert_type(words & jnp.uint32(0xFFFF0000), F32)
            lo = a if lo is None else lo + a
            hi = b if hi is None else hi + b
        lo_cols.append(lo)
        hi_cols.append(hi)
    y = h_ref[...] + jnp.concatenate(lo_cols + hi_cols, axis=1)
    r = lax.rsqrt(jnp.mean(y * y, axis=-1, keepdims=True) + EPS)
    y = y * r * nw_ref[...]

    @pl.when(i < n_first)
    def _():
        yp_ref[...] = y

    @pl.when(i >= n_first)
    def _():
        ys_ref[...] = y


def _combine_call(dest, h, gate, nw, eo, t_first):
    t, d = h.shape
    rows = _pick_tile(np.gcd(t_first, t - t_first), (256, 128, 64))
    n_first = t_first // rows
    return pl.pallas_call(
        functools.partial(_combine_kernel, rows=rows, n_first=n_first),
        name="k_combine",
        grid_spec=pltpu.PrefetchScalarGridSpec(
            num_scalar_prefetch=1,
            grid=(t // rows,),
            in_specs=[pl.BlockSpec((rows, d), lambda i, de: (i, 0)),
                      pl.BlockSpec((rows, LANES), lambda i, de: (i, 0)),
                      pl.BlockSpec((1, d), lambda i, de: (0, 0)),
                      pl.BlockSpec(memory_space=pl.ANY)],
            out_specs=_two_part_specs(rows, d, n_first),
            scratch_shapes=[pltpu.VMEM((2, TOP_K, rows * _packed_rows(d), LANES), jnp.uint32),
                            pltpu.SemaphoreType.DMA((2,))]),
        out_shape=[jax.ShapeDtypeStruct((t_first, d), F32),
                   jax.ShapeDtypeStruct((t - t_first, d), F32)],
        compiler_params=_params(("arbitrary",)),
    )(dest, h, gate, nw, eo)


def kernel(x_prompt, x_sample, state_ssm, state_conv, state_pool, norm_mix_w, w_in, conv_w, conv_b,
           dt_bias, a_log, d_skip, ssd_norm_w, w_pool, pool_scale, w_proj_ssd, w_proj_pool, w_out,
           norm_ffn_w, w_router, b_router, w_up, b_up, w_down, b_down, norm_final_w):
    batch, seq_len, d_model = x_prompt.shape
    dec_batch, dec_seq, _ = x_sample.shape
    depth, _, n_heads, head_dim, d_state = state_ssm.shape
    assert depth == 1 and batch == 1 and dec_seq == CHUNK and seq_len % CHUNK == 0
    assert head_dim == HEAD_DIM and d_state == D_STATE
    d_inner = n_heads * head_dim
    assert d_inner == N_GROUPS * COL_BLOCK
    d_bc = 2 * N_GROUPS * D_STATE
    d_pool = state_pool.shape[-1]
    assert d_pool == len(POOL_WINDOWS) * COL_BLOCK and state_pool.shape[-2] == POOL_BUF
    n_experts = w_router.shape[-1]
    n_prompt_chunks = seq_len // CHUNK
    n_seq = batch + dec_batch
    t_prompt = batch * seq_len
    t = t_prompt + dec_batch * dec_seq

    xp = x_prompt.reshape(t_prompt, d_model)
    xs_tok = x_sample.reshape(-1, d_model)

    w_t = jnp.swapaxes(w_in[0], 0, 1)
    xbc_end = 2 * d_inner + d_bc
    dt_end = xbc_end + n_heads
    w_dt1, w_dt2 = _hi_lo(jnp.pad(w_t[xbc_end:dt_end].T, ((0, 0), (0, LANES - n_heads))))
    pad_h = lambda a: jnp.pad(a.reshape(1, n_heads), ((0, 0), (0, LANES - n_heads)))
    cw = conv_w[0]
    cwx, cwbc = cw[:, :d_inner], cw[:, d_inner:]
    cb = conv_b[0].reshape(1, -1)
    cbx, cbbc = cb[:, :d_inner], cb[:, d_inner:]
    dsk = jnp.repeat(d_skip[0], head_dim).reshape(1, d_inner)

    conv0 = jnp.pad(state_conv[0], ((batch, 0), (8 - (CONV_WIDTH - 1), 0), (0, 0)))
    cx0, cbc0 = conv0[..., :d_inner], conv0[..., d_inner:]
    pool0 = jnp.pad(state_pool[0], ((batch, 0), (1, 0), (0, 0)))

    u, dt_raw = _norm_call(xp, xs_tok, norm_mix_w[0].reshape(1, -1), w_dt1, w_dt2)
    p = _inproj_call(u, w_t, xbc_end, dt_end)
    yg, yp, ssm_new = _mixer_call(p, dt_raw, state_ssm[0], cx0, cbc0, pool0, pad_h(dt_bias[0]),
                                  pad_h(a_log[0]), dsk, cwx, cbx, cwbc, cbbc,
                                  w_pool[0].astype(BF16), pool_scale[0].reshape(1, -1),
                                  _ssd_consts(), n_prompt_chunks)
    merged = _merge_call(yg, yp, p, ssd_norm_w[0].reshape(1, -1), w_proj_ssd[0].astype(BF16),
                         w_proj_pool[0].astype(BF16))

    wr1, wr2 = _hi_lo(jnp.pad(w_router[0], ((0, 0), (0, LANES - n_experts))))
    br = jnp.pad(b_router[0].reshape(1, -1), ((0, 0), (0, LANES - n_experts)), constant_values=-1e30)
    h, hn, eidx, rank, gate, cnt = _route_call(merged, xp, xs_tok, w_out[0].astype(BF16),
                                               norm_ffn_w[0].reshape(1, -1), wr1, wr2, br)

    counts = cnt[0, :n_experts]
    padded = (counts + MOE_ROWS - 1) // MOE_ROWS * MOE_ROWS
    pad_ends = jnp.cumsum(padded)
    pad_starts = pad_ends - padded
    n_blocks = -(-(t * TOP_K) // MOE_ROWS) + n_experts
    n_rows = n_blocks * MOE_ROWS
    e_flat = eidx[:, :TOP_K].reshape(-1)
    dest = (pad_starts[e_flat] + rank[:, :TOP_K].reshape(-1)).astype(jnp.int32)
    block_start = jnp.arange(n_blocks, dtype=jnp.int32) * MOE_ROWS
    block_e = jnp.minimum(jnp.sum(block_start[:, None] >= pad_ends[None, :], axis=1),
                          n_experts - 1).astype(jnp.int32)
    block_nv = jnp.clip(pad_starts[block_e] + counts[block_e] - block_start, 0,
                        MOE_ROWS).astype(jnp.int32)
    nused = (pad_ends[-1:] // MOE_ROWS).astype(jnp.int32)

    xs = _dispatch_call(dest, counts, pad_starts.astype(jnp.int32), hn, n_rows, _packed_rows(d_model))
    eo = _moe_call(block_e, block_nv, nused, xs, w_up[0], b_up[0].reshape(n_experts, 1, -1),
                   w_down[0], b_down[0].reshape(n_experts, 1, -1))
    y_p, y_s = _combine_call(dest, h, gate, norm_final_w.reshape(1, -1), eo, t_prompt)

    y_prompt = y_p.reshape(batch, seq_len, d_model)
    y_sample = y_s.reshape(dec_batch, dec_seq, d_model)
    seq_ends = [t_prompt] * batch + [t_prompt + (s + 1) * dec_seq for s in range(dec_batch)]
    tail = jnp.stack([p[:, e - POOL_BUF:e] for e in seq_ends], axis=0)
    tail = tail.transpose(0, 2, 1, 3)
    ctail = tail[:, POOL_BUF - (CONV_WIDTH - 1):]
    conv_x = ctail[:, :, P_X:P_BC].reshape(n_seq, CONV_WIDTH - 1, d_inner)
    conv_bc = ctail[:, :, P_BC:P_POOL].reshape(n_seq, CONV_WIDTH - 1, d_bc)
    conv_new = jnp.concatenate([conv_x, conv_bc], axis=-1)
    pool_new = tail[:, :, P_POOL:P_GATE_SSD].reshape(n_seq, POOL_BUF, d_pool)
    return (y_prompt, y_sample,
            ssm_new[None, :batch], conv_new[None, :batch], pool_new[None, :batch],
            ssm_new[None, batch:], conv_new[None, batch:], pool_new[None, batch:])
```

```python
import functools

import numpy as np
import jax
import jax.numpy as jnp
from jax import lax
from jax.experimental import pallas as pl
from jax.experimental.pallas import tpu as pltpu

F32 = jnp.float32
BF16 = jnp.bfloat16

CHUNK = 64
HEAD_DIM = 64
N_GROUPS = 8
D_STATE = 128
CONV_WIDTH = 4
POOL_WINDOWS = (2, 4, 8, 16)
POOL_BUF = 15
PAST_LEN = 4096
TOP_K = 4
SWIGLU_ALPHA = 1.702
SWIGLU_LIMIT = 7.0
EPS = 1e-5
LANES = 128
COL_BLOCK = 512
MOE_ROWS = 1024
MOE_HALF = MOE_ROWS // 2
MOE_FF_TILE = 256
P_Z, P_X, P_BC, P_POOL, P_GATE_SSD, P_GATE_POOL = 0, 8, 16, 20, 24, 28
VMEM_LIMIT = 56 * 1024 * 1024


def _pick_tile(n, candidates):
    for c in candidates:
        if n % c == 0:
            return c
    raise ValueError(f"no tile for {n} in {candidates}")


def _params(sem, vmem=VMEM_LIMIT):
    return pltpu.CompilerParams(dimension_semantics=sem, vmem_limit_bytes=vmem)


def _split3(v):
    p1 = v.astype(BF16)
    r1 = v - p1.astype(F32)
    p2 = r1.astype(BF16)
    p3 = (r1 - p2.astype(F32)).astype(BF16)
    return p1, p2, p3


def _hi_lo(v):
    hi = v.astype(BF16)
    return hi, (v - hi.astype(F32)).astype(BF16)


def _dot_hi_lo(a1, a2, b1_ref, b2_ref):
    return (jnp.dot(a1, b1_ref[...], preferred_element_type=F32)
            + jnp.dot(a1, b2_ref[...], preferred_element_type=F32)
            + jnp.dot(a2, b1_ref[...], preferred_element_type=F32))


def _silu(v):
    h = 0.5 * v
    return h + h * jnp.tanh(h)


def _packed_rows(d):
    return d // 2 // LANES


def _store_packed_rows(ref, v_bf16, first_token=0):
    tm, d = v_bf16.shape
    bits = lax.bitcast_convert_type(v_bf16.astype(F32), jnp.uint32)
    words = (bits[:, :d // 2] >> 16) | (bits[:, d // 2:] & jnp.uint32(0xFFFF0000))
    nr = _packed_rows(d)
    for s in range(nr):
        ref[pl.ds(first_token * nr + s, tm, stride=nr), :] = words[:, s * LANES:(s + 1) * LANES]


def _load_packed_rows(ref, out_ref, first_token, n_tokens):
    d = out_ref.shape[1]
    nr = _packed_rows(d)
    rows = slice(first_token, first_token + n_tokens)
    for s in range(nr):
        words = ref[pl.ds(first_token * nr + s, n_tokens, stride=nr), :]
        lo = lax.bitcast_convert_type(words << 16, F32)
        hi = lax.bitcast_convert_type(words & jnp.uint32(0xFFFF0000), F32)
        out_ref[rows, s * LANES:(s + 1) * LANES] = lo.astype(BF16)
        out_ref[rows, d // 2 + s * LANES:d // 2 + (s + 1) * LANES] = hi.astype(BF16)


def _two_part_specs(tm, d, n_first):
    return [pl.BlockSpec((tm, d), lambda i, *_: (jnp.minimum(i, n_first - 1), 0)),
            pl.BlockSpec((tm, d), lambda i, *_: (jnp.maximum(i - n_first, 0), 0))]


def _norm_kernel(xp_ref, xs_ref, w_ref, wdt1_ref, wdt2_ref, u_ref, dt_ref, *, n_first):
    x = jnp.where(pl.program_id(0) < n_first, xp_ref[...], xs_ref[...])
    r = lax.rsqrt(jnp.mean(x * x, axis=-1, keepdims=True) + EPS)
    u = x * r * w_ref[...]
    u1, u2 = _hi_lo(u)
    u_ref[...] = u1
    dt_ref[...] = _dot_hi_lo(u1, u2, wdt1_ref, wdt2_ref)


def _norm_call(xp, xs, w, wdt1, wdt2):
    d = xp.shape[1]
    t = xp.shape[0] + xs.shape[0]
    tm = _pick_tile(np.gcd(xp.shape[0], xs.shape[0]), (512, 256, 128, 64))
    n_first = xp.shape[0] // tm
    return pl.pallas_call(
        functools.partial(_norm_kernel, n_first=n_first),
        name="k_norm",
        grid=(t // tm,),
        in_specs=_two_part_specs(tm, d, n_first) + [
            pl.BlockSpec((1, d), lambda i: (0, 0)),
            pl.BlockSpec((d, LANES), lambda i: (0, 0)),
            pl.BlockSpec((d, LANES), lambda i: (0, 0))],
        out_specs=[pl.BlockSpec((tm, d), lambda i: (i, 0)),
                   pl.BlockSpec((tm, LANES), lambda i: (i, 0))],
        out_shape=[jax.ShapeDtypeStruct((t, d), BF16), jax.ShapeDtypeStruct((t, LANES), F32)],
        compiler_params=_params(("parallel",)),
    )(xp, xs, w, wdt1, wdt2)


def _inproj_kernel(u_ref, wt_ref, o_ref, wbf):
    @pl.when(pl.program_id(1) == 0)
    def _():
        wbf[...] = wt_ref[...].T.astype(BF16)

    r = jnp.dot(u_ref[...], wbf[...], preferred_element_type=F32)
    for k in range(o_ref.shape[0]):
        o_ref[k] = r[:, k * COL_BLOCK:(k + 1) * COL_BLOCK]


def _inproj_call(u, w_t, n_head_cols, tail_start):
    t, d = u.shape
    per = 2
    tn = per * COL_BLOCK
    tm = _pick_tile(t, (1024, 512, 256, 128, 64))
    n_head_tiles = n_head_cols // tn
    n_tiles = n_head_tiles + (w_t.shape[0] - tail_start) // tn

    def first_row(j):
        assert tn % 8 == 0 and tail_start % 8 == 0
        return pl.multiple_of(
            jnp.where(j < n_head_tiles, j * tn, tail_start + (j - n_head_tiles) * tn), 8)

    return pl.pallas_call(
        _inproj_kernel,
        name="k_inproj",
        grid=(n_tiles, t // tm),
        in_specs=[pl.BlockSpec((tm, d), lambda j, i: (i, 0)),
                  pl.BlockSpec((pl.Element(tn), pl.Element(d)), lambda j, i: (first_row(j), 0))],
        out_specs=pl.BlockSpec((per, tm, COL_BLOCK), lambda j, i: (j, i, 0)),
        out_shape=jax.ShapeDtypeStruct((n_tiles * per, t, COL_BLOCK), F32),
        scratch_shapes=[pltpu.VMEM((d, tn), BF16)],
        compiler_params=_params(("arbitrary", "arbitrary")),
    )(u, w_t)


def _pool_chunk(c, first, pu_ref, pool0_ref, wp_ref, scale_ref, yp_ref, pbuf, n_prompt_chunks):
    hist = POOL_BUF + 1

    @pl.when(first)
    def _():
        pbuf[0:hist, :] = pool0_ref[...]

    @pl.when(jnp.logical_not(first))
    def _():
        pbuf[0:hist, :] = pbuf[CHUNK:CHUNK + hist, :]

    for g in range(len(POOL_WINDOWS)):
        pbuf[hist:hist + CHUNK, g * COL_BLOCK:(g + 1) * COL_BLOCK] = pu_ref[g]

    pos0 = jnp.where(c < n_prompt_chunks, c * CHUNK, PAST_LEN)
    pos = (pos0 + lax.broadcasted_iota(jnp.int32, (CHUNK, 1), 0)).astype(F32)
    for g, win in enumerate(POOL_WINDOWS):
        sl = slice(g * COL_BLOCK, (g + 1) * COL_BLOCK)
        tot = pbuf[:, sl]
        w = 1
        while w < win:
            tot = tot + pltpu.roll(tot, w, axis=0)
            w *= 2
        cur = pbuf[hist:hist + CHUNK, sl]
        count = jnp.minimum(pos + 1.0, float(win))
        pooled = tot[hist:] / count - cur
        yp_ref[:, sl] = jnp.dot(pooled.astype(BF16), wp_ref[g],
                                preferred_element_type=F32) * scale_ref[:, sl]


def _mixer_kernel(z_ref, x_ref, bc_ref, pu_ref, dt_ref, ssm0_ref, cx0_ref, cbc0_ref, pool0_ref,
                  dtb_ref, alog_ref, dsk_ref, cwx_ref, cbx_ref, cwbc_ref, cbbc_ref, wp_ref, scale_ref,
                  tril3_ref, e3_ref, diag_ref, caus_ref, bd_ref,
                  yg_ref, yp_ref, ssm_ref, st, bufx, bufbc, exs, pbuf, *, n_prompt_chunks):
    c = pl.program_id(0)
    first = jnp.logical_or(c == 0, c >= n_prompt_chunks)
    last = c >= n_prompt_chunks - 1
    pairs = N_GROUPS * COL_BLOCK // LANES

    @pl.when(c == 0)
    def _():
        st[...] = jnp.zeros_like(st)

    @pl.when(c >= n_prompt_chunks)
    def _():
        for q in range(pairs):
            blk = jnp.concatenate([ssm0_ref[2 * q], ssm0_ref[2 * q + 1]], axis=0)
            g, o = divmod(q * LANES, COL_BLOCK)
            st[g, :, o:o + LANES] = blk.T

    @pl.when(first)
    def _():
        bufx[0:8, :] = cx0_ref[...]
        bufbc[0:8, :] = cbc0_ref[...]

    @pl.when(jnp.logical_not(first))
    def _():
        bufx[0:8, :] = bufx[CHUNK:CHUNK + 8, :]
        bufbc[0:8, :] = bufbc[CHUNK:CHUNK + 8, :]

    for g in range(N_GROUPS):
        bufx[8:8 + CHUNK, g * COL_BLOCK:(g + 1) * COL_BLOCK] = x_ref[g]
    for q in range(N_GROUPS // 2):
        bufbc[8:8 + CHUNK, q * COL_BLOCK:(q + 1) * COL_BLOCK] = bc_ref[q]

    dtv = dt_ref[...] + dtb_ref[...]
    dt = jnp.maximum(dtv, 0.0) + jnp.log1p(jnp.exp(-jnp.abs(dtv)))
    d_a = dt * (-jnp.exp(alog_ref[...]))
    p1, p2, p3 = _split3(d_a)
    acum = jnp.dot(tril3_ref[...], jnp.concatenate([p1, p2, p3], axis=0),
                   preferred_element_type=F32)
    q1, q2, q3 = _split3(jnp.concatenate([acum, dt], axis=0))
    exs[...] = jnp.dot(jnp.concatenate([q1, q2, q3], axis=1), e3_ref[...],
                       preferred_element_type=F32)

    hw = 4 * HEAD_DIM
    for g in range(N_GROUPS):
        sl = slice(g * COL_BLOCK, (g + 1) * COL_BLOCK)
        xc = cbx_ref[:, sl]
        for k in range(CONV_WIDTH):
            xc = xc + cwx_ref[k:k + 1, sl] * bufx[5 + k:5 + k + CHUNK, sl]
        xs = _silu(xc)
        bc_g = []
        for lo in (g * D_STATE, (N_GROUPS + g) * D_STATE):
            slb = slice(lo, lo + D_STATE)
            acc = cbbc_ref[:, slb]
            for k in range(CONV_WIDTH):
                acc = acc + cwbc_ref[k:k + 1, slb] * bufbc[5 + k:5 + k + CHUNK, slb]
            bc_g.append(_silu(acc))
        b_g, c_g = bc_g
        acx = exs[0:CHUNK, sl]
        dtx = exs[CHUNK:2 * CHUNK, sl]
        alast = acx[CHUNK - 1:CHUNK, :]
        arow = jnp.sum(acx * diag_ref[:, sl], axis=0, keepdims=True)
        xdt = xs * dtx
        xdtb = xdt.astype(BF16)
        bb = b_g.astype(BF16)
        cb = c_g.astype(BF16)
        cb2 = lax.dot_general(cb, jnp.concatenate([bb, bb], axis=0),
                              (((1,), (1,)), ((), ())), preferred_element_type=F32)
        ydiag = []
        for q in range(2):
            lhs = []
            for d in range(2):
                lo = q * hw + d * LANES
                seg = acx[:, lo:lo + LANES] - arow[:, lo:lo + LANES]
                lhs.append((cb2 * jnp.exp(jnp.where(caus_ref[...] > 0.0, seg, -jnp.inf))).astype(BF16))
            xq = xdtb[:, q * hw:(q + 1) * hw]
            wq = jnp.concatenate([xq, xq, xq, xq], axis=0) * bd_ref[...]
            ydiag.append(jnp.dot(jnp.concatenate(lhs, axis=1), wq, preferred_element_type=F32))
        s_old = st[g]
        yoff = jnp.dot(cb, s_old.astype(BF16), preferred_element_type=F32) * jnp.exp(acx)
        y = jnp.concatenate(ydiag, axis=1) + yoff + dsk_ref[:, sl] * xs
        yg_ref[:, sl] = y * _silu(z_ref[g])
        v = (xdt * jnp.exp(alast - acx)).astype(BF16)
        st[g] = jnp.exp(alast) * s_old + jnp.dot(b_g.T.astype(BF16), v,
                                                 preferred_element_type=F32)

    @pl.when(last)
    def _():
        for q in range(pairs):
            g, o = divmod(q * LANES, COL_BLOCK)
            blk = st[g, :, o:o + LANES].T
            ssm_ref[2 * q] = blk[:HEAD_DIM]
            ssm_ref[2 * q + 1] = blk[HEAD_DIM:]

    _pool_chunk(c, first, pu_ref, pool0_ref, wp_ref, scale_ref, yp_ref, pbuf, n_prompt_chunks)


def _mixer_call(p, dt_raw, ssm_in, cx0, cbc0, pool0, dtb, alog, dsk, cwx, cbx, cwbc, cbbc, wp, scale,
                consts, n_prompt_chunks):
    _, t, _ = p.shape
    n_chunks = t // CHUNK
    n_seq = cx0.shape[0]
    n_heads, head_dim, d_state = ssm_in.shape[1:]
    d_inner = N_GROUPS * COL_BLOCK
    d_bc = N_GROUPS * 2 * D_STATE
    ng = len(POOL_WINDOWS)
    d_pool = ng * COL_BLOCK
    hist = POOL_BUF + 1
    tril3, e3, diag, caus, bd = consts

    def seq(c):
        return jnp.maximum(c - (n_prompt_chunks - 1), 0)

    def const(a):
        return pl.BlockSpec(a.shape, lambda c: (0,) * a.ndim)

    state_block = (None, n_heads, head_dim, d_state)
    return pl.pallas_call(
        functools.partial(_mixer_kernel, n_prompt_chunks=n_prompt_chunks),
        name="k_mixer",
        grid=(n_chunks,),
        in_specs=[pl.BlockSpec((N_GROUPS, CHUNK, COL_BLOCK), lambda c: (P_Z // N_GROUPS, c, 0)),
                  pl.BlockSpec((N_GROUPS, CHUNK, COL_BLOCK), lambda c: (P_X // N_GROUPS, c, 0)),
                  pl.BlockSpec((N_GROUPS // 2, CHUNK, COL_BLOCK),
                               lambda c: (P_BC // (N_GROUPS // 2), c, 0)),
                  pl.BlockSpec((ng, CHUNK, COL_BLOCK), lambda c: (P_POOL // ng, c, 0)),
                  pl.BlockSpec((CHUNK, LANES), lambda c: (c, 0)),
                  pl.BlockSpec(state_block, lambda c: (jnp.maximum(c - n_prompt_chunks, 0), 0, 0, 0)),
                  pl.BlockSpec((None, 8, d_inner), lambda c: (seq(c), 0, 0)),
                  pl.BlockSpec((None, 8, d_bc), lambda c: (seq(c), 0, 0)),
                  pl.BlockSpec((None, hist, d_pool), lambda c: (seq(c), 0, 0)),
                  const(dtb), const(alog), const(dsk), const(cwx), const(cbx),
                  const(cwbc), const(cbbc), const(wp), const(scale),
                  const(tril3), const(e3), const(diag), const(caus), const(bd)],
        out_specs=[pl.BlockSpec((CHUNK, d_inner), lambda c: (c, 0)),
                   pl.BlockSpec((CHUNK, d_pool), lambda c: (c, 0)),
                   pl.BlockSpec(state_block, lambda c: (seq(c), 0, 0, 0))],
        out_shape=[jax.ShapeDtypeStruct((t, d_inner), F32),
                   jax.ShapeDtypeStruct((t, d_pool), F32),
                   jax.ShapeDtypeStruct((n_seq, n_heads, head_dim, d_state), F32)],
        scratch_shapes=[pltpu.VMEM((N_GROUPS, D_STATE, COL_BLOCK), F32),
                        pltpu.VMEM((CHUNK + 8, d_inner), F32),
                        pltpu.VMEM((CHUNK + 8, d_bc), F32),
                        pltpu.VMEM((2 * CHUNK, d_inner), F32),
                        pltpu.VMEM((CHUNK + hist, d_pool), F32)],
        compiler_params=_params(("arbitrary",)),
    )(p, p, p, p, dt_raw, ssm_in, cx0, cbc0, pool0, dtb, alog, dsk, cwx, cbx, cwbc, cbbc, wp, scale,
      tril3, e3, diag, caus, bd)


def _ssd_consts():
    l = np.arange(CHUNK)
    tril = (l[:, None] >= l[None, :]).astype(np.float32)
    tril3 = np.concatenate([tril, tril, tril], axis=1)
    n_heads = N_GROUPS * COL_BLOCK // HEAD_DIM
    col_head = np.arange(n_heads * HEAD_DIM) // HEAD_DIM
    col_pos = np.arange(n_heads * HEAD_DIM) % HEAD_DIM
    e = (np.arange(LANES)[:, None] == col_head[None, :]).astype(np.float32)
    e3 = np.concatenate([e, e, e], axis=0)
    diag = (l[:, None] == col_pos[None, :]).astype(np.float32)
    caus = np.concatenate([tril, tril], axis=1)
    r = np.arange(4 * HEAD_DIM)
    bd = (r[:, None] // HEAD_DIM == r[None, :] // HEAD_DIM).astype(np.float32)
    return (jnp.asarray(tril3, BF16), jnp.asarray(e3, BF16), jnp.asarray(diag, F32),
            jnp.asarray(caus, F32), jnp.asarray(bd, BF16))


def _merge_kernel(yg_ref, yp_ref, gs_ref, gp_ref, nw_ref, wps_ref, wpp_ref, o_ref):
    y = yg_ref[...]
    r = lax.rsqrt(jnp.mean(y * y, axis=-1, keepdims=True) + EPS)
    yn = (y * r * nw_ref[...]).astype(BF16)
    a = jnp.dot(yn, wps_ref[...], preferred_element_type=F32)
    b = jnp.dot(yp_ref[...].astype(BF16), wpp_ref[...], preferred_element_type=F32)
    for j in range(gs_ref.shape[0]):
        sl = slice(j * COL_BLOCK, (j + 1) * COL_BLOCK)
        o_ref[:, sl] = (jax.nn.sigmoid(gs_ref[j]) * a[:, sl]
                        + jax.nn.sigmoid(gp_ref[j]) * b[:, sl]).astype(BF16)


def _merge_call(yg, yp, p, nw, wps, wpp):
    t, d_inner = yg.shape
    d_pool = yp.shape[1]
    d_model = wps.shape[1]
    tm = _pick_tile(t, (256, 128, 64))
    nj = d_model // COL_BLOCK
    once = pl.Buffered(1)
    return pl.pallas_call(
        _merge_kernel,
        name="k_merge",
        grid=(t // tm,),
        in_specs=[pl.BlockSpec((tm, d_inner), lambda i: (i, 0)),
                  pl.BlockSpec((tm, d_pool), lambda i: (i, 0)),
                  pl.BlockSpec((nj, tm, COL_BLOCK), lambda i: (P_GATE_SSD // nj, i, 0)),
                  pl.BlockSpec((nj, tm, COL_BLOCK), lambda i: (P_GATE_POOL // nj, i, 0)),
                  pl.BlockSpec((1, d_inner), lambda i: (0, 0)),
                  pl.BlockSpec((d_inner, d_model), lambda i: (0, 0), pipeline_mode=once),
                  pl.BlockSpec((d_pool, d_model), lambda i: (0, 0), pipeline_mode=once)],
        out_specs=pl.BlockSpec((tm, d_model), lambda i: (i, 0)),
        out_shape=jax.ShapeDtypeStruct((t, d_model), BF16),
        compiler_params=_params(("parallel",)),
    )(yg, yp, p, p, nw, wps, wpp)


def _route_kernel(m_ref, xp_ref, xs_ref, wo_ref, nw_ref, wr1_ref, wr2_ref, br_ref, trs_ref,
                  h_ref, hn_ref, eidx_ref, rank_ref, gate_ref, cnt_ref, carry, *, n_first):
    i = pl.program_id(0)

    @pl.when(i == 0)
    def _():
        carry[...] = jnp.zeros_like(carry)

    x = jnp.where(i < n_first, xp_ref[...], xs_ref[...])
    h = x + jnp.dot(m_ref[...], wo_ref[...], preferred_element_type=F32)
    h_ref[...] = h
    r = lax.rsqrt(jnp.mean(h * h, axis=-1, keepdims=True) + EPS)
    hn = h * r * nw_ref[...]
    h1, h2 = _hi_lo(hn)
    _store_packed_rows(hn_ref, h1)
    logits = _dot_hi_lo(h1, h2, wr1_ref, wr2_ref) + br_ref[...]
    lane = lax.broadcasted_iota(jnp.int32, logits.shape, 1)
    work = logits
    member = jnp.zeros(logits.shape, F32)
    vals, idxs = [], []
    for _ in range(TOP_K):
        m = jnp.max(work, axis=-1, keepdims=True)
        idx = jnp.min(jnp.where(work == m, lane, LANES), axis=-1, keepdims=True)
        hit = lane == idx
        member = member + hit.astype(F32)
        work = jnp.where(hit, -jnp.inf, work)
        vals.append(m)
        idxs.append(idx)
    ex = [jnp.exp(v - vals[0]) for v in vals]
    den = ex[0] + ex[1] + ex[2] + ex[3]
    before = jnp.dot(trs_ref[...], member.astype(BF16), preferred_element_type=F32) + carry[0:1, :]
    eidx = jnp.zeros(logits.shape, jnp.int32)
    rank = jnp.zeros(logits.shape, jnp.int32)
    gate = jnp.zeros(logits.shape, F32)
    for k in range(TOP_K):
        rk = jnp.sum(jnp.where(lane == idxs[k], before, 0.0), axis=-1, keepdims=True)
        eidx = jnp.where(lane == k, idxs[k], eidx)
        rank = jnp.where(lane == k, rk.astype(jnp.int32), rank)
        gate = jnp.where(lane == k, ex[k] / den, gate)
    eidx_ref[...] = eidx
    rank_ref[...] = rank
    gate_ref[...] = gate
    carry[0:1, :] = carry[0:1, :] + jnp.sum(member, axis=0, keepdims=True)
    cnt_ref[...] = carry[...].astype(jnp.int32)


def _route_call(merged, xp, xs, wo, nw, wr1, wr2, br):
    t, d = merged.shape
    tm = _pick_tile(np.gcd(xp.shape[0], xs.shape[0]), (512, 256, 128, 64))
    n_first = xp.shape[0] // tm
    ri = np.arange(tm)
    trs = jnp.asarray((ri[:, None] > ri[None, :]).astype(np.float32), BF16)
    row = lambda i: (i, 0)
    fix = lambda i: (0, 0)
    return pl.pallas_call(
        functools.partial(_route_kernel, n_first=n_first),
        name="k_route",
        grid=(t // tm,),
        in_specs=[pl.BlockSpec((tm, d), row)] + _two_part_specs(tm, d, n_first) + [
                  pl.BlockSpec((d, d), fix), pl.BlockSpec((1, d), fix),
                  pl.BlockSpec((d, LANES), fix), pl.BlockSpec((d, LANES), fix),
                  pl.BlockSpec((1, LANES), fix), pl.BlockSpec((tm, tm), fix)],
        out_specs=[pl.BlockSpec((tm, d), row), pl.BlockSpec((tm * _packed_rows(d), LANES), row),
                   pl.BlockSpec((tm, LANES), row), pl.BlockSpec((tm, LANES), row),
                   pl.BlockSpec((tm, LANES), row), pl.BlockSpec((8, LANES), fix)],
        out_shape=[jax.ShapeDtypeStruct((t, d), F32),
                   jax.ShapeDtypeStruct((t * _packed_rows(d), LANES), jnp.uint32),
                   jax.ShapeDtypeStruct((t, LANES), jnp.int32),
                   jax.ShapeDtypeStruct((t, LANES), jnp.int32),
                   jax.ShapeDtypeStruct((t, LANES), F32),
                   jax.ShapeDtypeStruct((8, LANES), jnp.int32)],
        scratch_shapes=[pltpu.VMEM((8, LANES), F32)],
        compiler_params=_params(("arbitrary",)),
    )(merged, xp, xs, wo, nw, wr1, wr2, br, trs)


_PAD_CHUNKS = tuple(1 << s for s in range(MOE_ROWS.bit_length() - 2, -1, -1))


def _dispatch_kernel(dest_ref, cnt_ref, pstart_ref, x_ref, o_hbm, zbuf, sem, zsem,
                     *, rows, nr, n_experts):
    i = pl.program_id(0)

    def tokens(ref, first, n):
        return ref.at[pl.ds(pl.multiple_of(first * nr, nr), n * nr)]

    def issue(t, carry):
        for k in range(TOP_K):
            d = dest_ref[(i * rows + t) * TOP_K + k]
            pltpu.make_async_copy(tokens(x_ref, t, 1), tokens(o_hbm, d, 1), sem).start(priority=k % 2)
        return carry

    lax.fori_loop(0, rows, issue, 0, unroll=4)
    for k in range(TOP_K):
        pltpu.make_async_copy(x_ref, tokens(o_hbm, 0, rows), sem).wait()

    @pl.when(i == pl.num_programs(0) - 1)
    def _():
        zbuf[...] = jnp.zeros_like(zbuf)

        def pad_copies(e, wait):
            cnt = cnt_ref[e]
            off = pstart_ref[e] + cnt
            npad = (-cnt) & (MOE_ROWS - 1)
            for n_c in _PAD_CHUNKS:
                cp = pltpu.make_async_copy(tokens(zbuf, 0, n_c), tokens(o_hbm, off, n_c), zsem)
                pl.when((npad & n_c) != 0)(cp.wait if wait else cp.start)
                off = off + (npad & n_c)

        def start_e(e, carry):
            pad_copies(e, False)
            return carry

        def wait_e(e, carry):
            pad_copies(e, True)
            return carry

        lax.fori_loop(0, n_experts, start_e, 0)
        lax.fori_loop(0, n_experts, wait_e, 0)

        ztok = zbuf.shape[0] // nr
        first = (pstart_ref[n_experts - 1] + cnt_ref[n_experts - 1] + MOE_ROWS - 1) // MOE_ROWS
        first = first * (MOE_ROWS // ztok)

        def tail_copy(c):
            return pltpu.make_async_copy(zbuf, tokens(o_hbm, c * ztok, ztok), zsem)

        def start_t(c, carry):
            tail_copy(c).start()
            return carry

        def wait_t(c, carry):
            tail_copy(c).wait()
            return carry

        lax.fori_loop(first, o_hbm.shape[0] // zbuf.shape[0], start_t, 0)
        lax.fori_loop(first, o_hbm.shape[0] // zbuf.shape[0], wait_t, 0)


def _dispatch_call(dest, counts, pad_starts, src, n_rows, nr):
    t = src.shape[0] // nr
    rows = _pick_tile(t, (256, 128, 64))
    return pl.pallas_call(
        functools.partial(_dispatch_kernel, rows=rows, nr=nr, n_experts=counts.shape[0]),
        name="k_dispatch",
        grid_spec=pltpu.PrefetchScalarGridSpec(
            num_scalar_prefetch=3,
            grid=(t // rows,),
            in_specs=[pl.BlockSpec((rows * nr, LANES), lambda i, de, cn, ps: (i, 0))],
            out_specs=pl.BlockSpec(memory_space=pl.ANY),
            scratch_shapes=[pltpu.VMEM((_PAD_CHUNKS[0] * nr, LANES), src.dtype),
                            pltpu.SemaphoreType.DMA, pltpu.SemaphoreType.DMA]),
        out_shape=jax.ShapeDtypeStruct((n_rows * nr, LANES), src.dtype),
        compiler_params=_params(("arbitrary",)),
    )(dest, counts, pad_starts, src)


def _moe_kernel(be_ref, nv_ref, nused_ref, x_ref, wg_ref, wl_ref, bg_ref, bl_ref, wd_ref, bd_ref,
                o_ref, xbf, acc, wub, wdb, *, n_steps):
    b = pl.program_id(0)
    j = pl.program_id(1)
    live = b < nused_ref[0]
    both = jnp.logical_and(live, nv_ref[b] > MOE_HALF)
    single = jnp.logical_and(live, jnp.logical_not(both))
    tf = wg_ref.shape[1]
    nr = _packed_rows(acc.shape[1])

    @pl.when(jnp.logical_and(j == 0, jnp.logical_not(live)))
    def _():
        o_ref[...] = jnp.zeros_like(o_ref)

    def half(lo, unpack, pack):
        rows = slice(lo, lo + MOE_HALF)
        if unpack:
            _load_packed_rows(x_ref, xbf, lo, MOE_HALF)
        gu = jnp.dot(xbf[rows, :], wub[...], preferred_element_type=F32)
        glu = jnp.minimum(gu[:, :tf] + bg_ref[...], SWIGLU_LIMIT)
        lin = jnp.clip(gu[:, tf:] + bl_ref[...], -SWIGLU_LIMIT, SWIGLU_LIMIT)
        act = glu * jax.nn.sigmoid(SWIGLU_ALPHA * glu) * (lin + 1.0)
        down = jnp.dot(act.astype(BF16), wdb[...], preferred_element_type=F32)
        total = down + (bd_ref[...] if unpack else acc[rows, :])
        if pack:
            _store_packed_rows(o_ref, total.astype(BF16), lo)
        else:
            acc[rows, :] = total

    def region(n_halves, unpack, pack):
        wub[:, :tf] = wg_ref[...].astype(BF16)
        wub[:, tf:] = wl_ref[...].astype(BF16)
        wdb[...] = wd_ref[...].astype(BF16)
        for h in range(n_halves):
            half(h * MOE_HALF, unpack, pack)
        if pack and n_halves == 1:
            o_ref[MOE_HALF * nr:, :] = jnp.zeros((MOE_HALF * nr, LANES), o_ref.dtype)

    assert n_steps >= 2
    for n_halves, rows_live in ((2, both), (1, single)):
        for unpack, pack, at in ((True, False, j == 0), (False, True, j == n_steps - 1),
                                 (False, False, jnp.logical_and(j != 0, j != n_steps - 1))):
            pl.when(jnp.logical_and(rows_live, at))(functools.partial(region, n_halves, unpack, pack))


def _moe_call(block_e, block_nv, nused, xs, w_up, b_up, w_down, b_down):
    d_ff, d = w_down.shape[1:]
    nr = _packed_rows(d)
    n_rows = xs.shape[0] // nr
    tf = MOE_FF_TILE
    nj = d_ff // tf
    nb = n_rows // MOE_ROWS

    def bb(b, nu):
        return jnp.minimum(b, nu[0] - 1)

    def jj(b, j, nu):
        return jnp.where(b < nu[0], j, nj - 1)

    return pl.pallas_call(
        functools.partial(_moe_kernel, n_steps=nj),
        name="k_moe",
        grid_spec=pltpu.PrefetchScalarGridSpec(
            num_scalar_prefetch=3,
            grid=(nb, nj),
            in_specs=[pl.BlockSpec((MOE_ROWS * nr, LANES), lambda b, j, be, nv, nu: (bb(b, nu), 0)),
                      pl.BlockSpec((None, d, tf),
                                   lambda b, j, be, nv, nu: (be[bb(b, nu)], 0, jj(b, j, nu))),
                      pl.BlockSpec((None, d, tf),
                                   lambda b, j, be, nv, nu: (be[bb(b, nu)], 0, nj + jj(b, j, nu))),
                      pl.BlockSpec((None, 1, tf),
                                   lambda b, j, be, nv, nu: (be[bb(b, nu)], 0, jj(b, j, nu))),
                      pl.BlockSpec((None, 1, tf),
                                   lambda b, j, be, nv, nu: (be[bb(b, nu)], 0, nj + jj(b, j, nu))),
                      pl.BlockSpec((None, tf, d),
                                   lambda b, j, be, nv, nu: (be[bb(b, nu)], jj(b, j, nu), 0)),
                      pl.BlockSpec((None, 1, d), lambda b, j, be, nv, nu: (be[bb(b, nu)], 0, 0))],
            out_specs=pl.BlockSpec((MOE_ROWS * nr, LANES), lambda b, j, be, nv, nu: (b, 0)),
            scratch_shapes=[pltpu.VMEM((MOE_ROWS, d), BF16), pltpu.VMEM((MOE_ROWS, d), F32),
                            pltpu.VMEM((d, 2 * tf), BF16), pltpu.VMEM((tf, d), BF16)]),
        out_shape=jax.ShapeDtypeStruct((n_rows * nr, LANES), jnp.uint32),
        compiler_params=_params(("arbitrary", "arbitrary")),
    )(block_e, block_nv, nused, xs, w_up, w_up, b_up, b_up, w_down, b_down)


def _combine_kernel(dest_ref, h_ref, gate_ref, nw_ref, eo_hbm, yp_ref, ys_ref, gbuf, sem,
                    *, rows, n_first):
    i = pl.program_id(0)
    n = pl.num_programs(0)
    slot = i % 2
    d = h_ref.shape[1]
    nr = _packed_rows(d)

    def issue(tile, into, t):
        dst_rows = pl.ds(pl.multiple_of(t * nr, nr), nr)
        for k in range(TOP_K):
            e_row = dest_ref[(tile * rows + t) * TOP_K + k]
            src = eo_hbm.at[pl.ds(pl.multiple_of(e_row * nr, nr), nr)]
            pltpu.make_async_copy(src, gbuf.at[into, k, dst_rows], sem.at[into]).start(priority=k % 2)

    def wait(which):
        for k in range(TOP_K):
            pltpu.make_async_copy(eo_hbm.at[pl.ds(0, rows * nr)], gbuf.at[which, k],
                                  sem.at[which]).wait()

    @pl.when(i == 0)
    def _():
        lax.fori_loop(0, rows, lambda t, c: issue(i, slot, t) or c, 0, unroll=4)

    wait(slot)
    nxt = jnp.minimum(i + 1, n - 1)
    for t in range(rows):
        issue(nxt, 1 - slot, t)
    gate = gate_ref[...]
    gates = [gate[:, k:k + 1] for k in range(TOP_K)]
    lo_cols, hi_cols = [], []
    for s in range(nr):
        lo = hi = None
        for k in range(TOP_K):
            words = gbuf[slot, k, pl.ds(s, rows, stride=nr), :]
            a = gates[k] * lax.bitcast_convert_type(words << 16, F32)
            b = gates[k] * lax.bitcast_convert_type(words & jnp.uint32(0xFFFF0000), F32)
            lo = a if lo is None else lo + a
            hi = b if hi is None else hi + b
        lo_cols.append(lo)
        hi_cols.append(hi)
    y = h_ref[...] + jnp.concatenate(lo_cols + hi_cols, axis=1)
    r = lax.rsqrt(jnp.mean(y * y, axis=-1, keepdims=True) + EPS)
    y = y * r * nw_ref[...]

    @pl.when(i < n_first)
    def _():
        yp_ref[...] = y

    @pl.when(i >= n_first)
    def _():
        ys_ref[...] = y

    @pl.when(i == n - 1)
    def _():
        wait(1 - slot)


def _combine_call(dest, h, gate, nw, eo, t_first):
    t, d = h.shape
    rows = _pick_tile(np.gcd(t_first, t - t_first), (256, 128, 64))
    n_first = t_first // rows
    return pl.pallas_call(
        functools.partial(_combine_kernel, rows=rows, n_first=n_first),
        name="k_combine",
        grid_spec=pltpu.PrefetchScalarGridSpec(
            num_scalar_prefetch=1,
            grid=(t // rows,),
            in_specs=[pl.BlockSpec((rows, d), lambda i, de: (i, 0)),
                      pl.BlockSpec((rows, LANES), lambda i, de: (i, 0)),
                      pl.BlockSpec((1, d), lambda i, de: (0, 0)),
                      pl.BlockSpec(memory_space=pl.ANY)],
            out_specs=_two_part_specs(rows, d, n_first),
            scratch_shapes=[pltpu.VMEM((2, TOP_K, rows * _packed_rows(d), LANES), jnp.uint32),
                            pltpu.SemaphoreType.DMA((2,))]),
        out_shape=[jax.ShapeDtypeStruct((t_first, d), F32),
                   jax.ShapeDtypeStruct((t - t_first, d), F32)],
        compiler_params=_params(("arbitrary",)),
    )(dest, h, gate, nw, eo)


def kernel(x_prompt, x_sample, state_ssm, state_conv, state_pool, norm_mix_w, w_in, conv_w, conv_b,
           dt_bias, a_log, d_skip, ssd_norm_w, w_pool, pool_scale, w_proj_ssd, w_proj_pool, w_out,
           norm_ffn_w, w_router, b_router, w_up, b_up, w_down, b_down, norm_final_w):
    batch, seq_len, d_model = x_prompt.shape
    dec_batch, dec_seq, _ = x_sample.shape
    depth, _, n_heads, head_dim, d_state = state_ssm.shape
    assert depth == 1 and batch == 1 and dec_seq == CHUNK and seq_len % CHUNK == 0
    assert head_dim == HEAD_DIM and d_state == D_STATE
    d_inner = n_heads * head_dim
    assert d_inner == N_GROUPS * COL_BLOCK
    d_bc = 2 * N_GROUPS * D_STATE
    d_pool = state_pool.shape[-1]
    assert d_pool == len(POOL_WINDOWS) * COL_BLOCK and state_pool.shape[-2] == POOL_BUF
    n_experts = w_router.shape[-1]
    n_prompt_chunks = seq_len // CHUNK
    n_seq = batch + dec_batch
    t_prompt = batch * seq_len
    t = t_prompt + dec_batch * dec_seq

    xp = x_prompt.reshape(t_prompt, d_model)
    xs_tok = x_sample.reshape(-1, d_model)

    w_t = jnp.swapaxes(w_in[0], 0, 1)
    xbc_end = 2 * d_inner + d_bc
    dt_end = xbc_end + n_heads
    w_dt1, w_dt2 = _hi_lo(jnp.pad(w_t[xbc_end:dt_end].T, ((0, 0), (0, LANES - n_heads))))
    pad_h = lambda a: jnp.pad(a.reshape(1, n_heads), ((0, 0), (0, LANES - n_heads)))
    cw = conv_w[0]
    cwx, cwbc = cw[:, :d_inner], cw[:, d_inner:]
    cb = conv_b[0].reshape(1, -1)
    cbx, cbbc = cb[:, :d_inner], cb[:, d_inner:]
    dsk = jnp.repeat(d_skip[0], head_dim).reshape(1, d_inner)

    conv0 = jnp.pad(state_conv[0], ((batch, 0), (8 - (CONV_WIDTH - 1), 0), (0, 0)))
    cx0, cbc0 = conv0[..., :d_inner], conv0[..., d_inner:]
    pool0 = jnp.pad(state_pool[0], ((batch, 0), (1, 0), (0, 0)))

    u, dt_raw = _norm_call(xp, xs_tok, norm_mix_w[0].reshape(1, -1), w_dt1, w_dt2)
    p = _inproj_call(u, w_t, xbc_end, dt_end)
    yg, yp, ssm_new = _mixer_call(p, dt_raw, state_ssm[0], cx0, cbc0, pool0, pad_h(dt_bias[0]),
                                  pad_h(a_log[0]), dsk, cwx, cbx, cwbc, cbbc,
                                  w_pool[0].astype(BF16), pool_scale[0].reshape(1, -1),
                                  _ssd_consts(), n_prompt_chunks)
    merged = _merge_call(yg, yp, p, ssd_norm_w[0].reshape(1, -1), w_proj_ssd[0].astype(BF16),
                         w_proj_pool[0].astype(BF16))

    wr1, wr2 = _hi_lo(jnp.pad(w_router[0], ((0, 0), (0, LANES - n_experts))))
    br = jnp.pad(b_router[0].reshape(1, -1), ((0, 0), (0, LANES - n_experts)), constant_values=-1e30)
    h, hn, eidx, rank, gate, cnt = _route_call(merged, xp, xs_tok, w_out[0].astype(BF16),
                                               norm_ffn_w[0].reshape(1, -1), wr1, wr2, br)

    counts = cnt[0, :n_experts]
    padded = (counts + MOE_ROWS - 1) // MOE_ROWS * MOE_ROWS
    pad_ends = jnp.cumsum(padded)
    pad_starts = pad_ends - padded
    n_blocks = -(-(t * TOP_K) // MOE_ROWS) + n_experts
    n_rows = n_blocks * MOE_ROWS
    e_flat = eidx[:, :TOP_K].reshape(-1)
    dest = (pad_starts[e_flat] + rank[:, :TOP_K].reshape(-1)).astype(jnp.int32)
    block_start = jnp.arange(n_blocks, dtype=jnp.int32) * MOE_ROWS
    block_e = jnp.minimum(jnp.sum(block_start[:, None] >= pad_ends[None, :], axis=1),
                          n_experts - 1).astype(jnp.int32)
    block_nv = jnp.clip(pad_starts[block_e] + counts[block_e] - block_start, 0,
                        MOE_ROWS).astype(jnp.int32)
    nused = (pad_ends[-1:] // MOE_ROWS).astype(jnp.int32)

    xs = _dispatch_call(dest, counts, pad_starts.astype(jnp.int32), hn, n_rows, _packed_rows(d_model))
    eo = _moe_call(block_e, block_nv, nused, xs, w_up[0], b_up[0].reshape(n_experts, 1, -1),
                   w_down[0], b_down[0].reshape(n_experts, 1, -1))
    y_p, y_s = _combine_call(dest, h, gate, norm_final_w.reshape(1, -1), eo, t_prompt)

    y_prompt = y_p.reshape(batch, seq_len, d_model)
    y_sample = y_s.reshape(dec_batch, dec_seq, d_model)
    seq_ends = [t_prompt] * batch + [t_prompt + (s + 1) * dec_seq for s in range(dec_batch)]
    tail = jnp.stack([p[:, e - POOL_BUF:e] for e in seq_ends], axis=0)
    tail = tail.transpose(0, 2, 1, 3)
    ctail = tail[:, POOL_BUF - (CONV_WIDTH - 1):]
    conv_x = ctail[:, :, P_X:P_BC].reshape(n_seq, CONV_WIDTH - 1, d_inner)
    conv_bc = ctail[:, :, P_BC:P_POOL].reshape(n_seq, CONV_WIDTH - 1, d_bc)
    conv_new = jnp.concatenate([conv_x, conv_bc], axis=-1)
    pool_new = tail[:, :, P_POOL:P_GATE_SSD].reshape(n_seq, POOL_BUF, d_pool)
    return (y_prompt, y_sample,
            ssm_new[None, :batch], conv_new[None, :batch], pool_new[None, :batch],
            ssm_new[None, batch:], conv_new[None, batch:], pool_new[None, batch:])
```

```python
import functools

import numpy as np
import jax
import jax.numpy as jnp
from jax import lax
from jax.experimental import pallas as pl
from jax.experimental.pallas import tpu as pltpu

F32 = jnp.float32
BF16 = jnp.bfloat16

CHUNK = 64
HEAD_DIM = 64
N_GROUPS = 8
D_STATE = 128
CONV_WIDTH = 4
POOL_WINDOWS = (2, 4, 8, 16)
POOL_BUF = 15
PAST_LEN = 4096
TOP_K = 4
SWIGLU_ALPHA = 1.702
SWIGLU_LIMIT = 7.0
EPS = 1e-5
LANES = 128
COL_BLOCK = 512
MOE_ROWS = 1024
MOE_HALF = MOE_ROWS // 2
MOE_FF_TILE = 256
P_Z, P_X, P_BC, P_POOL, P_GATE_SSD, P_GATE_POOL = 0, 8, 16, 20, 24, 28
VMEM_LIMIT = 56 * 1024 * 1024


def _pick_tile(n, candidates):
    for c in candidates:
        if n % c == 0:
            return c
    raise ValueError(f"no tile for {n} in {candidates}")


def _params(sem, vmem=VMEM_LIMIT):
    return pltpu.CompilerParams(dimension_semantics=sem, vmem_limit_bytes=vmem)


def _split3(v):
    p1 = v.astype(BF16)
    r1 = v - p1.astype(F32)
    p2 = r1.astype(BF16)
    p3 = (r1 - p2.astype(F32)).astype(BF16)
    return p1, p2, p3


def _hi_lo(v):
    hi = v.astype(BF16)
    return hi, (v - hi.astype(F32)).astype(BF16)


def _dot_hi_lo(a1, a2, b1_ref, b2_ref):
    return (jnp.dot(a1, b1_ref[...], preferred_element_type=F32)
            + jnp.dot(a1, b2_ref[...], preferred_element_type=F32)
            + jnp.dot(a2, b1_ref[...], preferred_element_type=F32))


def _silu(v):
    h = 0.5 * v
    return h + h * jnp.tanh(h)


def _packed_rows(d):
    return d // 2 // LANES


def _store_packed_rows(ref, v_bf16, first_token=0):
    tm, d = v_bf16.shape
    bits = lax.bitcast_convert_type(v_bf16.astype(F32), jnp.uint32)
    words = (bits[:, :d // 2] >> 16) | (bits[:, d // 2:] & jnp.uint32(0xFFFF0000))
    nr = _packed_rows(d)
    for s in range(nr):
        ref[pl.ds(first_token * nr + s, tm, stride=nr), :] = words[:, s * LANES:(s + 1) * LANES]


def _load_packed_rows(ref, out_ref, first_token, n_tokens):
    d = out_ref.shape[1]
    nr = _packed_rows(d)
    rows = slice(first_token, first_token + n_tokens)
    for s in range(nr):
        words = ref[pl.ds(first_token * nr + s, n_tokens, stride=nr), :]
        lo = lax.bitcast_convert_type(words << 16, F32)
        hi = lax.bitcast_convert_type(words & jnp.uint32(0xFFFF0000), F32)
        out_ref[rows, s * LANES:(s + 1) * LANES] = lo.astype(BF16)
        out_ref[rows, d // 2 + s * LANES:d // 2 + (s + 1) * LANES] = hi.astype(BF16)


def _two_part_specs(tm, d, n_first):
    return [pl.BlockSpec((tm, d), lambda i, *_: (jnp.minimum(i, n_first - 1), 0)),
            pl.BlockSpec((tm, d), lambda i, *_: (jnp.maximum(i - n_first, 0), 0))]


def _norm_kernel(xp_ref, xs_ref, w_ref, wdt1_ref, wdt2_ref, u_ref, dt_ref, *, n_first):
    x = jnp.where(pl.program_id(0) < n_first, xp_ref[...], xs_ref[...])
    r = lax.rsqrt(jnp.mean(x * x, axis=-1, keepdims=True) + EPS)
    u = x * r * w_ref[...]
    u1, u2 = _hi_lo(u)
    u_ref[...] = u1
    dt_ref[...] = _dot_hi_lo(u1, u2, wdt1_ref, wdt2_ref)


def _norm_call(xp, xs, w, wdt1, wdt2):
    d = xp.shape[1]
    t = xp.shape[0] + xs.shape[0]
    tm = _pick_tile(np.gcd(xp.shape[0], xs.shape[0]), (512, 256, 128, 64))
    n_first = xp.shape[0] // tm
    return pl.pallas_call(
        functools.partial(_norm_kernel, n_first=n_first),
        name="k_norm",
        grid=(t // tm,),
        in_specs=_two_part_specs(tm, d, n_first) + [
            pl.BlockSpec((1, d), lambda i: (0, 0)),
            pl.BlockSpec((d, LANES), lambda i: (0, 0)),
            pl.BlockSpec((d, LANES), lambda i: (0, 0))],
        out_specs=[pl.BlockSpec((tm, d), lambda i: (i, 0)),
                   pl.BlockSpec((tm, LANES), lambda i: (i, 0))],
        out_shape=[jax.ShapeDtypeStruct((t, d), BF16), jax.ShapeDtypeStruct((t, LANES), F32)],
        compiler_params=_params(("parallel",)),
    )(xp, xs, w, wdt1, wdt2)


def _inproj_kernel(u_ref, wt_ref, o_ref, wbf):
    @pl.when(pl.program_id(1) == 0)
    def _():
        wbf[...] = wt_ref[...].T.astype(BF16)

    r = jnp.dot(u_ref[...], wbf[...], preferred_element_type=F32)
    for k in range(o_ref.shape[0]):
        o_ref[k] = r[:, k * COL_BLOCK:(k + 1) * COL_BLOCK]


def _inproj_call(u, w_t, n_head_cols, tail_start):
    t, d = u.shape
    per = 2
    tn = per * COL_BLOCK
    tm = _pick_tile(t, (1024, 512, 256, 128, 64))
    n_head_tiles = n_head_cols // tn
    n_tiles = n_head_tiles + (w_t.shape[0] - tail_start) // tn

    def first_row(j):
        assert tn % 8 == 0 and tail_start % 8 == 0
        return pl.multiple_of(
            jnp.where(j < n_head_tiles, j * tn, tail_start + (j - n_head_tiles) * tn), 8)

    return pl.pallas_call(
        _inproj_kernel,
        name="k_inproj",
        grid=(n_tiles, t // tm),
        in_specs=[pl.BlockSpec((tm, d), lambda j, i: (i, 0)),
                  pl.BlockSpec((pl.Element(tn), pl.Element(d)), lambda j, i: (first_row(j), 0))],
        out_specs=pl.BlockSpec((per, tm, COL_BLOCK), lambda j, i: (j, i, 0)),
        out_shape=jax.ShapeDtypeStruct((n_tiles * per, t, COL_BLOCK), F32),
        scratch_shapes=[pltpu.VMEM((d, tn), BF16)],
        compiler_params=_params(("arbitrary", "arbitrary")),
    )(u, w_t)


def _pool_chunk(c, first, pu_ref, pool0_ref, wp_ref, scale_ref, yp_ref, pbuf, n_prompt_chunks):
    hist = POOL_BUF + 1

    @pl.when(first)
    def _():
        pbuf[0:hist, :] = pool0_ref[...]

    @pl.when(jnp.logical_not(first))
    def _():
        pbuf[0:hist, :] = pbuf[CHUNK:CHUNK + hist, :]

    for g in range(len(POOL_WINDOWS)):
        pbuf[hist:hist + CHUNK, g * COL_BLOCK:(g + 1) * COL_BLOCK] = pu_ref[g]

    pos0 = jnp.where(c < n_prompt_chunks, c * CHUNK, PAST_LEN)
    pos = (pos0 + lax.broadcasted_iota(jnp.int32, (CHUNK, 1), 0)).astype(F32)
    for g, win in enumerate(POOL_WINDOWS):
        sl = slice(g * COL_BLOCK, (g + 1) * COL_BLOCK)
        tot = pbuf[:, sl]
        w = 1
        while w < win:
            tot = tot + pltpu.roll(tot, w, axis=0)
            w *= 2
        cur = pbuf[hist:hist + CHUNK, sl]
        count = jnp.minimum(pos + 1.0, float(win))
        pooled = tot[hist:] / count - cur
        yp_ref[:, sl] = jnp.dot(pooled.astype(BF16), wp_ref[g],
                                preferred_element_type=F32) * scale_ref[:, sl]


def _mixer_kernel(z_ref, x_ref, bc_ref, pu_ref, dt_ref, ssm0_ref, cx0_ref, cbc0_ref, pool0_ref,
                  dtb_ref, alog_ref, dsk_ref, cwx_ref, cbx_ref, cwbc_ref, cbbc_ref, wp_ref, scale_ref,
                  tril3_ref, e3_ref, diag_ref, caus_ref, bd_ref,
                  yg_ref, yp_ref, ssm_ref, st, bufx, bufbc, exs, pbuf, *, n_prompt_chunks):
    c = pl.program_id(0)
    first = jnp.logical_or(c == 0, c >= n_prompt_chunks)
    last = c >= n_prompt_chunks - 1
    pairs = N_GROUPS * COL_BLOCK // LANES

    @pl.when(c == 0)
    def _():
        st[...] = jnp.zeros_like(st)

    @pl.when(c >= n_prompt_chunks)
    def _():
        for q in range(pairs):
            blk = jnp.concatenate([ssm0_ref[2 * q], ssm0_ref[2 * q + 1]], axis=0)
            g, o = divmod(q * LANES, COL_BLOCK)
            st[g, :, o:o + LANES] = blk.T

    @pl.when(first)
    def _():
        bufx[0:8, :] = cx0_ref[...]
        bufbc[0:8, :] = cbc0_ref[...]

    @pl.when(jnp.logical_not(first))
    def _():
        bufx[0:8, :] = bufx[CHUNK:CHUNK + 8, :]
        bufbc[0:8, :] = bufbc[CHUNK:CHUNK + 8, :]

    for g in range(N_GROUPS):
        bufx[8:8 + CHUNK, g * COL_BLOCK:(g + 1) * COL_BLOCK] = x_ref[g]
    for q in range(N_GROUPS // 2):
        bufbc[8:8 + CHUNK, q * COL_BLOCK:(q + 1) * COL_BLOCK] = bc_ref[q]

    dtv = dt_ref[...] + dtb_ref[...]
    dt = jnp.maximum(dtv, 0.0) + jnp.log1p(jnp.exp(-jnp.abs(dtv)))
    d_a = dt * (-jnp.exp(alog_ref[...]))
    p1, p2, p3 = _split3(d_a)
    acum = jnp.dot(tril3_ref[...], jnp.concatenate([p1, p2, p3], axis=0),
                   preferred_element_type=F32)
    q1, q2, q3 = _split3(jnp.concatenate([acum, dt], axis=0))
    exs[...] = jnp.dot(jnp.concatenate([q1, q2, q3], axis=1), e3_ref[...],
                       preferred_element_type=F32)

    hw = 4 * HEAD_DIM
    for g in range(N_GROUPS):
        sl = slice(g * COL_BLOCK, (g + 1) * COL_BLOCK)
        xc = cbx_ref[:, sl]
        for k in range(CONV_WIDTH):
            xc = xc + cwx_ref[k:k + 1, sl] * bufx[5 + k:5 + k + CHUNK, sl]
        xs = _silu(xc)
        bc_g = []
        for lo in (g * D_STATE, (N_GROUPS + g) * D_STATE):
            slb = slice(lo, lo + D_STATE)
            acc = cbbc_ref[:, slb]
            for k in range(CONV_WIDTH):
                acc = acc + cwbc_ref[k:k + 1, slb] * bufbc[5 + k:5 + k + CHUNK, slb]
            bc_g.append(_silu(acc))
        b_g, c_g = bc_g
        acx = exs[0:CHUNK, sl]
        dtx = exs[CHUNK:2 * CHUNK, sl]
        alast = acx[CHUNK - 1:CHUNK, :]
        arow = jnp.sum(acx * diag_ref[:, sl], axis=0, keepdims=True)
        xdt = xs * dtx
        xdtb = xdt.astype(BF16)
        bb = b_g.astype(BF16)
        cb = c_g.astype(BF16)
        cb2 = lax.dot_general(cb, jnp.concatenate([bb, bb], axis=0),
                              (((1,), (1,)), ((), ())), preferred_element_type=F32)
        ydiag = []
        for q in range(2):
            lhs = []
            for d in range(2):
                lo = q * hw + d * LANES
                seg = acx[:, lo:lo + LANES] - arow[:, lo:lo + LANES]
                lhs.append((cb2 * jnp.exp(jnp.where(caus_ref[...] > 0.0, seg, -jnp.inf))).astype(BF16))
            xq = xdtb[:, q * hw:(q + 1) * hw]
            wq = jnp.concatenate([xq, xq, xq, xq], axis=0) * bd_ref[...]
            ydiag.append(jnp.dot(jnp.concatenate(lhs, axis=1), wq, preferred_element_type=F32))
        s_old = st[g]
        yoff = jnp.dot(cb, s_old.astype(BF16), preferred_element_type=F32) * jnp.exp(acx)
        y = jnp.concatenate(ydiag, axis=1) + yoff + dsk_ref[:, sl] * xs
        yg_ref[:, sl] = y * _silu(z_ref[g])
        v = (xdt * jnp.exp(alast - acx)).astype(BF16)
        st[g] = jnp.exp(alast) * s_old + jnp.dot(b_g.T.astype(BF16), v,
                                                 preferred_element_type=F32)

    @pl.when(last)
    def _():
        for q in range(pairs):
            g, o = divmod(q * LANES, COL_BLOCK)
            blk = st[g, :, o:o + LANES].T
            ssm_ref[2 * q] = blk[:HEAD_DIM]
            ssm_ref[2 * q + 1] = blk[HEAD_DIM:]

    _pool_chunk(c, first, pu_ref, pool0_ref, wp_ref, scale_ref, yp_ref, pbuf, n_prompt_chunks)


def _mixer_call(p, dt_raw, ssm_in, cx0, cbc0, pool0, dtb, alog, dsk, cwx, cbx, cwbc, cbbc, wp, scale,
                consts, n_prompt_chunks):
    _, t, _ = p.shape
    n_chunks = t // CHUNK
    n_seq = cx0.shape[0]
    n_heads, head_dim, d_state = ssm_in.shape[1:]
    d_inner = N_GROUPS * COL_BLOCK
    d_bc = N_GROUPS * 2 * D_STATE
    ng = len(POOL_WINDOWS)
    d_pool = ng * COL_BLOCK
    hist = POOL_BUF + 1
    tril3, e3, diag, caus, bd = consts

    def seq(c):
        return jnp.maximum(c - (n_prompt_chunks - 1), 0)

    def const(a):
        return pl.BlockSpec(a.shape, lambda c: (0,) * a.ndim)

    state_block = (None, n_heads, head_dim, d_state)
    return pl.pallas_call(
        functools.partial(_mixer_kernel, n_prompt_chunks=n_prompt_chunks),
        name="k_mixer",
        grid=(n_chunks,),
        in_specs=[pl.BlockSpec((N_GROUPS, CHUNK, COL_BLOCK), lambda c: (P_Z // N_GROUPS, c, 0)),
                  pl.BlockSpec((N_GROUPS, CHUNK, COL_BLOCK), lambda c: (P_X // N_GROUPS, c, 0)),
                  pl.BlockSpec((N_GROUPS // 2, CHUNK, COL_BLOCK),
                               lambda c: (P_BC // (N_GROUPS // 2), c, 0)),
                  pl.BlockSpec((ng, CHUNK, COL_BLOCK), lambda c: (P_POOL // ng, c, 0)),
                  pl.BlockSpec((CHUNK, LANES), lambda c: (c, 0)),
                  pl.BlockSpec(state_block, lambda c: (jnp.maximum(c - n_prompt_chunks, 0), 0, 0, 0)),
                  pl.BlockSpec((None, 8, d_inner), lambda c: (seq(c), 0, 0)),
                  pl.BlockSpec((None, 8, d_bc), lambda c: (seq(c), 0, 0)),
                  pl.BlockSpec((None, hist, d_pool), lambda c: (seq(c), 0, 0)),
                  const(dtb), const(alog), const(dsk), const(cwx), const(cbx),
                  const(cwbc), const(cbbc), const(wp), const(scale),
                  const(tril3), const(e3), const(diag), const(caus), const(bd)],
        out_specs=[pl.BlockSpec((CHUNK, d_inner), lambda c: (c, 0)),
                   pl.BlockSpec((CHUNK, d_pool), lambda c: (c, 0)),
                   pl.BlockSpec(state_block, lambda c: (seq(c), 0, 0, 0))],
        out_shape=[jax.ShapeDtypeStruct((t, d_inner), F32),
                   jax.ShapeDtypeStruct((t, d_pool), F32),
                   jax.ShapeDtypeStruct((n_seq, n_heads, head_dim, d_state), F32)],
        scratch_shapes=[pltpu.VMEM((N_GROUPS, D_STATE, COL_BLOCK), F32),
                        pltpu.VMEM((CHUNK + 8, d_inner), F32),
                        pltpu.VMEM((CHUNK + 8, d_bc), F32),
                        pltpu.VMEM((2 * CHUNK, d_inner), F32),
                        pltpu.VMEM((CHUNK + hist, d_pool), F32)],
        compiler_params=_params(("arbitrary",)),
    )(p, p, p, p, dt_raw, ssm_in, cx0, cbc0, pool0, dtb, alog, dsk, cwx, cbx, cwbc, cbbc, wp, scale,
      tril3, e3, diag, caus, bd)


def _ssd_consts():
    l = np.arange(CHUNK)
    tril = (l[:, None] >= l[None, :]).astype(np.float32)
    tril3 = np.concatenate([tril, tril, tril], axis=1)
    n_heads = N_GROUPS * COL_BLOCK // HEAD_DIM
    col_head = np.arange(n_heads * HEAD_DIM) // HEAD_DIM
    col_pos = np.arange(n_heads * HEAD_DIM) % HEAD_DIM
    e = (np.arange(LANES)[:, None] == col_head[None, :]).astype(np.float32)
    e3 = np.concatenate([e, e, e], axis=0)
    diag = (l[:, None] == col_pos[None, :]).astype(np.float32)
    caus = np.concatenate([tril, tril], axis=1)
    r = np.arange(4 * HEAD_DIM)
    bd = (r[:, None] // HEAD_DIM == r[None, :] // HEAD_DIM).astype(np.float32)
    return (jnp.asarray(tril3, BF16), jnp.asarray(e3, BF16), jnp.asarray(diag, F32),
            jnp.asarray(caus, F32), jnp.asarray(bd, BF16))


def _merge_kernel(yg_ref, yp_ref, gs_ref, gp_ref, nw_ref, wps_ref, wpp_ref, o_ref):
    y = yg_ref[...]
    r = lax.rsqrt(jnp.mean(y * y, axis=-1, keepdims=True) + EPS)
    yn = (y * r * nw_ref[...]).astype(BF16)
    a = jnp.dot(yn, wps_ref[...], preferred_element_type=F32)
    b = jnp.dot(yp_ref[...].astype(BF16), wpp_ref[...], preferred_element_type=F32)
    for j in range(gs_ref.shape[0]):
        sl = slice(j * COL_BLOCK, (j + 1) * COL_BLOCK)
        o_ref[:, sl] = (jax.nn.sigmoid(gs_ref[j]) * a[:, sl]
                        + jax.nn.sigmoid(gp_ref[j]) * b[:, sl]).astype(BF16)


def _merge_call(yg, yp, p, nw, wps, wpp):
    t, d_inner = yg.shape
    d_pool = yp.shape[1]
    d_model = wps.shape[1]
    tm = _pick_tile(t, (256, 128, 64))
    nj = d_model // COL_BLOCK
    once = pl.Buffered(1)
    return pl.pallas_call(
        _merge_kernel,
        name="k_merge",
        grid=(t // tm,),
        in_specs=[pl.BlockSpec((tm, d_inner), lambda i: (i, 0)),
                  pl.BlockSpec((tm, d_pool), lambda i: (i, 0)),
                  pl.BlockSpec((nj, tm, COL_BLOCK), lambda i: (P_GATE_SSD // nj, i, 0)),
                  pl.BlockSpec((nj, tm, COL_BLOCK), lambda i: (P_GATE_POOL // nj, i, 0)),
                  pl.BlockSpec((1, d_inner), lambda i: (0, 0)),
                  pl.BlockSpec((d_inner, d_model), lambda i: (0, 0), pipeline_mode=once),
                  pl.BlockSpec((d_pool, d_model), lambda i: (0, 0), pipeline_mode=once)],
        out_specs=pl.BlockSpec((tm, d_model), lambda i: (i, 0)),
        out_shape=jax.ShapeDtypeStruct((t, d_model), BF16),
        compiler_params=_params(("parallel",)),
    )(yg, yp, p, p, nw, wps, wpp)


def _route_kernel(m_ref, xp_ref, xs_ref, wo_ref, nw_ref, wr1_ref, wr2_ref, br_ref, trs_ref,
                  h_ref, hn_ref, eidx_ref, rank_ref, gate_ref, cnt_ref, carry, *, n_first):
    i = pl.program_id(0)

    @pl.when(i == 0)
    def _():
        carry[...] = jnp.zeros_like(carry)

    x = jnp.where(i < n_first, xp_ref[...], xs_ref[...])
    h = x + jnp.dot(m_ref[...], wo_ref[...], preferred_element_type=F32)
    h_ref[...] = h
    r = lax.rsqrt(jnp.mean(h * h, axis=-1, keepdims=True) + EPS)
    hn = h * r * nw_ref[...]
    h1, h2 = _hi_lo(hn)
    _store_packed_rows(hn_ref, h1)
    logits = _dot_hi_lo(h1, h2, wr1_ref, wr2_ref) + br_ref[...]
    lane = lax.broadcasted_iota(jnp.int32, logits.shape, 1)
    work = logits
    member = jnp.zeros(logits.shape, F32)
    vals, idxs = [], []
    for _ in range(TOP_K):
        m = jnp.max(work, axis=-1, keepdims=True)
        idx = jnp.min(jnp.where(work == m, lane, LANES), axis=-1, keepdims=True)
        hit = lane == idx
        member = member + hit.astype(F32)
        work = jnp.where(hit, -jnp.inf, work)
        vals.append(m)
        idxs.append(idx)
    ex = [jnp.exp(v - vals[0]) for v in vals]
    den = ex[0] + ex[1] + ex[2] + ex[3]
    before = jnp.dot(trs_ref[...], member.astype(BF16), preferred_element_type=F32) + carry[0:1, :]
    eidx = jnp.zeros(logits.shape, jnp.int32)
    rank = jnp.zeros(logits.shape, jnp.int32)
    gate = jnp.zeros(logits.shape, F32)
    for k in range(TOP_K):
        rk = jnp.sum(jnp.where(lane == idxs[k], before, 0.0), axis=-1, keepdims=True)
        eidx = jnp.where(lane == k, idxs[k], eidx)
        rank = jnp.where(lane == k, rk.astype(jnp.int32), rank)
        gate = jnp.where(lane == k, ex[k] / den, gate)
    eidx_ref[...] = eidx
    rank_ref[...] = rank
    gate_ref[...] = gate
    carry[0:1, :] = carry[0:1, :] + jnp.sum(member, axis=0, keepdims=True)
    cnt_ref[...] = carry[...].astype(jnp.int32)


def _route_call(merged, xp, xs, wo, nw, wr1, wr2, br):
    t, d = merged.shape
    tm = _pick_tile(np.gcd(xp.shape[0], xs.shape[0]), (512, 256, 128, 64))
    n_first = xp.shape[0] // tm
    ri = np.arange(tm)
    trs = jnp.asarray((ri[:, None] > ri[None, :]).astype(np.float32), BF16)
    row = lambda i: (i, 0)
    fix = lambda i: (0, 0)
    return pl.pallas_call(
        functools.partial(_route_kernel, n_first=n_first),
        name="k_route",
        grid=(t // tm,),
        in_specs=[pl.BlockSpec((tm, d), row)] + _two_part_specs(tm, d, n_first) + [
                  pl.BlockSpec((d, d), fix), pl.BlockSpec((1, d), fix),
                  pl.BlockSpec((d, LANES), fix), pl.BlockSpec((d, LANES), fix),
                  pl.BlockSpec((1, LANES), fix), pl.BlockSpec((tm, tm), fix)],
        out_specs=[pl.BlockSpec((tm, d), row), pl.BlockSpec((tm * _packed_rows(d), LANES), row),
                   pl.BlockSpec((tm, LANES), row), pl.BlockSpec((tm, LANES), row),
                   pl.BlockSpec((tm, LANES), row), pl.BlockSpec((8, LANES), fix)],
        out_shape=[jax.ShapeDtypeStruct((t, d), F32),
                   jax.ShapeDtypeStruct((t * _packed_rows(d), LANES), jnp.uint32),
                   jax.ShapeDtypeStruct((t, LANES), jnp.int32),
                   jax.ShapeDtypeStruct((t, LANES), jnp.int32),
                   jax.ShapeDtypeStruct((t, LANES), F32),
                   jax.ShapeDtypeStruct((8, LANES), jnp.int32)],
        scratch_shapes=[pltpu.VMEM((8, LANES), F32)],
        compiler_params=_params(("arbitrary",)),
    )(merged, xp, xs, wo, nw, wr1, wr2, br, trs)


_PAD_CHUNKS = tuple(1 << s for s in range(MOE_ROWS.bit_length() - 2, -1, -1))


def _dispatch_kernel(dest_ref, cnt_ref, pstart_ref, x_ref, o_hbm, zbuf, sem, zsem,
                     *, rows, nr, n_experts):
    i = pl.program_id(0)

    def tokens(ref, first, n):
        return ref.at[pl.ds(pl.multiple_of(first * nr, nr), n * nr)]

    def issue(t, carry):
        for k in range(TOP_K):
            d = dest_ref[(i * rows + t) * TOP_K + k]
            pltpu.make_async_copy(tokens(x_ref, t, 1), tokens(o_hbm, d, 1), sem).start(priority=k % 2)
        return carry

    lax.fori_loop(0, rows, issue, 0, unroll=4)
    for k in range(TOP_K):
        pltpu.make_async_copy(x_ref, tokens(o_hbm, 0, rows), sem).wait()

    @pl.when(i == pl.num_programs(0) - 1)
    def _():
        zbuf[...] = jnp.zeros_like(zbuf)

        def pad_copies(e, wait):
            cnt = cnt_ref[e]
            off = pstart_ref[e] + cnt
            npad = (-cnt) & (MOE_ROWS - 1)
            for n_c in _PAD_CHUNKS:
                cp = pltpu.make_async_copy(tokens(zbuf, 0, n_c), tokens(o_hbm, off, n_c), zsem)
                pl.when((npad & n_c) != 0)(cp.wait if wait else cp.start)
                off = off + (npad & n_c)

        def start_e(e, carry):
            pad_copies(e, False)
            return carry

        def wait_e(e, carry):
            pad_copies(e, True)
            return carry

        lax.fori_loop(0, n_experts, start_e, 0)
        lax.fori_loop(0, n_experts, wait_e, 0)

        ztok = zbuf.shape[0] // nr
        first = (pstart_ref[n_experts - 1] + cnt_ref[n_experts - 1] + MOE_ROWS - 1) // MOE_ROWS
        first = first * (MOE_ROWS // ztok)

        def tail_copy(c):
            return pltpu.make_async_copy(zbuf, tokens(o_hbm, c * ztok, ztok), zsem)

        def start_t(c, carry):
            tail_copy(c).start()
            return carry

        def wait_t(c, carry):
            tail_copy(c).wait()
            return carry

        lax.fori_loop(first, o_hbm.shape[0] // zbuf.shape[0], start_t, 0)
        lax.fori_loop(first, o_hbm.shape[0] // zbuf.shape[0], wait_t, 0)


def _dispatch_call(dest, counts, pad_starts, src, n_rows, nr):
    t = src.shape[0] // nr
    rows = _pick_tile(t, (512, 256, 128, 64))
    return pl.pallas_call(
        functools.partial(_dispatch_kernel, rows=rows, nr=nr, n_experts=counts.shape[0]),
        name="k_dispatch",
        grid_spec=pltpu.PrefetchScalarGridSpec(
            num_scalar_prefetch=3,
            grid=(t // rows,),
            in_specs=[pl.BlockSpec((rows * nr, LANES), lambda i, de, cn, ps: (i, 0))],
            out_specs=pl.BlockSpec(memory_space=pl.ANY),
            scratch_shapes=[pltpu.VMEM((_PAD_CHUNKS[0] * nr, LANES), src.dtype),
                            pltpu.SemaphoreType.DMA, pltpu.SemaphoreType.DMA]),
        out_shape=jax.ShapeDtypeStruct((n_rows * nr, LANES), src.dtype),
        compiler_params=_params(("arbitrary",)),
    )(dest, counts, pad_starts, src)


def _moe_kernel(be_ref, nv_ref, nused_ref, x_ref, wg_ref, wl_ref, bg_ref, bl_ref, wd_ref, bd_ref,
                o_ref, xbf, acc, wub, wdb, *, n_steps):
    b = pl.program_id(0)
    j = pl.program_id(1)
    live = b < nused_ref[0]
    both = jnp.logical_and(live, nv_ref[b] > MOE_HALF)
    single = jnp.logical_and(live, jnp.logical_not(both))
    tf = wg_ref.shape[1]
    nr = _packed_rows(acc.shape[1])

    @pl.when(jnp.logical_and(j == 0, jnp.logical_not(live)))
    def _():
        o_ref[...] = jnp.zeros_like(o_ref)

    def half(lo, unpack, pack):
        rows = slice(lo, lo + MOE_HALF)
        if unpack:
            _load_packed_rows(x_ref, xbf, lo, MOE_HALF)
        gu = jnp.dot(xbf[rows, :], wub[...], preferred_element_type=F32)
        glu = jnp.minimum(gu[:, :tf] + bg_ref[...], SWIGLU_LIMIT)
        lin = jnp.clip(gu[:, tf:] + bl_ref[...], -SWIGLU_LIMIT, SWIGLU_LIMIT)
        hg = 0.5 * glu
        act = (hg + hg * jnp.tanh((0.5 * SWIGLU_ALPHA) * glu)) * (lin + 1.0)
        down = jnp.dot(act.astype(BF16), wdb[...], preferred_element_type=F32)
        total = down + (bd_ref[...] if unpack else acc[rows, :])
        if pack:
            _store_packed_rows(o_ref, total.astype(BF16), lo)
        else:
            acc[rows, :] = total

    def region(n_halves, unpack, pack):
        wub[:, :tf] = wg_ref[...].astype(BF16)
        wub[:, tf:] = wl_ref[...].astype(BF16)
        wdb[...] = wd_ref[...].astype(BF16)
        for h in range(n_halves):
            half(h * MOE_HALF, unpack, pack)
        if pack and n_halves == 1:
            o_ref[MOE_HALF * nr:, :] = jnp.zeros((MOE_HALF * nr, LANES), o_ref.dtype)

    assert n_steps >= 2
    for n_halves, rows_live in ((2, both), (1, single)):
        for unpack, pack, at in ((True, False, j == 0), (False, True, j == n_steps - 1),
                                 (False, False, jnp.logical_and(j != 0, j != n_steps - 1))):
            pl.when(jnp.logical_and(rows_live, at))(functools.partial(region, n_halves, unpack, pack))


def _moe_call(block_e, block_nv, nused, xs, w_up, b_up, w_down, b_down):
    d_ff, d = w_down.shape[1:]
    nr = _packed_rows(d)
    n_rows = xs.shape[0] // nr
    tf = MOE_FF_TILE
    nj = d_ff // tf
    nb = n_rows // MOE_ROWS

    def bb(b, nu):
        return jnp.minimum(b, nu[0] - 1)

    def jj(b, j, nu):
        return jnp.where(b < nu[0], j, nj - 1)

    return pl.pallas_call(
        functools.partial(_moe_kernel, n_steps=nj),
        name="k_moe",
        grid_spec=pltpu.PrefetchScalarGridSpec(
            num_scalar_prefetch=3,
            grid=(nb, nj),
            in_specs=[pl.BlockSpec((MOE_ROWS * nr, LANES), lambda b, j, be, nv, nu: (bb(b, nu), 0)),
                      pl.BlockSpec((None, d, tf),
                                   lambda b, j, be, nv, nu: (be[bb(b, nu)], 0, jj(b, j, nu))),
                      pl.BlockSpec((None, d, tf),
                                   lambda b, j, be, nv, nu: (be[bb(b, nu)], 0, nj + jj(b, j, nu))),
                      pl.BlockSpec((None, 1, tf),
                                   lambda b, j, be, nv, nu: (be[bb(b, nu)], 0, jj(b, j, nu))),
                      pl.BlockSpec((None, 1, tf),
                                   lambda b, j, be, nv, nu: (be[bb(b, nu)], 0, nj + jj(b, j, nu))),
                      pl.BlockSpec((None, tf, d),
                                   lambda b, j, be, nv, nu: (be[bb(b, nu)], jj(b, j, nu), 0)),
                      pl.BlockSpec((None, 1, d), lambda b, j, be, nv, nu: (be[bb(b, nu)], 0, 0))],
            out_specs=pl.BlockSpec((MOE_ROWS * nr, LANES), lambda b, j, be, nv, nu: (b, 0)),
            scratch_shapes=[pltpu.VMEM((MOE_ROWS, d), BF16), pltpu.VMEM((MOE_ROWS, d), F32),
                            pltpu.VMEM((d, 2 * tf), BF16), pltpu.VMEM((tf, d), BF16)]),
        out_shape=jax.ShapeDtypeStruct((n_rows * nr, LANES), jnp.uint32),
        compiler_params=_params(("arbitrary", "arbitrary")),
    )(block_e, block_nv, nused, xs, w_up, w_up, b_up, b_up, w_down, b_down)


def _combine_kernel(dest_ref, h_ref, gate_ref, nw_ref, eo_hbm, yp_ref, ys_ref, gbuf, sem,
                    *, rows, n_first):
    i = pl.program_id(0)
    n = pl.num_programs(0)
    slot = i % 2
    d = h_ref.shape[1]
    nr = _packed_rows(d)

    def issue(tile, into, t):
        dst_rows = pl.ds(pl.multiple_of(t * nr, nr), nr)
        for k in range(TOP_K):
            e_row = dest_ref[(tile * rows + t) * TOP_K + k]
            src = eo_hbm.at[pl.ds(pl.multiple_of(e_row * nr, nr), nr)]
            pltpu.make_async_copy(src, gbuf.at[into, k, dst_rows], sem.at[into]).start(priority=k % 2)

    def wait(which):
        for k in range(TOP_K):
            pltpu.make_async_copy(eo_hbm.at[pl.ds(0, rows * nr)], gbuf.at[which, k],
                                  sem.at[which]).wait()

    @pl.when(i == 0)
    def _():
        lax.fori_loop(0, rows, lambda t, c: issue(i, slot, t) or c, 0, unroll=4)

    wait(slot)
    nxt = jnp.minimum(i + 1, n - 1)
    for t in range(rows):
        issue(nxt, 1 - slot, t)
    gate = gate_ref[...]
    gates = [gate[:, k:k + 1] for k in range(TOP_K)]
    lo_cols, hi_cols = [], []
    for s in range(nr):
        lo = hi = None
        for k in range(TOP_K):
            words = gbuf[slot, k, pl.ds(s, rows, stride=nr), :]
            a = gates[k] * lax.bitcast_convert_type(words << 16, F32)
            b = gates[k] * lax.bitcast_convert_type(words & jnp.uint32(0xFFFF0000), F32)
            lo = a if lo is None else lo + a
            hi = b if hi is None else hi + b
        lo_cols.append(lo)
        hi_cols.append(hi)
    y = h_ref[...] + jnp.concatenate(lo_cols + hi_cols, axis=1)
    r = lax.rsqrt(jnp.mean(y * y, axis=-1, keepdims=True) + EPS)
    y = y * r * nw_ref[...]

    @pl.when(i < n_first)
    def _():
        yp_ref[...] = y

    @pl.when(i >= n_first)
    def _():
        ys_ref[...] = y

    @pl.when(i == n - 1)
    def _():
        wait(1 - slot)


def _combine_call(dest, h, gate, nw, eo, t_first):
    t, d = h.shape
    rows = _pick_tile(np.gcd(t_first, t - t_first), (256, 128, 64))
    n_first = t_first // rows
    return pl.pallas_call(
        functools.partial(_combine_kernel, rows=rows, n_first=n_first),
        name="k_combine",
        grid_spec=pltpu.PrefetchScalarGridSpec(
            num_scalar_prefetch=1,
            grid=(t // rows,),
            in_specs=[pl.BlockSpec((rows, d), lambda i, de: (i, 0)),
                      pl.BlockSpec((rows, LANES), lambda i, de: (i, 0)),
                      pl.BlockSpec((1, d), lambda i, de: (0, 0)),
                      pl.BlockSpec(memory_space=pl.ANY)],
            out_specs=_two_part_specs(rows, d, n_first),
            scratch_shapes=[pltpu.VMEM((2, TOP_K, rows * _packed_rows(d), LANES), jnp.uint32),
                            pltpu.SemaphoreType.DMA((2,))]),
        out_shape=[jax.ShapeDtypeStruct((t_first, d), F32),
                   jax.ShapeDtypeStruct((t - t_first, d), F32)],
        compiler_params=_params(("arbitrary",)),
    )(dest, h, gate, nw, eo)


def kernel(x_prompt, x_sample, state_ssm, state_conv, state_pool, norm_mix_w, w_in, conv_w, conv_b,
           dt_bias, a_log, d_skip, ssd_norm_w, w_pool, pool_scale, w_proj_ssd, w_proj_pool, w_out,
           norm_ffn_w, w_router, b_router, w_up, b_up, w_down, b_down, norm_final_w):
    batch, seq_len, d_model = x_prompt.shape
    dec_batch, dec_seq, _ = x_sample.shape
    depth, _, n_heads, head_dim, d_state = state_ssm.shape
    assert depth == 1 and batch == 1 and dec_seq == CHUNK and seq_len % CHUNK == 0
    assert head_dim == HEAD_DIM and d_state == D_STATE
    d_inner = n_heads * head_dim
    assert d_inner == N_GROUPS * COL_BLOCK
    d_bc = 2 * N_GROUPS * D_STATE
    d_pool = state_pool.shape[-1]
    assert d_pool == len(POOL_WINDOWS) * COL_BLOCK and state_pool.shape[-2] == POOL_BUF
    n_experts = w_router.shape[-1]
    n_prompt_chunks = seq_len // CHUNK
    n_seq = batch + dec_batch
    t_prompt = batch * seq_len
    t = t_prompt + dec_batch * dec_seq

    xp = x_prompt.reshape(t_prompt, d_model)
    xs_tok = x_sample.reshape(-1, d_model)

    w_t = jnp.swapaxes(w_in[0], 0, 1)
    xbc_end = 2 * d_inner + d_bc
    dt_end = xbc_end + n_heads
    w_dt1, w_dt2 = _hi_lo(jnp.pad(w_t[xbc_end:dt_end].T, ((0, 0), (0, LANES - n_heads))))
    pad_h = lambda a: jnp.pad(a.reshape(1, n_heads), ((0, 0), (0, LANES - n_heads)))
    cw = conv_w[0]
    cwx, cwbc = cw[:, :d_inner], cw[:, d_inner:]
    cb = conv_b[0].reshape(1, -1)
    cbx, cbbc = cb[:, :d_inner], cb[:, d_inner:]
    dsk = jnp.repeat(d_skip[0], head_dim).reshape(1, d_inner)

    conv0 = jnp.pad(state_conv[0], ((batch, 0), (8 - (CONV_WIDTH - 1), 0), (0, 0)))
    cx0, cbc0 = conv0[..., :d_inner], conv0[..., d_inner:]
    pool0 = jnp.pad(state_pool[0], ((batch, 0), (1, 0), (0, 0)))

    u, dt_raw = _norm_call(xp, xs_tok, norm_mix_w[0].reshape(1, -1), w_dt1, w_dt2)
    p = _inproj_call(u, w_t, xbc_end, dt_end)
    yg, yp, ssm_new = _mixer_call(p, dt_raw, state_ssm[0], cx0, cbc0, pool0, pad_h(dt_bias[0]),
                                  pad_h(a_log[0]), dsk, cwx, cbx, cwbc, cbbc,
                                  w_pool[0].astype(BF16), pool_scale[0].reshape(1, -1),
                                  _ssd_consts(), n_prompt_chunks)
    merged = _merge_call(yg, yp, p, ssd_norm_w[0].reshape(1, -1), w_proj_ssd[0].astype(BF16),
                         w_proj_pool[0].astype(BF16))

    wr1, wr2 = _hi_lo(jnp.pad(w_router[0], ((0, 0), (0, LANES - n_experts))))
    br = jnp.pad(b_router[0].reshape(1, -1), ((0, 0), (0, LANES - n_experts)), constant_values=-1e30)
    h, hn, eidx, rank, gate, cnt = _route_call(merged, xp, xs_tok, w_out[0].astype(BF16),
                                               norm_ffn_w[0].reshape(1, -1), wr1, wr2, br)

    counts = cnt[0, :n_experts]
    padded = (counts + MOE_ROWS - 1) // MOE_ROWS * MOE_ROWS
    pad_ends = jnp.cumsum(padded)
    pad_starts = pad_ends - padded
    n_blocks = -(-(t * TOP_K) // MOE_ROWS) + n_experts
    n_rows = n_blocks * MOE_ROWS
    e_flat = eidx[:, :TOP_K].reshape(-1)
    dest = (pad_starts[e_flat] + rank[:, :TOP_K].reshape(-1)).astype(jnp.int32)
    block_start = jnp.arange(n_blocks, dtype=jnp.int32) * MOE_ROWS
    block_e = jnp.minimum(jnp.sum(block_start[:, None] >= pad_ends[None, :], axis=1),
                          n_experts - 1).astype(jnp.int32)
    block_nv = jnp.clip(pad_starts[block_e] + counts[block_e] - block_start, 0,
                        MOE_ROWS).astype(jnp.int32)
    nused = (pad_ends[-1:] // MOE_ROWS).astype(jnp.int32)

    xs = _dispatch_call(dest, counts, pad_starts.astype(jnp.int32), hn, n_rows, _packed_rows(d_model))
    eo = _moe_call(block_e, block_nv, nused, xs, w_up[0], b_up[0].reshape(n_experts, 1, -1),
                   w_down[0], b_down[0].reshape(n_experts, 1, -1))
    y_p, y_s = _combine_call(dest, h, gate, norm_final_w.reshape(1, -1), eo, t_prompt)

    y_prompt = y_p.reshape(batch, seq_len, d_model)
    y_sample = y_s.reshape(dec_batch, dec_seq, d_model)
    seq_ends = [t_prompt] * batch + [t_prompt + (s + 1) * dec_seq for s in range(dec_batch)]
    tail = jnp.stack([p[:, e - POOL_BUF:e] for e in seq_ends], axis=0)
    tail = tail.transpose(0, 2, 1, 3)
    ctail = tail[:, POOL_BUF - (CONV_WIDTH - 1):]
    conv_x = ctail[:, :, P_X:P_BC].reshape(n_seq, CONV_WIDTH - 1, d_inner)
    conv_bc = ctail[:, :, P_BC:P_POOL].reshape(n_seq, CONV_WIDTH - 1, d_bc)
    conv_new = jnp.concatenate([conv_x, conv_bc], axis=-1)
    pool_new = tail[:, :, P_POOL:P_GATE_SSD].reshape(n_seq, POOL_BUF, d_pool)
    return (y_prompt, y_sample,
            ssm_new[None, :batch], conv_new[None, :batch], pool_new[None, :batch],
            ssm_new[None, batch:], conv_new[None, batch:], pool_new[None, batch:])
```

```python
import functools

import numpy as np
import jax
import jax.numpy as jnp
from jax import lax
from jax.experimental import pallas as pl
from jax.experimental.pallas import tpu as pltpu

F32 = jnp.float32
BF16 = jnp.bfloat16

CHUNK = 64
HEAD_DIM = 64
N_GROUPS = 8
D_STATE = 128
CONV_WIDTH = 4
POOL_WINDOWS = (2, 4, 8, 16)
POOL_BUF = 15
PAST_LEN = 4096
TOP_K = 4
SWIGLU_ALPHA = 1.702
SWIGLU_LIMIT = 7.0
EPS = 1e-5
LANES = 128
COL_BLOCK = 512
MOE_ROWS = 1024
MOE_HALF = MOE_ROWS // 2
MOE_FF_TILE = 256
P_Z, P_X, P_BC, P_POOL, P_GATE_SSD, P_GATE_POOL = 0, 8, 16, 20, 24, 28
VMEM_LIMIT = 56 * 1024 * 1024


def _pick_tile(n, candidates):
    for c in candidates:
        if n % c == 0:
            return c
    raise ValueError(f"no tile for {n} in {candidates}")


def _params(sem, vmem=VMEM_LIMIT):
    return pltpu.CompilerParams(dimension_semantics=sem, vmem_limit_bytes=vmem)


def _split3(v):
    p1 = v.astype(BF16)
    r1 = v - p1.astype(F32)
    p2 = r1.astype(BF16)
    p3 = (r1 - p2.astype(F32)).astype(BF16)
    return p1, p2, p3


def _hi_lo(v):
    hi = v.astype(BF16)
    return hi, (v - hi.astype(F32)).astype(BF16)


def _dot_hi_lo(a1, a2, b1_ref, b2_ref):
    return (jnp.dot(a1, b1_ref[...], preferred_element_type=F32)
            + jnp.dot(a1, b2_ref[...], preferred_element_type=F32)
            + jnp.dot(a2, b1_ref[...], preferred_element_type=F32))


def _silu(v):
    h = 0.5 * v
    return h + h * jnp.tanh(h)


def _packed_rows(d):
    return d // 2 // LANES


def _store_packed_rows(ref, v_bf16, first_token=0):
    tm, d = v_bf16.shape
    bits = lax.bitcast_convert_type(v_bf16.astype(F32), jnp.uint32)
    words = (bits[:, :d // 2] >> 16) | (bits[:, d // 2:] & jnp.uint32(0xFFFF0000))
    nr = _packed_rows(d)
    for s in range(nr):
        ref[pl.ds(first_token * nr + s, tm, stride=nr), :] = words[:, s * LANES:(s + 1) * LANES]


def _load_packed_rows(ref, out_ref, first_token, n_tokens):
    d = out_ref.shape[1]
    nr = _packed_rows(d)
    rows = slice(first_token, first_token + n_tokens)
    for s in range(nr):
        words = ref[pl.ds(first_token * nr + s, n_tokens, stride=nr), :]
        lo = lax.bitcast_convert_type(words << 16, F32)
        hi = lax.bitcast_convert_type(words & jnp.uint32(0xFFFF0000), F32)
        out_ref[rows, s * LANES:(s + 1) * LANES] = lo.astype(BF16)
        out_ref[rows, d // 2 + s * LANES:d // 2 + (s + 1) * LANES] = hi.astype(BF16)


def _two_part_specs(tm, d, n_first):
    return [pl.BlockSpec((tm, d), lambda i, *_: (jnp.minimum(i, n_first - 1), 0)),
            pl.BlockSpec((tm, d), lambda i, *_: (jnp.maximum(i - n_first, 0), 0))]


def _norm_kernel(xp_ref, xs_ref, w_ref, wdt1_ref, wdt2_ref, u_ref, dt_ref, *, n_first):
    x = jnp.where(pl.program_id(0) < n_first, xp_ref[...], xs_ref[...])
    r = lax.rsqrt(jnp.mean(x * x, axis=-1, keepdims=True) + EPS)
    u = x * r * w_ref[...]
    u1, u2 = _hi_lo(u)
    u_ref[...] = u1
    dt_ref[...] = _dot_hi_lo(u1, u2, wdt1_ref, wdt2_ref)


def _norm_call(xp, xs, w, wdt1, wdt2):
    d = xp.shape[1]
    t = xp.shape[0] + xs.shape[0]
    tm = _pick_tile(np.gcd(xp.shape[0], xs.shape[0]), (512, 256, 128, 64))
    n_first = xp.shape[0] // tm
    return pl.pallas_call(
        functools.partial(_norm_kernel, n_first=n_first),
        name="k_norm",
        grid=(t // tm,),
        in_specs=_two_part_specs(tm, d, n_first) + [
            pl.BlockSpec((1, d), lambda i: (0, 0)),
            pl.BlockSpec((d, LANES), lambda i: (0, 0)),
            pl.BlockSpec((d, LANES), lambda i: (0, 0))],
        out_specs=[pl.BlockSpec((tm, d), lambda i: (i, 0)),
                   pl.BlockSpec((tm, LANES), lambda i: (i, 0))],
        out_shape=[jax.ShapeDtypeStruct((t, d), BF16), jax.ShapeDtypeStruct((t, LANES), F32)],
        compiler_params=_params(("parallel",)),
    )(xp, xs, w, wdt1, wdt2)


def _inproj_kernel(u_ref, wt_ref, o_ref, wbf):
    @pl.when(pl.program_id(1) == 0)
    def _():
        wbf[...] = wt_ref[...].T.astype(BF16)

    r = jnp.dot(u_ref[...], wbf[...], preferred_element_type=F32)
    for k in range(o_ref.shape[0]):
        o_ref[k] = r[:, k * COL_BLOCK:(k + 1) * COL_BLOCK]


def _inproj_call(u, w_t, n_head_cols, tail_start):
    t, d = u.shape
    per = 2
    tn = per * COL_BLOCK
    tm = _pick_tile(t, (1024, 512, 256, 128, 64))
    n_head_tiles = n_head_cols // tn
    n_tiles = n_head_tiles + (w_t.shape[0] - tail_start) // tn

    def first_row(j):
        assert tn % 8 == 0 and tail_start % 8 == 0
        return pl.multiple_of(
            jnp.where(j < n_head_tiles, j * tn, tail_start + (j - n_head_tiles) * tn), 8)

    return pl.pallas_call(
        _inproj_kernel,
        name="k_inproj",
        grid=(n_tiles, t // tm),
        in_specs=[pl.BlockSpec((tm, d), lambda j, i: (i, 0)),
                  pl.BlockSpec((pl.Element(tn), pl.Element(d)), lambda j, i: (first_row(j), 0))],
        out_specs=pl.BlockSpec((per, tm, COL_BLOCK), lambda j, i: (j, i, 0)),
        out_shape=jax.ShapeDtypeStruct((n_tiles * per, t, COL_BLOCK), F32),
        scratch_shapes=[pltpu.VMEM((d, tn), BF16)],
        compiler_params=_params(("arbitrary", "arbitrary")),
    )(u, w_t)


def _pool_chunk(c, first, pu_ref, pool0_ref, wp_ref, scale_ref, yp_ref, pbuf, n_prompt_chunks):
    hist = POOL_BUF + 1

    @pl.when(first)
    def _():
        pbuf[0:hist, :] = pool0_ref[...]

    @pl.when(jnp.logical_not(first))
    def _():
        pbuf[0:hist, :] = pbuf[CHUNK:CHUNK + hist, :]

    for g in range(len(POOL_WINDOWS)):
        pbuf[hist:hist + CHUNK, g * COL_BLOCK:(g + 1) * COL_BLOCK] = pu_ref[g]

    pos0 = jnp.where(c < n_prompt_chunks, c * CHUNK, PAST_LEN)
    pos = (pos0 + lax.broadcasted_iota(jnp.int32, (CHUNK, 1), 0)).astype(F32)
    for g, win in enumerate(POOL_WINDOWS):
        sl = slice(g * COL_BLOCK, (g + 1) * COL_BLOCK)
        tot = pbuf[:, sl]
        w = 1
        while w < win:
            tot = tot + pltpu.roll(tot, w, axis=0)
            w *= 2
        cur = pbuf[hist:hist + CHUNK, sl]
        count = jnp.minimum(pos + 1.0, float(win))
        pooled = tot[hist:] / count - cur
        yp_ref[:, sl] = jnp.dot(pooled.astype(BF16), wp_ref[g],
                                preferred_element_type=F32) * scale_ref[:, sl]


def _mixer_kernel(z_ref, x_ref, bc_ref, pu_ref, dt_ref, ssm0_ref, cx0_ref, cbc0_ref, pool0_ref,
                  dtb_ref, alog_ref, dsk_ref, cwx_ref, cbx_ref, cwbc_ref, cbbc_ref, wp_ref, scale_ref,
                  tril3_ref, e3_ref, diag_ref, caus_ref, bd_ref,
                  yg_ref, yp_ref, ssm_ref, st, bufx, bufbc, exs, pbuf, *, n_prompt_chunks):
    c = pl.program_id(0)
    first = jnp.logical_or(c == 0, c >= n_prompt_chunks)
    last = c >= n_prompt_chunks - 1
    pairs = N_GROUPS * COL_BLOCK // LANES

    @pl.when(c == 0)
    def _():
        st[...] = jnp.zeros_like(st)

    @pl.when(c >= n_prompt_chunks)
    def _():
        for q in range(pairs):
            blk = jnp.concatenate([ssm0_ref[2 * q], ssm0_ref[2 * q + 1]], axis=0)
            g, o = divmod(q * LANES, COL_BLOCK)
            st[g, :, o:o + LANES] = blk.T

    @pl.when(first)
    def _():
        bufx[0:8, :] = cx0_ref[...]
        bufbc[0:8, :] = cbc0_ref[...]

    @pl.when(jnp.logical_not(first))
    def _():
        bufx[0:8, :] = bufx[CHUNK:CHUNK + 8, :]
        bufbc[0:8, :] = bufbc[CHUNK:CHUNK + 8, :]

    for g in range(N_GROUPS):
        bufx[8:8 + CHUNK, g * COL_BLOCK:(g + 1) * COL_BLOCK] = x_ref[g]
    for q in range(N_GROUPS // 2):
        bufbc[8:8 + CHUNK, q * COL_BLOCK:(q + 1) * COL_BLOCK] = bc_ref[q]

    dtv = dt_ref[...] + dtb_ref[...]
    dt = jnp.maximum(dtv, 0.0) + jnp.log1p(jnp.exp(-jnp.abs(dtv)))
    d_a = dt * (-jnp.exp(alog_ref[...]))
    p1, p2, p3 = _split3(d_a)
    acum = jnp.dot(tril3_ref[...], jnp.concatenate([p1, p2, p3], axis=0),
                   preferred_element_type=F32)
    q1, q2, q3 = _split3(jnp.concatenate([acum, dt], axis=0))
    exs[...] = jnp.dot(jnp.concatenate([q1, q2, q3], axis=1), e3_ref[...],
                       preferred_element_type=F32)

    hw = 4 * HEAD_DIM
    for g in range(N_GROUPS):
        sl = slice(g * COL_BLOCK, (g + 1) * COL_BLOCK)
        xc = cbx_ref[:, sl]
        for k in range(CONV_WIDTH):
            xc = xc + cwx_ref[k:k + 1, sl] * bufx[5 + k:5 + k + CHUNK, sl]
        xs = _silu(xc)
        bc_g = []
        for lo in (g * D_STATE, (N_GROUPS + g) * D_STATE):
            slb = slice(lo, lo + D_STATE)
            acc = cbbc_ref[:, slb]
            for k in range(CONV_WIDTH):
                acc = acc + cwbc_ref[k:k + 1, slb] * bufbc[5 + k:5 + k + CHUNK, slb]
            bc_g.append(_silu(acc))
        b_g, c_g = bc_g
        acx = exs[0:CHUNK, sl]
        dtx = exs[CHUNK:2 * CHUNK, sl]
        alast = acx[CHUNK - 1:CHUNK, :]
        arow = jnp.sum(acx * diag_ref[:, sl], axis=0, keepdims=True)
        xdt = xs * dtx
        xdtb = xdt.astype(BF16)
        bb = b_g.astype(BF16)
        cb = c_g.astype(BF16)
        cb2 = lax.dot_general(cb, jnp.concatenate([bb, bb], axis=0),
                              (((1,), (1,)), ((), ())), preferred_element_type=F32)
        ydiag = []
        for q in range(2):
            lhs = []
            for d in range(2):
                lo = q * hw + d * LANES
                seg = acx[:, lo:lo + LANES] - arow[:, lo:lo + LANES]
                lhs.append((cb2 * jnp.exp(jnp.where(caus_ref[...] > 0.0, seg, -jnp.inf))).astype(BF16))
            xq = xdtb[:, q * hw:(q + 1) * hw]
            wq = jnp.concatenate([xq, xq, xq, xq], axis=0) * bd_ref[...]
            ydiag.append(jnp.dot(jnp.concatenate(lhs, axis=1), wq, preferred_element_type=F32))
        s_old = st[g]
        yoff = jnp.dot(cb, s_old.astype(BF16), preferred_element_type=F32) * jnp.exp(acx)
        y = jnp.concatenate(ydiag, axis=1) + yoff + dsk_ref[:, sl] * xs
        yg_ref[:, sl] = y * _silu(z_ref[g])
        v = (xdt * jnp.exp(alast - acx)).astype(BF16)
        st[g] = jnp.exp(alast) * s_old + jnp.dot(b_g.T.astype(BF16), v,
                                                 preferred_element_type=F32)

    @pl.when(last)
    def _():
        for q in range(pairs):
            g, o = divmod(q * LANES, COL_BLOCK)
            blk = st[g, :, o:o + LANES].T
            ssm_ref[2 * q] = blk[:HEAD_DIM]
            ssm_ref[2 * q + 1] = blk[HEAD_DIM:]

    _pool_chunk(c, first, pu_ref, pool0_ref, wp_ref, scale_ref, yp_ref, pbuf, n_prompt_chunks)


def _mixer_call(p, dt_raw, ssm_in, cx0, cbc0, pool0, dtb, alog, dsk, cwx, cbx, cwbc, cbbc, wp, scale,
                consts, n_prompt_chunks):
    _, t, _ = p.shape
    n_chunks = t // CHUNK
    n_seq = cx0.shape[0]
    n_heads, head_dim, d_state = ssm_in.shape[1:]
    d_inner = N_GROUPS * COL_BLOCK
    d_bc = N_GROUPS * 2 * D_STATE
    ng = len(POOL_WINDOWS)
    d_pool = ng * COL_BLOCK
    hist = POOL_BUF + 1
    tril3, e3, diag, caus, bd = consts

    def seq(c):
        return jnp.maximum(c - (n_prompt_chunks - 1), 0)

    def const(a):
        return pl.BlockSpec(a.shape, lambda c: (0,) * a.ndim)

    state_block = (None, n_heads, head_dim, d_state)
    return pl.pallas_call(
        functools.partial(_mixer_kernel, n_prompt_chunks=n_prompt_chunks),
        name="k_mixer",
        grid=(n_chunks,),
        in_specs=[pl.BlockSpec((N_GROUPS, CHUNK, COL_BLOCK), lambda c: (P_Z // N_GROUPS, c, 0)),
                  pl.BlockSpec((N_GROUPS, CHUNK, COL_BLOCK), lambda c: (P_X // N_GROUPS, c, 0)),
                  pl.BlockSpec((N_GROUPS // 2, CHUNK, COL_BLOCK),
                               lambda c: (P_BC // (N_GROUPS // 2), c, 0)),
                  pl.BlockSpec((ng, CHUNK, COL_BLOCK), lambda c: (P_POOL // ng, c, 0)),
                  pl.BlockSpec((CHUNK, LANES), lambda c: (c, 0)),
                  pl.BlockSpec(state_block, lambda c: (jnp.maximum(c - n_prompt_chunks, 0), 0, 0, 0)),
                  pl.BlockSpec((None, 8, d_inner), lambda c: (seq(c), 0, 0)),
                  pl.BlockSpec((None, 8, d_bc), lambda c: (seq(c), 0, 0)),
                  pl.BlockSpec((None, hist, d_pool), lambda c: (seq(c), 0, 0)),
                  const(dtb), const(alog), const(dsk), const(cwx), const(cbx),
                  const(cwbc), const(cbbc), const(wp), const(scale),
                  const(tril3), const(e3), const(diag), const(caus), const(bd)],
        out_specs=[pl.BlockSpec((CHUNK, d_inner), lambda c: (c, 0)),
                   pl.BlockSpec((CHUNK, d_pool), lambda c: (c, 0)),
                   pl.BlockSpec(state_block, lambda c: (seq(c), 0, 0, 0))],
        out_shape=[jax.ShapeDtypeStruct((t, d_inner), F32),
                   jax.ShapeDtypeStruct((t, d_pool), F32),
                   jax.ShapeDtypeStruct((n_seq, n_heads, head_dim, d_state), F32)],
        scratch_shapes=[pltpu.VMEM((N_GROUPS, D_STATE, COL_BLOCK), F32),
                        pltpu.VMEM((CHUNK + 8, d_inner), F32),
                        pltpu.VMEM((CHUNK + 8, d_bc), F32),
                        pltpu.VMEM((2 * CHUNK, d_inner), F32),
                        pltpu.VMEM((CHUNK + hist, d_pool), F32)],
        compiler_params=_params(("arbitrary",)),
    )(p, p, p, p, dt_raw, ssm_in, cx0, cbc0, pool0, dtb, alog, dsk, cwx, cbx, cwbc, cbbc, wp, scale,
      tril3, e3, diag, caus, bd)


def _ssd_consts():
    l = np.arange(CHUNK)
    tril = (l[:, None] >= l[None, :]).astype(np.float32)
    tril3 = np.concatenate([tril, tril, tril], axis=1)
    n_heads = N_GROUPS * COL_BLOCK // HEAD_DIM
    col_head = np.arange(n_heads * HEAD_DIM) // HEAD_DIM
    col_pos = np.arange(n_heads * HEAD_DIM) % HEAD_DIM
    e = (np.arange(LANES)[:, None] == col_head[None, :]).astype(np.float32)
    e3 = np.concatenate([e, e, e], axis=0)
    diag = (l[:, None] == col_pos[None, :]).astype(np.float32)
    caus = np.concatenate([tril, tril], axis=1)
    r = np.arange(4 * HEAD_DIM)
    bd = (r[:, None] // HEAD_DIM == r[None, :] // HEAD_DIM).astype(np.float32)
    return (jnp.asarray(tril3, BF16), jnp.asarray(e3, BF16), jnp.asarray(diag, F32),
            jnp.asarray(caus, F32), jnp.asarray(bd, BF16))


def _merge_kernel(yg_ref, yp_ref, gs_ref, gp_ref, nw_ref, wps_ref, wpp_ref, o_ref):
    y = yg_ref[...]
    r = lax.rsqrt(jnp.mean(y * y, axis=-1, keepdims=True) + EPS)
    yn = (y * r * nw_ref[...]).astype(BF16)
    a = jnp.dot(yn, wps_ref[...], preferred_element_type=F32)
    b = jnp.dot(yp_ref[...].astype(BF16), wpp_ref[...], preferred_element_type=F32)
    for j in range(gs_ref.shape[0]):
        sl = slice(j * COL_BLOCK, (j + 1) * COL_BLOCK)
        o_ref[:, sl] = (jax.nn.sigmoid(gs_ref[j]) * a[:, sl]
                        + jax.nn.sigmoid(gp_ref[j]) * b[:, sl]).astype(BF16)


def _merge_call(yg, yp, p, nw, wps, wpp):
    t, d_inner = yg.shape
    d_pool = yp.shape[1]
    d_model = wps.shape[1]
    tm = _pick_tile(t, (256, 128, 64))
    nj = d_model // COL_BLOCK
    once = pl.Buffered(1)
    return pl.pallas_call(
        _merge_kernel,
        name="k_merge",
        grid=(t // tm,),
        in_specs=[pl.BlockSpec((tm, d_inner), lambda i: (i, 0)),
                  pl.BlockSpec((tm, d_pool), lambda i: (i, 0)),
                  pl.BlockSpec((nj, tm, COL_BLOCK), lambda i: (P_GATE_SSD // nj, i, 0)),
                  pl.BlockSpec((nj, tm, COL_BLOCK), lambda i: (P_GATE_POOL // nj, i, 0)),
                  pl.BlockSpec((1, d_inner), lambda i: (0, 0)),
                  pl.BlockSpec((d_inner, d_model), lambda i: (0, 0), pipeline_mode=once),
                  pl.BlockSpec((d_pool, d_model), lambda i: (0, 0), pipeline_mode=once)],
        out_specs=pl.BlockSpec((tm, d_model), lambda i: (i, 0)),
        out_shape=jax.ShapeDtypeStruct((t, d_model), BF16),
        compiler_params=_params(("parallel",)),
    )(yg, yp, p, p, nw, wps, wpp)


def _route_kernel(m_ref, xp_ref, xs_ref, wo_ref, nw_ref, wr1_ref, wr2_ref, br_ref, trs_ref,
                  h_ref, hn_ref, eidx_ref, rank_ref, gate_ref, cnt_ref, carry, *, n_first):
    i = pl.program_id(0)

    @pl.when(i == 0)
    def _():
        carry[...] = jnp.zeros_like(carry)

    x = jnp.where(i < n_first, xp_ref[...], xs_ref[...])
    h = x + jnp.dot(m_ref[...], wo_ref[...], preferred_element_type=F32)
    h_ref[...] = h
    r = lax.rsqrt(jnp.mean(h * h, axis=-1, keepdims=True) + EPS)
    hn = h * r * nw_ref[...]
    h1, h2 = _hi_lo(hn)
    _store_packed_rows(hn_ref, h1)
    logits = _dot_hi_lo(h1, h2, wr1_ref, wr2_ref) + br_ref[...]
    lane = lax.broadcasted_iota(jnp.int32, logits.shape, 1)
    work = logits
    member = jnp.zeros(logits.shape, F32)
    vals, idxs = [], []
    for _ in range(TOP_K):
        m = jnp.max(work, axis=-1, keepdims=True)
        idx = jnp.min(jnp.where(work == m, lane, LANES), axis=-1, keepdims=True)
        hit = lane == idx
        member = member + hit.astype(F32)
        work = jnp.where(hit, -jnp.inf, work)
        vals.append(m)
        idxs.append(idx)
    ex = [jnp.exp(v - vals[0]) for v in vals]
    den = ex[0] + ex[1] + ex[2] + ex[3]
    before = jnp.dot(trs_ref[...], member.astype(BF16), preferred_element_type=F32) + carry[0:1, :]
    eidx = jnp.zeros(logits.shape, jnp.int32)
    rank = jnp.zeros(logits.shape, jnp.int32)
    gate = jnp.zeros(logits.shape, F32)
    for k in range(TOP_K):
        rk = jnp.sum(jnp.where(lane == idxs[k], before, 0.0), axis=-1, keepdims=True)
        eidx = jnp.where(lane == k, idxs[k], eidx)
        rank = jnp.where(lane == k, rk.astype(jnp.int32), rank)
        gate = jnp.where(lane == k, ex[k] / den, gate)
    eidx_ref[...] = eidx
    rank_ref[...] = rank
    gate_ref[...] = gate
    carry[0:1, :] = carry[0:1, :] + jnp.sum(member, axis=0, keepdims=True)
    cnt_ref[...] = carry[...].astype(jnp.int32)


def _route_call(merged, xp, xs, wo, nw, wr1, wr2, br):
    t, d = merged.shape
    tm = _pick_tile(np.gcd(xp.shape[0], xs.shape[0]), (512, 256, 128, 64))
    n_first = xp.shape[0] // tm
    ri = np.arange(tm)
    trs = jnp.asarray((ri[:, None] > ri[None, :]).astype(np.float32), BF16)
    row = lambda i: (i, 0)
    fix = lambda i: (0, 0)
    return pl.pallas_call(
        functools.partial(_route_kernel, n_first=n_first),
        name="k_route",
        grid=(t // tm,),
        in_specs=[pl.BlockSpec((tm, d), row)] + _two_part_specs(tm, d, n_first) + [
                  pl.BlockSpec((d, d), fix), pl.BlockSpec((1, d), fix),
                  pl.BlockSpec((d, LANES), fix), pl.BlockSpec((d, LANES), fix),
                  pl.BlockSpec((1, LANES), fix), pl.BlockSpec((tm, tm), fix)],
        out_specs=[pl.BlockSpec((tm, d), row), pl.BlockSpec((tm * _packed_rows(d), LANES), row),
                   pl.BlockSpec((tm, LANES), row), pl.BlockSpec((tm, LANES), row),
                   pl.BlockSpec((tm, LANES), row), pl.BlockSpec((8, LANES), fix)],
        out_shape=[jax.ShapeDtypeStruct((t, d), F32),
                   jax.ShapeDtypeStruct((t * _packed_rows(d), LANES), jnp.uint32),
                   jax.ShapeDtypeStruct((t, LANES), jnp.int32),
                   jax.ShapeDtypeStruct((t, LANES), jnp.int32),
                   jax.ShapeDtypeStruct((t, LANES), F32),
                   jax.ShapeDtypeStruct((8, LANES), jnp.int32)],
        scratch_shapes=[pltpu.VMEM((8, LANES), F32)],
        compiler_params=_params(("arbitrary",)),
    )(merged, xp, xs, wo, nw, wr1, wr2, br, trs)


_PAD_CHUNKS = tuple(1 << s for s in range(MOE_ROWS.bit_length() - 2, -1, -1))


def _dispatch_kernel(dest_ref, cnt_ref, pstart_ref, x_ref, o_hbm, zbuf, sem, zsem,
                     *, rows, nr, n_experts):
    i = pl.program_id(0)

    def tokens(ref, first, n):
        return ref.at[pl.ds(pl.multiple_of(first * nr, nr), n * nr)]

    def issue(t, carry):
        for k in range(TOP_K):
            d = dest_ref[(i * rows + t) * TOP_K + k]
            pltpu.make_async_copy(tokens(x_ref, t, 1), tokens(o_hbm, d, 1), sem).start(priority=k % 2)
        return carry

    lax.fori_loop(0, rows, issue, 0, unroll=4)
    for k in range(TOP_K):
        pltpu.make_async_copy(x_ref, tokens(o_hbm, 0, rows), sem).wait()

    @pl.when(i == pl.num_programs(0) - 1)
    def _():
        zbuf[...] = jnp.zeros_like(zbuf)

        def pad_copies(e, wait):
            cnt = cnt_ref[e]
            off = pstart_ref[e] + cnt
            npad = (-cnt) & (MOE_ROWS - 1)
            for n_c in _PAD_CHUNKS:
                cp = pltpu.make_async_copy(tokens(zbuf, 0, n_c), tokens(o_hbm, off, n_c), zsem)
                pl.when((npad & n_c) != 0)(cp.wait if wait else cp.start)
                off = off + (npad & n_c)

        def start_e(e, carry):
            pad_copies(e, False)
            return carry

        def wait_e(e, carry):
            pad_copies(e, True)
            return carry

        lax.fori_loop(0, n_experts, start_e, 0)
        lax.fori_loop(0, n_experts, wait_e, 0)

        ztok = zbuf.shape[0] // nr
        first = (pstart_ref[n_experts - 1] + cnt_ref[n_experts - 1] + MOE_ROWS - 1) // MOE_ROWS
        first = first * (MOE_ROWS // ztok)

        def tail_copy(c):
            return pltpu.make_async_copy(zbuf, tokens(o_hbm, c * ztok, ztok), zsem)

        def start_t(c, carry):
            tail_copy(c).start()
            return carry

        def wait_t(c, carry):
            tail_copy(c).wait()
            return carry

        lax.fori_loop(first, o_hbm.shape[0] // zbuf.shape[0], start_t, 0)
        lax.fori_loop(first, o_hbm.shape[0] // zbuf.shape[0], wait_t, 0)


def _dispatch_call(dest, counts, pad_starts, src, n_rows, nr):
    t = src.shape[0] // nr
    rows = _pick_tile(t, (512, 256, 128, 64))
    return pl.pallas_call(
        functools.partial(_dispatch_kernel, rows=rows, nr=nr, n_experts=counts.shape[0]),
        name="k_dispatch",
        grid_spec=pltpu.PrefetchScalarGridSpec(
            num_scalar_prefetch=3,
            grid=(t // rows,),
            in_specs=[pl.BlockSpec((rows * nr, LANES), lambda i, de, cn, ps: (i, 0))],
            out_specs=pl.BlockSpec(memory_space=pl.ANY),
            scratch_shapes=[pltpu.VMEM((_PAD_CHUNKS[0] * nr, LANES), src.dtype),
                            pltpu.SemaphoreType.DMA, pltpu.SemaphoreType.DMA]),
        out_shape=jax.ShapeDtypeStruct((n_rows * nr, LANES), src.dtype),
        compiler_params=_params(("arbitrary",)),
    )(dest, counts, pad_starts, src)


def _moe_kernel(be_ref, nv_ref, nused_ref, x_ref, wg_ref, wl_ref, bu_ref, wd_ref, bd_ref,
                o_ref, xbf, acc, wub, wdb, *, n_steps):
    b = pl.program_id(0)
    j = pl.program_id(1)
    live = b < nused_ref[0]
    both = jnp.logical_and(live, nv_ref[b] > MOE_HALF)
    single = jnp.logical_and(live, jnp.logical_not(both))
    tf = wg_ref.shape[1]
    nr = _packed_rows(acc.shape[1])

    @pl.when(jnp.logical_and(j == 0, jnp.logical_not(live)))
    def _():
        o_ref[...] = jnp.zeros_like(o_ref)

    def half(lo, unpack, pack):
        rows = slice(lo, lo + MOE_HALF)
        if unpack:
            _load_packed_rows(x_ref, xbf, lo, MOE_HALF)
        gu = jnp.dot(xbf[rows, :], wub[...], preferred_element_type=F32)
        glu = jnp.minimum(gu[:, :tf] + bu_ref[pl.ds(j, 1), :], SWIGLU_LIMIT)
        lin = jnp.clip(gu[:, tf:] + bu_ref[pl.ds(n_steps + j, 1), :], -SWIGLU_LIMIT, SWIGLU_LIMIT)
        hg = 0.5 * glu
        act = (hg + hg * jnp.tanh((0.5 * SWIGLU_ALPHA) * glu)) * (lin + 1.0)
        down = jnp.dot(act.astype(BF16), wdb[...], preferred_element_type=F32)
        total = down + (bd_ref[...] if unpack else acc[rows, :])
        if pack:
            _store_packed_rows(o_ref, total.astype(BF16), lo)
        else:
            acc[rows, :] = total

    def region(n_halves, unpack, pack):
        wub[:, :tf] = wg_ref[...].astype(BF16)
        wub[:, tf:] = wl_ref[...].astype(BF16)
        wdb[...] = wd_ref[...].astype(BF16)
        for h in range(n_halves):
            half(h * MOE_HALF, unpack, pack)
        if pack and n_halves == 1:
            o_ref[MOE_HALF * nr:, :] = jnp.zeros((MOE_HALF * nr, LANES), o_ref.dtype)

    assert n_steps >= 2
    for n_halves, rows_live in ((2, both), (1, single)):
        for unpack, pack, at in ((True, False, j == 0), (False, True, j == n_steps - 1),
                                 (False, False, jnp.logical_and(j != 0, j != n_steps - 1))):
            pl.when(jnp.logical_and(rows_live, at))(functools.partial(region, n_halves, unpack, pack))


def _moe_call(block_e, block_nv, nused, xs, w_up, b_up, w_down, b_down):
    d_ff, d = w_down.shape[1:]
    nr = _packed_rows(d)
    n_rows = xs.shape[0] // nr
    tf = MOE_FF_TILE
    nj = d_ff // tf
    nb = n_rows // MOE_ROWS

    def bb(b, nu):
        return jnp.minimum(b, nu[0] - 1)

    def jj(b, j, nu):
        return jnp.where(b < nu[0], j, nj - 1)

    return pl.pallas_call(
        functools.partial(_moe_kernel, n_steps=nj),
        name="k_moe",
        grid_spec=pltpu.PrefetchScalarGridSpec(
            num_scalar_prefetch=3,
            grid=(nb, nj),
            in_specs=[pl.BlockSpec((MOE_ROWS * nr, LANES), lambda b, j, be, nv, nu: (bb(b, nu), 0)),
                      pl.BlockSpec((None, d, tf),
                                   lambda b, j, be, nv, nu: (be[bb(b, nu)], 0, jj(b, j, nu))),
                      pl.BlockSpec((None, d, tf),
                                   lambda b, j, be, nv, nu: (be[bb(b, nu)], 0, nj + jj(b, j, nu))),
                      pl.BlockSpec((None, 2 * nj, tf), lambda b, j, be, nv, nu: (be[bb(b, nu)], 0, 0)),
                      pl.BlockSpec((None, tf, d),
                                   lambda b, j, be, nv, nu: (be[bb(b, nu)], jj(b, j, nu), 0)),
                      pl.BlockSpec((None, 1, d), lambda b, j, be, nv, nu: (be[bb(b, nu)], 0, 0))],
            out_specs=pl.BlockSpec((MOE_ROWS * nr, LANES), lambda b, j, be, nv, nu: (b, 0)),
            scratch_shapes=[pltpu.VMEM((MOE_ROWS, d), BF16), pltpu.VMEM((MOE_ROWS, d), F32),
                            pltpu.VMEM((d, 2 * tf), BF16), pltpu.VMEM((tf, d), BF16)]),
        out_shape=jax.ShapeDtypeStruct((n_rows * nr, LANES), jnp.uint32),
        compiler_params=_params(("arbitrary", "arbitrary")),
    )(block_e, block_nv, nused, xs, w_up, w_up, b_up.reshape(-1, 2 * nj, tf), w_down, b_down)


def _combine_kernel(dest_ref, h_ref, gate_ref, nw_ref, eo_hbm, yp_ref, ys_ref, gbuf, sem,
                    *, rows, n_first):
    i = pl.program_id(0)
    n = pl.num_programs(0)
    slot = i % 2
    d = h_ref.shape[1]
    nr = _packed_rows(d)

    def issue(tile, into, t):
        dst_rows = pl.ds(pl.multiple_of(t * nr, nr), nr)
        for k in range(TOP_K):
            e_row = dest_ref[(tile * rows + t) * TOP_K + k]
            src = eo_hbm.at[pl.ds(pl.multiple_of(e_row * nr, nr), nr)]
            pltpu.make_async_copy(src, gbuf.at[into, k, dst_rows], sem.at[into]).start(priority=k % 2)

    def wait(which):
        for k in range(TOP_K):
            pltpu.make_async_copy(eo_hbm.at[pl.ds(0, rows * nr)], gbuf.at[which, k],
                                  sem.at[which]).wait()

    @pl.when(i == 0)
    def _():
        lax.fori_loop(0, rows, lambda t, c: issue(i, slot, t) or c, 0, unroll=4)

    wait(slot)
    nxt = jnp.minimum(i + 1, n - 1)
    for t in range(rows):
        issue(nxt, 1 - slot, t)
    gate = gate_ref[...]
    gates = [gate[:, k:k + 1] for k in range(TOP_K)]
    lo_cols, hi_cols = [], []
    for s in range(nr):
        lo = hi = None
        for k in range(TOP_K):
            words = gbuf[slot, k, pl.ds(s, rows, stride=nr), :]
            a = gates[k] * lax.bitcast_convert_type(words << 16, F32)
            b = gates[k] * lax.bitcast_convert_type(words & jnp.uint32(0xFFFF0000), F32)
            lo = a if lo is None else lo + a
            hi = b if hi is None else hi + b
        lo_cols.append(lo)
        hi_cols.append(hi)
    y = h_ref[...] + jnp.concatenate(lo_cols + hi_cols, axis=1)
    r = lax.rsqrt(jnp.mean(y * y, axis=-1, keepdims=True) + EPS)
    y = y * r * nw_ref[...]

    @pl.when(i < n_first)
    def _():
        yp_ref[...] = y

    @pl.when(i >= n_first)
    def _():
        ys_ref[...] = y

    @pl.when(i == n - 1)
    def _():
        wait(1 - slot)


def _combine_call(dest, h, gate, nw, eo, t_first):
    t, d = h.shape
    rows = _pick_tile(np.gcd(t_first, t - t_first), (256, 128, 64))
    n_first = t_first // rows
    return pl.pallas_call(
        functools.partial(_combine_kernel, rows=rows, n_first=n_first),
        name="k_combine",
        grid_spec=pltpu.PrefetchScalarGridSpec(
            num_scalar_prefetch=1,
            grid=(t // rows,),
            in_specs=[pl.BlockSpec((rows, d), lambda i, de: (i, 0)),
                      pl.BlockSpec((rows, LANES), lambda i, de: (i, 0)),
                      pl.BlockSpec((1, d), lambda i, de: (0, 0)),
                      pl.BlockSpec(memory_space=pl.ANY)],
            out_specs=_two_part_specs(rows, d, n_first),
            scratch_shapes=[pltpu.VMEM((2, TOP_K, rows * _packed_rows(d), LANES), jnp.uint32),
                            pltpu.SemaphoreType.DMA((2,))]),
        out_shape=[jax.ShapeDtypeStruct((t_first, d), F32),
                   jax.ShapeDtypeStruct((t - t_first, d), F32)],
        compiler_params=_params(("arbitrary",)),
    )(dest, h, gate, nw, eo)


def kernel(x_prompt, x_sample, state_ssm, state_conv, state_pool, norm_mix_w, w_in, conv_w, conv_b,
           dt_bias, a_log, d_skip, ssd_norm_w, w_pool, pool_scale, w_proj_ssd, w_proj_pool, w_out,
           norm_ffn_w, w_router, b_router, w_up, b_up, w_down, b_down, norm_final_w):
    batch, seq_len, d_model = x_prompt.shape
    dec_batch, dec_seq, _ = x_sample.shape
    depth, _, n_heads, head_dim, d_state = state_ssm.shape
    assert depth == 1 and batch == 1 and dec_seq == CHUNK and seq_len % CHUNK == 0
    assert head_dim == HEAD_DIM and d_state == D_STATE
    d_inner = n_heads * head_dim
    assert d_inner == N_GROUPS * COL_BLOCK
    d_bc = 2 * N_GROUPS * D_STATE
    d_pool = state_pool.shape[-1]
    assert d_pool == len(POOL_WINDOWS) * COL_BLOCK and state_pool.shape[-2] == POOL_BUF
    n_experts = w_router.shape[-1]
    n_prompt_chunks = seq_len // CHUNK
    n_seq = batch + dec_batch
    t_prompt = batch * seq_len
    t = t_prompt + dec_batch * dec_seq

    xp = x_prompt.reshape(t_prompt, d_model)
    xs_tok = x_sample.reshape(-1, d_model)

    w_t = jnp.swapaxes(w_in[0], 0, 1)
    xbc_end = 2 * d_inner + d_bc
    dt_end = xbc_end + n_heads
    w_dt1, w_dt2 = _hi_lo(jnp.pad(w_t[xbc_end:dt_end].T, ((0, 0), (0, LANES - n_heads))))
    pad_h = lambda a: jnp.pad(a.reshape(1, n_heads), ((0, 0), (0, LANES - n_heads)))
    cw = conv_w[0]
    cwx, cwbc = cw[:, :d_inner], cw[:, d_inner:]
    cb = conv_b[0].reshape(1, -1)
    cbx, cbbc = cb[:, :d_inner], cb[:, d_inner:]
    dsk = jnp.repeat(d_skip[0], head_dim).reshape(1, d_inner)

    conv0 = jnp.pad(state_conv[0], ((batch, 0), (8 - (CONV_WIDTH - 1), 0), (0, 0)))
    cx0, cbc0 = conv0[..., :d_inner], conv0[..., d_inner:]
    pool0 = jnp.pad(state_pool[0], ((batch, 0), (1, 0), (0, 0)))

    u, dt_raw = _norm_call(xp, xs_tok, norm_mix_w[0].reshape(1, -1), w_dt1, w_dt2)
    p = _inproj_call(u, w_t, xbc_end, dt_end)
    yg, yp, ssm_new = _mixer_call(p, dt_raw, state_ssm[0], cx0, cbc0, pool0, pad_h(dt_bias[0]),
                                  pad_h(a_log[0]), dsk, cwx, cbx, cwbc, cbbc,
                                  w_pool[0].astype(BF16), pool_scale[0].reshape(1, -1),
                                  _ssd_consts(), n_prompt_chunks)
    merged = _merge_call(yg, yp, p, ssd_norm_w[0].reshape(1, -1), w_proj_ssd[0].astype(BF16),
                         w_proj_pool[0].astype(BF16))

    wr1, wr2 = _hi_lo(jnp.pad(w_router[0], ((0, 0), (0, LANES - n_experts))))
    br = jnp.pad(b_router[0].reshape(1, -1), ((0, 0), (0, LANES - n_experts)), constant_values=-1e30)
    h, hn, eidx, rank, gate, cnt = _route_call(merged, xp, xs_tok, w_out[0].astype(BF16),
                                               norm_ffn_w[0].reshape(1, -1), wr1, wr2, br)

    counts = cnt[0, :n_experts]
    padded = (counts + MOE_ROWS - 1) // MOE_ROWS * MOE_ROWS
    pad_ends = jnp.cumsum(padded)
    pad_starts = pad_ends - padded
    n_blocks = -(-(t * TOP_K) // MOE_ROWS) + n_experts
    n_rows = n_blocks * MOE_ROWS
    e_flat = eidx[:, :TOP_K].reshape(-1)
    dest = (pad_starts[e_flat] + rank[:, :TOP_K].reshape(-1)).astype(jnp.int32)
    block_start = jnp.arange(n_blocks, dtype=jnp.int32) * MOE_ROWS
    block_e = jnp.minimum(jnp.sum(block_start[:, None] >= pad_ends[None, :], axis=1),
                          n_experts - 1).astype(jnp.int32)
    block_nv = jnp.clip(pad_starts[block_e] + counts[block_e] - block_start, 0,
                        MOE_ROWS).astype(jnp.int32)
    nused = (pad_ends[-1:] // MOE_ROWS).astype(jnp.int32)

    xs = _dispatch_call(dest, counts, pad_starts.astype(jnp.int32), hn, n_rows, _packed_rows(d_model))
    eo = _moe_call(block_e, block_nv, nused, xs, w_up[0], b_up[0],
                   w_down[0], b_down[0].reshape(n_experts, 1, -1))
    y_p, y_s = _combine_call(dest, h, gate, norm_final_w.reshape(1, -1), eo, t_prompt)

    y_prompt = y_p.reshape(batch, seq_len, d_model)
    y_sample = y_s.reshape(dec_batch, dec_seq, d_model)
    seq_ends = [t_prompt] * batch + [t_prompt + (s + 1) * dec_seq for s in range(dec_batch)]
    tail = jnp.stack([p[:, e - POOL_BUF:e] for e in seq_ends], axis=0)
    tail = tail.transpose(0, 2, 1, 3)
    ctail = tail[:, POOL_BUF - (CONV_WIDTH - 1):]
    conv_x = ctail[:, :, P_X:P_BC].reshape(n_seq, CONV_WIDTH - 1, d_inner)
    conv_bc = ctail[:, :, P_BC:P_POOL].reshape(n_seq, CONV_WIDTH - 1, d_bc)
    conv_new = jnp.concatenate([conv_x, conv_bc], axis=-1)
    pool_new = tail[:, :, P_POOL:P_GATE_SSD].reshape(n_seq, POOL_BUF, d_pool)
    return (y_prompt, y_sample,
            ssm_new[None, :batch], conv_new[None, :batch], pool_new[None, :batch],
            ssm_new[None, batch:], conv_new[None, batch:], pool_new[None, batch:])
```

```python
import functools

import numpy as np
import jax
import jax.numpy as jnp
from jax import lax
from jax.experimental import pallas as pl
from jax.experimental.pallas import tpu as pltpu

F32 = jnp.float32
BF16 = jnp.bfloat16

CHUNK = 64
HEAD_DIM = 64
N_GROUPS = 8
D_STATE = 128
CONV_WIDTH = 4
POOL_WINDOWS = (2, 4, 8, 16)
POOL_BUF = 15
PAST_LEN = 4096
TOP_K = 4
SWIGLU_ALPHA = 1.702
SWIGLU_LIMIT = 7.0
EPS = 1e-5
LANES = 128
COL_BLOCK = 512
MOE_ROWS = 1024
MOE_HALF = MOE_ROWS // 2
MOE_FF_TILE = 256
P_Z, P_X, P_BC, P_POOL, P_GATE_SSD, P_GATE_POOL = 0, 8, 16, 20, 24, 28
VMEM_LIMIT = 56 * 1024 * 1024


def _pick_tile(n, candidates):
    for c in candidates:
        if n % c == 0:
            return c
    raise ValueError(f"no tile for {n} in {candidates}")


def _params(sem, vmem=VMEM_LIMIT):
    return pltpu.CompilerParams(dimension_semantics=sem, vmem_limit_bytes=vmem)


def _split3(v):
    p1 = v.astype(BF16)
    r1 = v - p1.astype(F32)
    p2 = r1.astype(BF16)
    p3 = (r1 - p2.astype(F32)).astype(BF16)
    return p1, p2, p3


def _hi_lo(v):
    hi = v.astype(BF16)
    return hi, (v - hi.astype(F32)).astype(BF16)


def _dot_hi_lo(a1, a2, b1_ref, b2_ref):
    return (jnp.dot(a1, b1_ref[...], preferred_element_type=F32)
            + jnp.dot(a1, b2_ref[...], preferred_element_type=F32)
            + jnp.dot(a2, b1_ref[...], preferred_element_type=F32))


def _silu(v):
    h = 0.5 * v
    return h + h * jnp.tanh(h)


def _packed_rows(d):
    return d // 2 // LANES


def _store_packed_rows(ref, v_bf16, first_token=0):
    tm, d = v_bf16.shape
    bits = lax.bitcast_convert_type(v_bf16.astype(F32), jnp.uint32)
    words = (bits[:, :d // 2] >> 16) | (bits[:, d // 2:] & jnp.uint32(0xFFFF0000))
    nr = _packed_rows(d)
    for s in range(nr):
        ref[pl.ds(first_token * nr + s, tm, stride=nr), :] = words[:, s * LANES:(s + 1) * LANES]


def _load_packed_rows(ref, out_ref, first_token, n_tokens):
    d = out_ref.shape[1]
    nr = _packed_rows(d)
    rows = slice(first_token, first_token + n_tokens)
    for s in range(nr):
        words = ref[pl.ds(first_token * nr + s, n_tokens, stride=nr), :]
        lo = lax.bitcast_convert_type(words << 16, F32)
        hi = lax.bitcast_convert_type(words & jnp.uint32(0xFFFF0000), F32)
        out_ref[rows, s * LANES:(s + 1) * LANES] = lo.astype(BF16)
        out_ref[rows, d // 2 + s * LANES:d // 2 + (s + 1) * LANES] = hi.astype(BF16)


def _two_part_specs(tm, d, n_first):
    return [pl.BlockSpec((tm, d), lambda i, *_: (jnp.minimum(i, n_first - 1), 0)),
            pl.BlockSpec((tm, d), lambda i, *_: (jnp.maximum(i - n_first, 0), 0))]


def _norm_kernel(xp_ref, xs_ref, w_ref, wdt1_ref, wdt2_ref, u_ref, dt_ref, *, n_first):
    x = jnp.where(pl.program_id(0) < n_first, xp_ref[...], xs_ref[...])
    r = lax.rsqrt(jnp.mean(x * x, axis=-1, keepdims=True) + EPS)
    u = x * r * w_ref[...]
    u1, u2 = _hi_lo(u)
    u_ref[...] = u1
    dt_ref[...] = _dot_hi_lo(u1, u2, wdt1_ref, wdt2_ref)


def _norm_call(xp, xs, w, wdt1, wdt2):
    d = xp.shape[1]
    t = xp.shape[0] + xs.shape[0]
    tm = _pick_tile(np.gcd(xp.shape[0], xs.shape[0]), (512, 256, 128, 64))
    n_first = xp.shape[0] // tm
    return pl.pallas_call(
        functools.partial(_norm_kernel, n_first=n_first),
        name="k_norm",
        grid=(t // tm,),
        in_specs=_two_part_specs(tm, d, n_first) + [
            pl.BlockSpec((1, d), lambda i: (0, 0)),
            pl.BlockSpec((d, LANES), lambda i: (0, 0)),
            pl.BlockSpec((d, LANES), lambda i: (0, 0))],
        out_specs=[pl.BlockSpec((tm, d), lambda i: (i, 0)),
                   pl.BlockSpec((tm, LANES), lambda i: (i, 0))],
        out_shape=[jax.ShapeDtypeStruct((t, d), BF16), jax.ShapeDtypeStruct((t, LANES), F32)],
        compiler_params=_params(("parallel",)),
    )(xp, xs, w, wdt1, wdt2)


def _inproj_kernel(u_ref, wt_ref, o_ref, wbf):
    @pl.when(pl.program_id(1) == 0)
    def _():
        wbf[...] = wt_ref[...].T.astype(BF16)

    r = jnp.dot(u_ref[...], wbf[...], preferred_element_type=F32)
    for k in range(o_ref.shape[0]):
        o_ref[k] = r[:, k * COL_BLOCK:(k + 1) * COL_BLOCK]


def _inproj_call(u, w_t, n_head_cols, tail_start):
    t, d = u.shape
    per = 2
    tn = per * COL_BLOCK
    tm = _pick_tile(t, (1024, 512, 256, 128, 64))
    n_head_tiles = n_head_cols // tn
    n_tiles = n_head_tiles + (w_t.shape[0] - tail_start) // tn

    def first_row(j):
        assert tn % 8 == 0 and tail_start % 8 == 0
        return pl.multiple_of(
            jnp.where(j < n_head_tiles, j * tn, tail_start + (j - n_head_tiles) * tn), 8)

    return pl.pallas_call(
        _inproj_kernel,
        name="k_inproj",
        grid=(n_tiles, t // tm),
        in_specs=[pl.BlockSpec((tm, d), lambda j, i: (i, 0)),
                  pl.BlockSpec((pl.Element(tn), pl.Element(d)), lambda j, i: (first_row(j), 0))],
        out_specs=pl.BlockSpec((per, tm, COL_BLOCK), lambda j, i: (j, i, 0)),
        out_shape=jax.ShapeDtypeStruct((n_tiles * per, t, COL_BLOCK), F32),
        scratch_shapes=[pltpu.VMEM((d, tn), BF16)],
        compiler_params=_params(("arbitrary", "arbitrary")),
    )(u, w_t)


def _pool_chunk(c, first, pu_ref, pool0_ref, wp_ref, scale_ref, yp_ref, pbuf, n_prompt_chunks):
    hist = POOL_BUF + 1

    @pl.when(first)
    def _():
        pbuf[0:hist, :] = pool0_ref[...]

    @pl.when(jnp.logical_not(first))
    def _():
        pbuf[0:hist, :] = pbuf[CHUNK:CHUNK + hist, :]

    for g in range(len(POOL_WINDOWS)):
        pbuf[hist:hist + CHUNK, g * COL_BLOCK:(g + 1) * COL_BLOCK] = pu_ref[g]

    pos0 = jnp.where(c < n_prompt_chunks, c * CHUNK, PAST_LEN)
    pos = (pos0 + lax.broadcasted_iota(jnp.int32, (CHUNK, 1), 0)).astype(F32)
    for g, win in enumerate(POOL_WINDOWS):
        sl = slice(g * COL_BLOCK, (g + 1) * COL_BLOCK)
        tot = pbuf[:, sl]
        w = 1
        while w < win:
            tot = tot + pltpu.roll(tot, w, axis=0)
            w *= 2
        cur = pbuf[hist:hist + CHUNK, sl]
        count = jnp.minimum(pos + 1.0, float(win))
        pooled = tot[hist:] / count - cur
        yp_ref[:, sl] = jnp.dot(pooled.astype(BF16), wp_ref[g],
                                preferred_element_type=F32) * scale_ref[:, sl]


def _mixer_kernel(z_ref, x_ref, bc_ref, pu_ref, dt_ref, ssm0_ref, cx0_ref, cbc0_ref, pool0_ref,
                  dtb_ref, alog_ref, dsk_ref, cwx_ref, cbx_ref, cwbc_ref, cbbc_ref, wp_ref, scale_ref,
                  tril3_ref, e3_ref, diag_ref, caus_ref, bd_ref,
                  yg_ref, yp_ref, ssm_ref, st, bufx, bufbc, exs, pbuf, *, n_prompt_chunks):
    c = pl.program_id(0)
    first = jnp.logical_or(c == 0, c >= n_prompt_chunks)
    last = c >= n_prompt_chunks - 1
    pairs = N_GROUPS * COL_BLOCK // LANES

    @pl.when(c == 0)
    def _():
        st[...] = jnp.zeros_like(st)

    @pl.when(c >= n_prompt_chunks)
    def _():
        for q in range(pairs):
            blk = jnp.concatenate([ssm0_ref[2 * q], ssm0_ref[2 * q + 1]], axis=0)
            g, o = divmod(q * LANES, COL_BLOCK)
            st[g, :, o:o + LANES] = blk.T

    @pl.when(first)
    def _():
        bufx[0:8, :] = cx0_ref[...]
        bufbc[0:8, :] = cbc0_ref[...]

    @pl.when(jnp.logical_not(first))
    def _():
        bufx[0:8, :] = bufx[CHUNK:CHUNK + 8, :]
        bufbc[0:8, :] = bufbc[CHUNK:CHUNK + 8, :]

    for g in range(N_GROUPS):
        bufx[8:8 + CHUNK, g * COL_BLOCK:(g + 1) * COL_BLOCK] = x_ref[g]
    for q in range(N_GROUPS // 2):
        bufbc[8:8 + CHUNK, q * COL_BLOCK:(q + 1) * COL_BLOCK] = bc_ref[q]

    dtv = dt_ref[...] + dtb_ref[...]
    dt = jnp.maximum(dtv, 0.0) + jnp.log1p(jnp.exp(-jnp.abs(dtv)))
    d_a = dt * (-jnp.exp(alog_ref[...]))
    p1, p2, p3 = _split3(d_a)
    acum = jnp.dot(tril3_ref[...], jnp.concatenate([p1, p2, p3], axis=0),
                   preferred_element_type=F32)
    q1, q2, q3 = _split3(jnp.concatenate([acum, dt], axis=0))
    exs[...] = jnp.dot(jnp.concatenate([q1, q2, q3], axis=1), e3_ref[...],
                       preferred_element_type=F32)

    hw = 4 * HEAD_DIM
    for g in range(N_GROUPS):
        sl = slice(g * COL_BLOCK, (g + 1) * COL_BLOCK)
        xc = cbx_ref[:, sl]
        for k in range(CONV_WIDTH):
            xc = xc + cwx_ref[k:k + 1, sl] * bufx[5 + k:5 + k + CHUNK, sl]
        xs = _silu(xc)
        bc_g = []
        for lo in (g * D_STATE, (N_GROUPS + g) * D_STATE):
            slb = slice(lo, lo + D_STATE)
            acc = cbbc_ref[:, slb]
            for k in range(CONV_WIDTH):
                acc = acc + cwbc_ref[k:k + 1, slb] * bufbc[5 + k:5 + k + CHUNK, slb]
            bc_g.append(_silu(acc))
        b_g, c_g = bc_g
        acx = exs[0:CHUNK, sl]
        dtx = exs[CHUNK:2 * CHUNK, sl]
        alast = acx[CHUNK - 1:CHUNK, :]
        arow = jnp.sum(acx * diag_ref[:, sl], axis=0, keepdims=True)
        xdt = xs * dtx
        xdtb = xdt.astype(BF16)
        bb = b_g.astype(BF16)
        cb = c_g.astype(BF16)
        cb2 = lax.dot_general(cb, jnp.concatenate([bb, bb], axis=0),
                              (((1,), (1,)), ((), ())), preferred_element_type=F32)
        ydiag = []
        for q in range(2):
            lhs = []
            for d in range(2):
                lo = q * hw + d * LANES
                seg = acx[:, lo:lo + LANES] - arow[:, lo:lo + LANES]
                lhs.append((cb2 * jnp.exp(jnp.where(caus_ref[...] > 0.0, seg, -jnp.inf))).astype(BF16))
            xq = xdtb[:, q * hw:(q + 1) * hw]
            wq = jnp.concatenate([xq, xq, xq, xq], axis=0) * bd_ref[...]
            ydiag.append(jnp.dot(jnp.concatenate(lhs, axis=1), wq, preferred_element_type=F32))
        s_old = st[g]
        yoff = jnp.dot(cb, s_old.astype(BF16), preferred_element_type=F32) * jnp.exp(acx)
        y = jnp.concatenate(ydiag, axis=1) + yoff + dsk_ref[:, sl] * xs
        yg_ref[:, sl] = y * _silu(z_ref[g])
        v = (xdt * jnp.exp(alast - acx)).astype(BF16)
        st[g] = jnp.exp(alast) * s_old + lax.dot_general(
            bb, v, (((0,), (0,)), ((), ())), preferred_element_type=F32)

    @pl.when(last)
    def _():
        for q in range(pairs):
            g, o = divmod(q * LANES, COL_BLOCK)
            blk = st[g, :, o:o + LANES].T
            ssm_ref[2 * q] = blk[:HEAD_DIM]
            ssm_ref[2 * q + 1] = blk[HEAD_DIM:]

    _pool_chunk(c, first, pu_ref, pool0_ref, wp_ref, scale_ref, yp_ref, pbuf, n_prompt_chunks)


def _mixer_call(p, dt_raw, ssm_in, cx0, cbc0, pool0, dtb, alog, dsk, cwx, cbx, cwbc, cbbc, wp, scale,
                consts, n_prompt_chunks):
    _, t, _ = p.shape
    n_chunks = t // CHUNK
    n_seq = cx0.shape[0]
    n_heads, head_dim, d_state = ssm_in.shape[1:]
    d_inner = N_GROUPS * COL_BLOCK
    d_bc = N_GROUPS * 2 * D_STATE
    ng = len(POOL_WINDOWS)
    d_pool = ng * COL_BLOCK
    hist = POOL_BUF + 1
    tril3, e3, diag, caus, bd = consts

    def seq(c):
        return jnp.maximum(c - (n_prompt_chunks - 1), 0)

    def const(a):
        return pl.BlockSpec(a.shape, lambda c: (0,) * a.ndim)

    state_block = (None, n_heads, head_dim, d_state)
    return pl.pallas_call(
        functools.partial(_mixer_kernel, n_prompt_chunks=n_prompt_chunks),
        name="k_mixer",
        grid=(n_chunks,),
        in_specs=[pl.BlockSpec((N_GROUPS, CHUNK, COL_BLOCK), lambda c: (P_Z // N_GROUPS, c, 0)),
                  pl.BlockSpec((N_GROUPS, CHUNK, COL_BLOCK), lambda c: (P_X // N_GROUPS, c, 0)),
                  pl.BlockSpec((N_GROUPS // 2, CHUNK, COL_BLOCK),
                               lambda c: (P_BC // (N_GROUPS // 2), c, 0)),
                  pl.BlockSpec((ng, CHUNK, COL_BLOCK), lambda c: (P_POOL // ng, c, 0)),
                  pl.BlockSpec((CHUNK, LANES), lambda c: (c, 0)),
                  pl.BlockSpec(state_block, lambda c: (jnp.maximum(c - n_prompt_chunks, 0), 0, 0, 0)),
                  pl.BlockSpec((None, 8, d_inner), lambda c: (seq(c), 0, 0)),
                  pl.BlockSpec((None, 8, d_bc), lambda c: (seq(c), 0, 0)),
                  pl.BlockSpec((None, hist, d_pool), lambda c: (seq(c), 0, 0)),
                  const(dtb), const(alog), const(dsk), const(cwx), const(cbx),
                  const(cwbc), const(cbbc), const(wp), const(scale),
                  const(tril3), const(e3), const(diag), const(caus), const(bd)],
        out_specs=[pl.BlockSpec((CHUNK, d_inner), lambda c: (c, 0)),
                   pl.BlockSpec((CHUNK, d_pool), lambda c: (c, 0)),
                   pl.BlockSpec(state_block, lambda c: (seq(c), 0, 0, 0))],
        out_shape=[jax.ShapeDtypeStruct((t, d_inner), F32),
                   jax.ShapeDtypeStruct((t, d_pool), F32),
                   jax.ShapeDtypeStruct((n_seq, n_heads, head_dim, d_state), F32)],
        scratch_shapes=[pltpu.VMEM((N_GROUPS, D_STATE, COL_BLOCK), F32),
                        pltpu.VMEM((CHUNK + 8, d_inner), F32),
                        pltpu.VMEM((CHUNK + 8, d_bc), F32),
                        pltpu.VMEM((2 * CHUNK, d_inner), F32),
                        pltpu.VMEM((CHUNK + hist, d_pool), F32)],
        compiler_params=_params(("arbitrary",)),
    )(p, p, p, p, dt_raw, ssm_in, cx0, cbc0, pool0, dtb, alog, dsk, cwx, cbx, cwbc, cbbc, wp, scale,
      tril3, e3, diag, caus, bd)


def _ssd_consts():
    l = np.arange(CHUNK)
    tril = (l[:, None] >= l[None, :]).astype(np.float32)
    tril3 = np.concatenate([tril, tril, tril], axis=1)
    n_heads = N_GROUPS * COL_BLOCK // HEAD_DIM
    col_head = np.arange(n_heads * HEAD_DIM) // HEAD_DIM
    col_pos = np.arange(n_heads * HEAD_DIM) % HEAD_DIM
    e = (np.arange(LANES)[:, None] == col_head[None, :]).astype(np.float32)
    e3 = np.concatenate([e, e, e], axis=0)
    diag = (l[:, None] == col_pos[None, :]).astype(np.float32)
    caus = np.concatenate([tril, tril], axis=1)
    r = np.arange(4 * HEAD_DIM)
    bd = (r[:, None] // HEAD_DIM == r[None, :] // HEAD_DIM).astype(np.float32)
    return (jnp.asarray(tril3, BF16), jnp.asarray(e3, BF16), jnp.asarray(diag, F32),
            jnp.asarray(caus, F32), jnp.asarray(bd, BF16))


def _merge_kernel(yg_ref, yp_ref, gs_ref, gp_ref, nw_ref, wps_ref, wpp_ref, o_ref):
    y = yg_ref[...]
    r = lax.rsqrt(jnp.mean(y * y, axis=-1, keepdims=True) + EPS)
    yn = (y * r * nw_ref[...]).astype(BF16)
    a = jnp.dot(yn, wps_ref[...], preferred_element_type=F32)
    b = jnp.dot(yp_ref[...].astype(BF16), wpp_ref[...], preferred_element_type=F32)
    for j in range(gs_ref.shape[0]):
        sl = slice(j * COL_BLOCK, (j + 1) * COL_BLOCK)
        o_ref[:, sl] = (jax.nn.sigmoid(gs_ref[j]) * a[:, sl]
                        + jax.nn.sigmoid(gp_ref[j]) * b[:, sl]).astype(BF16)


def _merge_call(yg, yp, p, nw, wps, wpp):
    t, d_inner = yg.shape
    d_pool = yp.shape[1]
    d_model = wps.shape[1]
    tm = _pick_tile(t, (256, 128, 64))
    nj = d_model // COL_BLOCK
    once = pl.Buffered(1)
    return pl.pallas_call(
        _merge_kernel,
        name="k_merge",
        grid=(t // tm,),
        in_specs=[pl.BlockSpec((tm, d_inner), lambda i: (i, 0)),
                  pl.BlockSpec((tm, d_pool), lambda i: (i, 0)),
                  pl.BlockSpec((nj, tm, COL_BLOCK), lambda i: (P_GATE_SSD // nj, i, 0)),
                  pl.BlockSpec((nj, tm, COL_BLOCK), lambda i: (P_GATE_POOL // nj, i, 0)),
                  pl.BlockSpec((1, d_inner), lambda i: (0, 0)),
                  pl.BlockSpec((d_inner, d_model), lambda i: (0, 0), pipeline_mode=once),
                  pl.BlockSpec((d_pool, d_model), lambda i: (0, 0), pipeline_mode=once)],
        out_specs=pl.BlockSpec((tm, d_model), lambda i: (i, 0)),
        out_shape=jax.ShapeDtypeStruct((t, d_model), BF16),
        compiler_params=_params(("parallel",)),
    )(yg, yp, p, p, nw, wps, wpp)


def _route_kernel(m_ref, xp_ref, xs_ref, wo_ref, nw_ref, wr1_ref, wr2_ref, br_ref, trs_ref,
                  h_ref, hn_ref, eidx_ref, rank_ref, gate_ref, cnt_ref, carry, *, n_first):
    i = pl.program_id(0)

    @pl.when(i == 0)
    def _():
        carry[...] = jnp.zeros_like(carry)

    x = jnp.where(i < n_first, xp_ref[...], xs_ref[...])
    h = x + jnp.dot(m_ref[...], wo_ref[...], preferred_element_type=F32)
    h_ref[...] = h
    r = lax.rsqrt(jnp.mean(h * h, axis=-1, keepdims=True) + EPS)
    hn = h * r * nw_ref[...]
    h1, h2 = _hi_lo(hn)
    _store_packed_rows(hn_ref, h1)
    logits = _dot_hi_lo(h1, h2, wr1_ref, wr2_ref) + br_ref[...]
    lane = lax.broadcasted_iota(jnp.int32, logits.shape, 1)
    work = logits
    member = jnp.zeros(logits.shape, F32)
    vals, idxs = [], []
    for _ in range(TOP_K):
        m = jnp.max(work, axis=-1, keepdims=True)
        idx = jnp.min(jnp.where(work == m, lane, LANES), axis=-1, keepdims=True)
        hit = lane == idx
        member = member + hit.astype(F32)
        work = jnp.where(hit, -jnp.inf, work)
        vals.append(m)
        idxs.append(idx)
    ex = [jnp.exp(v - vals[0]) for v in vals]
    den = ex[0] + ex[1] + ex[2] + ex[3]
    before = jnp.dot(trs_ref[...], member.astype(BF16), preferred_element_type=F32) + carry[0:1, :]
    eidx = jnp.zeros(logits.shape, jnp.int32)
    rank = jnp.zeros(logits.shape, jnp.int32)
    gate = jnp.zeros(logits.shape, F32)
    for k in range(TOP_K):
        rk = jnp.sum(jnp.where(lane == idxs[k], before, 0.0), axis=-1, keepdims=True)
        eidx = jnp.where(lane == k, idxs[k], eidx)
        rank = jnp.where(lane == k, rk.astype(jnp.int32), rank)
        gate = jnp.where(lane == k, ex[k] / den, gate)
    eidx_ref[...] = eidx
    rank_ref[...] = rank
    gate_ref[...] = gate
    carry[0:1, :] = carry[0:1, :] + jnp.sum(member, axis=0, keepdims=True)
    cnt_ref[...] = carry[...].astype(jnp.int32)


def _route_call(merged, xp, xs, wo, nw, wr1, wr2, br):
    t, d = merged.shape
    tm = _pick_tile(np.gcd(xp.shape[0], xs.shape[0]), (512, 256, 128, 64))
    n_first = xp.shape[0] // tm
    ri = np.arange(tm)
    trs = jnp.asarray((ri[:, None] > ri[None, :]).astype(np.float32), BF16)
    row = lambda i: (i, 0)
    fix = lambda i: (0, 0)
    return pl.pallas_call(
        functools.partial(_route_kernel, n_first=n_first),
        name="k_route",
        grid=(t // tm,),
        in_specs=[pl.BlockSpec((tm, d), row)] + _two_part_specs(tm, d, n_first) + [
                  pl.BlockSpec((d, d), fix), pl.BlockSpec((1, d), fix),
                  pl.BlockSpec((d, LANES), fix), pl.BlockSpec((d, LANES), fix),
                  pl.BlockSpec((1, LANES), fix), pl.BlockSpec((tm, tm), fix)],
        out_specs=[pl.BlockSpec((tm, d), row), pl.BlockSpec((tm * _packed_rows(d), LANES), row),
                   pl.BlockSpec((tm, LANES), row), pl.BlockSpec((tm, LANES), row),
                   pl.BlockSpec((tm, LANES), row), pl.BlockSpec((8, LANES), fix)],
        out_shape=[jax.ShapeDtypeStruct((t, d), F32),
                   jax.ShapeDtypeStruct((t * _packed_rows(d), LANES), jnp.uint32),
                   jax.ShapeDtypeStruct((t, LANES), jnp.int32),
                   jax.ShapeDtypeStruct((t, LANES), jnp.int32),
                   jax.ShapeDtypeStruct((t, LANES), F32),
                   jax.ShapeDtypeStruct((8, LANES), jnp.int32)],
        scratch_shapes=[pltpu.VMEM((8, LANES), F32)],
        compiler_params=_params(("arbitrary",)),
    )(merged, xp, xs, wo, nw, wr1, wr2, br, trs)


_PAD_CHUNKS = tuple(1 << s for s in range(MOE_ROWS.bit_length() - 2, -1, -1))


def _dispatch_kernel(dest_ref, cnt_ref, pstart_ref, x_ref, o_hbm, zbuf, sem, zsem,
                     *, rows, nr, n_experts):
    i = pl.program_id(0)

    def tokens(ref, first, n):
        return ref.at[pl.ds(pl.multiple_of(first * nr, nr), n * nr)]

    def issue(t, carry):
        for k in range(TOP_K):
            d = dest_ref[(i * rows + t) * TOP_K + k]
            pltpu.make_async_copy(tokens(x_ref, t, 1), tokens(o_hbm, d, 1), sem).start(priority=k % 2)
        return carry

    lax.fori_loop(0, rows, issue, 0, unroll=4)
    for k in range(TOP_K):
        pltpu.make_async_copy(x_ref, tokens(o_hbm, 0, rows), sem).wait()

    @pl.when(i == pl.num_programs(0) - 1)
    def _():
        zbuf[...] = jnp.zeros_like(zbuf)

        def pad_copies(e, wait):
            cnt = cnt_ref[e]
            off = pstart_ref[e] + cnt
            npad = (-cnt) & (MOE_ROWS - 1)
            for n_c in _PAD_CHUNKS:
                cp = pltpu.make_async_copy(tokens(zbuf, 0, n_c), tokens(o_hbm, off, n_c), zsem)
                pl.when((npad & n_c) != 0)(cp.wait if wait else cp.start)
                off = off + (npad & n_c)

        def start_e(e, carry):
            pad_copies(e, False)
            return carry

        def wait_e(e, carry):
            pad_copies(e, True)
            return carry

        lax.fori_loop(0, n_experts, start_e, 0)
        lax.fori_loop(0, n_experts, wait_e, 0)

        ztok = zbuf.shape[0] // nr
        first = (pstart_ref[n_experts - 1] + cnt_ref[n_experts - 1] + MOE_ROWS - 1) // MOE_ROWS
        first = first * (MOE_ROWS // ztok)

        def tail_copy(c):
            return pltpu.make_async_copy(zbuf, tokens(o_hbm, c * ztok, ztok), zsem)

        def start_t(c, carry):
            tail_copy(c).start()
            return carry

        def wait_t(c, carry):
            tail_copy(c).wait()
            return carry

        lax.fori_loop(first, o_hbm.shape[0] // zbuf.shape[0], start_t, 0)
        lax.fori_loop(first, o_hbm.shape[0] // zbuf.shape[0], wait_t, 0)


def _dispatch_call(dest, counts, pad_starts, src, n_rows, nr):
    t = src.shape[0] // nr
    rows = _pick_tile(t, (512, 256, 128, 64))
    return pl.pallas_call(
        functools.partial(_dispatch_kernel, rows=rows, nr=nr, n_experts=counts.shape[0]),
        name="k_dispatch",
        grid_spec=pltpu.PrefetchScalarGridSpec(
            num_scalar_prefetch=3,
            grid=(t // rows,),
            in_specs=[pl.BlockSpec((rows * nr, LANES), lambda i, de, cn, ps: (i, 0))],
            out_specs=pl.BlockSpec(memory_space=pl.ANY),
            scratch_shapes=[pltpu.VMEM((_PAD_CHUNKS[0] * nr, LANES), src.dtype),
                            pltpu.SemaphoreType.DMA, pltpu.SemaphoreType.DMA]),
        out_shape=jax.ShapeDtypeStruct((n_rows * nr, LANES), src.dtype),
        compiler_params=_params(("arbitrary",)),
    )(dest, counts, pad_starts, src)


def _moe_kernel(be_ref, nv_ref, nused_ref, x_ref, wg_ref, wl_ref, bu_ref, wd_ref, bd_ref,
                o_ref, xbf, acc, wub, wdb, *, n_steps):
    b = pl.program_id(0)
    j = pl.program_id(1)
    live = b < nused_ref[0]
    both = jnp.logical_and(live, nv_ref[b] > MOE_HALF)
    single = jnp.logical_and(live, jnp.logical_not(both))
    tf = wg_ref.shape[1]
    nr = _packed_rows(acc.shape[1])

    @pl.when(jnp.logical_and(j == 0, jnp.logical_not(live)))
    def _():
        o_ref[...] = jnp.zeros_like(o_ref)

    def half(lo, unpack, pack):
        rows = slice(lo, lo + MOE_HALF)
        if unpack:
            _load_packed_rows(x_ref, xbf, lo, MOE_HALF)
        gu = jnp.dot(xbf[rows, :], wub[...], preferred_element_type=F32)
        glu = jnp.minimum(gu[:, :tf] + bu_ref[pl.ds(j, 1), :], SWIGLU_LIMIT)
        lin = jnp.clip(gu[:, tf:] + bu_ref[pl.ds(n_steps + j, 1), :], -SWIGLU_LIMIT, SWIGLU_LIMIT)
        hg = 0.5 * glu
        act = (hg + hg * jnp.tanh((0.5 * SWIGLU_ALPHA) * glu)) * (lin + 1.0)
        down = jnp.dot(act.astype(BF16), wdb[...], preferred_element_type=F32)
        total = down + (bd_ref[...] if unpack else acc[rows, :])
        if pack:
            _store_packed_rows(o_ref, total.astype(BF16), lo)
        else:
            acc[rows, :] = total

    def region(n_halves, unpack, pack):
        wub[:, :tf] = wg_ref[...].astype(BF16)
        wub[:, tf:] = wl_ref[...].astype(BF16)
        wdb[...] = wd_ref[...].astype(BF16)
        for h in range(n_halves):
            half(h * MOE_HALF, unpack, pack)
        if pack and n_halves == 1:
            o_ref[MOE_HALF * nr:, :] = jnp.zeros((MOE_HALF * nr, LANES), o_ref.dtype)

    assert n_steps >= 2
    for n_halves, rows_live in ((2, both), (1, single)):
        for unpack, pack, at in ((True, False, j == 0), (False, True, j == n_steps - 1),
                                 (False, False, jnp.logical_and(j != 0, j != n_steps - 1))):
            pl.when(jnp.logical_and(rows_live, at))(functools.partial(region, n_halves, unpack, pack))


def _moe_call(block_e, block_nv, nused, xs, w_up, b_up, w_down, b_down):
    d_ff, d = w_down.shape[1:]
    nr = _packed_rows(d)
    n_rows = xs.shape[0] // nr
    tf = MOE_FF_TILE
    nj = d_ff // tf
    nb = n_rows // MOE_ROWS

    def bb(b, nu):
        return jnp.minimum(b, nu[0] - 1)

    def jj(b, j, nu):
        return jnp.where(b < nu[0], j, nj - 1)

    return pl.pallas_call(
        functools.partial(_moe_kernel, n_steps=nj),
        name="k_moe",
        grid_spec=pltpu.PrefetchScalarGridSpec(
            num_scalar_prefetch=3,
            grid=(nb, nj),
            in_specs=[pl.BlockSpec((MOE_ROWS * nr, LANES), lambda b, j, be, nv, nu: (bb(b, nu), 0)),
                      pl.BlockSpec((None, d, tf),
                                   lambda b, j, be, nv, nu: (be[bb(b, nu)], 0, jj(b, j, nu))),
                      pl.BlockSpec((None, d, tf),
                                   lambda b, j, be, nv, nu: (be[bb(b, nu)], 0, nj + jj(b, j, nu))),
                      pl.BlockSpec((None, 2 * nj, tf), lambda b, j, be, nv, nu: (be[bb(b, nu)], 0, 0)),
                      pl.BlockSpec((None, tf, d),
                                   lambda b, j, be, nv, nu: (be[bb(b, nu)], jj(b, j, nu), 0)),
                      pl.BlockSpec((None, 1, d), lambda b, j, be, nv, nu: (be[bb(b, nu)], 0, 0))],
            out_specs=pl.BlockSpec((MOE_ROWS * nr, LANES), lambda b, j, be, nv, nu: (b, 0)),
            scratch_shapes=[pltpu.VMEM((MOE_ROWS, d), BF16), pltpu.VMEM((MOE_ROWS, d), F32),
                            pltpu.VMEM((d, 2 * tf), BF16), pltpu.VMEM((tf, d), BF16)]),
        out_shape=jax.ShapeDtypeStruct((n_rows * nr, LANES), jnp.uint32),
        compiler_params=_params(("arbitrary", "arbitrary")),
    )(block_e, block_nv, nused, xs, w_up, w_up, b_up.reshape(-1, 2 * nj, tf), w_down, b_down)


def _combine_kernel(dest_ref, h_ref, gate_ref, nw_ref, eo_hbm, yp_ref, ys_ref, gbuf, sem,
                    *, rows, n_first):
    i = pl.program_id(0)
    n = pl.num_programs(0)
    slot = i % 2
    d = h_ref.shape[1]
    nr = _packed_rows(d)

    def issue(tile, into, t):
        dst_rows = pl.ds(pl.multiple_of(t * nr, nr), nr)
        for k in range(TOP_K):
            e_row = dest_ref[(tile * rows + t) * TOP_K + k]
            src = eo_hbm.at[pl.ds(pl.multiple_of(e_row * nr, nr), nr)]
            pltpu.make_async_copy(src, gbuf.at[into, k, dst_rows], sem.at[into]).start(priority=k % 2)

    def wait(which):
        for k in range(TOP_K):
            pltpu.make_async_copy(eo_hbm.at[pl.ds(0, rows * nr)], gbuf.at[which, k],
                                  sem.at[which]).wait()

    @pl.when(i == 0)
    def _():
        lax.fori_loop(0, rows, lambda t, c: issue(i, slot, t) or c, 0, unroll=4)

    wait(slot)
    nxt = jnp.minimum(i + 1, n - 1)
    for t in range(rows):
        issue(nxt, 1 - slot, t)
    gate = gate_ref[...]
    gates = [gate[:, k:k + 1] for k in range(TOP_K)]
    lo_cols, hi_cols = [], []
    for s in range(nr):
        lo = hi = None
        for k in range(TOP_K):
            words = gbuf[slot, k, pl.ds(s, rows, stride=nr), :]
            a = gates[k] * lax.bitcast_convert_type(words << 16, F32)
            b = gates[k] * lax.bitcast_convert_type(words & jnp.uint32(0xFFFF0000), F32)
            lo = a if lo is None else lo + a
            hi = b if hi is None else hi + b
        lo_cols.append(lo)
        hi_cols.append(hi)
    y = h_ref[...] + jnp.concatenate(lo_cols + hi_cols, axis=1)
    r = lax.rsqrt(jnp.mean(y * y, axis=-1, keepdims=True) + EPS)
    y = y * r * nw_ref[...]

    @pl.when(i < n_first)
    def _():
        yp_ref[...] = y

    @pl.when(i >= n_first)
    def _():
        ys_ref[...] = y

    @pl.when(i == n - 1)
    def _():
        wait(1 - slot)


def _combine_call(dest, h, gate, nw, eo, t_first):
    t, d = h.shape
    rows = _pick_tile(np.gcd(t_first, t - t_first), (256, 128, 64))
    n_first = t_first // rows
    return pl.pallas_call(
        functools.partial(_combine_kernel, rows=rows, n_first=n_first),
        name="k_combine",
        grid_spec=pltpu.PrefetchScalarGridSpec(
            num_scalar_prefetch=1,
            grid=(t // rows,),
            in_specs=[pl.BlockSpec((rows, d), lambda i, de: (i, 0)),
                      pl.BlockSpec((rows, LANES), lambda i, de: (i, 0)),
                      pl.BlockSpec((1, d), lambda i, de: (0, 0)),
                      pl.BlockSpec(memory_space=pl.ANY)],
            out_specs=_two_part_specs(rows, d, n_first),
            scratch_shapes=[pltpu.VMEM((2, TOP_K, rows * _packed_rows(d), LANES), jnp.uint32),
                            pltpu.SemaphoreType.DMA((2,))]),
        out_shape=[jax.ShapeDtypeStruct((t_first, d), F32),
                   jax.ShapeDtypeStruct((t - t_first, d), F32)],
        compiler_params=_params(("arbitrary",)),
    )(dest, h, gate, nw, eo)


def kernel(x_prompt, x_sample, state_ssm, state_conv, state_pool, norm_mix_w, w_in, conv_w, conv_b,
           dt_bias, a_log, d_skip, ssd_norm_w, w_pool, pool_scale, w_proj_ssd, w_proj_pool, w_out,
           norm_ffn_w, w_router, b_router, w_up, b_up, w_down, b_down, norm_final_w):
    batch, seq_len, d_model = x_prompt.shape
    dec_batch, dec_seq, _ = x_sample.shape
    depth, _, n_heads, head_dim, d_state = state_ssm.shape
    assert depth == 1 and batch == 1 and dec_seq == CHUNK and seq_len % CHUNK == 0
    assert head_dim == HEAD_DIM and d_state == D_STATE
    d_inner = n_heads * head_dim
    assert d_inner == N_GROUPS * COL_BLOCK
    d_bc = 2 * N_GROUPS * D_STATE
    d_pool = state_pool.shape[-1]
    assert d_pool == len(POOL_WINDOWS) * COL_BLOCK and state_pool.shape[-2] == POOL_BUF
    n_experts = w_router.shape[-1]
    n_prompt_chunks = seq_len // CHUNK
    n_seq = batch + dec_batch
    t_prompt = batch * seq_len
    t = t_prompt + dec_batch * dec_seq

    xp = x_prompt.reshape(t_prompt, d_model)
    xs_tok = x_sample.reshape(-1, d_model)

    w_t = jnp.swapaxes(w_in[0], 0, 1)
    xbc_end = 2 * d_inner + d_bc
    dt_end = xbc_end + n_heads
    w_dt1, w_dt2 = _hi_lo(jnp.pad(w_t[xbc_end:dt_end].T, ((0, 0), (0, LANES - n_heads))))
    pad_h = lambda a: jnp.pad(a.reshape(1, n_heads), ((0, 0), (0, LANES - n_heads)))
    cw = conv_w[0]
    cwx, cwbc = cw[:, :d_inner], cw[:, d_inner:]
    cb = conv_b[0].reshape(1, -1)
    cbx, cbbc = cb[:, :d_inner], cb[:, d_inner:]
    dsk = jnp.repeat(d_skip[0], head_dim).reshape(1, d_inner)

    conv0 = jnp.pad(state_conv[0], ((batch, 0), (8 - (CONV_WIDTH - 1), 0), (0, 0)))
    cx0, cbc0 = conv0[..., :d_inner], conv0[..., d_inner:]
    pool0 = jnp.pad(state_pool[0], ((batch, 0), (1, 0), (0, 0)))

    u, dt_raw = _norm_call(xp, xs_tok, norm_mix_w[0].reshape(1, -1), w_dt1, w_dt2)
    p = _inproj_call(u, w_t, xbc_end, dt_end)
    yg, yp, ssm_new = _mixer_call(p, dt_raw, state_ssm[0], cx0, cbc0, pool0, pad_h(dt_bias[0]),
                                  pad_h(a_log[0]), dsk, cwx, cbx, cwbc, cbbc,
                                  w_pool[0].astype(BF16), pool_scale[0].reshape(1, -1),
                                  _ssd_consts(), n_prompt_chunks)
    merged = _merge_call(yg, yp, p, ssd_norm_w[0].reshape(1, -1), w_proj_ssd[0].astype(BF16),
                         w_proj_pool[0].astype(BF16))

    wr1, wr2 = _hi_lo(jnp.pad(w_router[0], ((0, 0), (0, LANES - n_experts))))
    br = jnp.pad(b_router[0].reshape(1, -1), ((0, 0), (0, LANES - n_experts)), constant_values=-1e30)
    h, hn, eidx, rank, gate, cnt = _route_call(merged, xp, xs_tok, w_out[0].astype(BF16),
                                               norm_ffn_w[0].reshape(1, -1), wr1, wr2, br)

    counts = cnt[0, :n_experts]
    padded = (counts + MOE_ROWS - 1) // MOE_ROWS * MOE_ROWS
    pad_ends = jnp.cumsum(padded)
    pad_starts = pad_ends - padded
    n_blocks = -(-(t * TOP_K) // MOE_ROWS) + n_experts
    n_rows = n_blocks * MOE_ROWS
    e_flat = eidx[:, :TOP_K].reshape(-1)
    dest = (pad_starts[e_flat] + rank[:, :TOP_K].reshape(-1)).astype(jnp.int32)
    block_start = jnp.arange(n_blocks, dtype=jnp.int32) * MOE_ROWS
    block_e = jnp.minimum(jnp.sum(block_start[:, None] >= pad_ends[None, :], axis=1),
                          n_experts - 1).astype(jnp.int32)
    block_nv = jnp.clip(pad_starts[block_e] + counts[block_e] - block_start, 0,
                        MOE_ROWS).astype(jnp.int32)
    nused = (pad_ends[-1:] // MOE_ROWS).astype(jnp.int32)

    xs = _dispatch_call(dest, counts, pad_starts.astype(jnp.int32), hn, n_rows, _packed_rows(d_model))
    eo = _moe_call(block_e, block_nv, nused, xs, w_up[0], b_up[0],
                   w_down[0], b_down[0].reshape(n_experts, 1, -1))
    y_p, y_s = _combine_call(dest, h, gate, norm_final_w.reshape(1, -1), eo, t_prompt)

    y_prompt = y_p.reshape(batch, seq_len, d_model)
    y_sample = y_s.reshape(dec_batch, dec_seq, d_model)
    seq_ends = [t_prompt] * batch + [t_prompt + (s + 1) * dec_seq for s in range(dec_batch)]
    tail = jnp.stack([p[:, e - POOL_BUF:e] for e in seq_ends], axis=0)
    tail = tail.transpose(0, 2, 1, 3)
    ctail = tail[:, POOL_BUF - (CONV_WIDTH - 1):]
    conv_x = ctail[:, :, P_X:P_BC].reshape(n_seq, CONV_WIDTH - 1, d_inner)
    conv_bc = ctail[:, :, P_BC:P_POOL].reshape(n_seq, CONV_WIDTH - 1, d_bc)
    conv_new = jnp.concatenate([conv_x, conv_bc], axis=-1)
    pool_new = tail[:, :, P_POOL:P_GATE_SSD].reshape(n_seq, POOL_BUF, d_pool)
    return (y_prompt, y_sample,
            ssm_new[None, :batch], conv_new[None, :batch], pool_new[None, :batch],
            ssm_new[None, batch:], conv_new[None, batch:], pool_new[None, batch:])
```
